```python
import math
import jax, jax.numpy as jnp
from jax import lax
import numpy as np

D_MODEL = 1024
BATCH = 16
SEQ = 256
DEPTH = 4
DEC_BATCH = 4
DEC_SEQ = 1024
PAST_LEN = 512

GRID_W = 64
ROPE_THETA = 10000.0
Q_BLOCK = 128
WINDOW = 128
RMS_EPS = 1e-6
N_EVEN = (DEPTH + 1) // 2
N_ODD = DEPTH // 2
MLA_HEADS = 8
MLA_Q_LORA = D_MODEL // 4
MLA_KV_LORA = D_MODEL // 8
MLA_NOPE = 64
MLA_ROPE = 32
MLA_VD = 64
DIFF_HEADS = 4
DIFF_HD = 64
DIFF_VD = 2 * DIFF_HD
HEAD_DIM = 64
WIN_HEADS = 8
WIN_KV_HEADS = 2
AX_HEADS = 8
AX_KV_HEADS = 2
N_EXPERTS = 16
EC_CAPACITY_FACTOR = 2
EXPERT_FF = 2 * D_MODEL

EVEN_IN_WIDTH = MLA_Q_LORA + MLA_KV_LORA + MLA_ROPE + 2 * (DIFF_HEADS * 2 * DIFF_HD) + DIFF_HEADS * DIFF_VD
EVEN_OUT_WIDTH = MLA_HEADS * MLA_VD + DIFF_HEADS * DIFF_VD
ODD_IN_WIDTH = (WIN_HEADS + 2 * WIN_KV_HEADS + AX_HEADS + 2 * AX_KV_HEADS) * HEAD_DIM
ODD_OUT_WIDTH = (WIN_HEADS + AX_HEADS) * HEAD_DIM

kernel_name = 'hybrid_diffusion_mla_diff_window_axial_ec'


def _rms(x, g):
    xf = x.astype(jnp.float32)
    y = xf * lax.rsqrt(jnp.mean(xf * xf, axis=-1, keepdims=True) + RMS_EPS)
    return (y * g.astype(jnp.float32)).astype(x.dtype)


def _axial_rope_tables(rows, rot_dim):
    row = jnp.repeat(jnp.arange(rows), GRID_W)
    col = jnp.tile(jnp.arange(GRID_W), rows)
    nf = rot_dim // 4
    inv = ROPE_THETA ** (-jnp.arange(nf, dtype=jnp.float32) / nf)
    ang = jnp.stack([row[:, None] * inv, col[:, None] * inv], axis=1)
    return jnp.cos(ang), jnp.sin(ang)


def _rope(x, tables):
    cos, sin = tables
    B, N, H, r = x.shape
    nf = r // 4
    xr = x.reshape(B, N, H, 2, 2, nf)
    x1, x2 = xr[..., 0, :], xr[..., 1, :]
    c = cos[:, None].astype(x.dtype)
    s = sin[:, None].astype(x.dtype)
    return jnp.stack([x1 * c - x2 * s, x2 * c + x1 * s], axis=-2).reshape(B, N, H, r)


def _softmax_with_sink(s, sink):
    if sink is None:
        return jax.nn.softmax(s, axis=-1)
    sk = jnp.broadcast_to(sink.astype(jnp.float32)[:, :, None, None], s.shape[:-1] + (1,))
    return jax.nn.softmax(jnp.concatenate([s, sk], axis=-1), axis=-1)[..., :-1]


def _dense_attn(q, k, v, sink=None):
    B, N, Hk, G, d = q.shape
    scale = d ** -0.5
    nb = N // Q_BLOCK
    qb = jnp.moveaxis(q.reshape(B, nb, Q_BLOCK, Hk, G, d), 1, 0)

    def block(qi):
        s = jnp.einsum('bqhgd,bmhd->bhgqm', qi, k, preferred_element_type=jnp.float32) * scale
        p = _softmax_with_sink(s, sink)
        return jnp.einsum('bhgqm,bmhe->bqhge', p.astype(v.dtype), v)

    o = lax.map(block, qb)
    return jnp.moveaxis(o, 0, 1).reshape(B, N, Hk, G, v.shape[-1])


def _banded_attn(q, k, v, k_ctx, v_ctx, sink):
    B, N, Hk, G, d = q.shape
    scale = d ** -0.5
    nb = N // Q_BLOCK
    KB = 3 * Q_BLOCK
    pad = ((0, 0), (Q_BLOCK, Q_BLOCK), (0, 0), (0, 0))
    kp = jnp.pad(k, pad)
    vp = jnp.pad(v, pad)
    idx = jnp.arange(nb)[:, None] * Q_BLOCK + jnp.arange(KB)[None, :]
    kb = kp[:, idx]
    vb = vp[:, idx]
    qb = q.reshape(B, nb, Q_BLOCK, Hk, G, d)
    s_band = jnp.einsum('bnqhgd,bnkhd->bnhgqk', qb, kb, preferred_element_type=jnp.float32) * scale
    s_ctx = jnp.einsum('bnqhgd,bmhd->bnhgqm', qb, k_ctx, preferred_element_type=jnp.float32) * scale
    qpos = jnp.arange(nb)[:, None] * Q_BLOCK + jnp.arange(Q_BLOCK)[None, :]
    kpos = idx - Q_BLOCK
    valid = ((jnp.abs(qpos[:, :, None] - kpos[:, None, :]) <= WINDOW)
             & (kpos[:, None, :] >= 0) & (kpos[:, None, :] < N))
    s_band = jnp.where(valid[None, :, None, None], s_band, -jnp.inf)
    M = k_ctx.shape[1]
    p = _softmax_with_sink(jnp.concatenate([s_ctx, s_band], axis=-1), sink).astype(v.dtype)
    o = (jnp.einsum('bnhgqm,bmhe->bnqhge', p[..., :M], v_ctx)
         + jnp.einsum('bnhgqk,bnkhe->bnqhge', p[..., M:], vb))
    return o.reshape(B, N, Hk, G, v.shape[-1])


def _ec_moe(h, w_r, w1, w3, w2):
    B, N, _ = h.shape
    cap = (EC_CAPACITY_FACTOR * N) // N_EXPERTS
    aff = jax.nn.softmax(jnp.einsum('bnd,de->bne', h, w_r, preferred_element_type=jnp.float32), axis=-1)
    gate, idx = lax.top_k(jnp.swapaxes(aff, 1, 2), cap)
    bidx = jnp.arange(B)[:, None, None]
    xs = h[bidx, idx]
    a = jnp.einsum('becd,edf->becf', xs, w1)
    u = jnp.einsum('becd,edf->becf', xs, w3)
    y = jnp.einsum('becf,efd->becd', jax.nn.silu(a) * u, w2) * gate[..., None].astype(h.dtype)
    return jnp.zeros_like(h).at[bidx, idx].add(y)


def _even_mixer(h, j, layer_idx, P, cache, cs_r, cs_h):
    B, N, _ = h.shape
    o1 = MLA_Q_LORA
    o2 = o1 + MLA_KV_LORA
    o3 = o2 + MLA_ROPE
    o4 = o3 + DIFF_HEADS * 2 * DIFF_HD
    o5 = o4 + DIFF_HEADS * 2 * DIFF_HD
    c_q, c_kv, k_r, mq, mk, mv = jnp.split(h @ P['w_in_even'][j], [o1, o2, o3, o4, o5], axis=-1)
    g_q = P['mla_g_q'][j]
    g_k = P['mla_g_k'][j]
    c_q = _rms(c_q, P['mla_g_qa'][j])
    c_kv = _rms(c_kv, P['mla_g_kva'][j])
    q = (c_q @ P['mla_w_uq'][j]).reshape(B, N, MLA_HEADS, MLA_NOPE + MLA_ROPE)
    q_nope = _rms(q[..., :MLA_NOPE], g_q[:MLA_NOPE])
    q_rope = _rms(q[..., MLA_NOPE:], g_q[MLA_NOPE:])
    k_r = _rms(k_r, g_k[MLA_NOPE:])
    mq = _rms(mq.reshape(B, N, 2 * DIFF_HEADS, DIFF_HD), P['diff_g_q'][j])
    mk = _rms(mk.reshape(B, N, 2 * DIFF_HEADS, DIFF_HD), P['diff_g_k'][j])
    mv = mv.reshape(B, N, DIFF_HEADS, DIFF_VD)
    if cache is None:
        ckv_all, kr_all, dk_all, dv_all = c_kv, k_r, mk, mv
        state = (c_kv, k_r, mk.reshape(B, N, DIFF_HEADS, 2 * DIFF_HD), mv)
    else:
        q_rope = _rope(q_rope, cs_r)
        k_r = _rope(k_r[:, :, None], cs_r)[:, :, 0]
        mq = _rope(mq, cs_h)
        mk = _rope(mk, cs_h)
        c_ckv, c_kr, c_dk, c_dv = cache
        M = c_ckv.shape[1]
        ckv_all = jnp.concatenate([c_ckv, c_kv], axis=1)
        kr_all = jnp.concatenate([c_kr, k_r], axis=1)
        dk_all = jnp.concatenate([c_dk.reshape(B, M, 2 * DIFF_HEADS, DIFF_HD), mk], axis=1)
        dv_all = jnp.concatenate([c_dv, mv], axis=1)
        state = None
    T = ckv_all.shape[1]
    kv = (ckv_all @ P['mla_w_ukv'][j]).reshape(B, T, MLA_HEADS, MLA_NOPE + MLA_VD)
    k_nope = _rms(kv[..., :MLA_NOPE], g_k[:MLA_NOPE])
    v_m = kv[..., MLA_NOPE:]
    k_m = jnp.concatenate([k_nope, jnp.broadcast_to(kr_all[:, :, None, :], (B, T, MLA_HEADS, MLA_ROPE))], axis=-1)
    q_m = jnp.concatenate([q_nope, q_rope], axis=-1)[:, :, :, None]
    o_mla = _dense_attn(q_m, k_m, v_m).reshape(B, N, MLA_HEADS * MLA_VD)
    lam_vec = P['diff_lambda'][j].astype(jnp.float32)
    lam_init = 0.8 - 0.6 * math.exp(-0.3 * layer_idx)
    lam = (jnp.exp(jnp.sum(lam_vec[0] * lam_vec[1])) - jnp.exp(jnp.sum(lam_vec[2] * lam_vec[3])) + lam_init).astype(h.dtype)
    qd = mq.reshape(B, N, DIFF_HEADS, 2, DIFF_HD)
    kd = dk_all.reshape(B, T, DIFF_HEADS, 2, DIFF_HD)
    a1 = _dense_attn(qd[:, :, :, 0:1], kd[:, :, :, 0], dv_all)
    a2 = _dense_attn(qd[:, :, :, 1:2], kd[:, :, :, 1], dv_all)
    o_diff = _rms((a1 - lam * a2)[:, :, :, 0], P['diff_g_sub'][j]) * (1.0 - lam_init)
    out = jnp.concatenate([o_mla, o_diff.reshape(B, N, DIFF_HEADS * DIFF_VD)], axis=-1) @ P['w_out_even'][j]
    return out, state


def _odd_mixer(h, j, P, cache, cs_h):
    B, N, _ = h.shape
    Gc = WIN_HEADS // WIN_KV_HEADS
    Gd = AX_HEADS // AX_KV_HEADS
    s1 = WIN_HEADS * HEAD_DIM
    s2 = s1 + WIN_KV_HEADS * HEAD_DIM
    s3 = s2 + WIN_KV_HEADS * HEAD_DIM
    s4 = s3 + AX_HEADS * HEAD_DIM
    s5 = s4 + AX_KV_HEADS * HEAD_DIM
    qc, kc, vc, qd, kd, vd = jnp.split(h @ P['w_in_odd'][j], [s1, s2, s3, s4, s5], axis=-1)
    g = P['odd_g_qk'][j]
    qc = _rms(qc.reshape(B, N, WIN_HEADS, HEAD_DIM), g[0])
    kc = _rms(kc.reshape(B, N, WIN_KV_HEADS, HEAD_DIM), g[1])
    vc = vc.reshape(B, N, WIN_KV_HEADS, HEAD_DIM)
    qd = _rms(qd.reshape(B, N, AX_HEADS, HEAD_DIM), g[2])
    kd = _rms(kd.reshape(B, N, AX_KV_HEADS, HEAD_DIM), g[3])
    vd = vd.reshape(B, N, AX_KV_HEADS, HEAD_DIM)
    sink = P['win_sink'][j].reshape(WIN_KV_HEADS, Gc)
    if cache is None:
        state = (kc, vc, kd, vd)
        o_c = _dense_attn(qc.reshape(B, N, WIN_KV_HEADS, Gc, HEAD_DIM), kc, vc, sink)
        o_d = _dense_attn(qd.reshape(B, N, AX_KV_HEADS, Gd, HEAD_DIM), kd, vd)
    else:
        qc = _rope(qc, cs_h)
        kc = _rope(kc, cs_h)
        qd = _rope(qd, cs_h)
        kd = _rope(kd, cs_h)
        ck_c, cv_c, ck_d, cv_d = cache
        o_c = _banded_attn(qc.reshape(B, N, WIN_KV_HEADS, Gc, HEAD_DIM), kc, vc, ck_c, cv_c, sink)
        o_d = _dense_attn(qd.reshape(B, N, AX_KV_HEADS, Gd, HEAD_DIM),
                          jnp.concatenate([ck_d, kd], axis=1), jnp.concatenate([cv_d, vd], axis=1))
        state = None
    out = jnp.concatenate([o_c.reshape(B, N, WIN_HEADS * HEAD_DIM), o_d.reshape(B, N, AX_HEADS * HEAD_DIM)], axis=-1)
    return out @ P['w_out_odd'][j], state


def _adaln(c, w_mod, b_mod):
    m = jax.nn.silu(c) @ w_mod + b_mod
    return m.reshape(m.shape[0], 1, 6, D_MODEL)


def _layer(l, x, mod, cache, P, cs_r, cs_h):
    sh1, sc1, gt1, sh2, sc2, gt2 = [mod[:, :, i] for i in range(6)]
    h = _rms(x, P['g_norm'][l, 0]) * (1.0 + sc1) + sh1
    if l % 2 == 0:
        out, state = _even_mixer(h, l // 2, l, P, cache, cs_r, cs_h)
    else:
        out, state = _odd_mixer(h, l // 2, P, cache, cs_h)
    x = x + gt1 * out
    h = _rms(x, P['g_norm'][l, 1]) * (1.0 + sc2) + sh2
    x = x + gt2 * _ec_moe(h, P['moe_w_router'][l], P['moe_w1'][l], P['moe_w3'][l], P['moe_w2'][l])
    return x, state


def setup_inputs(seed: int = 0) -> dict:
    key = jax.random.key(seed)
    ks = iter(jax.random.split(key, 48))

    def nrm(shape, scale):
        return jax.random.normal(next(ks), shape, jnp.float32) * scale

    def gain(shape):
        return 1.0 + nrm(shape, 0.05)

    D = D_MODEL
    return {
        'x_prompt': nrm((BATCH, SEQ, D), 1.0),
        'x_sample': nrm((DEC_BATCH, DEC_SEQ, D), 1.0),
        'cache_mla_ckv': nrm((DEC_BATCH, N_EVEN, PAST_LEN, MLA_KV_LORA), 1.0),
        'cache_mla_krope': nrm((DEC_BATCH, N_EVEN, PAST_LEN, MLA_ROPE), 1.0),
        'cache_diff_k': nrm((DEC_BATCH, N_EVEN, PAST_LEN, DIFF_HEADS, 2 * DIFF_HD), 1.0),
        'cache_diff_v': nrm((DEC_BATCH, N_EVEN, PAST_LEN, DIFF_HEADS, DIFF_VD), 1.0),
        'cache_win_k': nrm((DEC_BATCH, N_ODD, PAST_LEN, WIN_KV_HEADS, HEAD_DIM), 1.0),
        'cache_win_v': nrm((DEC_BATCH, N_ODD, PAST_LEN, WIN_KV_HEADS, HEAD_DIM), 1.0),
        'cache_axial_k': nrm((DEC_BATCH, N_ODD, PAST_LEN, AX_KV_HEADS, HEAD_DIM), 1.0),
        'cache_axial_v': nrm((DEC_BATCH, N_ODD, PAST_LEN, AX_KV_HEADS, HEAD_DIM), 1.0),
        'c': nrm((DEC_BATCH, D), 1.0),
        'c_ctx': nrm((D,), 1.0),
        'g_norm': gain((DEPTH, 2, D)),
        'w_mod': nrm((DEPTH, D, 6 * D), 0.5 * D ** -0.5),
        'b_mod': nrm((DEPTH, 6 * D), 0.02),
        'w_in_even': nrm((N_EVEN, D, EVEN_IN_WIDTH), D ** -0.5),
        'w_out_even': nrm((N_EVEN, EVEN_OUT_WIDTH, D), EVEN_OUT_WIDTH ** -0.5),
        'mla_g_qa': gain((N_EVEN, MLA_Q_LORA)),
        'mla_g_kva': gain((N_EVEN, MLA_KV_LORA)),
        'mla_w_uq': nrm((N_EVEN, MLA_Q_LORA, MLA_HEADS * (MLA_NOPE + MLA_ROPE)), MLA_Q_LORA ** -0.5),
        'mla_w_ukv': nrm((N_EVEN, MLA_KV_LORA, MLA_HEADS * (MLA_NOPE + MLA_VD)), MLA_KV_LORA ** -0.5),
        'mla_g_q': gain((N_EVEN, MLA_NOPE + MLA_ROPE)),
        'mla_g_k': gain((N_EVEN, MLA_NOPE + MLA_ROPE)),
        'diff_g_q': gain((N_EVEN, DIFF_HD)),
        'diff_g_k': gain((N_EVEN, DIFF_HD)),
        'diff_lambda': nrm((N_EVEN, 4, DIFF_HD), 0.1),
        'diff_g_sub': gain((N_EVEN, DIFF_VD)),
        'w_in_odd': nrm((N_ODD, D, ODD_IN_WIDTH), D ** -0.5),
        'w_out_odd': nrm((N_ODD, ODD_OUT_WIDTH, D), ODD_OUT_WIDTH ** -0.5),
        'odd_g_qk': gain((N_ODD, 4, HEAD_DIM)),
        'win_sink': nrm((N_ODD, WIN_HEADS), 1.0),
        'moe_w_router': nrm((DEPTH, D, N_EXPERTS), D ** -0.5),
        'moe_w1': nrm((DEPTH, N_EXPERTS, D, EXPERT_FF), D ** -0.5),
        'moe_w3': nrm((DEPTH, N_EXPERTS, D, EXPERT_FF), D ** -0.5),
        'moe_w2': nrm((DEPTH, N_EXPERTS, EXPERT_FF, D), EXPERT_FF ** -0.5),
    }


def reference(x_prompt, x_sample, cache_mla_ckv, cache_mla_krope, cache_diff_k, cache_diff_v,
              cache_win_k, cache_win_v, cache_axial_k, cache_axial_v, c, c_ctx,
              g_norm, w_mod, b_mod, w_in_even, w_out_even, mla_g_qa, mla_g_kva, mla_w_uq, mla_w_ukv,
              mla_g_q, mla_g_k, diff_g_q, diff_g_k, diff_lambda, diff_g_sub,
              w_in_odd, w_out_odd, odd_g_qk, win_sink, moe_w_router, moe_w1, moe_w3, moe_w2):
    P = {'g_norm': g_norm, 'w_in_even': w_in_even, 'w_out_even': w_out_even,
         'mla_g_qa': mla_g_qa, 'mla_g_kva': mla_g_kva, 'mla_w_uq': mla_w_uq, 'mla_w_ukv': mla_w_ukv,
         'mla_g_q': mla_g_q, 'mla_g_k': mla_g_k, 'diff_g_q': diff_g_q, 'diff_g_k': diff_g_k,
         'diff_lambda': diff_lambda, 'diff_g_sub': diff_g_sub, 'w_in_odd': w_in_odd, 'w_out_odd': w_out_odd,
         'odd_g_qk': odd_g_qk, 'win_sink': win_sink, 'moe_w_router': moe_w_router,
         'moe_w1': moe_w1, 'moe_w3': moe_w3, 'moe_w2': moe_w2}

    x = x_prompt
    even_states = []
    odd_states = []
    for l in range(DEPTH):
        mod = _adaln(c_ctx[None], w_mod[l], b_mod[l])
        x, st = _layer(l, x, mod, None, P, None, None)
        if l % 2 == 0:
            even_states.append(st)
        else:
            odd_states.append(st)
    y_prompt = x
    new_mla_ckv = jnp.stack([s[0] for s in even_states], axis=1)
    new_mla_krope = jnp.stack([s[1] for s in even_states], axis=1)
    new_diff_k = jnp.stack([s[2] for s in even_states], axis=1)
    new_diff_v = jnp.stack([s[3] for s in even_states], axis=1)
    new_win_k = jnp.stack([s[0] for s in odd_states], axis=1)
    new_win_v = jnp.stack([s[1] for s in odd_states], axis=1)
    new_axial_k = jnp.stack([s[2] for s in odd_states], axis=1)
    new_axial_v = jnp.stack([s[3] for s in odd_states], axis=1)

    rows = x_sample.shape[1] // GRID_W
    cs_r = _axial_rope_tables(rows, MLA_ROPE)
    cs_h = _axial_rope_tables(rows, HEAD_DIM)
    x = x_sample
    for l in range(DEPTH):
        mod = _adaln(c, w_mod[l], b_mod[l])
        j = l // 2
        if l % 2 == 0:
            cache = (cache_mla_ckv[:, j], cache_mla_krope[:, j], cache_diff_k[:, j], cache_diff_v[:, j])
        else:
            cache = (cache_win_k[:, j], cache_win_v[:, j], cache_axial_k[:, j], cache_axial_v[:, j])
        x, _ = _layer(l, x, mod, cache, P, cs_r, cs_h)
    y_sample = x
    return (y_prompt, y_sample, new_mla_ckv, new_mla_krope, new_diff_k, new_diff_v,
            new_win_k, new_win_v, new_axial_k, new_axial_v)
```

```python
import functools
import math

import jax
import jax.numpy as jnp
from jax import lax
from jax.experimental import pallas as pl
from jax.experimental.pallas import tpu as pltpu

F32 = jnp.float32
BF16 = jnp.bfloat16

D_MODEL = 1024
DEPTH = 4
GRID_W = 64
ROPE_THETA = 10000.0
WINDOW = 128
RMS_EPS = 1e-6
MLA_HEADS = 8
MLA_Q_LORA = 256
MLA_KV_LORA = 128
MLA_NOPE = 64
MLA_ROPE = 32
MLA_VD = 64
DIFF_HEADS = 4
DIFF_HD = 64
HEAD_DIM = 64
N_EXPERTS = 16
EC_CAPACITY_FACTOR = 2
EXPERT_FF = 2048

LANES = 128
ROW_TILE = 256
Q_TILE = 256
FF_CHUNK = 512
VMEM_LIMIT = 50 * 1024 * 1024
NEG_BIG = -1e30


def _params(*sem):
    return pltpu.CompilerParams(dimension_semantics=sem, vmem_limit_bytes=VMEM_LIMIT)


def _full(shape):
    zeros = (0,) * len(shape)
    return pl.BlockSpec(shape, lambda *_: zeros)


def _dot(a, b):
    return jnp.dot(a, b, preferred_element_type=F32)


def _dot_nt(a, b):
    return lax.dot_general(a, b, (((1,), (1,)), ((), ())), preferred_element_type=F32)


def _rms(x, g):
    ms = jnp.mean(x * x, axis=-1, keepdims=True)
    return x * lax.rsqrt(ms + RMS_EPS) * g


def _group_rms(x, g, gmat):
    outs = []
    for k in range(x.shape[1] // LANES):
        xb = x[:, k * LANES:(k + 1) * LANES]
        ms = _dot((xb * xb).astype(BF16), gmat)
        outs.append(xb * lax.rsqrt(ms + RMS_EPS))
    y = outs[0] if len(outs) == 1 else jnp.concatenate(outs, axis=1)
    return y * g


def _rope(x, cos, sin, half):
    lane = lax.broadcasted_iota(jnp.int32, (1, LANES), 1)
    first = (lane % (2 * half)) < half
    outs = []
    for k in range(x.shape[1] // LANES):
        xb = x[:, k * LANES:(k + 1) * LANES]
        fwd = pltpu.roll(xb, LANES - half, 1)
        bwd = pltpu.roll(xb, half, 1)
        outs.append(xb * cos + jnp.where(first, fwd, bwd) * sin)
    return outs[0] if len(outs) == 1 else jnp.concatenate(outs, axis=1)


def _silu(a):
    return a / (1.0 + jnp.exp(-a))


def _adaln_kernel(c_ref, w_ref, b_ref, o_ref):
    a = _silu(c_ref[...]).astype(BF16)
    o_ref[0] = _dot(a, w_ref[0].astype(BF16)) + b_ref[0]


def _adaln(cvec, w_mod, b_mod):
    rows = cvec.shape[0]
    nc = 4
    wc = 6 * D_MODEL // nc
    return pl.pallas_call(
        _adaln_kernel,
        grid=(DEPTH, nc),
        in_specs=[_full((rows, D_MODEL)),
                  pl.BlockSpec((1, D_MODEL, wc), lambda l, n: (l, 0, n)),
                  pl.BlockSpec((1, 1, wc), lambda l, n: (l, 0, n))],
        out_specs=pl.BlockSpec((1, rows, wc), lambda l, n: (l, 0, n)),
        out_shape=jax.ShapeDtypeStruct((DEPTH, rows, 6 * D_MODEL), F32),
        compiler_params=_params("parallel", "parallel"),
        name="adaln",
    )(cvec, w_mod, b_mod.reshape(DEPTH, 1, 6 * D_MODEL))


def _modulated(x_ref, mod_ref, g_ref, shift_row, scale_row):
    mod = mod_ref[0]
    return (_rms(x_ref[...], g_ref[...]) * (1.0 + mod[scale_row:scale_row + 1])
            + mod[shift_row:shift_row + 1])


def _in_even_kernel(latent, *refs):
    (x_ref, mod_ref, g1_ref, w_ref, gqa_ref, gkva_ref, wuq_ref, wukv_ref,
     gqn_ref, gqr_ref, gkn_ref, gkr_ref, gdq_ref, gdk_ref, g64_ref, g32_ref) = refs[:16]
    refs = refs[16:]
    if latent:
        c64_ref, s64_ref, c32_ref, s32_ref = refs[:4]
        refs = refs[4:]
    qn_ref, qr_ref, kn_ref, vm_ref, krt_ref, mq_ref, mk_ref, mv_ref = refs[:8]
    refs = refs[8:]
    g64 = g64_ref[...]
    g32 = g32_ref[...]
    mla_scale = (MLA_NOPE + MLA_ROPE) ** -0.5
    diff_scale = DIFF_HD ** -0.5

    h = _modulated(x_ref, mod_ref, g1_ref, 0, 1)
    proj = _dot(h.astype(BF16), w_ref[...])
    cq = _rms(proj[:, 0:256], gqa_ref[...])
    q = _dot(cq.astype(BF16), wuq_ref[...])
    qn = _group_rms(q[:, 0:512], gqn_ref[...], g64)
    qr = _group_rms(q[:, 512:768], gqr_ref[...], g32)
    ckv = _rms(proj[:, 256:384], gkva_ref[...])
    kv = _dot(ckv.astype(BF16), wukv_ref[...])
    kn = _group_rms(kv[:, 0:512], gkn_ref[...], g64)
    mq = _group_rms(proj[:, 384:896], gdq_ref[...], g64)
    mk = _group_rms(proj[:, 896:1408], gdk_ref[...], g64)
    mv = proj[:, 1408:1920]
    kr = _group_rms(proj[:, 1920:2048], gkr_ref[...], g32)
    if latent:
        c64, s64, c32, s32 = c64_ref[...], s64_ref[...], c32_ref[...], s32_ref[...]
        qr = _rope(qr, c32, s32, MLA_ROPE // 4)
        kr = _rope(kr, c32, s32, MLA_ROPE // 4)
        mq = _rope(mq, c64, s64, DIFF_HD // 4)
        mk = _rope(mk, c64, s64, DIFF_HD // 4)
    else:
        ckv_s_ref, kr_s_ref, mk_s_ref, mv_s_ref = refs
        ckv_s_ref[...] = ckv
        kr_s_ref[...] = kr
        mk_s_ref[...] = mk
        mv_s_ref[...] = mv
    qn_ref[...] = (qn * mla_scale).astype(BF16)
    qr_ref[...] = (qr * mla_scale).astype(BF16)
    kn_ref[...] = kn.astype(BF16)
    vm_ref[...] = kv[:, 512:1024].astype(BF16)
    krt_ref[...] = kr.astype(BF16)
    mq_ref[...] = (mq * diff_scale).astype(BF16)
    mk_ref[...] = mk.astype(BF16)
    mv_ref[...] = mv.astype(BF16)


def _in_odd_kernel(latent, *refs):
    x_ref, mod_ref, g1_ref, w_ref, gqc_ref, gkc_ref, gqd_ref, gkd_ref, g64_ref = refs[:9]
    refs = refs[9:]
    if latent:
        c64_ref, s64_ref = refs[:2]
        refs = refs[2:]
    qc_ref, qd_ref, kc_ref, vc_ref, kd_ref, vd_ref = refs
    g64 = g64_ref[...]
    scale = HEAD_DIM ** -0.5

    h = _modulated(x_ref, mod_ref, g1_ref, 0, 1)
    proj = _dot(h.astype(BF16), w_ref[...])
    qc = _group_rms(proj[:, 0:512], gqc_ref[...], g64)
    kc = _group_rms(proj[:, 512:640], gkc_ref[...], g64)
    qd = _group_rms(proj[:, 768:1280], gqd_ref[...], g64)
    kd = _group_rms(proj[:, 1280:1408], gkd_ref[...], g64)
    if latent:
        c64, s64 = c64_ref[...], s64_ref[...]
        qc = _rope(qc, c64, s64, HEAD_DIM // 4)
        kc = _rope(kc, c64, s64, HEAD_DIM // 4)
        qd = _rope(qd, c64, s64, HEAD_DIM // 4)
        kd = _rope(kd, c64, s64, HEAD_DIM // 4)
    qc_ref[...] = (qc * scale).astype(BF16)
    qd_ref[...] = (qd * scale).astype(BF16)
    kc_ref[...] = kc
    vc_ref[...] = proj[:, 640:768]
    kd_ref[...] = kd
    vd_ref[...] = proj[:, 1408:1536]


def _row_specs(latent, n_tokens):
    tiles_per_batch = n_tokens // ROW_TILE
    if latent:
        mod_map = lambda i: (1 + i // tiles_per_batch, 0, 0)
    else:
        mod_map = lambda i: (0, 0, 0)
    mod_spec = pl.BlockSpec((1, 6, D_MODEL), mod_map)
    table_spec = pl.BlockSpec((ROW_TILE, LANES), lambda i: (i % tiles_per_batch, 0))
    return mod_spec, table_spec


def _rows(width):
    return pl.BlockSpec((ROW_TILE, width), lambda i: (i, 0))


def _in_even(latent, n_tokens, x, mods, g1, w, wuq, wukv, gains, gmats, tables):
    rows = x.shape[0]
    mod_spec, table_spec = _row_specs(latent, n_tokens)
    vec_specs = [_full(g.shape) for g in gains]
    in_specs = ([_rows(D_MODEL), mod_spec, _full(g1.shape), _full(w.shape)]
                + vec_specs[:2] + [_full(wuq.shape), _full(wukv.shape)] + vec_specs[2:]
                + [_full((LANES, LANES))] * 2)
    args = [x, mods, g1, w, gains[0], gains[1], wuq, wukv] + list(gains[2:]) + list(gmats)
    widths = [512, 256, 512, 512, 128, 512, 512, 512]
    out_shape = [jax.ShapeDtypeStruct((rows, wd), BF16) for wd in widths]
    out_specs = [_rows(wd) for wd in widths]
    if latent:
        in_specs += [table_spec] * 4
        args += list(tables)
    else:
        for wd in (128, 128, 512, 512):
            out_shape.append(jax.ShapeDtypeStruct((rows, wd), F32))
            out_specs.append(_rows(wd))
    return pl.pallas_call(
        functools.partial(_in_even_kernel, latent),
        grid=(rows // ROW_TILE,),
        in_specs=in_specs, out_specs=out_specs, out_shape=out_shape,
        compiler_params=_params("parallel"),
        name="in_even_lat" if latent else "in_even_ctx",
    )(*args)


def _in_odd(latent, n_tokens, x, mods, g1, w, gains, g64, tables):
    rows = x.shape[0]
    mod_spec, table_spec = _row_specs(latent, n_tokens)
    in_specs = ([_rows(D_MODEL), mod_spec, _full(g1.shape), _full(w.shape)]
                + [_full(g.shape) for g in gains] + [_full((LANES, LANES))])
    args = [x, mods, g1, w] + list(gains) + [g64]
    if latent:
        in_specs += [table_spec] * 2
        args += list(tables)
    widths = [512, 512, 128, 128, 128, 128]
    dtypes = [BF16, BF16, F32, F32, F32, F32]
    return pl.pallas_call(
        functools.partial(_in_odd_kernel, latent),
        grid=(rows // ROW_TILE,),
        in_specs=in_specs,
        out_specs=[_rows(wd) for wd in widths],
        out_shape=[jax.ShapeDtypeStruct((rows, wd), dt) for wd, dt in zip(widths, dtypes)],
        compiler_params=_params("parallel"),
        name="in_odd_lat" if latent else "in_odd_ctx",
    )(*args)


def _cache_kv_kernel(ckv_ref, wukv_ref, gkn_ref, g64_ref, kn_ref, vm_ref):
    kv = _dot(ckv_ref[...].astype(BF16), wukv_ref[...])
    kn_ref[...] = _group_rms(kv[:, 0:512], gkn_ref[...], g64_ref[...]).astype(BF16)
    vm_ref[...] = kv[:, 512:1024].astype(BF16)


def _cache_kv(ckv, wukv, gkn, g64):
    rows = ckv.shape[0]
    tile = 512
    spec = lambda wd: pl.BlockSpec((tile, wd), lambda i: (i, 0))
    return pl.pallas_call(
        _cache_kv_kernel,
        grid=(rows // tile,),
        in_specs=[spec(MLA_KV_LORA), _full(wukv.shape), _full(gkn.shape), _full((LANES, LANES))],
        out_specs=[spec(512), spec(512)],
        out_shape=[jax.ShapeDtypeStruct((rows, 512), BF16)] * 2,
        compiler_params=_params("parallel"),
        name="cache_kv",
    )(ckv, wukv, gkn, g64)


def _softmax_pv(s_parts, v_parts, sink=None):
    m = None
    for s in s_parts:
        mi = jnp.max(s, axis=-1, keepdims=True)
        m = mi if m is None else jnp.maximum(m, mi)
    if sink is not None:
        m = jnp.maximum(m, sink)
    denom = None
    out = None
    for s, v in zip(s_parts, v_parts):
        p = jnp.exp(s - m)
        li = jnp.sum(p, axis=-1, keepdims=True)
        oi = _dot(p.astype(BF16), v)
        denom = li if denom is None else denom + li
        out = oi if out is None else out + oi
    if sink is not None:
        denom = denom + jnp.exp(sink - m)
    return out * (1.0 / denom)


def _lane_masks(width):
    lane = lax.broadcasted_iota(jnp.int32, (1, LANES), 1)
    return [jnp.where(lane // width == k, 1.0, 0.0).astype(BF16) for k in range(LANES // width)]


def _attn_even_kernel(latent, lam_init, *refs):
    qn_ref, qr_ref, mq_ref = refs[:3]
    refs = refs[3:]
    n_parts = 2 if latent else 1
    parts = [refs[5 * k:5 * k + 5] for k in range(n_parts)]
    lamv_ref, gsub_ref, o_ref = refs[5 * n_parts:]
    lane = lax.broadcasted_iota(jnp.int32, (1, LANES), 1)
    low = lane < 64
    m64 = _lane_masks(64)
    m32 = _lane_masks(32)

    for i in range(MLA_HEADS // 2):
        cols = slice(i * LANES, (i + 1) * LANES)
        qn_b = qn_ref[:, cols]
        kcat = [jnp.concatenate([p[0][:, cols], p[2][...]], axis=1) for p in parts]
        vms = [p[1][:, cols] for p in parts]
        outs = []
        for half in range(2):
            head = 2 * i + half
            rb = head // 4
            qr_b = qr_ref[:, rb * LANES:(rb + 1) * LANES]
            lhs = jnp.concatenate([qn_b * m64[half], qr_b * m32[head % 4]], axis=1)
            outs.append(_softmax_pv([_dot_nt(lhs, kc) for kc in kcat], vms))
        o_ref[:, cols] = jnp.where(low, outs[0], outs[1]).astype(BF16)

    lv = lamv_ref[...]
    lam = (jnp.exp(jnp.sum(lv[0:1] * lv[1:2], axis=-1, keepdims=True))
           - jnp.exp(jnp.sum(lv[2:3] * lv[3:4], axis=-1, keepdims=True)) + lam_init)
    gsub = gsub_ref[...]
    for hd in range(DIFF_HEADS):
        cols = slice(hd * LANES, (hd + 1) * LANES)
        q_b = mq_ref[:, cols]
        ks = [p[3][:, cols] for p in parts]
        vs = [p[4][:, cols] for p in parts]
        a1 = _softmax_pv([_dot_nt(q_b * m64[0], k) for k in ks], vs)
        a2 = _softmax_pv([_dot_nt(q_b * m64[1], k) for k in ks], vs)
        d = _rms(a1 - lam * a2, gsub) * (1.0 - lam_init)
        o_ref[:, 512 + hd * LANES:512 + (hd + 1) * LANES] = d.astype(BF16)


def _attn_even(latent, batch, n_tokens, lam_init, q_arrs, new_arrs, cache_arrs, lamv, gsub):
    tq = min(Q_TILE, n_tokens)
    nt = n_tokens // tq
    qspec = lambda wd: pl.BlockSpec((tq, wd), lambda b, t: (b * nt + t, 0))
    kspec = lambda rows, wd: pl.BlockSpec((rows, wd), lambda b, t: (b, 0))
    kwidths = [512, 512, 128, 512, 512]
    in_specs = [qspec(512), qspec(256), qspec(512)]
    args = list(q_arrs)
    if latent:
        past = cache_arrs[0].shape[0] // batch
        in_specs += [kspec(past, wd) for wd in kwidths]
        args += list(cache_arrs)
    in_specs += [kspec(n_tokens, wd) for wd in kwidths]
    args += list(new_arrs)
    in_specs += [_full(lamv.shape), _full(gsub.shape)]
    args += [lamv, gsub]
    return pl.pallas_call(
        functools.partial(_attn_even_kernel, latent, lam_init),
        grid=(batch, nt),
        in_specs=in_specs,
        out_specs=qspec(D_MODEL),
        out_shape=jax.ShapeDtypeStruct((batch * n_tokens, D_MODEL), BF16),
        compiler_params=_params("parallel", "parallel"),
        name="attn_even_lat" if latent else "attn_even_ctx",
    )(*args)


def _attn_odd_kernel(latent, tq, *refs):
    qc_ref, qd_ref = refs[:2]
    refs = refs[2:]
    n_parts = 2 if latent else 1
    parts = [refs[4 * k:4 * k + 4] for k in range(n_parts)]
    sink_ref, o_ref = refs[4 * n_parts:]
    lane = lax.broadcasted_iota(jnp.int32, (1, LANES), 1)
    low = lane < 64
    m64 = _lane_masks(64)

    def dup(ref, g):
        x = ref[...]
        r = pltpu.roll(x, 64, 1)
        return (jnp.where(low, x, r) if g == 0 else jnp.where(low, r, x)).astype(BF16)

    band = None
    if latent:
        n_new = parts[-1][0].shape[0]
        qpos = pl.program_id(1) * tq + lax.broadcasted_iota(jnp.int32, (tq, 1), 0)
        kpos = lax.broadcasted_iota(jnp.int32, (1, n_new), 1)
        band = jnp.abs(qpos - kpos) <= WINDOW

    for kind in range(2):
        q_ref = qc_ref if kind == 0 else qd_ref
        for g in range(2):
            ks = [dup(p[2 * kind], g) for p in parts]
            vs = [dup(p[2 * kind + 1], g) for p in parts]
            for i in (2 * g, 2 * g + 1):
                cols = slice(i * LANES, (i + 1) * LANES)
                q_b = q_ref[:, cols]
                outs = []
                for half in range(2):
                    s_parts = [_dot_nt(q_b * m64[half], k) for k in ks]
                    sink = None
                    if kind == 0:
                        sink = sink_ref[2 * i + half]
                        if latent:
                            s_parts[-1] = jnp.where(band, s_parts[-1], NEG_BIG)
                    outs.append(_softmax_pv(s_parts, vs, sink))
                o_ref[:, kind * 512 + i * LANES:kind * 512 + (i + 1) * LANES] = (
                    jnp.where(low, outs[0], outs[1]).astype(BF16))


def _attn_odd(latent, batch, n_tokens, q_arrs, new_arrs, cache_arrs, sink):
    tq = min(Q_TILE, n_tokens)
    nt = n_tokens // tq
    qspec = lambda wd: pl.BlockSpec((tq, wd), lambda b, t: (b * nt + t, 0))
    kspec = lambda rows: pl.BlockSpec((rows, LANES), lambda b, t: (b, 0))
    in_specs = [qspec(512), qspec(512)]
    args = list(q_arrs)
    if latent:
        past = cache_arrs[0].shape[0] // batch
        in_specs += [kspec(past)] * 4
        args += list(cache_arrs)
    in_specs += [kspec(n_tokens)] * 4
    args += list(new_arrs)
    in_specs += [pl.BlockSpec(memory_space=pltpu.SMEM)]
    args += [sink]
    return pl.pallas_call(
        functools.partial(_attn_odd_kernel, latent, tq),
        grid=(batch, nt),
        in_specs=in_specs,
        out_specs=qspec(D_MODEL),
        out_shape=jax.ShapeDtypeStruct((batch * n_tokens, D_MODEL), BF16),
        compiler_params=_params("parallel", "parallel"),
        name="attn_odd_lat" if latent else "attn_odd_ctx",
    )(*args)


def _post_kernel(o_ref, wout_ref, x_ref, mod_ref, g2_ref, wrh_ref, wrl_ref, x1_ref, h2_ref, afft_ref):
    mod = mod_ref[0]
    x1 = x_ref[...] + mod[2:3] * _dot(o_ref[...], wout_ref[...])
    x1_ref[...] = x1
    h2 = _rms(x1, g2_ref[...]) * (1.0 + mod[4:5]) + mod[3:4]
    h_hi = h2.astype(BF16)
    h_lo = (h2 - h_hi.astype(F32)).astype(BF16)
    h2_ref[...] = h_hi
    wrh = wrh_ref[...]
    logits = _dot(h_hi, wrh) + _dot(h_lo, wrh) + _dot(h_hi, wrl_ref[...])
    lane = lax.broadcasted_iota(jnp.int32, (1, LANES), 1)
    logits = jnp.where(lane < N_EXPERTS, logits, NEG_BIG)
    e = jnp.exp(logits - jnp.max(logits, axis=-1, keepdims=True))
    aff = e / jnp.sum(e, axis=-1, keepdims=True)
    afft_ref[...] = aff.T[0:N_EXPERTS]


def _post(latent, n_tokens, o, wout, x, mods, g2, wrh, wrl):
    rows = x.shape[0]
    mod_spec, _ = _row_specs(latent, n_tokens)
    return pl.pallas_call(
        _post_kernel,
        grid=(rows // ROW_TILE,),
        in_specs=[_rows(D_MODEL), _full(wout.shape), _rows(D_MODEL), mod_spec, _full(g2.shape),
                  _full(wrh.shape), _full(wrl.shape)],
        out_specs=[_rows(D_MODEL), _rows(D_MODEL),
                   pl.BlockSpec((N_EXPERTS, ROW_TILE), lambda i: (0, i))],
        out_shape=[jax.ShapeDtypeStruct((rows, D_MODEL), F32),
                   jax.ShapeDtypeStruct((rows, D_MODEL), BF16),
                   jax.ShapeDtypeStruct((N_EXPERTS, rows), F32)],
        compiler_params=_params("parallel"),
        name="post_lat" if latent else "post_ctx",
    )(o, wout, x, mods, g2, wrh, wrl)


def _route_kernel(cap, aff_ref, out_ref):
    aff = aff_ref[...]
    n_rows, n = aff.shape
    bits = lax.bitcast_convert_type(aff, jnp.int32)

    def body(_, carry):
        lo, hi = carry
        mid = lo + ((hi - lo + 1) >> 1)
        cnt = jnp.sum(jnp.where(bits >= mid, 1.0, 0.0), axis=-1, keepdims=True)
        ok = cnt >= cap
        return jnp.where(ok, mid, lo), jnp.where(ok, hi, mid - 1)

    lo0 = jnp.zeros((n_rows, 1), jnp.int32)
    hi0 = jnp.full((n_rows, 1), 0x7F800000, jnp.int32)
    thr, _ = lax.fori_loop(0, 31, body, (lo0, hi0))
    above = jnp.where(bits > thr, 1.0, 0.0)
    equal = jnp.where(bits == thr, 1.0, 0.0)
    room = cap - jnp.sum(above, axis=-1, keepdims=True)
    before = jnp.where(lax.broadcasted_iota(jnp.int32, (n, n), 0)
                       < lax.broadcasted_iota(jnp.int32, (n, n), 1), 1.0, 0.0).astype(BF16)
    equal_before = _dot(equal.astype(BF16), before)
    chosen = above + equal * jnp.where(equal_before < room, 1.0, 0.0)
    slot = _dot(chosen.astype(BF16), before)
    out_ref[...] = jnp.where(chosen > 0.5, slot, -1.0)


def _route(cap, aff_rows):
    return pl.pallas_call(
        functools.partial(_route_kernel, cap),
        grid=(1,),
        in_specs=[_full(aff_rows.shape)],
        out_specs=_full(aff_rows.shape),
        out_shape=jax.ShapeDtypeStruct(aff_rows.shape, F32),
        compiler_params=_params("arbitrary"),
        name="route",
    )(aff_rows)


def _gather_kernel(cap, slot_ref, aff_ref, h_ref, xs_ref, gate_ref):
    slots = slot_ref[...]
    aff = aff_ref[...]
    want = lax.broadcasted_iota(jnp.int32, (cap, 1), 0).astype(F32)
    rows = []
    for e in range(N_EXPERTS):
        hit = slots[e:e + 1, :] == want
        rows.append(jnp.where(hit, 1.0, 0.0).astype(BF16))
        gate_ref[e] = jnp.sum(jnp.where(hit, aff[e:e + 1, :], 0.0), axis=-1, keepdims=True)
    picked = _dot(jnp.concatenate(rows, axis=0), h_ref[...]).astype(BF16)
    for e in range(N_EXPERTS):
        xs_ref[e] = picked[e * cap:(e + 1) * cap]


def _gather(cap, batch, n_tokens, slots, aff_rows, h2):
    return pl.pallas_call(
        functools.partial(_gather_kernel, cap),
        grid=(batch,),
        in_specs=[pl.BlockSpec((N_EXPERTS, n_tokens), lambda b: (b, 0)),
                  pl.BlockSpec((N_EXPERTS, n_tokens), lambda b: (b, 0)),
                  pl.BlockSpec((n_tokens, D_MODEL), lambda b: (b, 0))],
        out_specs=[pl.BlockSpec((N_EXPERTS, cap, D_MODEL), lambda b: (0, b, 0)),
                   pl.BlockSpec((N_EXPERTS, cap, 1), lambda b: (0, b, 0))],
        out_shape=[jax.ShapeDtypeStruct((N_EXPERTS, batch * cap, D_MODEL), BF16),
                   jax.ShapeDtypeStruct((N_EXPERTS, batch * cap, 1), F32)],
        compiler_params=_params("parallel"),
        name="gather",
    )(slots, aff_rows, h2)


def _ffn_kernel(n_groups, *refs):
    xs_refs = refs[:n_groups]
    gate_refs = refs[n_groups:2 * n_groups]
    w1_ref, w3_ref, w2_ref = refs[2 * n_groups:2 * n_groups + 3]
    ys_refs = refs[2 * n_groups + 3:3 * n_groups + 3]
    acc_refs = refs[3 * n_groups + 3:]
    f = pl.program_id(1)
    w1 = w1_ref[0, 0].astype(BF16)
    w3 = w3_ref[0, 0].astype(BF16)
    w2 = w2_ref[0, 0].astype(BF16)
    for xs_ref, gate_ref, ys_ref, acc_ref in zip(xs_refs, gate_refs, ys_refs, acc_refs):
        xs = xs_ref[0]
        hid = (_silu(_dot(xs, w1)) * _dot(xs, w3)).astype(BF16)
        part = _dot(hid, w2)

        @pl.when(f == 0)
        def _():
            acc_ref[...] = part

        @pl.when(f > 0)
        def _():
            acc_ref[...] += part

        @pl.when(f == pl.num_programs(1) - 1)
        def _():
            ys_ref[0] = (acc_ref[...] * gate_ref[0]).astype(BF16)


def _ffn(layer, xs_list, gate_list, w1, w3, w2):
    n_groups = len(xs_list)
    n_chunks = EXPERT_FF // FF_CHUNK
    xs_specs = [pl.BlockSpec((1,) + xs.shape[1:], lambda e, f: (e, 0, 0)) for xs in xs_list]
    gate_specs = [pl.BlockSpec((1,) + g.shape[1:], lambda e, f: (e, 0, 0)) for g in gate_list]
    return pl.pallas_call(
        functools.partial(_ffn_kernel, n_groups),
        grid=(N_EXPERTS, n_chunks),
        in_specs=xs_specs + gate_specs + [
            pl.BlockSpec((1, 1, D_MODEL, FF_CHUNK), lambda e, f: (layer, e, 0, f)),
            pl.BlockSpec((1, 1, D_MODEL, FF_CHUNK), lambda e, f: (layer, e, 0, f)),
            pl.BlockSpec((1, 1, FF_CHUNK, D_MODEL), lambda e, f: (layer, e, f, 0))],
        out_specs=xs_specs,
        out_shape=[jax.ShapeDtypeStruct(xs.shape, BF16) for xs in xs_list],
        scratch_shapes=[pltpu.VMEM(xs.shape[1:], F32) for xs in xs_list],
        compiler_params=_params("parallel", "arbitrary"),
        name="ffn",
    )(*xs_list, *gate_list, w1, w3, w2)


def _combine_kernel(cap, slot_ref, ys_ref, x_ref, mod_ref, o_ref):
    slots = slot_ref[...]
    n = slots.shape[1]
    pad = jnp.zeros((LANES - N_EXPERTS, n), F32)
    slots_t = jnp.concatenate([slots, pad], axis=0).T
    lane = lax.broadcasted_iota(jnp.int32, (1, LANES), 1).astype(F32)
    per_block = LANES // cap
    blocks = []
    for k in range(N_EXPERTS // per_block):
        hit = None
        for e in range(k * per_block, (k + 1) * per_block):
            col = slots_t[:, e:e + 1]
            target = jnp.where(col >= 0.0, col + float((e - k * per_block) * cap), -1.0)
            he = jnp.where(target == lane, 1.0, 0.0)
            hit = he if hit is None else hit + he
        blocks.append(hit.astype(BF16))
    onehot = jnp.concatenate(blocks, axis=1)
    ys = jnp.concatenate([ys_ref[e] for e in range(N_EXPERTS)], axis=0)
    o_ref[...] = x_ref[...] + mod_ref[0][5:6] * _dot(onehot, ys)


def _combine(latent, cap, batch, n_tokens, slots, ys, x1, mods):
    mod_map = (lambda b: (1 + b, 0, 0)) if latent else (lambda b: (0, 0, 0))
    return pl.pallas_call(
        functools.partial(_combine_kernel, cap),
        grid=(batch,),
        in_specs=[pl.BlockSpec((N_EXPERTS, n_tokens), lambda b: (b, 0)),
                  pl.BlockSpec((N_EXPERTS, cap, D_MODEL), lambda b: (0, b, 0)),
                  pl.BlockSpec((n_tokens, D_MODEL), lambda b: (b, 0)),
                  pl.BlockSpec((1, 6, D_MODEL), mod_map)],
        out_specs=pl.BlockSpec((n_tokens, D_MODEL), lambda b: (b, 0)),
        out_shape=jax.ShapeDtypeStruct(x1.shape, F32),
        compiler_params=_params("parallel"),
        name="combine",
    )(slots, ys, x1, mods)


def _rope_tables(n_pos, rot_dim):
    t = jnp.arange(n_pos)
    row = t // GRID_W
    col = t % GRID_W
    nf = rot_dim // 4
    inv = ROPE_THETA ** (-jnp.arange(nf, dtype=F32) / nf)
    ang_r = row[:, None] * inv
    ang_c = col[:, None] * inv
    cr, sr, cc, sc = jnp.cos(ang_r), jnp.sin(ang_r), jnp.cos(ang_c), jnp.sin(ang_c)
    cos = jnp.concatenate([cr, cr, cc, cc], axis=1)
    sin = jnp.concatenate([-sr, sr, -sc, sc], axis=1)
    reps = LANES // rot_dim
    return jnp.tile(cos, (1, reps)), jnp.tile(sin, (1, reps))


def _group_mean_matrix(width):
    idx = jnp.arange(LANES) // width
    return jnp.where(idx[:, None] == idx[None, :], 1.0 / width, 0.0).astype(BF16)


def _tile_row(g, reps):
    return jnp.tile(g, reps).reshape(1, -1)


def kernel(x_prompt, x_sample, cache_mla_ckv, cache_mla_krope, cache_diff_k, cache_diff_v,
           cache_win_k, cache_win_v, cache_axial_k, cache_axial_v, c, c_ctx,
           g_norm, w_mod, b_mod, w_in_even, w_out_even, mla_g_qa, mla_g_kva, mla_w_uq, mla_w_ukv,
           mla_g_q, mla_g_k, diff_g_q, diff_g_k, diff_lambda, diff_g_sub,
           w_in_odd, w_out_odd, odd_g_qk, win_sink, moe_w_router, moe_w1, moe_w3, moe_w2):
    n_ctx_b, n_ctx = x_prompt.shape[:2]
    n_lat_b, n_lat = x_sample.shape[:2]
    past = cache_mla_ckv.shape[2]
    passes = [
        (False, n_ctx_b, n_ctx, EC_CAPACITY_FACTOR * n_ctx // N_EXPERTS),
        (True, n_lat_b, n_lat, EC_CAPACITY_FACTOR * n_lat // N_EXPERTS),
    ]

    cvec = jnp.zeros((16, D_MODEL), F32).at[0].set(c_ctx).at[1:1 + n_lat_b].set(c)
    mods = _adaln(cvec, w_mod, b_mod).reshape(DEPTH, 16, 6, D_MODEL)

    g64 = _group_mean_matrix(64)
    g32 = _group_mean_matrix(32)
    tab64 = _rope_tables(n_lat, HEAD_DIM)
    tab32 = _rope_tables(n_lat, MLA_ROPE)

    perm_in = jnp.concatenate([jnp.arange(0, 384), jnp.arange(416, 1952), jnp.tile(jnp.arange(384, 416), 4)])
    hq = jnp.arange(MLA_HEADS)[:, None] * (MLA_NOPE + MLA_ROPE)
    perm_q = jnp.concatenate([(hq + jnp.arange(MLA_NOPE)[None, :]).reshape(-1),
                              (hq + MLA_NOPE + jnp.arange(MLA_ROPE)[None, :]).reshape(-1)])
    hk = jnp.arange(MLA_HEADS)[:, None] * (MLA_NOPE + MLA_VD)
    perm_kv = jnp.concatenate([(hk + jnp.arange(MLA_NOPE)[None, :]).reshape(-1),
                               (hk + MLA_NOPE + jnp.arange(MLA_VD)[None, :]).reshape(-1)])

    xs_state = [x_prompt.reshape(-1, D_MODEL), x_sample.reshape(-1, D_MODEL)]
    even_states, odd_states = [], []

    for layer in range(DEPTH):
        j = layer // 2
        mods_l = mods[layer]
        g1 = g_norm[layer, 0].reshape(1, -1)
        g2 = g_norm[layer, 1].reshape(1, -1)
        attn_out = []
        if layer % 2 == 0:
            w_in = w_in_even[j][:, perm_in].astype(BF16)
            wuq = mla_w_uq[j][:, perm_q].astype(BF16)
            wukv = mla_w_ukv[j][:, perm_kv].astype(BF16)
            w_out = w_out_even[j].astype(BF16)
            gkn = _tile_row(mla_g_k[j, :MLA_NOPE], MLA_HEADS)
            gains = [mla_g_qa[j].reshape(1, -1), mla_g_kva[j].reshape(1, -1),
                     _tile_row(mla_g_q[j, :MLA_NOPE], MLA_HEADS), _tile_row(mla_g_q[j, MLA_NOPE:], MLA_HEADS),
                     gkn, _tile_row(mla_g_k[j, MLA_NOPE:], 4),
                     _tile_row(diff_g_q[j], 2 * DIFF_HEADS), _tile_row(diff_g_k[j], 2 * DIFF_HEADS)]
            lam_init = 0.8 - 0.6 * math.exp(-0.3 * layer)
            gsub = diff_g_sub[j].reshape(1, -1)
            for p, (latent, batch, n_tok, cap) in enumerate(passes):
                outs = _in_even(latent, n_tok, xs_state[p], mods_l, g1, w_in, wuq, wukv, gains,
                                (g64, g32), tab64 + tab32)
                qn, qr, kn, vm, krt, mq, mk, mv = outs[:8]
                cache_arrs = None
                if latent:
                    kn_c, vm_c = _cache_kv(cache_mla_ckv[:, j].reshape(-1, MLA_KV_LORA), wukv, gkn, g64)
                    cache_arrs = (kn_c, vm_c,
                                  jnp.tile(cache_mla_krope[:, j].reshape(-1, MLA_ROPE), (1, 4)).astype(BF16),
                                  cache_diff_k[:, j].reshape(-1, 512).astype(BF16),
                                  cache_diff_v[:, j].reshape(-1, 512).astype(BF16))
                else:
                    even_states.append(outs[8:])
                attn_out.append(_attn_even(latent, batch, n_tok, lam_init, (qn, qr, mq),
                                           (kn, vm, krt, mk, mv), cache_arrs, diff_lambda[j], gsub))
        else:
            w_in = w_in_odd[j].astype(BF16)
            w_out = w_out_odd[j].astype(BF16)
            gains = [_tile_row(odd_g_qk[j, 0], 8), _tile_row(odd_g_qk[j, 1], 2),
                     _tile_row(odd_g_qk[j, 2], 8), _tile_row(odd_g_qk[j, 3], 2)]
            for p, (latent, batch, n_tok, cap) in enumerate(passes):
                qc, qd, kc, vc, kd, vd = _in_odd(latent, n_tok, xs_state[p], mods_l, g1, w_in, gains, g64, tab64)
                cache_arrs = None
                if latent:
                    cache_arrs = tuple(a[:, j].reshape(-1, LANES)
                                       for a in (cache_win_k, cache_win_v, cache_axial_k, cache_axial_v))
                else:
                    odd_states.append((kc, vc, kd, vd))
                attn_out.append(_attn_odd(latent, batch, n_tok, (qc, qd), (kc, vc, kd, vd), cache_arrs,
                                          win_sink[j]))

        w_r = jnp.pad(moe_w_router[layer], ((0, 0), (0, LANES - N_EXPERTS)))
        wrh = w_r.astype(BF16)
        wrl = (w_r - wrh.astype(F32)).astype(BF16)
        x1s, slot_list, xs_list, gate_list = [], [], [], []
        for p, (latent, batch, n_tok, cap) in enumerate(passes):
            x1, h2, afft = _post(latent, n_tok, attn_out[p], w_out, xs_state[p], mods_l, g2, wrh, wrl)
            aff_rows = afft.reshape(N_EXPERTS, batch, n_tok).transpose(1, 0, 2).reshape(batch * N_EXPERTS, n_tok)
            slots = _route(cap, aff_rows)
            xs, gates = _gather(cap, batch, n_tok, slots, aff_rows, h2)
            x1s.append(x1)
            slot_list.append(slots)
            xs_list.append(xs)
            gate_list.append(gates)
        ys_list = _ffn(layer, xs_list, gate_list, moe_w1, moe_w3, moe_w2)
        for p, (latent, batch, n_tok, cap) in enumerate(passes):
            xs_state[p] = _combine(latent, cap, batch, n_tok, slot_list[p], ys_list[p], x1s[p], mods_l)

    y_prompt = xs_state[0].reshape(x_prompt.shape)
    y_sample = xs_state[1].reshape(x_sample.shape)

    def stack(states, idx, width, tail):
        arrs = [s[idx][:, :width].reshape((n_ctx_b, n_ctx) + tail) for s in states]
        return jnp.stack(arrs, axis=1)

    return (y_prompt, y_sample,
            stack(even_states, 0, MLA_KV_LORA, (MLA_KV_LORA,)),
            stack(even_states, 1, MLA_ROPE, (MLA_ROPE,)),
            stack(even_states, 2, 512, (DIFF_HEADS, 2 * DIFF_HD)),
            stack(even_states, 3, 512, (DIFF_HEADS, 2 * DIFF_HD)),
            stack(odd_states, 0, LANES, (2, HEAD_DIM)),
            stack(odd_states, 1, LANES, (2, HEAD_DIM)),
            stack(odd_states, 2, LANES, (2, HEAD_DIM)),
            stack(odd_states, 3, LANES, (2, HEAD_DIM)))
```

```python
import functools
import math

import jax
import jax.numpy as jnp
from jax import lax
from jax.experimental import pallas as pl
from jax.experimental.pallas import tpu as pltpu

F32 = jnp.float32
BF16 = jnp.bfloat16

D_MODEL = 1024
DEPTH = 4
GRID_W = 64
ROPE_THETA = 10000.0
WINDOW = 128
RMS_EPS = 1e-6
MLA_HEADS = 8
MLA_Q_LORA = 256
MLA_KV_LORA = 128
MLA_NOPE = 64
MLA_ROPE = 32
MLA_VD = 64
DIFF_HEADS = 4
DIFF_HD = 64
HEAD_DIM = 64
N_EXPERTS = 16
EC_CAPACITY_FACTOR = 2
EXPERT_FF = 2048

LANES = 128
MAX_ROW_TILE = 512
Q_TILE = 256
FF_CHUNK = 512
VMEM_LIMIT = 50 * 1024 * 1024
NEG_BIG = -1e30
LOG2E = 1.4426950408889634


def _params(*sem):
    return pltpu.CompilerParams(dimension_semantics=sem, vmem_limit_bytes=VMEM_LIMIT)


def _full(shape):
    zeros = (0,) * len(shape)
    return pl.BlockSpec(shape, lambda *_: zeros)


def _dot(a, b):
    return jnp.dot(a, b, preferred_element_type=F32)


def _dot_nt(a, b):
    return lax.dot_general(a, b, (((1,), (1,)), ((), ())), preferred_element_type=F32)


def _rms(x, g):
    ms = jnp.mean(x * x, axis=-1, keepdims=True)
    return x * lax.rsqrt(ms + RMS_EPS) * g


def _group_rms(x, g, gmat):
    outs = []
    for k in range(x.shape[1] // LANES):
        xb = x[:, k * LANES:(k + 1) * LANES]
        ms = _dot((xb * xb).astype(BF16), gmat)
        outs.append(xb * lax.rsqrt(ms + RMS_EPS))
    y = outs[0] if len(outs) == 1 else jnp.concatenate(outs, axis=1)
    return y * g


def _rope(x, cos, sin, half):
    lane = lax.broadcasted_iota(jnp.int32, (1, LANES), 1)
    first = (lane % (2 * half)) < half
    outs = []
    for k in range(x.shape[1] // LANES):
        xb = x[:, k * LANES:(k + 1) * LANES]
        fwd = pltpu.roll(xb, LANES - half, 1)
        bwd = pltpu.roll(xb, half, 1)
        outs.append(xb * cos + jnp.where(first, fwd, bwd) * sin)
    return outs[0] if len(outs) == 1 else jnp.concatenate(outs, axis=1)


def _silu(a):
    return a / (1.0 + jnp.exp(-a))


def _row_tile(n_tokens):
    return min(MAX_ROW_TILE, n_tokens)


def _adaln_kernel(c_ref, w_ref, b_ref, o_ref):
    a = _silu(c_ref[...]).astype(BF16)
    bias = b_ref[pl.ds(pl.program_id(0), 1), :]
    o_ref[0] = _dot(a, w_ref[0].astype(BF16)) + bias


def _adaln(cvec, w_mod, b_mod):
    rows = cvec.shape[0]
    nc = 4
    wc = 6 * D_MODEL // nc
    return pl.pallas_call(
        _adaln_kernel,
        grid=(DEPTH, nc),
        in_specs=[_full((rows, D_MODEL)),
                  pl.BlockSpec((1, D_MODEL, wc), lambda l, n: (l, 0, n)),
                  pl.BlockSpec((DEPTH, wc), lambda l, n: (0, n))],
        out_specs=pl.BlockSpec((1, rows, wc), lambda l, n: (l, 0, n)),
        out_shape=jax.ShapeDtypeStruct((DEPTH, rows, 6 * D_MODEL), F32),
        compiler_params=_params("parallel", "parallel"),
        name="adaln",
    )(cvec, w_mod, b_mod)


def _modulated(x_ref, mod_ref, g_ref, shift_row, scale_row):
    mod = mod_ref[0]
    return (_rms(x_ref[...], g_ref[...]) * (1.0 + mod[scale_row:scale_row + 1])
            + mod[shift_row:shift_row + 1])


def _in_even_kernel(latent, *refs):
    (x_ref, mod_ref, g1_ref, wt_ref, gqa_ref, gkva_ref, wuq_ref, wukv_ref,
     gqn_ref, gqr_ref, gkn_ref, gkr_ref, gdq_ref, gdk_ref, g64_ref, g32_ref) = refs[:16]
    refs = refs[16:]
    if latent:
        c64_ref, s64_ref, c32_ref, s32_ref = refs[:4]
        refs = refs[4:]
    qn_ref, qr_ref, kn_ref, vm_ref, krt_ref, mq_ref, mk_ref, mv_ref = refs[:8]
    refs = refs[8:]
    g64 = g64_ref[...]
    g32 = g32_ref[...]
    mla_scale = (MLA_NOPE + MLA_ROPE) ** -0.5 * LOG2E
    diff_scale = DIFF_HD ** -0.5 * LOG2E

    h = _modulated(x_ref, mod_ref, g1_ref, 0, 1)
    proj = _dot_nt(h.astype(BF16), wt_ref[...])
    cq = _rms(proj[:, 0:256], gqa_ref[...])
    q = _dot(cq.astype(BF16), wuq_ref[...])
    qn = _group_rms(q[:, 0:512], gqn_ref[...], g64)
    qr = _group_rms(q[:, 512:768], gqr_ref[...], g32)
    ckv = _rms(proj[:, 256:384], gkva_ref[...])
    kv = _dot(ckv.astype(BF16), wukv_ref[...])
    kn = _group_rms(kv[:, 0:512], gkn_ref[...], g64)
    mq = _group_rms(proj[:, 384:896], gdq_ref[...], g64)
    mk = _group_rms(proj[:, 896:1408], gdk_ref[...], g64)
    mv = proj[:, 1408:1920]
    kr = _group_rms(proj[:, 1920:2048], gkr_ref[...], g32)
    if latent:
        c64, s64, c32, s32 = c64_ref[...], s64_ref[...], c32_ref[...], s32_ref[...]
        qr = _rope(qr, c32, s32, MLA_ROPE // 4)
        kr = _rope(kr, c32, s32, MLA_ROPE // 4)
        mq = _rope(mq, c64, s64, DIFF_HD // 4)
        mk = _rope(mk, c64, s64, DIFF_HD // 4)
    else:
        ckv_s_ref, krt_s_ref, mk_s_ref, mv_s_ref = refs
        ckv_s_ref[...] = ckv
        krt_s_ref[...] = kr.T[0:MLA_ROPE]
        mk_s_ref[...] = mk
        mv_s_ref[...] = mv
    qn_ref[...] = (qn * mla_scale).astype(BF16)
    qr_ref[...] = (qr * mla_scale).astype(BF16)
    kn_ref[...] = kn.astype(BF16)
    vm_ref[...] = kv[:, 512:1024].astype(BF16)
    krt_ref[...] = kr.astype(BF16)
    mq_ref[...] = (mq * diff_scale).astype(BF16)
    mk_ref[...] = mk.astype(BF16)
    mv_ref[...] = mv.astype(BF16)


def _in_odd_kernel(latent, *refs):
    x_ref, mod_ref, g1_ref, w_ref, gqc_ref, gkc_ref, gqd_ref, gkd_ref, g64_ref = refs[:9]
    refs = refs[9:]
    if latent:
        c64_ref, s64_ref = refs[:2]
        refs = refs[2:]
    qc_ref, qd_ref, kct_ref, vct_ref, kdt_ref, vdt_ref = refs
    g64 = g64_ref[...]
    scale = HEAD_DIM ** -0.5 * LOG2E

    h = _modulated(x_ref, mod_ref, g1_ref, 0, 1)
    proj = _dot(h.astype(BF16), w_ref[...])
    qc = _group_rms(proj[:, 0:512], gqc_ref[...], g64)
    kc = _group_rms(proj[:, 512:640], gkc_ref[...], g64)
    qd = _group_rms(proj[:, 768:1280], gqd_ref[...], g64)
    kd = _group_rms(proj[:, 1280:1408], gkd_ref[...], g64)
    if latent:
        c64, s64 = c64_ref[...], s64_ref[...]
        qc = _rope(qc, c64, s64, HEAD_DIM // 4)
        kc = _rope(kc, c64, s64, HEAD_DIM // 4)
        qd = _rope(qd, c64, s64, HEAD_DIM // 4)
        kd = _rope(kd, c64, s64, HEAD_DIM // 4)
    qc_ref[...] = (qc * scale).astype(BF16)
    qd_ref[...] = (qd * scale).astype(BF16)
    kct_ref[...] = kc.T
    vct_ref[...] = proj[:, 640:768].T
    kdt_ref[...] = kd.T
    vdt_ref[...] = proj[:, 1408:1536].T


def _row_specs(latent, n_tokens):
    tm = _row_tile(n_tokens)
    tiles_per_batch = n_tokens // tm
    if latent:
        mod_map = lambda i: (1 + i // tiles_per_batch, 0, 0)
    else:
        mod_map = lambda i: (0, 0, 0)
    mod_spec = pl.BlockSpec((1, 6, D_MODEL), mod_map)
    table_spec = pl.BlockSpec((tm, LANES), lambda i: (i % tiles_per_batch, 0))
    rows = lambda width: pl.BlockSpec((tm, width), lambda i: (i, 0))
    cols = lambda feats: pl.BlockSpec((None, feats, tm), lambda i: (i // tiles_per_batch, 0, i % tiles_per_batch))
    return tm, mod_spec, table_spec, rows, cols


def _in_even(latent, batch, n_tokens, x, mods, g1, wt, wuq, wukv, gains, gmats, tables):
    n_rows = x.shape[0]
    tm, mod_spec, table_spec, rows, cols = _row_specs(latent, n_tokens)
    vec_specs = [_full(g.shape) for g in gains]
    in_specs = ([rows(D_MODEL), mod_spec, _full(g1.shape), _full(wt.shape)]
                + vec_specs[:2] + [_full(wuq.shape), _full(wukv.shape)] + vec_specs[2:]
                + [_full((LANES, LANES))] * 2)
    args = [x, mods, g1, wt, gains[0], gains[1], wuq, wukv] + list(gains[2:]) + list(gmats)
    widths = [512, 256, 512, 512, 128, 512, 512, 512]
    out_shape = [jax.ShapeDtypeStruct((n_rows, wd), BF16) for wd in widths]
    out_specs = [rows(wd) for wd in widths]
    if latent:
        in_specs += [table_spec] * 4
        args += list(tables)
    else:
        out_shape += [jax.ShapeDtypeStruct((n_rows, MLA_KV_LORA), F32),
                      jax.ShapeDtypeStruct((batch, MLA_ROPE, n_tokens), F32),
                      jax.ShapeDtypeStruct((n_rows, 512), F32),
                      jax.ShapeDtypeStruct((n_rows, 512), F32)]
        out_specs += [rows(MLA_KV_LORA), cols(MLA_ROPE), rows(512), rows(512)]
    return pl.pallas_call(
        functools.partial(_in_even_kernel, latent),
        grid=(n_rows // tm,),
        in_specs=in_specs, out_specs=out_specs, out_shape=out_shape,
        compiler_params=_params("parallel"),
        name="in_even_lat" if latent else "in_even_ctx",
    )(*args)


def _in_odd(latent, batch, n_tokens, x, mods, g1, w, gains, g64, tables):
    n_rows = x.shape[0]
    tm, mod_spec, table_spec, rows, cols = _row_specs(latent, n_tokens)
    in_specs = ([rows(D_MODEL), mod_spec, _full(g1.shape), _full(w.shape)]
                + [_full(g.shape) for g in gains] + [_full((LANES, LANES))])
    args = [x, mods, g1, w] + list(gains) + [g64]
    if latent:
        in_specs += [table_spec] * 2
        args += list(tables)
    kv_shape = jax.ShapeDtypeStruct((batch, LANES, n_tokens), F32)
    return pl.pallas_call(
        functools.partial(_in_odd_kernel, latent),
        grid=(n_rows // tm,),
        in_specs=in_specs,
        out_specs=[rows(512), rows(512)] + [cols(LANES)] * 4,
        out_shape=[jax.ShapeDtypeStruct((n_rows, 512), BF16)] * 2 + [kv_shape] * 4,
        compiler_params=_params("parallel"),
        name="in_odd_lat" if latent else "in_odd_ctx",
    )(*args)


def _cache_kv_kernel(ckv_ref, wukv_ref, gkn_ref, g64_ref, kn_ref, vm_ref):
    kv = _dot(ckv_ref[...].astype(BF16), wukv_ref[...])
    kn_ref[...] = _group_rms(kv[:, 0:512], gkn_ref[...], g64_ref[...]).astype(BF16)
    vm_ref[...] = kv[:, 512:1024].astype(BF16)


def _cache_kv(ckv, wukv, gkn, g64):
    rows = ckv.shape[0]
    tile = 512
    spec = lambda wd: pl.BlockSpec((tile, wd), lambda i: (i, 0))
    return pl.pallas_call(
        _cache_kv_kernel,
        grid=(rows // tile,),
        in_specs=[spec(MLA_KV_LORA), _full(wukv.shape), _full(gkn.shape), _full((LANES, LANES))],
        out_specs=[spec(512), spec(512)],
        out_shape=[jax.ShapeDtypeStruct((rows, 512), BF16)] * 2,
        compiler_params=_params("parallel"),
        name="cache_kv",
    )(ckv, wukv, gkn, g64)


def _softmax_pv(s_parts, v_parts, v_is_feature_major, sink=None):
    m = None
    for s in s_parts:
        mi = jnp.max(s, axis=-1, keepdims=True)
        m = mi if m is None else jnp.maximum(m, mi)
    if sink is not None:
        m = jnp.maximum(m, sink)
    acc = None
    for s, v in zip(s_parts, v_parts):
        p = jnp.exp2(s - m).astype(BF16)
        oi = _dot_nt(p, v) if v_is_feature_major else _dot(p, v)
        acc = oi if acc is None else acc + oi
    denom = acc[:, LANES:]
    if sink is not None:
        denom = denom + jnp.exp2(sink - m)
    return acc[:, :LANES] / denom


def _lane_masks(width):
    lane = lax.broadcasted_iota(jnp.int32, (1, LANES), 1)
    return [jnp.where(lane // width == k, 1.0, 0.0).astype(BF16) for k in range(LANES // width)]


def _attn_even_kernel(latent, lam_init, *refs):
    qn_ref, qr_ref, mq_ref = refs[:3]
    refs = refs[3:]
    n_parts = 2 if latent else 1
    parts = [refs[5 * k:5 * k + 5] for k in range(n_parts)]
    lamv_ref, gsub_ref, o_ref = refs[5 * n_parts:]
    lane = lax.broadcasted_iota(jnp.int32, (1, LANES), 1)
    low = lane < 64
    m64 = _lane_masks(64)
    m32 = _lane_masks(32)
    ones = [jnp.ones((p[0].shape[0], LANES), BF16) for p in parts]

    for i in range(MLA_HEADS // 2):
        cols = slice(i * LANES, (i + 1) * LANES)
        qn_b = qn_ref[:, cols]
        kcat = [jnp.concatenate([p[0][:, cols], p[2][...]], axis=1) for p in parts]
        vms = [jnp.concatenate([p[1][:, cols], one], axis=1) for p, one in zip(parts, ones)]
        outs = []
        for half in range(2):
            head = 2 * i + half
            rb = head // 4
            qr_b = qr_ref[:, rb * LANES:(rb + 1) * LANES]
            lhs = jnp.concatenate([qn_b * m64[half], qr_b * m32[head % 4]], axis=1)
            outs.append(_softmax_pv([_dot_nt(lhs, kc) for kc in kcat], vms, False))
        o_ref[:, cols] = jnp.where(low, outs[0], outs[1]).astype(BF16)

    lv = lamv_ref[...]
    lam = (jnp.exp(jnp.sum(lv[0:1] * lv[1:2], axis=-1, keepdims=True))
           - jnp.exp(jnp.sum(lv[2:3] * lv[3:4], axis=-1, keepdims=True)) + lam_init)
    gsub = gsub_ref[...]
    for hd in range(DIFF_HEADS):
        cols = slice(hd * LANES, (hd + 1) * LANES)
        q_b = mq_ref[:, cols]
        ks = [p[3][:, cols] for p in parts]
        vs = [jnp.concatenate([p[4][:, cols], one], axis=1) for p, one in zip(parts, ones)]
        a1 = _softmax_pv([_dot_nt(q_b * m64[0], k) for k in ks], vs, False)
        a2 = _softmax_pv([_dot_nt(q_b * m64[1], k) for k in ks], vs, False)
        d = _rms(a1 - lam * a2, gsub) * (1.0 - lam_init)
        o_ref[:, 512 + hd * LANES:512 + (hd + 1) * LANES] = d.astype(BF16)


def _attn_even(latent, batch, n_tokens, lam_init, q_arrs, new_arrs, cache_arrs, lamv, gsub):
    tq = min(Q_TILE, n_tokens)
    nt = n_tokens // tq
    qspec = lambda wd: pl.BlockSpec((tq, wd), lambda b, t: (b * nt + t, 0))
    kspec = lambda rows, wd: pl.BlockSpec((rows, wd), lambda b, t: (b, 0))
    kwidths = [512, 512, 128, 512, 512]
    in_specs = [qspec(512), qspec(256), qspec(512)]
    args = list(q_arrs)
    if latent:
        past = cache_arrs[0].shape[0] // batch
        in_specs += [kspec(past, wd) for wd in kwidths]
        args += list(cache_arrs)
    in_specs += [kspec(n_tokens, wd) for wd in kwidths]
    args += list(new_arrs)
    in_specs += [_full(lamv.shape), _full(gsub.shape)]
    args += [lamv, gsub]
    return pl.pallas_call(
        functools.partial(_attn_even_kernel, latent, lam_init),
        grid=(batch, nt),
        in_specs=in_specs,
        out_specs=qspec(D_MODEL),
        out_shape=jax.ShapeDtypeStruct((batch * n_tokens, D_MODEL), BF16),
        compiler_params=_params("parallel", "parallel"),
        name="attn_even_lat" if latent else "attn_even_ctx",
    )(*args)


def _attn_odd_kernel(latent, tq, *refs):
    qc_ref, qd_ref = refs[:2]
    refs = refs[2:]
    n_parts = 2 if latent else 1
    parts = [refs[4 * k:4 * k + 4] for k in range(n_parts)]
    sink_ref, o_ref = refs[4 * n_parts:]
    lane = lax.broadcasted_iota(jnp.int32, (1, LANES), 1)
    low = lane < 64
    m64 = _lane_masks(64)
    n_new = parts[-1][0].shape[1]
    span = min(n_new, tq + 2 * WINDOW)

    band = None
    start = 0
    if latent:
        t = pl.program_id(1)
        start = pl.multiple_of(jnp.clip(t * tq - WINDOW, 0, n_new - span), LANES)
        qpos = t * tq + lax.broadcasted_iota(jnp.int32, (tq, 1), 0)
        kpos = start + lax.broadcasted_iota(jnp.int32, (1, span), 1)
        band = jnp.abs(qpos - kpos) <= WINDOW

    for kind in range(2):
        q_ref = qc_ref if kind == 0 else qd_ref
        windowed = latent and kind == 0
        for g in range(2):
            rows = slice(g * HEAD_DIM, (g + 1) * HEAD_DIM)
            ks, vs = [], []
            for idx, p in enumerate(parts):
                k_ref, v_ref = p[2 * kind], p[2 * kind + 1]
                if windowed and idx == n_parts - 1 and span < n_new:
                    k = k_ref[rows, pl.ds(start, span)]
                    v = v_ref[rows, pl.ds(start, span)]
                else:
                    k = k_ref[rows, :]
                    v = v_ref[rows, :]
                ks.append(jnp.concatenate([k, k], axis=0).astype(BF16))
                vs.append(jnp.concatenate([v, v, jnp.ones((LANES, v.shape[1]), F32)], axis=0).astype(BF16))
            for i in (2 * g, 2 * g + 1):
                cols = slice(i * LANES, (i + 1) * LANES)
                q_b = q_ref[:, cols]
                outs = []
                for half in range(2):
                    s_parts = [_dot(q_b * m64[half], k) for k in ks]
                    sink = None
                    if kind == 0:
                        sink = sink_ref[2 * i + half] * LOG2E
                        if latent:
                            s_parts[-1] = jnp.where(band, s_parts[-1], NEG_BIG)
                    outs.append(_softmax_pv(s_parts, vs, True, sink))
                o_ref[:, kind * 512 + i * LANES:kind * 512 + (i + 1) * LANES] = (
                    jnp.where(low, outs[0], outs[1]).astype(BF16))


def _attn_odd(latent, batch, n_tokens, layer_j, q_arrs, new_arrs, cache_arrs, sink):
    tq = min(Q_TILE, n_tokens)
    nt = n_tokens // tq
    qspec = lambda wd: pl.BlockSpec((tq, wd), lambda b, t: (b * nt + t, 0))
    in_specs = [qspec(512), qspec(512)]
    args = list(q_arrs)
    if latent:
        past = cache_arrs[0].shape[-1]
        in_specs += [pl.BlockSpec((None, None, LANES, past), lambda b, t: (b, layer_j, 0, 0))] * 4
        args += list(cache_arrs)
    in_specs += [pl.BlockSpec((None, LANES, n_tokens), lambda b, t: (b, 0, 0))] * 4
    args += list(new_arrs)
    in_specs += [pl.BlockSpec(memory_space=pltpu.SMEM)]
    args += [sink]
    return pl.pallas_call(
        functools.partial(_attn_odd_kernel, latent, tq),
        grid=(batch, nt),
        in_specs=in_specs,
        out_specs=qspec(D_MODEL),
        out_shape=jax.ShapeDtypeStruct((batch * n_tokens, D_MODEL), BF16),
        compiler_params=_params("parallel", "parallel"),
        name="attn_odd_lat" if latent else "attn_odd_ctx",
    )(*args)


def _post_kernel(o_ref, wout_ref, x_ref, mod_ref, g2_ref, wrh_ref, wrl_ref, x1_ref, h2_ref, afft_ref):
    mod = mod_ref[0]
    x1 = x_ref[...] + mod[2:3] * _dot(o_ref[...], wout_ref[...])
    x1_ref[...] = x1
    h2 = _rms(x1, g2_ref[...]) * (1.0 + mod[4:5]) + mod[3:4]
    h_hi = h2.astype(BF16)
    h_lo = (h2 - h_hi.astype(F32)).astype(BF16)
    h2_ref[...] = h_hi
    wrh = wrh_ref[...]
    logits = _dot(h_hi, wrh) + _dot(h_lo, wrh) + _dot(h_hi, wrl_ref[...])
    lane = lax.broadcasted_iota(jnp.int32, (1, LANES), 1)
    logits = jnp.where(lane < N_EXPERTS, logits, NEG_BIG)
    e = jnp.exp(logits - jnp.max(logits, axis=-1, keepdims=True))
    aff = e / jnp.sum(e, axis=-1, keepdims=True)
    afft_ref[...] = aff.T[0:N_EXPERTS]


def _post(latent, batch, n_tokens, o, wout, x, mods, g2, wrh, wrl):
    n_rows = x.shape[0]
    tm, mod_spec, _, rows, cols = _row_specs(latent, n_tokens)
    return pl.pallas_call(
        _post_kernel,
        grid=(n_rows // tm,),
        in_specs=[rows(D_MODEL), _full(wout.shape), rows(D_MODEL), mod_spec, _full(g2.shape),
                  _full(wrh.shape), _full(wrl.shape)],
        out_specs=[rows(D_MODEL), rows(D_MODEL), cols(N_EXPERTS)],
        out_shape=[jax.ShapeDtypeStruct((n_rows, D_MODEL), F32),
                   jax.ShapeDtypeStruct((n_rows, D_MODEL), BF16),
                   jax.ShapeDtypeStruct((batch, N_EXPERTS, n_tokens), F32)],
        compiler_params=_params("parallel"),
        name="post_lat" if latent else "post_ctx",
    )(o, wout, x, mods, g2, wrh, wrl)


def _route_kernel(cap, aff_ref, out_ref):
    aff = aff_ref[...]
    n_rows, n = aff.shape
    bits = lax.bitcast_convert_type(aff, jnp.int32)

    def body(_, carry):
        lo, hi = carry
        mid = lo + ((hi - lo + 1) >> 1)
        cnt = jnp.sum(jnp.where(bits >= mid, 1.0, 0.0), axis=-1, keepdims=True)
        ok = cnt >= cap
        return jnp.where(ok, mid, lo), jnp.where(ok, hi, mid - 1)

    lo0 = jnp.zeros((n_rows, 1), jnp.int32)
    hi0 = jnp.full((n_rows, 1), 0x7F800000, jnp.int32)
    thr, _ = lax.fori_loop(0, 31, body, (lo0, hi0))
    above = jnp.where(bits > thr, 1.0, 0.0)
    equal = jnp.where(bits == thr, 1.0, 0.0)
    room = cap - jnp.sum(above, axis=-1, keepdims=True)
    before = jnp.where(lax.broadcasted_iota(jnp.int32, (n, n), 0)
                       < lax.broadcasted_iota(jnp.int32, (n, n), 1), 1.0, 0.0).astype(BF16)
    equal_before = _dot(equal.astype(BF16), before)
    chosen = above + equal * jnp.where(equal_before < room, 1.0, 0.0)
    slot = _dot(chosen.astype(BF16), before)
    out_ref[...] = jnp.where(chosen > 0.5, slot, -1.0)


def _route(cap, aff_rows):
    return pl.pallas_call(
        functools.partial(_route_kernel, cap),
        grid=(1,),
        in_specs=[_full(aff_rows.shape)],
        out_specs=_full(aff_rows.shape),
        out_shape=jax.ShapeDtypeStruct(aff_rows.shape, F32),
        compiler_params=_params("arbitrary"),
        name="route",
    )(aff_rows)


def _gather_kernel(cap, slot_ref, aff_ref, h_ref, xs_ref, gate_ref):
    slots = slot_ref[...]
    aff = aff_ref[...]
    want = lax.broadcasted_iota(jnp.int32, (cap, 1), 0).astype(F32)
    rows = []
    for e in range(N_EXPERTS):
        hit = slots[e:e + 1, :] == want
        rows.append(jnp.where(hit, 1.0, 0.0).astype(BF16))
        gate_ref[e] = jnp.sum(jnp.where(hit, aff[e:e + 1, :], 0.0), axis=-1, keepdims=True)
    picked = _dot(jnp.concatenate(rows, axis=0), h_ref[...]).astype(BF16)
    for e in range(N_EXPERTS):
        xs_ref[e] = picked[e * cap:(e + 1) * cap]


def _gather(cap, batch, n_tokens, slots, aff_rows, h2):
    return pl.pallas_call(
        functools.partial(_gather_kernel, cap),
        grid=(batch,),
        in_specs=[pl.BlockSpec((N_EXPERTS, n_tokens), lambda b: (b, 0)),
                  pl.BlockSpec((N_EXPERTS, n_tokens), lambda b: (b, 0)),
                  pl.BlockSpec((n_tokens, D_MODEL), lambda b: (b, 0))],
        out_specs=[pl.BlockSpec((N_EXPERTS, cap, D_MODEL), lambda b: (0, b, 0)),
                   pl.BlockSpec((N_EXPERTS, cap, 1), lambda b: (0, b, 0))],
        out_shape=[jax.ShapeDtypeStruct((N_EXPERTS, batch * cap, D_MODEL), BF16),
                   jax.ShapeDtypeStruct((N_EXPERTS, batch * cap, 1), F32)],
        compiler_params=_params("parallel"),
        name="gather",
    )(slots, aff_rows, h2)


def _ffn_kernel(n_groups, *refs):
    xs_refs = refs[:n_groups]
    gate_refs = refs[n_groups:2 * n_groups]
    w1_ref, w3_ref, w2_ref = refs[2 * n_groups:2 * n_groups + 3]
    ys_refs = refs[2 * n_groups + 3:3 * n_groups + 3]
    acc_refs = refs[3 * n_groups + 3:]
    f = pl.program_id(1)
    w1 = w1_ref[0, 0].astype(BF16)
    w3 = w3_ref[0, 0].astype(BF16)
    w2 = w2_ref[0, 0].astype(BF16)
    for xs_ref, gate_ref, ys_ref, acc_ref in zip(xs_refs, gate_refs, ys_refs, acc_refs):
        xs = xs_ref[0]
        hid = (_silu(_dot(xs, w1)) * _dot(xs, w3)).astype(BF16)
        part = _dot(hid, w2)

        @pl.when(f == 0)
        def _():
            acc_ref[...] = part

        @pl.when(f > 0)
        def _():
            acc_ref[...] += part

        @pl.when(f == pl.num_programs(1) - 1)
        def _():
            ys_ref[0] = (acc_ref[...] * gate_ref[0]).astype(BF16)


def _ffn(layer, xs_list, gate_list, w1, w3, w2):
    n_groups = len(xs_list)
    n_chunks = EXPERT_FF // FF_CHUNK
    xs_specs = [pl.BlockSpec((1,) + xs.shape[1:], lambda e, f: (e, 0, 0)) for xs in xs_list]
    gate_specs = [pl.BlockSpec((1,) + g.shape[1:], lambda e, f: (e, 0, 0)) for g in gate_list]
    return pl.pallas_call(
        functools.partial(_ffn_kernel, n_groups),
        grid=(N_EXPERTS, n_chunks),
        in_specs=xs_specs + gate_specs + [
            pl.BlockSpec((1, 1, D_MODEL, FF_CHUNK), lambda e, f: (layer, e, 0, f)),
            pl.BlockSpec((1, 1, D_MODEL, FF_CHUNK), lambda e, f: (layer, e, 0, f)),
            pl.BlockSpec((1, 1, FF_CHUNK, D_MODEL), lambda e, f: (layer, e, f, 0))],
        out_specs=xs_specs,
        out_shape=[jax.ShapeDtypeStruct(xs.shape, BF16) for xs in xs_list],
        scratch_shapes=[pltpu.VMEM(xs.shape[1:], F32) for xs in xs_list],
        compiler_params=_params("parallel", "arbitrary"),
        name="ffn",
    )(*xs_list, *gate_list, w1, w3, w2)


def _combine_kernel(cap, slot_ref, ys_ref, x_ref, mod_ref, o_ref):
    slots = slot_ref[...]
    n = slots.shape[1]
    pad = jnp.zeros((LANES - N_EXPERTS, n), F32)
    slots_t = jnp.concatenate([slots, pad], axis=0).T
    lane = lax.broadcasted_iota(jnp.int32, (1, LANES), 1).astype(F32)
    per_block = LANES // cap
    blocks = []
    for k in range(N_EXPERTS // per_block):
        hit = None
        for e in range(k * per_block, (k + 1) * per_block):
            col = slots_t[:, e:e + 1]
            target = jnp.where(col >= 0.0, col + float((e - k * per_block) * cap), -1.0)
            he = jnp.where(target == lane, 1.0, 0.0)
            hit = he if hit is None else hit + he
        blocks.append(hit.astype(BF16))
    onehot = jnp.concatenate(blocks, axis=1)
    ys = jnp.concatenate([ys_ref[e] for e in range(N_EXPERTS)], axis=0)
    o_ref[...] = x_ref[...] + mod_ref[0][5:6] * _dot(onehot, ys)


def _combine(latent, cap, batch, n_tokens, slots, ys, x1, mods):
    mod_map = (lambda b: (1 + b, 0, 0)) if latent else (lambda b: (0, 0, 0))
    return pl.pallas_call(
        functools.partial(_combine_kernel, cap),
        grid=(batch,),
        in_specs=[pl.BlockSpec((N_EXPERTS, n_tokens), lambda b: (b, 0)),
                  pl.BlockSpec((N_EXPERTS, cap, D_MODEL), lambda b: (0, b, 0)),
                  pl.BlockSpec((n_tokens, D_MODEL), lambda b: (b, 0)),
                  pl.BlockSpec((1, 6, D_MODEL), mod_map)],
        out_specs=pl.BlockSpec((n_tokens, D_MODEL), lambda b: (b, 0)),
        out_shape=jax.ShapeDtypeStruct(x1.shape, F32),
        compiler_params=_params("parallel"),
        name="combine",
    )(slots, ys, x1, mods)


def _rope_tables(n_pos, rot_dim):
    t = jnp.arange(n_pos)
    row = t // GRID_W
    col = t % GRID_W
    nf = rot_dim // 4
    inv = ROPE_THETA ** (-jnp.arange(nf, dtype=F32) / nf)
    ang_r = row[:, None] * inv
    ang_c = col[:, None] * inv
    cr, sr, cc, sc = jnp.cos(ang_r), jnp.sin(ang_r), jnp.cos(ang_c), jnp.sin(ang_c)
    cos = jnp.concatenate([cr, cr, cc, cc], axis=1)
    sin = jnp.concatenate([-sr, sr, -sc, sc], axis=1)
    reps = LANES // rot_dim
    return jnp.tile(cos, (1, reps)), jnp.tile(sin, (1, reps))


def _group_mean_matrix(width):
    idx = jnp.arange(LANES) // width
    return jnp.where(idx[:, None] == idx[None, :], 1.0 / width, 0.0).astype(BF16)


def _tile_row(g, reps):
    return jnp.tile(g, reps).reshape(1, -1)


def _split_heads(w, n_heads, first):
    k = w.shape[0]
    w3 = w.reshape(k, n_heads, -1)
    return jnp.concatenate([w3[:, :, :first].reshape(k, -1), w3[:, :, first:].reshape(k, -1)], axis=1)


def _feature_major(cache):
    b, l, t, h, d = cache.shape
    return jnp.transpose(cache, (0, 1, 3, 4, 2)).reshape(b, l, h * d, t)


def kernel(x_prompt, x_sample, cache_mla_ckv, cache_mla_krope, cache_diff_k, cache_diff_v,
           cache_win_k, cache_win_v, cache_axial_k, cache_axial_v, c, c_ctx,
           g_norm, w_mod, b_mod, w_in_even, w_out_even, mla_g_qa, mla_g_kva, mla_w_uq, mla_w_ukv,
           mla_g_q, mla_g_k, diff_g_q, diff_g_k, diff_lambda, diff_g_sub,
           w_in_odd, w_out_odd, odd_g_qk, win_sink, moe_w_router, moe_w1, moe_w3, moe_w2):
    n_ctx_b, n_ctx = x_prompt.shape[:2]
    n_lat_b, n_lat = x_sample.shape[:2]
    passes = [
        (False, n_ctx_b, n_ctx, EC_CAPACITY_FACTOR * n_ctx // N_EXPERTS),
        (True, n_lat_b, n_lat, EC_CAPACITY_FACTOR * n_lat // N_EXPERTS),
    ]

    cvec = jnp.zeros((16, D_MODEL), F32).at[0].set(c_ctx).at[1:1 + n_lat_b].set(c)
    mods = _adaln(cvec, w_mod, b_mod).reshape(DEPTH, 16, 6, D_MODEL)

    g64 = _group_mean_matrix(64)
    g32 = _group_mean_matrix(32)
    tab64 = _rope_tables(n_lat, HEAD_DIM)
    tab32 = _rope_tables(n_lat, MLA_ROPE)
    odd_caches = [_feature_major(a) for a in (cache_win_k, cache_win_v, cache_axial_k, cache_axial_v)]

    xs_state = [x_prompt.reshape(-1, D_MODEL), x_sample.reshape(-1, D_MODEL)]
    even_states, odd_states = [], []

    for layer in range(DEPTH):
        j = layer // 2
        mods_l = mods[layer]
        g1 = g_norm[layer, 0].reshape(1, -1)
        g2 = g_norm[layer, 1].reshape(1, -1)
        attn_out = []
        if layer % 2 == 0:
            wt = jnp.swapaxes(w_in_even[j], 0, 1)
            wt = jnp.concatenate([wt[:384], wt[416:]] + [wt[384:416]] * 4, axis=0).astype(BF16)
            wuq = _split_heads(mla_w_uq[j], MLA_HEADS, MLA_NOPE).astype(BF16)
            wukv = _split_heads(mla_w_ukv[j], MLA_HEADS, MLA_NOPE).astype(BF16)
            w_out = w_out_even[j].astype(BF16)
            gkn = _tile_row(mla_g_k[j, :MLA_NOPE], MLA_HEADS)
            gains = [mla_g_qa[j].reshape(1, -1), mla_g_kva[j].reshape(1, -1),
                     _tile_row(mla_g_q[j, :MLA_NOPE], MLA_HEADS), _tile_row(mla_g_q[j, MLA_NOPE:], MLA_HEADS),
                     gkn, _tile_row(mla_g_k[j, MLA_NOPE:], 4),
                     _tile_row(diff_g_q[j], 2 * DIFF_HEADS), _tile_row(diff_g_k[j], 2 * DIFF_HEADS)]
            lam_init = 0.8 - 0.6 * math.exp(-0.3 * layer)
            gsub = diff_g_sub[j].reshape(1, -1)
            for p, (latent, batch, n_tok, cap) in enumerate(passes):
                outs = _in_even(latent, batch, n_tok, xs_state[p], mods_l, g1, wt, wuq, wukv, gains,
                                (g64, g32), tab64 + tab32)
                qn, qr, kn, vm, krt, mq, mk, mv = outs[:8]
                cache_arrs = None
                if latent:
                    kn_c, vm_c = _cache_kv(cache_mla_ckv[:, j].reshape(-1, MLA_KV_LORA), wukv, gkn, g64)
                    cache_arrs = (kn_c, vm_c,
                                  jnp.tile(cache_mla_krope[:, j].reshape(-1, MLA_ROPE), (1, 4)).astype(BF16),
                                  cache_diff_k[:, j].reshape(-1, 512).astype(BF16),
                                  cache_diff_v[:, j].reshape(-1, 512).astype(BF16))
                else:
                    even_states.append(outs[8:])
                attn_out.append(_attn_even(latent, batch, n_tok, lam_init, (qn, qr, mq),
                                           (kn, vm, krt, mk, mv), cache_arrs, diff_lambda[j], gsub))
        else:
            w_in = w_in_odd[j].astype(BF16)
            w_out = w_out_odd[j].astype(BF16)
            gains = [_tile_row(odd_g_qk[j, 0], 8), _tile_row(odd_g_qk[j, 1], 2),
                     _tile_row(odd_g_qk[j, 2], 8), _tile_row(odd_g_qk[j, 3], 2)]
            for p, (latent, batch, n_tok, cap) in enumerate(passes):
                qc, qd, kct, vct, kdt, vdt = _in_odd(latent, batch, n_tok, xs_state[p], mods_l, g1, w_in,
                                                     gains, g64, tab64)
                if not latent:
                    odd_states.append((kct, vct, kdt, vdt))
                attn_out.append(_attn_odd(latent, batch, n_tok, j, (qc, qd), (kct, vct, kdt, vdt),
                                          odd_caches if latent else None, win_sink[j]))

        w_r = jnp.pad(moe_w_router[layer], ((0, 0), (0, LANES - N_EXPERTS)))
        wrh = w_r.astype(BF16)
        wrl = (w_r - wrh.astype(F32)).astype(BF16)
        x1s, slot_list, xs_list, gate_list = [], [], [], []
        for p, (latent, batch, n_tok, cap) in enumerate(passes):
            x1, h2, afft = _post(latent, batch, n_tok, attn_out[p], w_out, xs_state[p], mods_l, g2, wrh, wrl)
            aff_rows = afft.reshape(batch * N_EXPERTS, n_tok)
            slots = _route(cap, aff_rows)
            xs, gates = _gather(cap, batch, n_tok, slots, aff_rows, h2)
            x1s.append(x1)
            slot_list.append(slots)
            xs_list.append(xs)
            gate_list.append(gates)
        ys_list = _ffn(layer, xs_list, gate_list, moe_w1, moe_w3, moe_w2)
        for p, (latent, batch, n_tok, cap) in enumerate(passes):
            xs_state[p] = _combine(latent, cap, batch, n_tok, slot_list[p], ys_list[p], x1s[p], mods_l)

    y_prompt = xs_state[0].reshape(x_prompt.shape)
    y_sample = xs_state[1].reshape(x_sample.shape)

    def stack_rows(idx, tail):
        return jnp.stack([s[idx].reshape((n_ctx_b, n_ctx) + tail) for s in even_states], axis=1)

    def stack_feature_major(states, idx, heads):
        arr = jnp.stack([s[idx] for s in states], axis=1)
        b, l, f, t = arr.shape
        return jnp.transpose(arr.reshape(b, l, heads, f // heads, t), (0, 1, 4, 2, 3))

    return (y_prompt, y_sample,
            stack_rows(0, (MLA_KV_LORA,)),
            stack_feature_major(even_states, 1, 1)[:, :, :, 0, :],
            stack_rows(2, (DIFF_HEADS, 2 * DIFF_HD)),
            stack_rows(3, (DIFF_HEADS, 2 * DIFF_HD)),
            stack_feature_major(odd_states, 0, 2),
            stack_feature_major(odd_states, 1, 2),
            stack_feature_major(odd_states, 2, 2),
            stack_feature_major(odd_states, 3, 2))
```

```python
import functools
import math

import jax
import jax.numpy as jnp
from jax import lax
from jax.experimental import pallas as pl
from jax.experimental.pallas import tpu as pltpu

F32 = jnp.float32
BF16 = jnp.bfloat16

D_MODEL = 1024
DEPTH = 4
GRID_W = 64
ROPE_THETA = 10000.0
WINDOW = 128
RMS_EPS = 1e-6
MLA_HEADS = 8
MLA_Q_LORA = 256
MLA_KV_LORA = 128
MLA_NOPE = 64
MLA_ROPE = 32
MLA_VD = 64
DIFF_HEADS = 4
DIFF_HD = 64
HEAD_DIM = 64
N_EXPERTS = 16
EC_CAPACITY_FACTOR = 2
EXPERT_FF = 2048

LANES = 128
ROW_TILE = 512
SUB_ROWS = 256
N_SUB = ROW_TILE // SUB_ROWS
Q_TILE = 256
FF_CHUNK = 512
VMEM_LIMIT = 50 * 1024 * 1024
NEG_BIG = -1e30
LOG2E = 1.4426950408889634


def _params(*sem):
    return pltpu.CompilerParams(dimension_semantics=sem, vmem_limit_bytes=VMEM_LIMIT)


def _full(shape):
    zeros = (0,) * len(shape)
    return pl.BlockSpec(shape, lambda *_: zeros)


def _dot(a, b):
    return jnp.dot(a, b, preferred_element_type=F32)


def _dot_nt(a, b):
    return lax.dot_general(a, b, (((1,), (1,)), ((), ())), preferred_element_type=F32)


def _rms(x, g):
    ms = jnp.mean(x * x, axis=-1, keepdims=True)
    return x * lax.rsqrt(ms + RMS_EPS) * g


def _group_rms(x, g, gmat):
    outs = []
    for k in range(x.shape[1] // LANES):
        xb = x[:, k * LANES:(k + 1) * LANES]
        ms = _dot((xb * xb).astype(BF16), gmat)
        outs.append(xb * lax.rsqrt(ms + RMS_EPS))
    y = outs[0] if len(outs) == 1 else jnp.concatenate(outs, axis=1)
    return y * g


def _rope(x, cos, sin, half):
    lane = lax.broadcasted_iota(jnp.int32, (1, LANES), 1)
    first = (lane % (2 * half)) < half
    outs = []
    for k in range(x.shape[1] // LANES):
        xb = x[:, k * LANES:(k + 1) * LANES]
        fwd = pltpu.roll(xb, LANES - half, 1)
        bwd = pltpu.roll(xb, half, 1)
        outs.append(xb * cos + jnp.where(first, fwd, bwd) * sin)
    return outs[0] if len(outs) == 1 else jnp.concatenate(outs, axis=1)


def _silu(a):
    return a / (1.0 + jnp.exp(-a))


def _sub(c):
    return slice(c * SUB_ROWS, (c + 1) * SUB_ROWS)


def _adaln_kernel(c_ref, w_ref, b_ref, o_ref):
    a = _silu(c_ref[...]).astype(BF16)
    bias = b_ref[pl.ds(pl.program_id(0), 1), :]
    o_ref[0] = _dot(a, w_ref[0].astype(BF16)) + bias


def _adaln(cvec, w_mod, b_mod):
    rows = cvec.shape[0]
    nc = 4
    wc = 6 * D_MODEL // nc
    return pl.pallas_call(
        _adaln_kernel,
        grid=(DEPTH, nc),
        in_specs=[_full((rows, D_MODEL)),
                  pl.BlockSpec((1, D_MODEL, wc), lambda l, n: (l, 0, n)),
                  pl.BlockSpec((DEPTH, wc), lambda l, n: (0, n))],
        out_specs=pl.BlockSpec((1, rows, wc), lambda l, n: (l, 0, n)),
        out_shape=jax.ShapeDtypeStruct((DEPTH, rows, 6 * D_MODEL), F32),
        compiler_params=_params("parallel", "parallel"),
        name="adaln",
    )(cvec, w_mod, b_mod)


def _modulated(x, mod, g, shift_row, scale_row):
    return _rms(x, g) * (1.0 + mod[scale_row:scale_row + 1]) + mod[shift_row:shift_row + 1]


def _in_even_kernel(latent, *refs):
    (x_ref, mod_ref, g1_ref, wt_ref, gqa_ref, gkva_ref, wuq_ref, wukv_ref,
     gqn_ref, gqr_ref, gkn_ref, gkr_ref, gdq_ref, gdk_ref, g64_ref, g32_ref) = refs[:16]
    refs = refs[16:]
    if latent:
        c64_ref, s64_ref, c32_ref, s32_ref = refs[:4]
    refs = refs[4:]
    qn_ref, qr_ref, kn_ref, vm_ref, krt_ref, mq_ref, mk_ref, mv_ref = refs[:8]
    refs = refs[8:]
    g64 = g64_ref[...]
    g32 = g32_ref[...]
    mod = mod_ref[0]
    mla_scale = (MLA_NOPE + MLA_ROPE) ** -0.5 * LOG2E
    diff_scale = DIFF_HD ** -0.5 * LOG2E

    for c in range(N_SUB):
        r = _sub(c)
        h = _modulated(x_ref[r, :], mod, g1_ref[...], 0, 1)
        proj = _dot_nt(h.astype(BF16), wt_ref[...])
        cq = _rms(proj[:, 0:256], gqa_ref[...])
        q = _dot(cq.astype(BF16), wuq_ref[...])
        qn = _group_rms(q[:, 0:512], gqn_ref[...], g64)
        qr = _group_rms(q[:, 512:768], gqr_ref[...], g32)
        ckv = _rms(proj[:, 256:384], gkva_ref[...])
        kv = _dot(ckv.astype(BF16), wukv_ref[...])
        kn = _group_rms(kv[:, 0:512], gkn_ref[...], g64)
        mq = _group_rms(proj[:, 384:896], gdq_ref[...], g64)
        mk = _group_rms(proj[:, 896:1408], gdk_ref[...], g64)
        mv = proj[:, 1408:1920]
        kr = _group_rms(proj[:, 1920:2048], gkr_ref[...], g32)
        if latent:
            c64, s64, c32, s32 = c64_ref[r, :], s64_ref[r, :], c32_ref[r, :], s32_ref[r, :]
            qr = _rope(qr, c32, s32, MLA_ROPE // 4)
            kr = _rope(kr, c32, s32, MLA_ROPE // 4)
            mq = _rope(mq, c64, s64, DIFF_HD // 4)
            mk = _rope(mk, c64, s64, DIFF_HD // 4)
        else:
            ckv_s_ref, krt_s_ref, mk_s_ref, mv_s_ref = refs
            ckv_s_ref[c] = ckv
            krt_s_ref[c] = kr.T[0:MLA_ROPE]
            for hd in range(DIFF_HEADS):
                mk_s_ref[c, pl.ds(hd, SUB_ROWS, stride=DIFF_HEADS), :] = mk[:, hd * LANES:(hd + 1) * LANES]
                mv_s_ref[c, pl.ds(hd, SUB_ROWS, stride=DIFF_HEADS), :] = mv[:, hd * LANES:(hd + 1) * LANES]
        qn_ref[r, :] = (qn * mla_scale).astype(BF16)
        qr_ref[r, :] = (qr * mla_scale).astype(BF16)
        kn_ref[r, :] = kn.astype(BF16)
        vm_ref[r, :] = kv[:, 512:1024].astype(BF16)
        krt_ref[r, :] = kr.astype(BF16)
        mq_ref[r, :] = (mq * diff_scale).astype(BF16)
        mk_ref[r, :] = mk.astype(BF16)
        mv_ref[r, :] = mv.astype(BF16)


def _in_odd_kernel(latent, *refs):
    x_ref, mod_ref, g1_ref, w_ref, gqc_ref, gkc_ref, gqd_ref, gkd_ref, g64_ref = refs[:9]
    refs = refs[9:]
    if latent:
        c64_ref, s64_ref = refs[:2]
        refs = refs[2:]
    else:
        refs = refs[4:]
    qc_ref, qd_ref, kct_ref, vct_ref, kdt_ref, vdt_ref = refs
    g64 = g64_ref[...]
    mod = mod_ref[0]
    scale = HEAD_DIM ** -0.5 * LOG2E

    for c in range(N_SUB):
        r = _sub(c)
        h = _modulated(x_ref[r, :], mod, g1_ref[...], 0, 1)
        proj = _dot(h.astype(BF16), w_ref[...])
        qc = _group_rms(proj[:, 0:512], gqc_ref[...], g64)
        kc = _group_rms(proj[:, 512:640], gkc_ref[...], g64)
        qd = _group_rms(proj[:, 768:1280], gqd_ref[...], g64)
        kd = _group_rms(proj[:, 1280:1408], gkd_ref[...], g64)
        if latent:
            c64, s64 = c64_ref[r, :], s64_ref[r, :]
            qc = _rope(qc, c64, s64, HEAD_DIM // 4)
            kc = _rope(kc, c64, s64, HEAD_DIM // 4)
            qd = _rope(qd, c64, s64, HEAD_DIM // 4)
            kd = _rope(kd, c64, s64, HEAD_DIM // 4)
        qc_ref[r, :] = (qc * scale).astype(BF16)
        qd_ref[r, :] = (qd * scale).astype(BF16)
        kct_ref[c] = kc.T
        vct_ref[c] = proj[:, 640:768].T
        kdt_ref[c] = kd.T
        vdt_ref[c] = proj[:, 1408:1536].T


def _row_specs(latent, n_tokens):
    tiles_per_batch = max(1, n_tokens // ROW_TILE)
    if latent:
        mod_map = lambda i: (1 + i // tiles_per_batch, 0, 0)
    else:
        mod_map = lambda i: (0, 0, 0)
    mod_spec = pl.BlockSpec((1, 6, D_MODEL), mod_map)
    table_spec = pl.BlockSpec((ROW_TILE, LANES), lambda i: (i % tiles_per_batch, 0))
    rows = lambda width: pl.BlockSpec((ROW_TILE, width), lambda i: (i, 0))
    chunks = lambda feats: pl.BlockSpec((N_SUB, feats, SUB_ROWS), lambda i: (i, 0, 0))
    return mod_spec, table_spec, rows, chunks


def _state_spec(layer_j, feats, cols):
    return pl.BlockSpec((N_SUB, None, feats, cols), lambda i: (i, layer_j, 0, 0))


def _in_even(latent, layer_j, x, mods, g1, wt, wuq, wukv, gains, gmats, tables, states, n_tokens):
    n_rows = x.shape[0]
    mod_spec, table_spec, rows, _ = _row_specs(latent, n_tokens)
    vec_specs = [_full(g.shape) for g in gains]
    in_specs = ([rows(D_MODEL), mod_spec, _full(g1.shape), _full(wt.shape)]
                + vec_specs[:2] + [_full(wuq.shape), _full(wukv.shape)] + vec_specs[2:]
                + [_full((LANES, LANES))] * 2)
    args = [x, mods, g1, wt, gains[0], gains[1], wuq, wukv] + list(gains[2:]) + list(gmats)
    widths = [512, 256, 512, 512, 128, 512, 512, 512]
    out_shape = [jax.ShapeDtypeStruct((n_rows, wd), BF16) for wd in widths]
    out_specs = [rows(wd) for wd in widths]
    aliases = {}
    if latent:
        in_specs += [table_spec] * 4
        args += list(tables)
    else:
        for k, st in enumerate(states):
            aliases[len(args)] = len(out_shape)
            in_specs.append(pl.BlockSpec(memory_space=pl.ANY))
            args.append(st)
            out_shape.append(jax.ShapeDtypeStruct(st.shape, st.dtype))
            out_specs.append(_state_spec(layer_j, st.shape[2], st.shape[3]))
    return pl.pallas_call(
        functools.partial(_in_even_kernel, latent),
        grid=(n_rows // ROW_TILE,),
        in_specs=in_specs, out_specs=out_specs, out_shape=out_shape,
        input_output_aliases=aliases,
        compiler_params=_params("parallel"),
        name="in_even_lat" if latent else "in_even_ctx",
    )(*args)


def _in_odd(latent, layer_j, x, mods, g1, w, gains, g64, tables, states, n_tokens):
    n_rows = x.shape[0]
    mod_spec, table_spec, rows, chunks = _row_specs(latent, n_tokens)
    in_specs = ([rows(D_MODEL), mod_spec, _full(g1.shape), _full(w.shape)]
                + [_full(g.shape) for g in gains] + [_full((LANES, LANES))])
    args = [x, mods, g1, w] + list(gains) + [g64]
    out_shape = [jax.ShapeDtypeStruct((n_rows, 512), BF16)] * 2
    out_specs = [rows(512), rows(512)]
    aliases = {}
    if latent:
        in_specs += [table_spec] * 2
        args += list(tables)
        out_shape += [jax.ShapeDtypeStruct((n_rows // SUB_ROWS, LANES, SUB_ROWS), F32)] * 4
        out_specs += [chunks(LANES)] * 4
    else:
        for st in states:
            aliases[len(args)] = len(out_shape)
            in_specs.append(pl.BlockSpec(memory_space=pl.ANY))
            args.append(st)
            out_shape.append(jax.ShapeDtypeStruct(st.shape, st.dtype))
            out_specs.append(_state_spec(layer_j, st.shape[2], st.shape[3]))
    return pl.pallas_call(
        functools.partial(_in_odd_kernel, latent),
        grid=(n_rows // ROW_TILE,),
        in_specs=in_specs, out_specs=out_specs, out_shape=out_shape,
        input_output_aliases=aliases,
        compiler_params=_params("parallel"),
        name="in_odd_lat" if latent else "in_odd_ctx",
    )(*args)


def _cache_kv_kernel(ckv_ref, wukv_ref, gkn_ref, g64_ref, kn_ref, vm_ref):
    kv = _dot(ckv_ref[...].astype(BF16), wukv_ref[...])
    kn_ref[...] = _group_rms(kv[:, 0:512], gkn_ref[...], g64_ref[...]).astype(BF16)
    vm_ref[...] = kv[:, 512:1024].astype(BF16)


def _cache_kv(ckv, wukv, gkn, g64):
    rows = ckv.shape[0]
    tile = 512
    spec = lambda wd: pl.BlockSpec((tile, wd), lambda i: (i, 0))
    return pl.pallas_call(
        _cache_kv_kernel,
        grid=(rows // tile,),
        in_specs=[spec(MLA_KV_LORA), _full(wukv.shape), _full(gkn.shape), _full((LANES, LANES))],
        out_specs=[spec(512), spec(512)],
        out_shape=[jax.ShapeDtypeStruct((rows, 512), BF16)] * 2,
        compiler_params=_params("parallel"),
        name="cache_kv",
    )(ckv, wukv, gkn, g64)


def _softmax_pv(s_parts, v_parts, v_is_feature_major, sink=None):
    m = None
    for s in s_parts:
        mi = jnp.max(s, axis=-1, keepdims=True)
        m = mi if m is None else jnp.maximum(m, mi)
    if sink is not None:
        m = jnp.maximum(m, sink)
    acc = None
    for s, v in zip(s_parts, v_parts):
        p = jnp.exp2(s - m).astype(BF16)
        oi = _dot_nt(p, v) if v_is_feature_major else _dot(p, v)
        acc = oi if acc is None else acc + oi
    denom = acc[:, LANES:]
    if sink is not None:
        denom = denom + jnp.exp2(sink - m)
    return acc[:, :LANES] / denom


def _lane_masks(width):
    lane = lax.broadcasted_iota(jnp.int32, (1, LANES), 1)
    return [jnp.where(lane // width == k, 1.0, 0.0).astype(BF16) for k in range(LANES // width)]


def _attn_even_kernel(latent, lam_init, *refs):
    qn_ref, qr_ref, mq_ref = refs[:3]
    refs = refs[3:]
    n_parts = 2 if latent else 1
    parts = [refs[5 * k:5 * k + 5] for k in range(n_parts)]
    lamv_ref, gsub_ref, o_ref = refs[5 * n_parts:]
    lane = lax.broadcasted_iota(jnp.int32, (1, LANES), 1)
    low = lane < 64
    m64 = _lane_masks(64)
    m32 = _lane_masks(32)
    ones = [jnp.ones((p[0].shape[0], LANES), BF16) for p in parts]

    def diff_head(ref, idx, hd):
        if latent and idx == 0:
            n_keys = ref.shape[0] // DIFF_HEADS
            return ref[pl.ds(hd, n_keys, stride=DIFF_HEADS), :].astype(BF16)
        return ref[:, hd * LANES:(hd + 1) * LANES]

    for i in range(MLA_HEADS // 2):
        cols = slice(i * LANES, (i + 1) * LANES)
        qn_b = qn_ref[:, cols]
        kcat = [jnp.concatenate([p[0][:, cols], p[2][...]], axis=1) for p in parts]
        vms = [jnp.concatenate([p[1][:, cols], one], axis=1) for p, one in zip(parts, ones)]
        outs = []
        for half in range(2):
            head = 2 * i + half
            rb = head // 4
            qr_b = qr_ref[:, rb * LANES:(rb + 1) * LANES]
            lhs = jnp.concatenate([qn_b * m64[half], qr_b * m32[head % 4]], axis=1)
            outs.append(_softmax_pv([_dot_nt(lhs, kc) for kc in kcat], vms, False))
        o_ref[:, cols] = jnp.where(low, outs[0], outs[1]).astype(BF16)

    lv = lamv_ref[...]
    lam = (jnp.exp(jnp.sum(lv[0:1] * lv[1:2], axis=-1, keepdims=True))
           - jnp.exp(jnp.sum(lv[2:3] * lv[3:4], axis=-1, keepdims=True)) + lam_init)
    gsub = gsub_ref[...]
    for hd in range(DIFF_HEADS):
        q_b = mq_ref[:, hd * LANES:(hd + 1) * LANES]
        ks = [diff_head(p[3], idx, hd) for idx, p in enumerate(parts)]
        vs = [jnp.concatenate([diff_head(p[4], idx, hd), one], axis=1)
              for idx, (p, one) in enumerate(zip(parts, ones))]
        a1 = _softmax_pv([_dot_nt(q_b * m64[0], k) for k in ks], vs, False)
        a2 = _softmax_pv([_dot_nt(q_b * m64[1], k) for k in ks], vs, False)
        d = _rms(a1 - lam * a2, gsub) * (1.0 - lam_init)
        o_ref[:, 512 + hd * LANES:512 + (hd + 1) * LANES] = d.astype(BF16)


def _attn_even(latent, batch, n_tokens, layer_j, lam_init, q_arrs, new_arrs, cache_arrs, lamv, gsub):
    tq = min(Q_TILE, n_tokens)
    nt = n_tokens // tq
    qspec = lambda wd: pl.BlockSpec((tq, wd), lambda b, t: (b * nt + t, 0))
    kspec = lambda rows, wd: pl.BlockSpec((rows, wd), lambda b, t: (b, 0))
    kwidths = [512, 512, 128, 512, 512]
    in_specs = [qspec(512), qspec(256), qspec(512)]
    args = list(q_arrs)
    if latent:
        past = cache_arrs[0].shape[0] // batch
        in_specs += [kspec(past, wd) for wd in kwidths[:3]]
        in_specs += [pl.BlockSpec((None, None, past * DIFF_HEADS, LANES), lambda b, t: (b, layer_j, 0, 0))] * 2
        args += list(cache_arrs)
    in_specs += [kspec(n_tokens, wd) for wd in kwidths]
    args += list(new_arrs)
    in_specs += [_full(lamv.shape), _full(gsub.shape)]
    args += [lamv, gsub]
    return pl.pallas_call(
        functools.partial(_attn_even_kernel, latent, lam_init),
        grid=(batch, nt),
        in_specs=in_specs,
        out_specs=qspec(D_MODEL),
        out_shape=jax.ShapeDtypeStruct((batch * n_tokens, D_MODEL), BF16),
        compiler_params=_params("parallel", "parallel"),
        name="attn_even_lat" if latent else "attn_even_ctx",
    )(*args)


def _attn_odd_kernel(latent, tq, *refs):
    qc_ref, qd_ref = refs[:2]
    refs = refs[2:]
    n_parts = 2 if latent else 1
    parts = [list(refs[4 * k:4 * k + 4]) for k in range(n_parts)]
    sink_ref, o_ref = refs[4 * n_parts:4 * n_parts + 2]
    scratch = refs[4 * n_parts + 2:]
    lane = lax.broadcasted_iota(jnp.int32, (1, LANES), 1)
    low = lane < 64
    m64 = _lane_masks(64)

    band = None
    start = 0
    span = 0
    if latent:
        for ref, scr in zip(parts[-1], scratch):
            for c in range(ref.shape[0]):
                scr[:, _sub(c)] = ref[c]
        parts[-1] = list(scratch)
        n_new = scratch[0].shape[1]
        span = min(n_new, tq + 2 * WINDOW)
        t = pl.program_id(1)
        start = pl.multiple_of(jnp.clip(t * tq - WINDOW, 0, n_new - span), LANES)
        qpos = t * tq + lax.broadcasted_iota(jnp.int32, (tq, 1), 0)
        kpos = start + lax.broadcasted_iota(jnp.int32, (1, span), 1)
        band = jnp.abs(qpos - kpos) <= WINDOW

    for kind in range(2):
        q_ref = qc_ref if kind == 0 else qd_ref
        windowed = latent and kind == 0
        for g in range(2):
            rows = slice(g * HEAD_DIM, (g + 1) * HEAD_DIM)
            ks, vs = [], []
            for idx, p in enumerate(parts):
                k_ref, v_ref = p[2 * kind], p[2 * kind + 1]
                if windowed and idx == n_parts - 1:
                    k = k_ref[rows, pl.ds(start, span)]
                    v = v_ref[rows, pl.ds(start, span)]
                else:
                    k = k_ref[rows, :]
                    v = v_ref[rows, :]
                ks.append(jnp.concatenate([k, k], axis=0).astype(BF16))
                vs.append(jnp.concatenate([v, v, jnp.ones((LANES, v.shape[1]), F32)], axis=0).astype(BF16))
            for i in (2 * g, 2 * g + 1):
                cols = slice(i * LANES, (i + 1) * LANES)
                q_b = q_ref[:, cols]
                outs = []
                for half in range(2):
                    s_parts = [_dot(q_b * m64[half], k) for k in ks]
                    sink = None
                    if kind == 0:
                        sink = sink_ref[2 * i + half] * LOG2E
                        if latent:
                            s_parts[-1] = jnp.where(band, s_parts[-1], NEG_BIG)
                    outs.append(_softmax_pv(s_parts, vs, True, sink))
                o_ref[:, kind * 512 + i * LANES:kind * 512 + (i + 1) * LANES] = (
                    jnp.where(low, outs[0], outs[1]).astype(BF16))


def _attn_odd(latent, batch, n_tokens, layer_j, q_arrs, new_arrs, cache_arrs, sink):
    tq = min(Q_TILE, n_tokens)
    nt = n_tokens // tq
    qspec = lambda wd: pl.BlockSpec((tq, wd), lambda b, t: (b * nt + t, 0))
    in_specs = [qspec(512), qspec(512)]
    args = list(q_arrs)
    scratch = []
    if latent:
        past = cache_arrs[0].shape[-1]
        in_specs += [pl.BlockSpec((None, None, LANES, past), lambda b, t: (b, layer_j, 0, 0))] * 4
        args += list(cache_arrs)
        chunks = n_tokens // SUB_ROWS
        in_specs += [pl.BlockSpec((chunks, LANES, SUB_ROWS), lambda b, t: (b, 0, 0))] * 4
        scratch = [pltpu.VMEM((LANES, n_tokens), F32)] * 4
    else:
        in_specs += [pl.BlockSpec((None, None, LANES, n_tokens), lambda b, t: (b, layer_j, 0, 0))] * 4
    args += list(new_arrs)
    in_specs += [pl.BlockSpec(memory_space=pltpu.SMEM)]
    args += [sink]
    return pl.pallas_call(
        functools.partial(_attn_odd_kernel, latent, tq),
        grid=(batch, nt),
        in_specs=in_specs,
        out_specs=qspec(D_MODEL),
        out_shape=jax.ShapeDtypeStruct((batch * n_tokens, D_MODEL), BF16),
        scratch_shapes=scratch,
        compiler_params=_params("parallel", "parallel"),
        name="attn_odd_lat" if latent else "attn_odd_ctx",
    )(*args)


def _post_kernel(o_ref, wout_ref, x_ref, mod_ref, g2_ref, wrh_ref, wrl_ref, x1_ref, h2_ref, afft_ref):
    mod = mod_ref[0]
    wrh = wrh_ref[...]
    lane = lax.broadcasted_iota(jnp.int32, (1, LANES), 1)
    for c in range(N_SUB):
        r = _sub(c)
        x1 = x_ref[r, :] + mod[2:3] * _dot(o_ref[r, :], wout_ref[...])
        x1_ref[r, :] = x1
        h2 = _modulated(x1, mod, g2_ref[...], 3, 4)
        h_hi = h2.astype(BF16)
        h_lo = (h2 - h_hi.astype(F32)).astype(BF16)
        h2_ref[r, :] = h_hi
        logits = _dot(h_hi, wrh) + _dot(h_lo, wrh) + _dot(h_hi, wrl_ref[...])
        logits = jnp.where(lane < N_EXPERTS, logits, NEG_BIG)
        e = jnp.exp(logits - jnp.max(logits, axis=-1, keepdims=True))
        aff = e / jnp.sum(e, axis=-1, keepdims=True)
        afft_ref[c] = aff.T[0:N_EXPERTS]


def _post(latent, n_tokens, o, wout, x, mods, g2, wrh, wrl):
    n_rows = x.shape[0]
    mod_spec, _, rows, chunks = _row_specs(latent, n_tokens)
    return pl.pallas_call(
        _post_kernel,
        grid=(n_rows // ROW_TILE,),
        in_specs=[rows(D_MODEL), _full(wout.shape), rows(D_MODEL), mod_spec, _full(g2.shape),
                  _full(wrh.shape), _full(wrl.shape)],
        out_specs=[rows(D_MODEL), rows(D_MODEL), chunks(N_EXPERTS)],
        out_shape=[jax.ShapeDtypeStruct((n_rows, D_MODEL), F32),
                   jax.ShapeDtypeStruct((n_rows, D_MODEL), BF16),
                   jax.ShapeDtypeStruct((n_rows // SUB_ROWS, N_EXPERTS, SUB_ROWS), F32)],
        compiler_params=_params("parallel"),
        name="post_lat" if latent else "post_ctx",
    )(o, wout, x, mods, g2, wrh, wrl)


def _route_kernel(cap, aff_ref, out_ref):
    aff = aff_ref[...]
    n_rows, n = aff.shape
    bits = lax.bitcast_convert_type(aff, jnp.int32)

    def body(_, carry):
        lo, hi = carry
        mid = lo + ((hi - lo + 1) >> 1)
        cnt = jnp.sum(jnp.where(bits >= mid, 1.0, 0.0), axis=-1, keepdims=True)
        ok = cnt >= cap
        return jnp.where(ok, mid, lo), jnp.where(ok, hi, mid - 1)

    lo0 = jnp.zeros((n_rows, 1), jnp.int32)
    hi0 = jnp.full((n_rows, 1), 0x7F800000, jnp.int32)
    thr, _ = lax.fori_loop(0, 31, body, (lo0, hi0))
    above = jnp.where(bits > thr, 1.0, 0.0)
    equal = jnp.where(bits == thr, 1.0, 0.0)
    room = cap - jnp.sum(above, axis=-1, keepdims=True)
    before = jnp.where(lax.broadcasted_iota(jnp.int32, (n, n), 0)
                       < lax.broadcasted_iota(jnp.int32, (n, n), 1), 1.0, 0.0).astype(BF16)
    equal_before = _dot(equal.astype(BF16), before)
    chosen = above + equal * jnp.where(equal_before < room, 1.0, 0.0)
    slot = _dot(chosen.astype(BF16), before)
    out_ref[...] = jnp.where(chosen > 0.5, slot, -1.0)


def _route(cap, aff_rows):
    return pl.pallas_call(
        functools.partial(_route_kernel, cap),
        grid=(1,),
        in_specs=[_full(aff_rows.shape)],
        out_specs=_full(aff_rows.shape),
        out_shape=jax.ShapeDtypeStruct(aff_rows.shape, F32),
        compiler_params=_params("arbitrary"),
        name="route",
    )(aff_rows)


def _gather_kernel(cap, slot_ref, aff_ref, h_ref, xs_ref, gate_ref):
    slots = slot_ref[...]
    aff = aff_ref[...]
    want = lax.broadcasted_iota(jnp.int32, (cap, 1), 0).astype(F32)
    rows = []
    for e in range(N_EXPERTS):
        hit = slots[e:e + 1, :] == want
        rows.append(jnp.where(hit, 1.0, 0.0).astype(BF16))
        gate_ref[e] = jnp.sum(jnp.where(hit, aff[e:e + 1, :], 0.0), axis=-1, keepdims=True)
    picked = _dot(jnp.concatenate(rows, axis=0), h_ref[...]).astype(BF16)
    for e in range(N_EXPERTS):
        xs_ref[e] = picked[e * cap:(e + 1) * cap]


def _gather(cap, batch, n_tokens, slots, aff_rows, h2):
    return pl.pallas_call(
        functools.partial(_gather_kernel, cap),
        grid=(batch,),
        in_specs=[pl.BlockSpec((N_EXPERTS, n_tokens), lambda b: (b, 0)),
                  pl.BlockSpec((N_EXPERTS, n_tokens), lambda b: (b, 0)),
                  pl.BlockSpec((n_tokens, D_MODEL), lambda b: (b, 0))],
        out_specs=[pl.BlockSpec((N_EXPERTS, cap, D_MODEL), lambda b: (0, b, 0)),
                   pl.BlockSpec((N_EXPERTS, cap, 1), lambda b: (0, b, 0))],
        out_shape=[jax.ShapeDtypeStruct((N_EXPERTS, batch * cap, D_MODEL), BF16),
                   jax.ShapeDtypeStruct((N_EXPERTS, batch * cap, 1), F32)],
        compiler_params=_params("parallel"),
        name="gather",
    )(slots, aff_rows, h2)


def _ffn_kernel(n_groups, *refs):
    xs_refs = refs[:n_groups]
    gate_refs = refs[n_groups:2 * n_groups]
    w1_ref, w3_ref, w2_ref = refs[2 * n_groups:2 * n_groups + 3]
    ys_refs = refs[2 * n_groups + 3:3 * n_groups + 3]
    acc_refs = refs[3 * n_groups + 3:]
    f = pl.program_id(1)

    @pl.when(f == 0)
    def _():
        for acc_ref in acc_refs:
            acc_ref[...] = jnp.zeros(acc_ref.shape, F32)

    w1 = w1_ref[0, 0].astype(BF16)
    w3 = w3_ref[0, 0].astype(BF16)
    w2 = w2_ref[0, 0].astype(BF16)
    for xs_ref, acc_ref in zip(xs_refs, acc_refs):
        xs = xs_ref[0]
        hid = (_silu(_dot(xs, w1)) * _dot(xs, w3)).astype(BF16)
        acc_ref[...] += _dot(hid, w2)

    @pl.when(f == pl.num_programs(1) - 1)
    def _():
        for gate_ref, ys_ref, acc_ref in zip(gate_refs, ys_refs, acc_refs):
            ys_ref[0] = (acc_ref[...] * gate_ref[0]).astype(BF16)


def _ffn(layer, xs_list, gate_list, w1, w3, w2):
    n_groups = len(xs_list)
    n_chunks = EXPERT_FF // FF_CHUNK
    xs_specs = [pl.BlockSpec((1,) + xs.shape[1:], lambda e, f: (e, 0, 0)) for xs in xs_list]
    gate_specs = [pl.BlockSpec((1,) + g.shape[1:], lambda e, f: (e, 0, 0)) for g in gate_list]
    return pl.pallas_call(
        functools.partial(_ffn_kernel, n_groups),
        grid=(N_EXPERTS, n_chunks),
        in_specs=xs_specs + gate_specs + [
            pl.BlockSpec((1, 1, D_MODEL, FF_CHUNK), lambda e, f: (layer, e, 0, f)),
            pl.BlockSpec((1, 1, D_MODEL, FF_CHUNK), lambda e, f: (layer, e, 0, f)),
            pl.BlockSpec((1, 1, FF_CHUNK, D_MODEL), lambda e, f: (layer, e, f, 0))],
        out_specs=xs_specs,
        out_shape=[jax.ShapeDtypeStruct(xs.shape, BF16) for xs in xs_list],
        scratch_shapes=[pltpu.VMEM(xs.shape[1:], F32) for xs in xs_list],
        compiler_params=_params("parallel", "arbitrary"),
        name="ffn",
    )(*xs_list, *gate_list, w1, w3, w2)


def _combine_kernel(cap, slot_ref, ys_ref, x_ref, mod_ref, o_ref):
    slots = slot_ref[...]
    n = slots.shape[1]
    pad = jnp.zeros((LANES - N_EXPERTS, n), F32)
    slots_t = jnp.concatenate([slots, pad], axis=0).T
    lane = lax.broadcasted_iota(jnp.int32, (1, LANES), 1).astype(F32)
    per_block = LANES // cap
    blocks = []
    for k in range(N_EXPERTS // per_block):
        hit = None
        for e in range(k * per_block, (k + 1) * per_block):
            col = slots_t[:, e:e + 1]
            target = jnp.where(col >= 0.0, col + float((e - k * per_block) * cap), -1.0)
            he = jnp.where(target == lane, 1.0, 0.0)
            hit = he if hit is None else hit + he
        blocks.append(hit.astype(BF16))
    onehot = jnp.concatenate(blocks, axis=1)
    ys = jnp.concatenate([ys_ref[e] for e in range(N_EXPERTS)], axis=0)
    o_ref[...] = x_ref[...] + mod_ref[0][5:6] * _dot(onehot, ys)


def _combine(latent, cap, batch, n_tokens, slots, ys, x1, mods):
    mod_map = (lambda b: (1 + b, 0, 0)) if latent else (lambda b: (0, 0, 0))
    return pl.pallas_call(
        functools.partial(_combine_kernel, cap),
        grid=(batch,),
        in_specs=[pl.BlockSpec((N_EXPERTS, n_tokens), lambda b: (b, 0)),
                  pl.BlockSpec((N_EXPERTS, cap, D_MODEL), lambda b: (0, b, 0)),
                  pl.BlockSpec((n_tokens, D_MODEL), lambda b: (b, 0)),
                  pl.BlockSpec((1, 6, D_MODEL), mod_map)],
        out_specs=pl.BlockSpec((n_tokens, D_MODEL), lambda b: (b, 0)),
        out_shape=jax.ShapeDtypeStruct(x1.shape, F32),
        compiler_params=_params("parallel"),
        name="combine",
    )(slots, ys, x1, mods)


def _rope_tables(n_pos, rot_dim):
    t = jnp.arange(n_pos)
    row = t // GRID_W
    col = t % GRID_W
    nf = rot_dim // 4
    inv = ROPE_THETA ** (-jnp.arange(nf, dtype=F32) / nf)
    ang_r = row[:, None] * inv
    ang_c = col[:, None] * inv
    cr, sr, cc, sc = jnp.cos(ang_r), jnp.sin(ang_r), jnp.cos(ang_c), jnp.sin(ang_c)
    cos = jnp.concatenate([cr, cr, cc, cc], axis=1)
    sin = jnp.concatenate([-sr, sr, -sc, sc], axis=1)
    reps = LANES // rot_dim
    return jnp.tile(cos, (1, reps)), jnp.tile(sin, (1, reps))


def _group_mean_matrix(width):
    idx = jnp.arange(LANES) // width
    return jnp.where(idx[:, None] == idx[None, :], 1.0 / width, 0.0).astype(BF16)


def _tile_row(g, reps):
    return jnp.tile(g, reps).reshape(1, -1)


def _split_heads(w, n_heads, first):
    k = w.shape[0]
    w3 = w.reshape(k, n_heads, -1)
    return jnp.concatenate([w3[:, :, :first].reshape(k, -1), w3[:, :, first:].reshape(k, -1)], axis=1)


def _feature_major(cache):
    b, l, t, h, d = cache.shape
    return jnp.transpose(cache, (0, 1, 3, 4, 2)).reshape(b, l, h * d, t)


def kernel(x_prompt, x_sample, cache_mla_ckv, cache_mla_krope, cache_diff_k, cache_diff_v,
           cache_win_k, cache_win_v, cache_axial_k, cache_axial_v, c, c_ctx,
           g_norm, w_mod, b_mod, w_in_even, w_out_even, mla_g_qa, mla_g_kva, mla_w_uq, mla_w_ukv,
           mla_g_q, mla_g_k, diff_g_q, diff_g_k, diff_lambda, diff_g_sub,
           w_in_odd, w_out_odd, odd_g_qk, win_sink, moe_w_router, moe_w1, moe_w3, moe_w2):
    n_ctx_b, n_ctx = x_prompt.shape[:2]
    n_lat_b, n_lat = x_sample.shape[:2]
    past = cache_mla_ckv.shape[2]
    n_even, n_odd = w_in_even.shape[0], w_in_odd.shape[0]
    assert n_ctx == SUB_ROWS and n_lat % ROW_TILE == 0
    passes = [
        (False, n_ctx_b, n_ctx, EC_CAPACITY_FACTOR * n_ctx // N_EXPERTS),
        (True, n_lat_b, n_lat, EC_CAPACITY_FACTOR * n_lat // N_EXPERTS),
    ]

    cvec = jnp.zeros((16, D_MODEL), F32).at[0].set(c_ctx).at[1:1 + n_lat_b].set(c)
    mods = _adaln(cvec, w_mod, b_mod).reshape(DEPTH, 16, 6, D_MODEL)

    g64 = _group_mean_matrix(64)
    g32 = _group_mean_matrix(32)
    tab64 = _rope_tables(n_lat, HEAD_DIM)
    tab32 = _rope_tables(n_lat, MLA_ROPE)
    odd_caches = [_feature_major(a) for a in (cache_win_k, cache_win_v, cache_axial_k, cache_axial_v)]
    diff_caches = [a.reshape(n_lat_b, n_even, past * DIFF_HEADS, LANES) for a in (cache_diff_k, cache_diff_v)]

    even_states = [jnp.zeros((n_ctx_b, n_even, n_ctx, MLA_KV_LORA), F32),
                   jnp.zeros((n_ctx_b, n_even, MLA_ROPE, n_ctx), F32),
                   jnp.zeros((n_ctx_b, n_even, n_ctx * DIFF_HEADS, LANES), F32),
                   jnp.zeros((n_ctx_b, n_even, n_ctx * DIFF_HEADS, LANES), F32)]
    odd_states = [jnp.zeros((n_ctx_b, n_odd, LANES, n_ctx), F32) for _ in range(4)]

    xs_state = [x_prompt.reshape(-1, D_MODEL), x_sample.reshape(-1, D_MODEL)]

    for layer in range(DEPTH):
        j = layer // 2
        mods_l = mods[layer]
        g1 = g_norm[layer, 0].reshape(1, -1)
        g2 = g_norm[layer, 1].reshape(1, -1)
        attn_out = []
        if layer % 2 == 0:
            wt = jnp.swapaxes(w_in_even[j], 0, 1)
            wt = jnp.concatenate([wt[:384], wt[416:]] + [wt[384:416]] * 4, axis=0).astype(BF16)
            wuq = _split_heads(mla_w_uq[j], MLA_HEADS, MLA_NOPE).astype(BF16)
            wukv = _split_heads(mla_w_ukv[j], MLA_HEADS, MLA_NOPE).astype(BF16)
            w_out = w_out_even[j].astype(BF16)
            gkn = _tile_row(mla_g_k[j, :MLA_NOPE], MLA_HEADS)
            gains = [mla_g_qa[j].reshape(1, -1), mla_g_kva[j].reshape(1, -1),
                     _tile_row(mla_g_q[j, :MLA_NOPE], MLA_HEADS), _tile_row(mla_g_q[j, MLA_NOPE:], MLA_HEADS),
                     gkn, _tile_row(mla_g_k[j, MLA_NOPE:], 4),
                     _tile_row(diff_g_q[j], 2 * DIFF_HEADS), _tile_row(diff_g_k[j], 2 * DIFF_HEADS)]
            lam_init = 0.8 - 0.6 * math.exp(-0.3 * layer)
            gsub = diff_g_sub[j].reshape(1, -1)
            for p, (latent, batch, n_tok, cap) in enumerate(passes):
                outs = _in_even(latent, j, xs_state[p], mods_l, g1, wt, wuq, wukv, gains,
                                (g64, g32), tab64 + tab32, even_states, n_tok)
                qn, qr, kn, vm, krt, mq, mk, mv = outs[:8]
                cache_arrs = None
                if latent:
                    kn_c, vm_c = _cache_kv(cache_mla_ckv[:, j].reshape(-1, MLA_KV_LORA), wukv, gkn, g64)
                    cache_arrs = (kn_c, vm_c,
                                  jnp.tile(cache_mla_krope[:, j].reshape(-1, MLA_ROPE), (1, 4)).astype(BF16),
                                  diff_caches[0], diff_caches[1])
                else:
                    even_states = list(outs[8:])
                attn_out.append(_attn_even(latent, batch, n_tok, j, lam_init, (qn, qr, mq),
                                           (kn, vm, krt, mk, mv), cache_arrs, diff_lambda[j], gsub))
        else:
            w_in = w_in_odd[j].astype(BF16)
            w_out = w_out_odd[j].astype(BF16)
            gains = [_tile_row(odd_g_qk[j, 0], 8), _tile_row(odd_g_qk[j, 1], 2),
                     _tile_row(odd_g_qk[j, 2], 8), _tile_row(odd_g_qk[j, 3], 2)]
            for p, (latent, batch, n_tok, cap) in enumerate(passes):
                outs = _in_odd(latent, j, xs_state[p], mods_l, g1, w_in, gains, g64, tab64, odd_states, n_tok)
                if not latent:
                    odd_states = list(outs[2:])
                attn_out.append(_attn_odd(latent, batch, n_tok, j, outs[:2], outs[2:],
                                          odd_caches if latent else None, win_sink[j]))

        w_r = jnp.pad(moe_w_router[layer], ((0, 0), (0, LANES - N_EXPERTS)))
        wrh = w_r.astype(BF16)
        wrl = (w_r - wrh.astype(F32)).astype(BF16)
        x1s, slot_list, xs_list, gate_list = [], [], [], []
        for p, (latent, batch, n_tok, cap) in enumerate(passes):
            x1, h2, afft = _post(latent, n_tok, attn_out[p], w_out, xs_state[p], mods_l, g2, wrh, wrl)
            aff_rows = (afft.reshape(batch, n_tok // SUB_ROWS, N_EXPERTS, SUB_ROWS)
                        .transpose(0, 2, 1, 3).reshape(batch * N_EXPERTS, n_tok))
            slots = _route(cap, aff_rows)
            xs, gates = _gather(cap, batch, n_tok, slots, aff_rows, h2)
            x1s.append(x1)
            slot_list.append(slots)
            xs_list.append(xs)
            gate_list.append(gates)
        ys_list = _ffn(layer, xs_list, gate_list, moe_w1, moe_w3, moe_w2)
        for p, (latent, batch, n_tok, cap) in enumerate(passes):
            xs_state[p] = _combine(latent, cap, batch, n_tok, slot_list[p], ys_list[p], x1s[p], mods_l)

    y_prompt = xs_state[0].reshape(x_prompt.shape)
    y_sample = xs_state[1].reshape(x_sample.shape)

    def token_major(arr, heads):
        b, l, f, t = arr.shape
        return jnp.transpose(arr.reshape(b, l, heads, f // heads, t), (0, 1, 4, 2, 3))

    diff_shape = (n_ctx_b, n_even, n_ctx, DIFF_HEADS, LANES)
    return (y_prompt, y_sample,
            even_states[0],
            jnp.swapaxes(even_states[1], 2, 3),
            even_states[2].reshape(diff_shape),
            even_states[3].reshape(diff_shape),
            token_major(odd_states[0], 2), token_major(odd_states[1], 2),
            token_major(odd_states[2], 2), token_major(odd_states[3], 2))
```

```python
import functools
import math

import jax
import jax.numpy as jnp
from jax import lax
from jax.experimental import pallas as pl
from jax.experimental.pallas import tpu as pltpu

F32 = jnp.float32
BF16 = jnp.bfloat16

D_MODEL = 1024
DEPTH = 4
GRID_W = 64
ROPE_THETA = 10000.0
WINDOW = 128
RMS_EPS = 1e-6
MLA_HEADS = 8
MLA_Q_LORA = 256
MLA_KV_LORA = 128
MLA_NOPE = 64
MLA_ROPE = 32
MLA_VD = 64
DIFF_HEADS = 4
DIFF_HD = 64
HEAD_DIM = 64
N_EXPERTS = 16
EC_CAPACITY_FACTOR = 2
EXPERT_FF = 2048

LANES = 128
MXU_DIM = 256
ROW_TILE = 512
SUB_ROWS = 256
N_SUB = ROW_TILE // SUB_ROWS
Q_TILE = 256
FF_CHUNK = 1024
VMEM_LIMIT = 50 * 1024 * 1024
FFN_VMEM_LIMIT = 60 * 1024 * 1024
NEG_BIG = -1e30
LOG2E = 1.4426950408889634


def _params(*sem, vmem=VMEM_LIMIT):
    return pltpu.CompilerParams(dimension_semantics=sem, vmem_limit_bytes=vmem)


def _full(shape):
    zeros = (0,) * len(shape)
    return pl.BlockSpec(shape, lambda *_: zeros)


def _dot(a, b):
    return jnp.dot(a, b, preferred_element_type=F32)


def _dot_nt(a, b):
    return lax.dot_general(a, b, (((1,), (1,)), ((), ())), preferred_element_type=F32)


def _rms(x, g):
    ms = jnp.mean(x * x, axis=-1, keepdims=True)
    return x * lax.rsqrt(ms + RMS_EPS) * g


def _group_rms(x, g, gmat):
    outs = []
    width = x.shape[1]
    for k in range(0, width, MXU_DIM):
        wd = min(MXU_DIM, width - k)
        xb = x[:, k:k + wd]
        ms = _dot((xb * xb).astype(BF16), gmat[:wd, :wd])
        outs.append(xb * lax.rsqrt(ms + RMS_EPS))
    y = outs[0] if len(outs) == 1 else jnp.concatenate(outs, axis=1)
    return y * g


def _rope(x, cos, sin, half):
    lane = lax.broadcasted_iota(jnp.int32, (1, LANES), 1)
    first = (lane % (2 * half)) < half
    outs = []
    for k in range(x.shape[1] // LANES):
        xb = x[:, k * LANES:(k + 1) * LANES]
        fwd = pltpu.roll(xb, LANES - half, 1)
        bwd = pltpu.roll(xb, half, 1)
        outs.append(xb * cos + jnp.where(first, fwd, bwd) * sin)
    return outs[0] if len(outs) == 1 else jnp.concatenate(outs, axis=1)


def _silu(a):
    return a / (1.0 + jnp.exp(-a))


def _sub(c):
    return slice(c * SUB_ROWS, (c + 1) * SUB_ROWS)


def _adaln_kernel(c_ref, w_ref, b_ref, o_ref):
    a = _silu(c_ref[...]).astype(BF16)
    bias = b_ref[pl.ds(pl.program_id(0), 1), :]
    o_ref[0] = _dot(a, w_ref[0].astype(BF16)) + bias


def _adaln(cvec, w_mod, b_mod):
    rows = cvec.shape[0]
    nc = 4
    wc = 6 * D_MODEL // nc
    return pl.pallas_call(
        _adaln_kernel,
        grid=(DEPTH, nc),
        in_specs=[_full((rows, D_MODEL)),
                  pl.BlockSpec((1, D_MODEL, wc), lambda l, n: (l, 0, n)),
                  pl.BlockSpec((DEPTH, wc), lambda l, n: (0, n))],
        out_specs=pl.BlockSpec((1, rows, wc), lambda l, n: (l, 0, n)),
        out_shape=jax.ShapeDtypeStruct((DEPTH, rows, 6 * D_MODEL), F32),
        compiler_params=_params("parallel", "parallel"),
        name="adaln",
    )(cvec, w_mod, b_mod)


def _modulated(x, mod, g, shift_row, scale_row):
    return _rms(x, g) * (1.0 + mod[scale_row:scale_row + 1]) + mod[shift_row:shift_row + 1]


def _in_even_kernel(latent, *refs):
    (x_ref, mod_ref, g1_ref, wt_ref, gqa_ref, gkva_ref, wuq_ref, wukv_ref,
     gqn_ref, gqr_ref, gkn_ref, gkr_ref, gdq_ref, gdk_ref, g64_ref, g32_ref) = refs[:16]
    refs = refs[16:]
    if latent:
        c64_ref, s64_ref, c32_ref, s32_ref = refs[:4]
    refs = refs[4:]
    qn_ref, qr_ref, kn_ref, vm_ref, krt_ref, mq_ref, mk_ref, mv_ref = refs[:8]
    refs = refs[8:]
    g64 = g64_ref[...]
    g32 = g32_ref[...]
    mod = mod_ref[0]
    mla_scale = (MLA_NOPE + MLA_ROPE) ** -0.5 * LOG2E
    diff_scale = DIFF_HD ** -0.5 * LOG2E

    for c in range(N_SUB):
        r = _sub(c)
        h = _modulated(x_ref[r, :], mod, g1_ref[...], 0, 1)
        proj = _dot_nt(h.astype(BF16), wt_ref[...])
        cq = _rms(proj[:, 0:256], gqa_ref[...])
        q = _dot(cq.astype(BF16), wuq_ref[...])
        qn = _group_rms(q[:, 0:512], gqn_ref[...], g64)
        qr = _group_rms(q[:, 512:768], gqr_ref[...], g32)
        ckv = _rms(proj[:, 256:384], gkva_ref[...])
        kv = _dot(ckv.astype(BF16), wukv_ref[...])
        kn = _group_rms(kv[:, 0:512], gkn_ref[...], g64)
        mq = _group_rms(proj[:, 384:896], gdq_ref[...], g64)
        mk = _group_rms(proj[:, 896:1408], gdk_ref[...], g64)
        mv = proj[:, 1408:1920]
        kr = _group_rms(proj[:, 1920:2048], gkr_ref[...], g32)
        if latent:
            c64, s64, c32, s32 = c64_ref[r, :], s64_ref[r, :], c32_ref[r, :], s32_ref[r, :]
            qr = _rope(qr, c32, s32, MLA_ROPE // 4)
            kr = _rope(kr, c32, s32, MLA_ROPE // 4)
            mq = _rope(mq, c64, s64, DIFF_HD // 4)
            mk = _rope(mk, c64, s64, DIFF_HD // 4)
        else:
            ckv_s_ref, krt_s_ref, mk_s_ref, mv_s_ref = refs
            ckv_s_ref[c] = ckv
            krt_s_ref[c] = kr.T[0:MLA_ROPE]
            for hd in range(DIFF_HEADS):
                mk_s_ref[c, pl.ds(hd, SUB_ROWS, stride=DIFF_HEADS), :] = mk[:, hd * LANES:(hd + 1) * LANES]
                mv_s_ref[c, pl.ds(hd, SUB_ROWS, stride=DIFF_HEADS), :] = mv[:, hd * LANES:(hd + 1) * LANES]
        qn_ref[r, :] = (qn * mla_scale).astype(BF16)
        qr_ref[r, :] = (qr * mla_scale).astype(BF16)
        kn_ref[r, :] = kn.astype(BF16)
        vm_ref[r, :] = kv[:, 512:1024].astype(BF16)
        krt_ref[r, :] = kr.astype(BF16)
        mq_ref[r, :] = (mq * diff_scale).astype(BF16)
        mk_ref[r, :] = mk.astype(BF16)
        mv_ref[r, :] = mv.astype(BF16)


def _in_odd_kernel(latent, *refs):
    x_ref, mod_ref, g1_ref, w_ref, gqc_ref, gkc_ref, gqd_ref, gkd_ref, g64_ref = refs[:9]
    refs = refs[9:]
    if latent:
        c64_ref, s64_ref = refs[:2]
        refs = refs[2:]
    else:
        refs = refs[4:]
    qc_ref, qd_ref, kct_ref, vct_ref, kdt_ref, vdt_ref = refs
    g64 = g64_ref[...]
    mod = mod_ref[0]
    scale = HEAD_DIM ** -0.5 * LOG2E

    for c in range(N_SUB):
        r = _sub(c)
        h = _modulated(x_ref[r, :], mod, g1_ref[...], 0, 1)
        proj = _dot(h.astype(BF16), w_ref[...])
        qc = _group_rms(proj[:, 0:512], gqc_ref[...], g64)
        kc = _group_rms(proj[:, 512:640], gkc_ref[...], g64)
        qd = _group_rms(proj[:, 768:1280], gqd_ref[...], g64)
        kd = _group_rms(proj[:, 1280:1408], gkd_ref[...], g64)
        if latent:
            c64, s64 = c64_ref[r, :], s64_ref[r, :]
            qc = _rope(qc, c64, s64, HEAD_DIM // 4)
            kc = _rope(kc, c64, s64, HEAD_DIM // 4)
            qd = _rope(qd, c64, s64, HEAD_DIM // 4)
            kd = _rope(kd, c64, s64, HEAD_DIM // 4)
        qc_ref[r, :] = (qc * scale).astype(BF16)
        qd_ref[r, :] = (qd * scale).astype(BF16)
        kct_ref[c] = kc.T
        vct_ref[c] = proj[:, 640:768].T
        kdt_ref[c] = kd.T
        vdt_ref[c] = proj[:, 1408:1536].T


def _row_specs(latent, n_tokens):
    tiles_per_batch = max(1, n_tokens // ROW_TILE)
    if latent:
        mod_map = lambda i: (1 + i // tiles_per_batch, 0, 0)
    else:
        mod_map = lambda i: (0, 0, 0)
    mod_spec = pl.BlockSpec((1, 6, D_MODEL), mod_map)
    table_spec = pl.BlockSpec((ROW_TILE, LANES), lambda i: (i % tiles_per_batch, 0))
    rows = lambda width: pl.BlockSpec((ROW_TILE, width), lambda i: (i, 0))
    chunks = lambda feats: pl.BlockSpec((N_SUB, feats, SUB_ROWS), lambda i: (i, 0, 0))
    return mod_spec, table_spec, rows, chunks


def _state_spec(layer_j, feats, cols):
    return pl.BlockSpec((N_SUB, None, feats, cols), lambda i: (i, layer_j, 0, 0))


def _in_even(latent, layer_j, x, mods, g1, wt, wuq, wukv, gains, gmats, tables, states, n_tokens):
    n_rows = x.shape[0]
    mod_spec, table_spec, rows, _ = _row_specs(latent, n_tokens)
    vec_specs = [_full(g.shape) for g in gains]
    in_specs = ([rows(D_MODEL), mod_spec, _full(g1.shape), _full(wt.shape)]
                + vec_specs[:2] + [_full(wuq.shape), _full(wukv.shape)] + vec_specs[2:]
                + [_full((MXU_DIM, MXU_DIM))] * 2)
    args = [x, mods, g1, wt, gains[0], gains[1], wuq, wukv] + list(gains[2:]) + list(gmats)
    widths = [512, 256, 512, 512, 128, 512, 512, 512]
    out_shape = [jax.ShapeDtypeStruct((n_rows, wd), BF16) for wd in widths]
    out_specs = [rows(wd) for wd in widths]
    aliases = {}
    if latent:
        in_specs += [table_spec] * 4
        args += list(tables)
    else:
        for k, st in enumerate(states):
            aliases[len(args)] = len(out_shape)
            in_specs.append(pl.BlockSpec(memory_space=pl.ANY))
            args.append(st)
            out_shape.append(jax.ShapeDtypeStruct(st.shape, st.dtype))
            out_specs.append(_state_spec(layer_j, st.shape[2], st.shape[3]))
    return pl.pallas_call(
        functools.partial(_in_even_kernel, latent),
        grid=(n_rows // ROW_TILE,),
        in_specs=in_specs, out_specs=out_specs, out_shape=out_shape,
        input_output_aliases=aliases,
        compiler_params=_params("parallel"),
        name="in_even_lat" if latent else "in_even_ctx",
    )(*args)


def _in_odd(latent, layer_j, x, mods, g1, w, gains, g64, tables, states, n_tokens):
    n_rows = x.shape[0]
    mod_spec, table_spec, rows, chunks = _row_specs(latent, n_tokens)
    in_specs = ([rows(D_MODEL), mod_spec, _full(g1.shape), _full(w.shape)]
                + [_full(g.shape) for g in gains] + [_full((MXU_DIM, MXU_DIM))])
    args = [x, mods, g1, w] + list(gains) + [g64]
    out_shape = [jax.ShapeDtypeStruct((n_rows, 512), BF16)] * 2
    out_specs = [rows(512), rows(512)]
    aliases = {}
    if latent:
        in_specs += [table_spec] * 2
        args += list(tables)
        out_shape += [jax.ShapeDtypeStruct((n_rows // SUB_ROWS, LANES, SUB_ROWS), F32)] * 4
        out_specs += [chunks(LANES)] * 4
    else:
        for st in states:
            aliases[len(args)] = len(out_shape)
            in_specs.append(pl.BlockSpec(memory_space=pl.ANY))
            args.append(st)
            out_shape.append(jax.ShapeDtypeStruct(st.shape, st.dtype))
            out_specs.append(_state_spec(layer_j, st.shape[2], st.shape[3]))
    return pl.pallas_call(
        functools.partial(_in_odd_kernel, latent),
        grid=(n_rows // ROW_TILE,),
        in_specs=in_specs, out_specs=out_specs, out_shape=out_shape,
        input_output_aliases=aliases,
        compiler_params=_params("parallel"),
        name="in_odd_lat" if latent else "in_odd_ctx",
    )(*args)


def _cache_kv_kernel(ckv_ref, wukv_ref, gkn_ref, g64_ref, kn_ref, vm_ref):
    kv = _dot(ckv_ref[...].astype(BF16), wukv_ref[...])
    kn_ref[...] = _group_rms(kv[:, 0:512], gkn_ref[...], g64_ref[...]).astype(BF16)
    vm_ref[...] = kv[:, 512:1024].astype(BF16)


def _cache_kv(ckv, wukv, gkn, g64):
    rows = ckv.shape[0]
    tile = 512
    spec = lambda wd: pl.BlockSpec((tile, wd), lambda i: (i, 0))
    return pl.pallas_call(
        _cache_kv_kernel,
        grid=(rows // tile,),
        in_specs=[spec(MLA_KV_LORA), _full(wukv.shape), _full(gkn.shape), _full((MXU_DIM, MXU_DIM))],
        out_specs=[spec(512), spec(512)],
        out_shape=[jax.ShapeDtypeStruct((rows, 512), BF16)] * 2,
        compiler_params=_params("parallel"),
        name="cache_kv",
    )(ckv, wukv, gkn, g64)


def _softmax_pv(s_parts, v_parts, v_is_feature_major, sink=None):
    m = None
    for s in s_parts:
        mi = jnp.max(s, axis=-1, keepdims=True)
        m = mi if m is None else jnp.maximum(m, mi)
    if sink is not None:
        m = jnp.maximum(m, sink)
    acc = None
    for s, v in zip(s_parts, v_parts):
        p = jnp.exp2(s - m).astype(BF16)
        oi = _dot_nt(p, v) if v_is_feature_major else _dot(p, v)
        acc = oi if acc is None else acc + oi
    denom = acc[:, LANES:]
    if sink is not None:
        denom = denom + jnp.exp2(sink - m)
    return acc[:, :LANES] / denom


def _lane_masks(width):
    lane = lax.broadcasted_iota(jnp.int32, (1, LANES), 1)
    return [jnp.where(lane // width == k, 1.0, 0.0).astype(BF16) for k in range(LANES // width)]


def _attn_even_kernel(latent, lam_init, *refs):
    qn_ref, qr_ref, mq_ref = refs[:3]
    refs = refs[3:]
    n_parts = 2 if latent else 1
    parts = [refs[5 * k:5 * k + 5] for k in range(n_parts)]
    lamv_ref, gsub_ref, o_ref = refs[5 * n_parts:]
    lane = lax.broadcasted_iota(jnp.int32, (1, LANES), 1)
    low = lane < 64
    m64 = _lane_masks(64)
    m32 = _lane_masks(32)
    ones = [jnp.ones((p[0].shape[0], LANES), BF16) for p in parts]

    def diff_head(ref, idx, hd):
        if latent and idx == 0:
            n_keys = ref.shape[0] // DIFF_HEADS
            return ref[pl.ds(hd, n_keys, stride=DIFF_HEADS), :].astype(BF16)
        return ref[:, hd * LANES:(hd + 1) * LANES]

    for i in range(MLA_HEADS // 2):
        cols = slice(i * LANES, (i + 1) * LANES)
        qn_b = qn_ref[:, cols]
        kcat = [jnp.concatenate([p[0][:, cols], p[2][...]], axis=1) for p in parts]
        vms = [jnp.concatenate([p[1][:, cols], one], axis=1) for p, one in zip(parts, ones)]
        outs = []
        for half in range(2):
            head = 2 * i + half
            rb = head // 4
            qr_b = qr_ref[:, rb * LANES:(rb + 1) * LANES]
            lhs = jnp.concatenate([qn_b * m64[half], qr_b * m32[head % 4]], axis=1)
            outs.append(_softmax_pv([_dot_nt(lhs, kc) for kc in kcat], vms, False))
        o_ref[:, cols] = jnp.where(low, outs[0], outs[1]).astype(BF16)

    lv = lamv_ref[...]
    lam = (jnp.exp(jnp.sum(lv[0:1] * lv[1:2], axis=-1, keepdims=True))
           - jnp.exp(jnp.sum(lv[2:3] * lv[3:4], axis=-1, keepdims=True)) + lam_init)
    gsub = gsub_ref[...]
    for hd in range(DIFF_HEADS):
        q_b = mq_ref[:, hd * LANES:(hd + 1) * LANES]
        ks = [diff_head(p[3], idx, hd) for idx, p in enumerate(parts)]
        vs = [jnp.concatenate([diff_head(p[4], idx, hd), one], axis=1)
              for idx, (p, one) in enumerate(zip(parts, ones))]
        a1 = _softmax_pv([_dot_nt(q_b * m64[0], k) for k in ks], vs, False)
        a2 = _softmax_pv([_dot_nt(q_b * m64[1], k) for k in ks], vs, False)
        d = _rms(a1 - lam * a2, gsub) * (1.0 - lam_init)
        o_ref[:, 512 + hd * LANES:512 + (hd + 1) * LANES] = d.astype(BF16)


def _attn_even(latent, batch, n_tokens, layer_j, lam_init, q_arrs, new_arrs, cache_arrs, lamv, gsub):
    tq = min(Q_TILE, n_tokens)
    nt = n_tokens // tq
    qspec = lambda wd: pl.BlockSpec((tq, wd), lambda b, t: (b * nt + t, 0))
    kspec = lambda rows, wd: pl.BlockSpec((rows, wd), lambda b, t: (b, 0))
    kwidths = [512, 512, 128, 512, 512]
    in_specs = [qspec(512), qspec(256), qspec(512)]
    args = list(q_arrs)
    if latent:
        past = cache_arrs[0].shape[0] // batch
        in_specs += [kspec(past, wd) for wd in kwidths[:3]]
        in_specs += [pl.BlockSpec((None, None, past * DIFF_HEADS, LANES), lambda b, t: (b, layer_j, 0, 0))] * 2
        args += list(cache_arrs)
    in_specs += [kspec(n_tokens, wd) for wd in kwidths]
    args += list(new_arrs)
    in_specs += [_full(lamv.shape), _full(gsub.shape)]
    args += [lamv, gsub]
    return pl.pallas_call(
        functools.partial(_attn_even_kernel, latent, lam_init),
        grid=(batch, nt),
        in_specs=in_specs,
        out_specs=qspec(D_MODEL),
        out_shape=jax.ShapeDtypeStruct((batch * n_tokens, D_MODEL), BF16),
        compiler_params=_params("parallel", "parallel"),
        name="attn_even_lat" if latent else "attn_even_ctx",
    )(*args)


def _attn_odd_kernel(latent, tq, *refs):
    qc_ref, qd_ref = refs[:2]
    refs = refs[2:]
    n_parts = 2 if latent else 1
    parts = [list(refs[4 * k:4 * k + 4]) for k in range(n_parts)]
    sink_ref, o_ref = refs[4 * n_parts:4 * n_parts + 2]
    scratch = refs[4 * n_parts + 2:]
    lane = lax.broadcasted_iota(jnp.int32, (1, LANES), 1)
    low = lane < 64
    m64 = _lane_masks(64)

    band = None
    start = 0
    span = 0
    if latent:
        for ref, scr in zip(parts[-1], scratch):
            for c in range(ref.shape[0]):
                scr[:, _sub(c)] = ref[c]
        parts[-1] = list(scratch)
        n_new = scratch[0].shape[1]
        span = min(n_new, tq + 2 * WINDOW)
        t = pl.program_id(1)
        start = pl.multiple_of(jnp.clip(t * tq - WINDOW, 0, n_new - span), LANES)
        qpos = t * tq + lax.broadcasted_iota(jnp.int32, (tq, 1), 0)
        kpos = start + lax.broadcasted_iota(jnp.int32, (1, span), 1)
        band = jnp.abs(qpos - kpos) <= WINDOW

    for kind in range(2):
        q_ref = qc_ref if kind == 0 else qd_ref
        windowed = latent and kind == 0
        for g in range(2):
            rows = slice(g * HEAD_DIM, (g + 1) * HEAD_DIM)
            ks, vs = [], []
            for idx, p in enumerate(parts):
                k_ref, v_ref = p[2 * kind], p[2 * kind + 1]
                if windowed and idx == n_parts - 1:
                    k = k_ref[rows, pl.ds(start, span)]
                    v = v_ref[rows, pl.ds(start, span)]
                else:
                    k = k_ref[rows, :]
                    v = v_ref[rows, :]
                ks.append(jnp.concatenate([k, k], axis=0).astype(BF16))
                vs.append(jnp.concatenate([v, v, jnp.ones((LANES, v.shape[1]), F32)], axis=0).astype(BF16))
            for i in (2 * g, 2 * g + 1):
                cols = slice(i * LANES, (i + 1) * LANES)
                q_b = q_ref[:, cols]
                outs = []
                for half in range(2):
                    s_parts = [_dot(q_b * m64[half], k) for k in ks]
                    sink = None
                    if kind == 0:
                        sink = sink_ref[2 * i + half] * LOG2E
                        if latent:
                            s_parts[-1] = jnp.where(band, s_parts[-1], NEG_BIG)
                    outs.append(_softmax_pv(s_parts, vs, True, sink))
                o_ref[:, kind * 512 + i * LANES:kind * 512 + (i + 1) * LANES] = (
                    jnp.where(low, outs[0], outs[1]).astype(BF16))


def _attn_odd(latent, batch, n_tokens, layer_j, q_arrs, new_arrs, cache_arrs, sink):
    tq = min(Q_TILE, n_tokens)
    nt = n_tokens // tq
    qspec = lambda wd: pl.BlockSpec((tq, wd), lambda b, t: (b * nt + t, 0))
    in_specs = [qspec(512), qspec(512)]
    args = list(q_arrs)
    scratch = []
    if latent:
        past = cache_arrs[0].shape[-1]
        in_specs += [pl.BlockSpec((None, None, LANES, past), lambda b, t: (b, layer_j, 0, 0))] * 4
        args += list(cache_arrs)
        chunks = n_tokens // SUB_ROWS
        in_specs += [pl.BlockSpec((chunks, LANES, SUB_ROWS), lambda b, t: (b, 0, 0))] * 4
        scratch = [pltpu.VMEM((LANES, n_tokens), F32)] * 4
    else:
        in_specs += [pl.BlockSpec((None, None, LANES, n_tokens), lambda b, t: (b, layer_j, 0, 0))] * 4
    args += list(new_arrs)
    in_specs += [pl.BlockSpec(memory_space=pltpu.SMEM)]
    args += [sink]
    return pl.pallas_call(
        functools.partial(_attn_odd_kernel, latent, tq),
        grid=(batch, nt),
        in_specs=in_specs,
        out_specs=qspec(D_MODEL),
        out_shape=jax.ShapeDtypeStruct((batch * n_tokens, D_MODEL), BF16),
        scratch_shapes=scratch,
        compiler_params=_params("parallel", "parallel"),
        name="attn_odd_lat" if latent else "attn_odd_ctx",
    )(*args)


def _post_kernel(o_ref, wout_ref, x_ref, mod_ref, g2_ref, wrh_ref, wrl_ref, x1_ref, h2_ref, afft_ref):
    mod = mod_ref[0]
    wrh = wrh_ref[...]
    lane = lax.broadcasted_iota(jnp.int32, (1, LANES), 1)
    for c in range(N_SUB):
        r = _sub(c)
        x1 = x_ref[r, :] + mod[2:3] * _dot(o_ref[r, :], wout_ref[...])
        x1_ref[r, :] = x1
        h2 = _modulated(x1, mod, g2_ref[...], 3, 4)
        h_hi = h2.astype(BF16)
        h_lo = (h2 - h_hi.astype(F32)).astype(BF16)
        h2_ref[r, :] = h_hi
        logits = _dot(h_hi, wrh) + _dot(h_lo, wrh) + _dot(h_hi, wrl_ref[...])
        logits = jnp.where(lane < N_EXPERTS, logits, NEG_BIG)
        e = jnp.exp(logits - jnp.max(logits, axis=-1, keepdims=True))
        aff = e / jnp.sum(e, axis=-1, keepdims=True)
        afft_ref[c] = aff.T[0:N_EXPERTS]


def _post(latent, n_tokens, o, wout, x, mods, g2, wrh, wrl):
    n_rows = x.shape[0]
    mod_spec, _, rows, chunks = _row_specs(latent, n_tokens)
    return pl.pallas_call(
        _post_kernel,
        grid=(n_rows // ROW_TILE,),
        in_specs=[rows(D_MODEL), _full(wout.shape), rows(D_MODEL), mod_spec, _full(g2.shape),
                  _full(wrh.shape), _full(wrl.shape)],
        out_specs=[rows(D_MODEL), rows(D_MODEL), chunks(N_EXPERTS)],
        out_shape=[jax.ShapeDtypeStruct((n_rows, D_MODEL), F32),
                   jax.ShapeDtypeStruct((n_rows, D_MODEL), BF16),
                   jax.ShapeDtypeStruct((n_rows // SUB_ROWS, N_EXPERTS, SUB_ROWS), F32)],
        compiler_params=_params("parallel"),
        name="post_lat" if latent else "post_ctx",
    )(o, wout, x, mods, g2, wrh, wrl)


def _route_kernel(cap, aff_ref, out_ref):
    aff = aff_ref[...]
    n_rows, n = aff.shape
    bits = lax.bitcast_convert_type(aff, jnp.int32)

    def body(_, carry):
        lo, hi = carry
        mid = lo + ((hi - lo + 1) >> 1)
        cnt = jnp.sum(jnp.where(bits >= mid, 1.0, 0.0), axis=-1, keepdims=True)
        ok = cnt >= cap
        return jnp.where(ok, mid, lo), jnp.where(ok, hi, mid - 1)

    lo0 = jnp.zeros((n_rows, 1), jnp.int32)
    hi0 = jnp.full((n_rows, 1), 0x7F800000, jnp.int32)
    thr, _ = lax.fori_loop(0, 31, body, (lo0, hi0))
    above = jnp.where(bits > thr, 1.0, 0.0)
    equal = jnp.where(bits == thr, 1.0, 0.0)
    room = cap - jnp.sum(above, axis=-1, keepdims=True)
    before = jnp.where(lax.broadcasted_iota(jnp.int32, (n, n), 0)
                       < lax.broadcasted_iota(jnp.int32, (n, n), 1), 1.0, 0.0).astype(BF16)
    equal_before = _dot(equal.astype(BF16), before)
    chosen = above + equal * jnp.where(equal_before < room, 1.0, 0.0)
    slot = _dot(chosen.astype(BF16), before)
    out_ref[...] = jnp.where(chosen > 0.5, slot, -1.0)


def _route(cap, aff_rows):
    return pl.pallas_call(
        functools.partial(_route_kernel, cap),
        grid=(1,),
        in_specs=[_full(aff_rows.shape)],
        out_specs=_full(aff_rows.shape),
        out_shape=jax.ShapeDtypeStruct(aff_rows.shape, F32),
        compiler_params=_params("arbitrary"),
        name="route",
    )(aff_rows)


def _moe_grid(batch, n_tokens):
    req_per_step = max(1, ROW_TILE // n_tokens)
    expert_groups = max(1, n_tokens // SUB_ROWS)
    return req_per_step, expert_groups


def _gather_kernel(cap, req_per_step, experts_per_step, slot_ref, aff_ref, h_ref, xs_ref, gate_ref):
    n = slot_ref.shape[1]
    first = pl.program_id(1) * experts_per_step
    want = lax.broadcasted_iota(jnp.int32, (cap, 1), 0).astype(F32)
    for rq in range(req_per_step):
        out_rows = slice(rq * cap, (rq + 1) * cap)
        rows = []
        for k in range(experts_per_step):
            row = rq * N_EXPERTS + first + k
            hit = slot_ref[pl.ds(row, 1), :] == want
            rows.append(jnp.where(hit, 1.0, 0.0).astype(BF16))
            gate_ref[k, out_rows, :] = jnp.sum(jnp.where(hit, aff_ref[pl.ds(row, 1), :], 0.0),
                                               axis=-1, keepdims=True)
        picked = _dot(jnp.concatenate(rows, axis=0), h_ref[rq * n:(rq + 1) * n, :]).astype(BF16)
        for k in range(experts_per_step):
            xs_ref[k, out_rows, :] = picked[k * cap:(k + 1) * cap]


def _gather(cap, batch, n_tokens, slots, aff_rows, h2):
    rps, groups = _moe_grid(batch, n_tokens)
    eps = N_EXPERTS // groups
    return pl.pallas_call(
        functools.partial(_gather_kernel, cap, rps, eps),
        grid=(batch // rps, groups),
        in_specs=[pl.BlockSpec((rps * N_EXPERTS, n_tokens), lambda b, g: (b, 0)),
                  pl.BlockSpec((rps * N_EXPERTS, n_tokens), lambda b, g: (b, 0)),
                  pl.BlockSpec((rps * n_tokens, D_MODEL), lambda b, g: (b, 0))],
        out_specs=[pl.BlockSpec((eps, rps * cap, D_MODEL), lambda b, g: (g, b, 0)),
                   pl.BlockSpec((eps, rps * cap, 1), lambda b, g: (g, b, 0))],
        out_shape=[jax.ShapeDtypeStruct((N_EXPERTS, batch * cap, D_MODEL), BF16),
                   jax.ShapeDtypeStruct((N_EXPERTS, batch * cap, 1), F32)],
        compiler_params=_params("parallel", "arbitrary"),
        name="gather",
    )(slots, aff_rows, h2)


def _ffn_kernel(n_groups, *refs):
    xs_refs = refs[:n_groups]
    gate_refs = refs[n_groups:2 * n_groups]
    w1_ref, w3_ref, w2_ref = refs[2 * n_groups:2 * n_groups + 3]
    ys_refs = refs[2 * n_groups + 3:3 * n_groups + 3]
    acc_refs = refs[3 * n_groups + 3:]
    f = pl.program_id(1)

    @pl.when(f == 0)
    def _():
        for acc_ref in acc_refs:
            acc_ref[...] = jnp.zeros(acc_ref.shape, F32)

    w1 = w1_ref[0, 0].astype(BF16)
    w3 = w3_ref[0, 0].astype(BF16)
    w2 = w2_ref[0, 0].astype(BF16)
    for xs_ref, acc_ref in zip(xs_refs, acc_refs):
        xs = xs_ref[0]
        hid = (_silu(_dot(xs, w1)) * _dot(xs, w3)).astype(BF16)
        acc_ref[...] += _dot(hid, w2)

    @pl.when(f == pl.num_programs(1) - 1)
    def _():
        for gate_ref, ys_ref, acc_ref in zip(gate_refs, ys_refs, acc_refs):
            ys_ref[0] = (acc_ref[...] * gate_ref[0]).astype(BF16)


def _ffn(layer, xs_list, gate_list, w1, w3, w2):
    n_groups = len(xs_list)
    n_chunks = EXPERT_FF // FF_CHUNK
    xs_specs = [pl.BlockSpec((1,) + xs.shape[1:], lambda e, f: (e, 0, 0)) for xs in xs_list]
    gate_specs = [pl.BlockSpec((1,) + g.shape[1:], lambda e, f: (e, 0, 0)) for g in gate_list]
    return pl.pallas_call(
        functools.partial(_ffn_kernel, n_groups),
        grid=(N_EXPERTS, n_chunks),
        in_specs=xs_specs + gate_specs + [
            pl.BlockSpec((1, 1, D_MODEL, FF_CHUNK), lambda e, f: (layer, e, 0, f)),
            pl.BlockSpec((1, 1, D_MODEL, FF_CHUNK), lambda e, f: (layer, e, 0, f)),
            pl.BlockSpec((1, 1, FF_CHUNK, D_MODEL), lambda e, f: (layer, e, f, 0))],
        out_specs=xs_specs,
        out_shape=[jax.ShapeDtypeStruct(xs.shape, BF16) for xs in xs_list],
        scratch_shapes=[pltpu.VMEM(xs.shape[1:], F32) for xs in xs_list],
        compiler_params=_params("parallel", "arbitrary", vmem=FFN_VMEM_LIMIT),
        name="ffn",
    )(*xs_list, *gate_list, w1, w3, w2)


def _combine_kernel(cap, req_per_step, n_chunks, slot_ref, ys_ref, x_ref, mod_ref, o_ref):
    lane = lax.broadcasted_iota(jnp.int32, (1, LANES), 1).astype(F32)
    per_block = LANES // cap
    pad = jnp.zeros((LANES - N_EXPERTS, SUB_ROWS), F32)
    gate2 = mod_ref[0][5:6]
    for rq in range(req_per_step):
        experts = slice(rq * N_EXPERTS, (rq + 1) * N_EXPERTS)
        if n_chunks == 1:
            slots = slot_ref[experts, :]
        else:
            tokens = pl.multiple_of(pl.program_id(1) * SUB_ROWS, SUB_ROWS)
            slots = slot_ref[experts, pl.ds(tokens, SUB_ROWS)]
        slots_t = jnp.concatenate([slots, pad], axis=0).T
        blocks = []
        for k in range(N_EXPERTS // per_block):
            hit = None
            for e in range(k * per_block, (k + 1) * per_block):
                col = slots_t[:, e:e + 1]
                target = jnp.where(col >= 0.0, col + float((e - k * per_block) * cap), -1.0)
                he = jnp.where(target == lane, 1.0, 0.0)
                hit = he if hit is None else hit + he
            blocks.append(hit.astype(BF16))
        onehot = jnp.concatenate(blocks, axis=1)
        ys = jnp.concatenate([ys_ref[e, rq * cap:(rq + 1) * cap, :] for e in range(N_EXPERTS)], axis=0)
        rows = _sub(rq)
        o_ref[rows, :] = x_ref[rows, :] + gate2 * _dot(onehot, ys)


def _combine(latent, cap, batch, n_tokens, slots, ys, x1, mods):
    rps, n_chunks = _moe_grid(batch, n_tokens)
    mod_map = (lambda b, t: (1 + b, 0, 0)) if latent else (lambda b, t: (0, 0, 0))
    return pl.pallas_call(
        functools.partial(_combine_kernel, cap, rps, n_chunks),
        grid=(batch // rps, n_chunks),
        in_specs=[pl.BlockSpec((rps * N_EXPERTS, n_tokens), lambda b, t: (b, 0)),
                  pl.BlockSpec((N_EXPERTS, rps * cap, D_MODEL), lambda b, t: (0, b, 0)),
                  pl.BlockSpec((rps * SUB_ROWS, D_MODEL), lambda b, t: (b * n_chunks + t, 0)),
                  pl.BlockSpec((1, 6, D_MODEL), mod_map)],
        out_specs=pl.BlockSpec((rps * SUB_ROWS, D_MODEL), lambda b, t: (b * n_chunks + t, 0)),
        out_shape=jax.ShapeDtypeStruct(x1.shape, F32),
        compiler_params=_params("parallel", "arbitrary"),
        name="combine",
    )(slots, ys, x1, mods)


def _rope_tables(n_pos, rot_dim):
    t = jnp.arange(n_pos)
    row = t // GRID_W
    col = t % GRID_W
    nf = rot_dim // 4
    inv = ROPE_THETA ** (-jnp.arange(nf, dtype=F32) / nf)
    ang_r = row[:, None] * inv
    ang_c = col[:, None] * inv
    cr, sr, cc, sc = jnp.cos(ang_r), jnp.sin(ang_r), jnp.cos(ang_c), jnp.sin(ang_c)
    cos = jnp.concatenate([cr, cr, cc, cc], axis=1)
    sin = jnp.concatenate([-sr, sr, -sc, sc], axis=1)
    reps = LANES // rot_dim
    return jnp.tile(cos, (1, reps)), jnp.tile(sin, (1, reps))


def _group_mean_matrix(width):
    idx = jnp.arange(MXU_DIM) // width
    return jnp.where(idx[:, None] == idx[None, :], 1.0 / width, 0.0).astype(BF16)


def _tile_row(g, reps):
    return jnp.tile(g, reps).reshape(1, -1)


def _split_heads(w, n_heads, first):
    k = w.shape[0]
    w3 = w.reshape(k, n_heads, -1)
    return jnp.concatenate([w3[:, :, :first].reshape(k, -1), w3[:, :, first:].reshape(k, -1)], axis=1)


def _feature_major(cache):
    b, l, t, h, d = cache.shape
    return jnp.transpose(cache, (0, 1, 3, 4, 2)).reshape(b, l, h * d, t)


def kernel(x_prompt, x_sample, cache_mla_ckv, cache_mla_krope, cache_diff_k, cache_diff_v,
           cache_win_k, cache_win_v, cache_axial_k, cache_axial_v, c, c_ctx,
           g_norm, w_mod, b_mod, w_in_even, w_out_even, mla_g_qa, mla_g_kva, mla_w_uq, mla_w_ukv,
           mla_g_q, mla_g_k, diff_g_q, diff_g_k, diff_lambda, diff_g_sub,
           w_in_odd, w_out_odd, odd_g_qk, win_sink, moe_w_router, moe_w1, moe_w3, moe_w2):
    n_ctx_b, n_ctx = x_prompt.shape[:2]
    n_lat_b, n_lat = x_sample.shape[:2]
    past = cache_mla_ckv.shape[2]
    n_even, n_odd = w_in_even.shape[0], w_in_odd.shape[0]
    assert n_ctx == SUB_ROWS and n_lat % ROW_TILE == 0
    passes = [
        (False, n_ctx_b, n_ctx, EC_CAPACITY_FACTOR * n_ctx // N_EXPERTS),
        (True, n_lat_b, n_lat, EC_CAPACITY_FACTOR * n_lat // N_EXPERTS),
    ]

    cvec = jnp.zeros((16, D_MODEL), F32).at[0].set(c_ctx).at[1:1 + n_lat_b].set(c)
    mods = _adaln(cvec, w_mod, b_mod).reshape(DEPTH, 16, 6, D_MODEL)

    g64 = _group_mean_matrix(64)
    g32 = _group_mean_matrix(32)
    tab64 = _rope_tables(n_lat, HEAD_DIM)
    tab32 = _rope_tables(n_lat, MLA_ROPE)
    odd_caches = [_feature_major(a) for a in (cache_win_k, cache_win_v, cache_axial_k, cache_axial_v)]
    diff_caches = [a.reshape(n_lat_b, n_even, past * DIFF_HEADS, LANES) for a in (cache_diff_k, cache_diff_v)]

    even_states = [jnp.zeros((n_ctx_b, n_even, n_ctx, MLA_KV_LORA), F32),
                   jnp.zeros((n_ctx_b, n_even, MLA_ROPE, n_ctx), F32),
                   jnp.zeros((n_ctx_b, n_even, n_ctx * DIFF_HEADS, LANES), F32),
                   jnp.zeros((n_ctx_b, n_even, n_ctx * DIFF_HEADS, LANES), F32)]
    odd_states = [jnp.zeros((n_ctx_b, n_odd, LANES, n_ctx), F32) for _ in range(4)]

    xs_state = [x_prompt.reshape(-1, D_MODEL), x_sample.reshape(-1, D_MODEL)]

    for layer in range(DEPTH):
        j = layer // 2
        mods_l = mods[layer]
        g1 = g_norm[layer, 0].reshape(1, -1)
        g2 = g_norm[layer, 1].reshape(1, -1)
        attn_out = []
        if layer % 2 == 0:
            wt = jnp.swapaxes(w_in_even[j], 0, 1)
            wt = jnp.concatenate([wt[:384], wt[416:]] + [wt[384:416]] * 4, axis=0).astype(BF16)
            wuq = _split_heads(mla_w_uq[j], MLA_HEADS, MLA_NOPE).astype(BF16)
            wukv = _split_heads(mla_w_ukv[j], MLA_HEADS, MLA_NOPE).astype(BF16)
            w_out = w_out_even[j].astype(BF16)
            gkn = _tile_row(mla_g_k[j, :MLA_NOPE], MLA_HEADS)
            gains = [mla_g_qa[j].reshape(1, -1), mla_g_kva[j].reshape(1, -1),
                     _tile_row(mla_g_q[j, :MLA_NOPE], MLA_HEADS), _tile_row(mla_g_q[j, MLA_NOPE:], MLA_HEADS),
                     gkn, _tile_row(mla_g_k[j, MLA_NOPE:], 4),
                     _tile_row(diff_g_q[j], 2 * DIFF_HEADS), _tile_row(diff_g_k[j], 2 * DIFF_HEADS)]
            lam_init = 0.8 - 0.6 * math.exp(-0.3 * layer)
            gsub = diff_g_sub[j].reshape(1, -1)
            for p, (latent, batch, n_tok, cap) in enumerate(passes):
                outs = _in_even(latent, j, xs_state[p], mods_l, g1, wt, wuq, wukv, gains,
                                (g64, g32), tab64 + tab32, even_states, n_tok)
                qn, qr, kn, vm, krt, mq, mk, mv = outs[:8]
                cache_arrs = None
                if latent:
                    kn_c, vm_c = _cache_kv(cache_mla_ckv[:, j].reshape(-1, MLA_KV_LORA), wukv, gkn, g64)
                    cache_arrs = (kn_c, vm_c,
                                  jnp.tile(cache_mla_krope[:, j].reshape(-1, MLA_ROPE), (1, 4)).astype(BF16),
                                  diff_caches[0], diff_caches[1])
                else:
                    even_states = list(outs[8:])
                attn_out.append(_attn_even(latent, batch, n_tok, j, lam_init, (qn, qr, mq),
                                           (kn, vm, krt, mk, mv), cache_arrs, diff_lambda[j], gsub))
        else:
            w_in = w_in_odd[j].astype(BF16)
            w_out = w_out_odd[j].astype(BF16)
            gains = [_tile_row(odd_g_qk[j, 0], 8), _tile_row(odd_g_qk[j, 1], 2),
                     _tile_row(odd_g_qk[j, 2], 8), _tile_row(odd_g_qk[j, 3], 2)]
            for p, (latent, batch, n_tok, cap) in enumerate(passes):
                outs = _in_odd(latent, j, xs_state[p], mods_l, g1, w_in, gains, g64, tab64, odd_states, n_tok)
                if not latent:
                    odd_states = list(outs[2:])
                attn_out.append(_attn_odd(latent, batch, n_tok, j, outs[:2], outs[2:],
                                          odd_caches if latent else None, win_sink[j]))

        w_r = jnp.pad(moe_w_router[layer], ((0, 0), (0, LANES - N_EXPERTS)))
        wrh = w_r.astype(BF16)
        wrl = (w_r - wrh.astype(F32)).astype(BF16)
        x1s, slot_list, xs_list, gate_list = [], [], [], []
        for p, (latent, batch, n_tok, cap) in enumerate(passes):
            x1, h2, afft = _post(latent, n_tok, attn_out[p], w_out, xs_state[p], mods_l, g2, wrh, wrl)
            aff_rows = (afft.reshape(batch, n_tok // SUB_ROWS, N_EXPERTS, SUB_ROWS)
                        .transpose(0, 2, 1, 3).reshape(batch * N_EXPERTS, n_tok))
            slots = _route(cap, aff_rows)
            xs, gates = _gather(cap, batch, n_tok, slots, aff_rows, h2)
            x1s.append(x1)
            slot_list.append(slots)
            xs_list.append(xs)
            gate_list.append(gates)
        ys_list = _ffn(layer, xs_list, gate_list, moe_w1, moe_w3, moe_w2)
        for p, (latent, batch, n_tok, cap) in enumerate(passes):
            xs_state[p] = _combine(latent, cap, batch, n_tok, slot_list[p], ys_list[p], x1s[p], mods_l)

    y_prompt = xs_state[0].reshape(x_prompt.shape)
    y_sample = xs_state[1].reshape(x_sample.shape)

    def token_major(arr, heads):
        b, l, f, t = arr.shape
        return jnp.transpose(arr.reshape(b, l, heads, f // heads, t), (0, 1, 4, 2, 3))

    diff_shape = (n_ctx_b, n_even, n_ctx, DIFF_HEADS, LANES)
    return (y_prompt, y_sample,
            even_states[0],
            jnp.swapaxes(even_states[1], 2, 3),
            even_states[2].reshape(diff_shape),
            even_states[3].reshape(diff_shape),
            token_major(odd_states[0], 2), token_major(odd_states[1], 2),
            token_major(odd_states[2], 2), token_major(odd_states[3], 2))
```

```python
import functools
import math
from typing import NamedTuple

import jax
import jax.numpy as jnp
from jax import lax
from jax.experimental import pallas as pl
from jax.experimental.pallas import tpu as pltpu

F32 = jnp.float32
BF16 = jnp.bfloat16

D_MODEL = 1024
DEPTH = 4
GRID_W = 64
ROPE_THETA = 10000.0
WINDOW = 128
RMS_EPS = 1e-6
MLA_HEADS = 8
MLA_Q_LORA = 256
MLA_KV_LORA = 128
MLA_NOPE = 64
MLA_ROPE = 32
MLA_VD = 64
DIFF_HEADS = 4
DIFF_HD = 64
HEAD_DIM = 64
N_EXPERTS = 16
EC_CAPACITY_FACTOR = 2
EXPERT_FF = 2048

LANES = 128
MXU_DIM = 256
ROW_TILE = 512
SUB_ROWS = 256
N_SUB = ROW_TILE // SUB_ROWS
Q_TILE = 256
FF_CHUNK = 1024
VMEM_LIMIT = 50 * 1024 * 1024
FFN_VMEM_LIMIT = 60 * 1024 * 1024
NEG_BIG = -1e30
LOG2E = 1.4426950408889634


def _params(*sem, vmem=VMEM_LIMIT):
    return pltpu.CompilerParams(dimension_semantics=sem, vmem_limit_bytes=vmem)


def _full(shape):
    zeros = (0,) * len(shape)
    return pl.BlockSpec(shape, lambda *_: zeros)


def _dot(a, b):
    return jnp.dot(a, b, preferred_element_type=F32)


def _dot_nt(a, b):
    return lax.dot_general(a, b, (((1,), (1,)), ((), ())), preferred_element_type=F32)


def _rms(x, g):
    ms = jnp.mean(x * x, axis=-1, keepdims=True)
    return x * lax.rsqrt(ms + RMS_EPS) * g


def _group_rms(x, g, gmat):
    outs = []
    width = x.shape[1]
    for k in range(0, width, MXU_DIM):
        wd = min(MXU_DIM, width - k)
        xb = x[:, k:k + wd]
        ms = _dot((xb * xb).astype(BF16), gmat[:wd, :wd])
        outs.append(xb * lax.rsqrt(ms + RMS_EPS))
    y = outs[0] if len(outs) == 1 else jnp.concatenate(outs, axis=1)
    return y * g


def _rope(x, cos, sin, half):
    lane = lax.broadcasted_iota(jnp.int32, (1, LANES), 1)
    first = (lane % (2 * half)) < half
    outs = []
    for k in range(x.shape[1] // LANES):
        xb = x[:, k * LANES:(k + 1) * LANES]
        fwd = pltpu.roll(xb, LANES - half, 1)
        bwd = pltpu.roll(xb, half, 1)
        outs.append(xb * cos + jnp.where(first, fwd, bwd) * sin)
    return outs[0] if len(outs) == 1 else jnp.concatenate(outs, axis=1)


def _silu(a):
    return a / (1.0 + jnp.exp(-a))


def _sub(c):
    return slice(c * SUB_ROWS, (c + 1) * SUB_ROWS)


def _adaln_kernel(c_ref, w_ref, b_ref, o_ref):
    a = _silu(c_ref[...]).astype(BF16)
    bias = b_ref[pl.ds(pl.program_id(0), 1), :]
    o_ref[0] = _dot(a, w_ref[0].astype(BF16)) + bias


def _adaln(cvec, w_mod, b_mod):
    rows = cvec.shape[0]
    nc = 4
    wc = 6 * D_MODEL // nc
    return pl.pallas_call(
        _adaln_kernel,
        grid=(DEPTH, nc),
        in_specs=[_full((rows, D_MODEL)),
                  pl.BlockSpec((1, D_MODEL, wc), lambda l, n: (l, 0, n)),
                  pl.BlockSpec((DEPTH, wc), lambda l, n: (0, n))],
        out_specs=pl.BlockSpec((1, rows, wc), lambda l, n: (l, 0, n)),
        out_shape=jax.ShapeDtypeStruct((DEPTH, rows, 6 * D_MODEL), F32),
        compiler_params=_params("parallel", "parallel"),
        name="adaln",
    )(cvec, w_mod, b_mod)


class _Rows(NamedTuple):
    lat_b: int
    lat_n: int
    ctx_b: int
    ctx_n: int

    @property
    def lat_rows(self):
        return self.lat_b * self.lat_n

    @property
    def ctx_rows(self):
        return self.ctx_b * self.ctx_n

    @property
    def lat_tiles(self):
        return self.lat_rows // ROW_TILE

    @property
    def tiles(self):
        return (self.lat_rows + self.ctx_rows) // ROW_TILE

    @property
    def tiles_per_lat_req(self):
        return self.lat_n // ROW_TILE


class _TileSpecs:
    def __init__(self, rows):
        lat_tiles = rows.lat_tiles
        per_req = rows.tiles_per_lat_req
        self.lat_tiles = lat_tiles
        self.mod = pl.BlockSpec((1, 6, D_MODEL), lambda i: (jnp.where(i < lat_tiles, 1 + i // per_req, 0), 0, 0))
        self.table = pl.BlockSpec((ROW_TILE, LANES), lambda i: (jnp.where(i < lat_tiles, i % per_req, per_req), 0))

    def rows(self, width):
        return pl.BlockSpec((ROW_TILE, width), lambda i: (i, 0))

    def lat_rows(self, width):
        last = self.lat_tiles - 1
        return pl.BlockSpec((ROW_TILE, width), lambda i: (jnp.minimum(i, last), 0))

    def ctx_rows(self, width):
        first = self.lat_tiles
        return pl.BlockSpec((ROW_TILE, width), lambda i: (jnp.maximum(i - first, 0), 0))

    def chunks(self, feats):
        return pl.BlockSpec((N_SUB, feats, SUB_ROWS), lambda i: (i, 0, 0))

    def state(self, layer_j, feats, cols):
        first = self.lat_tiles
        return pl.BlockSpec((N_SUB, None, feats, cols), lambda i: (jnp.maximum(i - first, 0), layer_j, 0, 0))


def _pick_rows(lat_tiles, lat_ref, ctx_ref, r):
    return jnp.where(pl.program_id(0) < lat_tiles, lat_ref[r, :], ctx_ref[r, :])


def _modulated(x, mod, g, shift_row, scale_row):
    return _rms(x, g) * (1.0 + mod[scale_row:scale_row + 1]) + mod[shift_row:shift_row + 1]


def _in_even_kernel(lat_tiles, *refs):
    (xl_ref, xc_ref, mod_ref, g1_ref, wt_ref, gqa_ref, gkva_ref, wuq_ref, wukv_ref,
     gqn_ref, gqr_ref, gkn_ref, gkr_ref, gdq_ref, gdk_ref, g64_ref, g32_ref,
     c64_ref, s64_ref, c32_ref, s32_ref) = refs[:21]
    refs = refs[21 + 4:]
    qn_ref, qr_ref, kn_ref, vm_ref, krt_ref, mq_ref, mk_ref, mv_ref = refs[:8]
    ckv_s_ref, krt_s_ref, mk_s_ref, mv_s_ref = refs[8:]
    g64 = g64_ref[...]
    g32 = g32_ref[...]
    mod = mod_ref[0]
    mla_scale = (MLA_NOPE + MLA_ROPE) ** -0.5 * LOG2E
    diff_scale = DIFF_HD ** -0.5 * LOG2E

    for c in range(N_SUB):
        r = _sub(c)
        h = _modulated(_pick_rows(lat_tiles, xl_ref, xc_ref, r), mod, g1_ref[...], 0, 1)
        proj = _dot_nt(h.astype(BF16), wt_ref[...])
        cq = _rms(proj[:, 0:256], gqa_ref[...])
        q = _dot(cq.astype(BF16), wuq_ref[...])
        qn = _group_rms(q[:, 0:512], gqn_ref[...], g64)
        qr = _group_rms(q[:, 512:768], gqr_ref[...], g32)
        ckv = _rms(proj[:, 256:384], gkva_ref[...])
        kv = _dot(ckv.astype(BF16), wukv_ref[...])
        kn = _group_rms(kv[:, 0:512], gkn_ref[...], g64)
        mq = _group_rms(proj[:, 384:896], gdq_ref[...], g64)
        mk = _group_rms(proj[:, 896:1408], gdk_ref[...], g64)
        mv = proj[:, 1408:1920]
        kr = _group_rms(proj[:, 1920:2048], gkr_ref[...], g32)
        ckv_s_ref[c] = ckv
        krt_s_ref[c] = kr.T[0:MLA_ROPE]
        for hd in range(DIFF_HEADS):
            mk_s_ref[c, pl.ds(hd, SUB_ROWS, stride=DIFF_HEADS), :] = mk[:, hd * LANES:(hd + 1) * LANES]
            mv_s_ref[c, pl.ds(hd, SUB_ROWS, stride=DIFF_HEADS), :] = mv[:, hd * LANES:(hd + 1) * LANES]
        c64, s64, c32, s32 = c64_ref[r, :], s64_ref[r, :], c32_ref[r, :], s32_ref[r, :]
        qr = _rope(qr, c32, s32, MLA_ROPE // 4)
        kr = _rope(kr, c32, s32, MLA_ROPE // 4)
        mq = _rope(mq, c64, s64, DIFF_HD // 4)
        mk = _rope(mk, c64, s64, DIFF_HD // 4)
        qn_ref[r, :] = (qn * mla_scale).astype(BF16)
        qr_ref[r, :] = (qr * mla_scale).astype(BF16)
        kn_ref[r, :] = kn.astype(BF16)
        vm_ref[r, :] = kv[:, 512:1024].astype(BF16)
        krt_ref[r, :] = kr.astype(BF16)
        mq_ref[r, :] = (mq * diff_scale).astype(BF16)
        mk_ref[r, :] = mk.astype(BF16)
        mv_ref[r, :] = mv.astype(BF16)


def _in_odd_kernel(lat_tiles, *refs):
    (xl_ref, xc_ref, mod_ref, g1_ref, w_ref, gqc_ref, gkc_ref, gqd_ref, gkd_ref, g64_ref,
     c64_ref, s64_ref) = refs[:12]
    refs = refs[12 + 4:]
    qc_ref, qd_ref = refs[:2]
    chunk_refs = refs[2:6]
    state_refs = refs[6:10]
    g64 = g64_ref[...]
    mod = mod_ref[0]
    w = w_ref[...].astype(BF16)
    scale = HEAD_DIM ** -0.5 * LOG2E

    for c in range(N_SUB):
        r = _sub(c)
        h = _modulated(_pick_rows(lat_tiles, xl_ref, xc_ref, r), mod, g1_ref[...], 0, 1)
        proj = _dot(h.astype(BF16), w)
        qc = _group_rms(proj[:, 0:512], gqc_ref[...], g64)
        kc = _group_rms(proj[:, 512:640], gkc_ref[...], g64)
        qd = _group_rms(proj[:, 768:1280], gqd_ref[...], g64)
        kd = _group_rms(proj[:, 1280:1408], gkd_ref[...], g64)
        vct = proj[:, 640:768].T
        vdt = proj[:, 1408:1536].T
        state_refs[0][c] = kc.T
        state_refs[1][c] = vct
        state_refs[2][c] = kd.T
        state_refs[3][c] = vdt
        c64, s64 = c64_ref[r, :], s64_ref[r, :]
        qc = _rope(qc, c64, s64, HEAD_DIM // 4)
        kc = _rope(kc, c64, s64, HEAD_DIM // 4)
        qd = _rope(qd, c64, s64, HEAD_DIM // 4)
        kd = _rope(kd, c64, s64, HEAD_DIM // 4)
        qc_ref[r, :] = (qc * scale).astype(BF16)
        qd_ref[r, :] = (qd * scale).astype(BF16)
        chunk_refs[0][c] = kc.T
        chunk_refs[1][c] = vct
        chunk_refs[2][c] = kd.T
        chunk_refs[3][c] = vdt


def _with_states(specs, layer_j, in_specs, args, out_shape, out_specs, states):
    aliases = {}
    for st in states:
        aliases[len(args)] = len(out_shape)
        in_specs.append(pl.BlockSpec(memory_space=pl.ANY))
        args.append(st)
        out_shape.append(jax.ShapeDtypeStruct(st.shape, st.dtype))
        out_specs.append(specs.state(layer_j, st.shape[2], st.shape[3]))
    return aliases


def _in_even(rows, layer_j, x_lat, x_ctx, mods, g1, wt, wuq, wukv, gains, gmats, tables, states):
    specs = _TileSpecs(rows)
    n_rows = rows.lat_rows + rows.ctx_rows
    vec_specs = [_full(g.shape) for g in gains]
    in_specs = ([specs.lat_rows(D_MODEL), specs.ctx_rows(D_MODEL), specs.mod, _full(g1.shape), _full(wt.shape)]
                + vec_specs[:2] + [_full(wuq.shape), _full(wukv.shape)] + vec_specs[2:]
                + [_full((MXU_DIM, MXU_DIM))] * 2 + [specs.table] * 4)
    args = ([x_lat, x_ctx, mods, g1, wt, gains[0], gains[1], wuq, wukv] + list(gains[2:]) + list(gmats)
            + list(tables))
    widths = [512, 256, 512, 512, 128, 512, 512, 512]
    out_shape = [jax.ShapeDtypeStruct((n_rows, wd), BF16) for wd in widths]
    out_specs = [specs.rows(wd) for wd in widths]
    aliases = _with_states(specs, layer_j, in_specs, args, out_shape, out_specs, states)
    return pl.pallas_call(
        functools.partial(_in_even_kernel, rows.lat_tiles),
        grid=(rows.tiles,),
        in_specs=in_specs, out_specs=out_specs, out_shape=out_shape,
        input_output_aliases=aliases,
        compiler_params=_params("arbitrary"),
        name="in_even",
    )(*args)


def _in_odd(rows, layer_j, x_lat, x_ctx, mods, g1, w_all, gains, g64, tables, states):
    specs = _TileSpecs(rows)
    n_rows = rows.lat_rows + rows.ctx_rows
    w_spec = pl.BlockSpec((None,) + w_all.shape[1:], lambda i: (layer_j, 0, 0))
    in_specs = ([specs.lat_rows(D_MODEL), specs.ctx_rows(D_MODEL), specs.mod, _full(g1.shape), w_spec]
                + [_full(g.shape) for g in gains] + [_full((MXU_DIM, MXU_DIM))] + [specs.table] * 2)
    args = [x_lat, x_ctx, mods, g1, w_all] + list(gains) + [g64] + list(tables)
    out_shape = ([jax.ShapeDtypeStruct((n_rows, 512), BF16)] * 2
                 + [jax.ShapeDtypeStruct((n_rows // SUB_ROWS, LANES, SUB_ROWS), F32)] * 4)
    out_specs = [specs.rows(512), specs.rows(512)] + [specs.chunks(LANES)] * 4
    aliases = _with_states(specs, layer_j, in_specs, args, out_shape, out_specs, states)
    return pl.pallas_call(
        functools.partial(_in_odd_kernel, rows.lat_tiles),
        grid=(rows.tiles,),
        in_specs=in_specs, out_specs=out_specs, out_shape=out_shape,
        input_output_aliases=aliases,
        compiler_params=_params("arbitrary"),
        name="in_odd",
    )(*args)


def _cache_kv_kernel(ckv_ref, wukv_ref, gkn_ref, g64_ref, kn_ref, vm_ref):
    kv = _dot(ckv_ref[...].astype(BF16), wukv_ref[...])
    kn_ref[...] = _group_rms(kv[:, 0:512], gkn_ref[...], g64_ref[...]).astype(BF16)
    vm_ref[...] = kv[:, 512:1024].astype(BF16)


def _cache_kv(ckv, wukv, gkn, g64):
    rows = ckv.shape[0]
    tile = 512
    spec = lambda wd: pl.BlockSpec((tile, wd), lambda i: (i, 0))
    return pl.pallas_call(
        _cache_kv_kernel,
        grid=(rows // tile,),
        in_specs=[spec(MLA_KV_LORA), _full(wukv.shape), _full(gkn.shape), _full((MXU_DIM, MXU_DIM))],
        out_specs=[spec(512), spec(512)],
        out_shape=[jax.ShapeDtypeStruct((rows, 512), BF16)] * 2,
        compiler_params=_params("parallel"),
        name="cache_kv",
    )(ckv, wukv, gkn, g64)


def _softmax_pv(s_parts, v_parts, v_is_feature_major, sink=None):
    m = None
    for s in s_parts:
        mi = jnp.max(s, axis=-1, keepdims=True)
        m = mi if m is None else jnp.maximum(m, mi)
    if sink is not None:
        m = jnp.maximum(m, sink)
    acc = None
    for s, v in zip(s_parts, v_parts):
        p = jnp.exp2(s - m).astype(BF16)
        oi = _dot_nt(p, v) if v_is_feature_major else _dot(p, v)
        acc = oi if acc is None else acc + oi
    denom = acc[:, LANES:]
    if sink is not None:
        denom = denom + jnp.exp2(sink - m)
    return acc[:, :LANES] / denom


def _lane_masks(width):
    lane = lax.broadcasted_iota(jnp.int32, (1, LANES), 1)
    return [jnp.where(lane // width == k, 1.0, 0.0).astype(BF16) for k in range(LANES // width)]


def _attn_even_kernel(latent, lam_init, *refs):
    qn_ref, qr_ref, mq_ref = refs[:3]
    refs = refs[3:]
    n_parts = 2 if latent else 1
    parts = [refs[5 * k:5 * k + 5] for k in range(n_parts)]
    lamv_ref, gsub_ref, o_ref = refs[5 * n_parts:]
    lane = lax.broadcasted_iota(jnp.int32, (1, LANES), 1)
    low = lane < 64
    m64 = _lane_masks(64)
    m32 = _lane_masks(32)
    ones = [jnp.ones((p[0].shape[0], LANES), BF16) for p in parts]

    def diff_head(ref, idx, hd):
        if latent and idx == 0:
            n_keys = ref.shape[0] // DIFF_HEADS
            return ref[pl.ds(hd, n_keys, stride=DIFF_HEADS), :].astype(BF16)
        return ref[:, hd * LANES:(hd + 1) * LANES]

    for i in range(MLA_HEADS // 2):
        cols = slice(i * LANES, (i + 1) * LANES)
        qn_b = qn_ref[:, cols]
        kcat = [jnp.concatenate([p[0][:, cols], p[2][...]], axis=1) for p in parts]
        vms = [jnp.concatenate([p[1][:, cols], one], axis=1) for p, one in zip(parts, ones)]
        outs = []
        for half in range(2):
            head = 2 * i + half
            rb = head // 4
            qr_b = qr_ref[:, rb * LANES:(rb + 1) * LANES]
            lhs = jnp.concatenate([qn_b * m64[half], qr_b * m32[head % 4]], axis=1)
            outs.append(_softmax_pv([_dot_nt(lhs, kc) for kc in kcat], vms, False))
        o_ref[:, cols] = jnp.where(low, outs[0], outs[1]).astype(BF16)

    lv = lamv_ref[...]
    lam = (jnp.exp(jnp.sum(lv[0:1] * lv[1:2], axis=-1, keepdims=True))
           - jnp.exp(jnp.sum(lv[2:3] * lv[3:4], axis=-1, keepdims=True)) + lam_init)
    gsub = gsub_ref[...]
    for hd in range(DIFF_HEADS):
        q_b = mq_ref[:, hd * LANES:(hd + 1) * LANES]
        ks = [diff_head(p[3], idx, hd) for idx, p in enumerate(parts)]
        vs = [jnp.concatenate([diff_head(p[4], idx, hd), one], axis=1)
              for idx, (p, one) in enumerate(zip(parts, ones))]
        a1 = _softmax_pv([_dot_nt(q_b * m64[0], k) for k in ks], vs, False)
        a2 = _softmax_pv([_dot_nt(q_b * m64[1], k) for k in ks], vs, False)
        d = _rms(a1 - lam * a2, gsub) * (1.0 - lam_init)
        o_ref[:, 512 + hd * LANES:512 + (hd + 1) * LANES] = d.astype(BF16)


def _attn_even(latent, rows, layer_j, lam_init, q_arrs, kv_arrs, cache_arrs, lamv, gsub):
    batch, n_tokens, row0 = (rows.lat_b, rows.lat_n, 0) if latent else (rows.ctx_b, rows.ctx_n, rows.lat_rows)
    tq = min(Q_TILE, n_tokens)
    nt = n_tokens // tq
    q0 = row0 // tq
    k0 = row0 // n_tokens
    qspec = lambda wd: pl.BlockSpec((tq, wd), lambda b, t: (q0 + b * nt + t, 0))
    kspec = lambda wd: pl.BlockSpec((n_tokens, wd), lambda b, t: (k0 + b, 0))
    kwidths = [512, 512, 128, 512, 512]
    in_specs = [qspec(512), qspec(256), qspec(512)]
    args = list(q_arrs)
    if latent:
        past = cache_arrs[0].shape[0] // batch
        in_specs += [pl.BlockSpec((past, wd), lambda b, t: (b, 0)) for wd in kwidths[:3]]
        in_specs += [pl.BlockSpec((None, None, past * DIFF_HEADS, LANES), lambda b, t: (b, layer_j, 0, 0))] * 2
        args += list(cache_arrs)
    in_specs += [kspec(wd) for wd in kwidths]
    args += list(kv_arrs)
    in_specs += [_full(lamv.shape), _full(gsub.shape)]
    args += [lamv, gsub]
    return pl.pallas_call(
        functools.partial(_attn_even_kernel, latent, lam_init),
        grid=(batch, nt),
        in_specs=in_specs,
        out_specs=pl.BlockSpec((tq, D_MODEL), lambda b, t: (b * nt + t, 0)),
        out_shape=jax.ShapeDtypeStruct((batch * n_tokens, D_MODEL), BF16),
        compiler_params=_params("parallel", "parallel"),
        name="attn_even_lat" if latent else "attn_even_ctx",
    )(*args)


def _attn_odd_kernel(latent, tq, *refs):
    qc_ref, qd_ref = refs[:2]
    refs = refs[2:]
    n_parts = 2 if latent else 1
    parts = [list(refs[4 * k:4 * k + 4]) for k in range(n_parts)]
    sink_ref, o_ref = refs[4 * n_parts:4 * n_parts + 2]
    scratch = refs[4 * n_parts + 2:]
    lane = lax.broadcasted_iota(jnp.int32, (1, LANES), 1)
    low = lane < 64
    m64 = _lane_masks(64)

    band = None
    start = 0
    span = 0
    if latent:
        for ref, scr in zip(parts[-1], scratch):
            for c in range(ref.shape[0]):
                scr[:, _sub(c)] = ref[c]
        parts[-1] = list(scratch)
        n_new = scratch[0].shape[1]
        span = min(n_new, tq + 2 * WINDOW)
        t = pl.program_id(1)
        start = pl.multiple_of(jnp.clip(t * tq - WINDOW, 0, n_new - span), LANES)
        qpos = t * tq + lax.broadcasted_iota(jnp.int32, (tq, 1), 0)
        kpos = start + lax.broadcasted_iota(jnp.int32, (1, span), 1)
        band = jnp.abs(qpos - kpos) <= WINDOW

    for kind in range(2):
        q_ref = qc_ref if kind == 0 else qd_ref
        windowed = latent and kind == 0
        for g in range(2):
            rows = slice(g * HEAD_DIM, (g + 1) * HEAD_DIM)
            ks, vs = [], []
            for idx, p in enumerate(parts):
                k_ref, v_ref = p[2 * kind], p[2 * kind + 1]
                if windowed and idx == n_parts - 1:
                    k = k_ref[rows, pl.ds(start, span)]
                    v = v_ref[rows, pl.ds(start, span)]
                else:
                    k = k_ref[rows, :]
                    v = v_ref[rows, :]
                ks.append(jnp.concatenate([k, k], axis=0).astype(BF16))
                vs.append(jnp.concatenate([v, v, jnp.ones((LANES, v.shape[1]), F32)], axis=0).astype(BF16))
            for i in (2 * g, 2 * g + 1):
                cols = slice(i * LANES, (i + 1) * LANES)
                q_b = q_ref[:, cols]
                outs = []
                for half in range(2):
                    s_parts = [_dot(q_b * m64[half], k) for k in ks]
                    sink = None
                    if kind == 0:
                        sink = sink_ref[2 * i + half] * LOG2E
                        if latent:
                            s_parts[-1] = jnp.where(band, s_parts[-1], NEG_BIG)
                    outs.append(_softmax_pv(s_parts, vs, True, sink))
                o_ref[:, kind * 512 + i * LANES:kind * 512 + (i + 1) * LANES] = (
                    jnp.where(low, outs[0], outs[1]).astype(BF16))


def _attn_odd(latent, rows, layer_j, q_arrs, new_arrs, cache_arrs, sink):
    batch, n_tokens, row0 = (rows.lat_b, rows.lat_n, 0) if latent else (rows.ctx_b, rows.ctx_n, rows.lat_rows)
    tq = min(Q_TILE, n_tokens)
    nt = n_tokens // tq
    q0 = row0 // tq
    in_specs = [pl.BlockSpec((tq, 512), lambda b, t: (q0 + b * nt + t, 0))] * 2
    args = list(q_arrs)
    scratch = []
    if latent:
        past = cache_arrs[0].shape[-1]
        in_specs += [pl.BlockSpec((None, None, LANES, past), lambda b, t: (b, layer_j, 0, 0))] * 4
        args += list(cache_arrs)
        chunks = n_tokens // SUB_ROWS
        in_specs += [pl.BlockSpec((chunks, LANES, SUB_ROWS), lambda b, t: (b, 0, 0))] * 4
        scratch = [pltpu.VMEM((LANES, n_tokens), F32)] * 4
    else:
        in_specs += [pl.BlockSpec((None, None, LANES, n_tokens), lambda b, t: (b, layer_j, 0, 0))] * 4
    args += list(new_arrs)
    in_specs += [pl.BlockSpec(memory_space=pltpu.SMEM)]
    args += [sink]
    return pl.pallas_call(
        functools.partial(_attn_odd_kernel, latent, tq),
        grid=(batch, nt),
        in_specs=in_specs,
        out_specs=pl.BlockSpec((tq, D_MODEL), lambda b, t: (b * nt + t, 0)),
        out_shape=jax.ShapeDtypeStruct((batch * n_tokens, D_MODEL), BF16),
        scratch_shapes=scratch,
        compiler_params=_params("parallel", "parallel"),
        name="attn_odd_lat" if latent else "attn_odd_ctx",
    )(*args)


def _post_kernel(lat_tiles, ol_ref, oc_ref, wout_ref, xl_ref, xc_ref, mod_ref, g2_ref, wr2_ref, wrh_ref,
                 x1_ref, h2_ref, afft_ref):
    mod = mod_ref[0]
    wout = wout_ref[...].astype(BF16)
    lane = lax.broadcasted_iota(jnp.int32, (1, LANES), 1)
    for c in range(N_SUB):
        r = _sub(c)
        o = _pick_rows(lat_tiles, ol_ref, oc_ref, r)
        x1 = _pick_rows(lat_tiles, xl_ref, xc_ref, r) + mod[2:3] * _dot(o, wout)
        x1_ref[r, :] = x1
        h2 = _modulated(x1, mod, g2_ref[...], 3, 4)
        h_hi = h2.astype(BF16)
        h_lo = (h2 - h_hi.astype(F32)).astype(BF16)
        h2_ref[r, :] = h_hi
        both = _dot(h_hi, wr2_ref[...])
        logits = both[:, :LANES] + both[:, LANES:] + _dot(h_lo, wrh_ref[...])
        logits = jnp.where(lane < N_EXPERTS, logits, NEG_BIG)
        e = jnp.exp(logits - jnp.max(logits, axis=-1, keepdims=True))
        aff = e / jnp.sum(e, axis=-1, keepdims=True)
        afft_ref[c] = aff.T[0:N_EXPERTS]


def _post(rows, layer_j, o_lat, o_ctx, wout_all, x_lat, x_ctx, mods, g2, wr2, wrh):
    specs = _TileSpecs(rows)
    n_rows = rows.lat_rows + rows.ctx_rows
    w_spec = pl.BlockSpec((None,) + wout_all.shape[1:], lambda i: (layer_j, 0, 0))
    return pl.pallas_call(
        functools.partial(_post_kernel, rows.lat_tiles),
        grid=(rows.tiles,),
        in_specs=[specs.lat_rows(D_MODEL), specs.ctx_rows(D_MODEL), w_spec,
                  specs.lat_rows(D_MODEL), specs.ctx_rows(D_MODEL), specs.mod, _full(g2.shape),
                  _full(wr2.shape), _full(wrh.shape)],
        out_specs=[specs.rows(D_MODEL), specs.rows(D_MODEL), specs.chunks(N_EXPERTS)],
        out_shape=[jax.ShapeDtypeStruct((n_rows, D_MODEL), F32),
                   jax.ShapeDtypeStruct((n_rows, D_MODEL), BF16),
                   jax.ShapeDtypeStruct((n_rows // SUB_ROWS, N_EXPERTS, SUB_ROWS), F32)],
        compiler_params=_params("parallel"),
        name="post",
    )(o_lat, o_ctx, wout_all, x_lat, x_ctx, mods, g2, wr2, wrh)


def _route_kernel(caps, *refs):
    n_groups = len(caps)
    aff_refs, out_refs = refs[:n_groups], refs[n_groups:]
    bits = [lax.bitcast_convert_type(ref[...], jnp.int32) for ref in aff_refs]

    def body(_, carry):
        new = []
        for b, cap, (lo, hi) in zip(bits, caps, carry):
            mid = lo + ((hi - lo + 1) >> 1)
            cnt = jnp.sum(jnp.where(b >= mid, 1.0, 0.0), axis=-1, keepdims=True)
            ok = cnt >= cap
            new.append((jnp.where(ok, mid, lo), jnp.where(ok, hi, mid - 1)))
        return tuple(new)

    init = tuple((jnp.zeros((b.shape[0], 1), jnp.int32), jnp.full((b.shape[0], 1), 0x7F800000, jnp.int32))
                 for b in bits)
    found = lax.fori_loop(0, 31, body, init)
    for b, cap, (thr, _), out_ref in zip(bits, caps, found, out_refs):
        n = b.shape[1]
        above = jnp.where(b > thr, 1.0, 0.0)
        equal = jnp.where(b == thr, 1.0, 0.0)
        room = cap - jnp.sum(above, axis=-1, keepdims=True)
        before = jnp.where(lax.broadcasted_iota(jnp.int32, (n, n), 0)
                           < lax.broadcasted_iota(jnp.int32, (n, n), 1), 1.0, 0.0).astype(BF16)
        equal_before = _dot(equal.astype(BF16), before)
        chosen = above + equal * jnp.where(equal_before < room, 1.0, 0.0)
        slot = _dot(chosen.astype(BF16), before)
        out_ref[...] = jnp.where(chosen > 0.5, slot, -1.0)


def _route(caps, aff_list):
    return pl.pallas_call(
        functools.partial(_route_kernel, tuple(caps)),
        grid=(1,),
        in_specs=[_full(a.shape) for a in aff_list],
        out_specs=[_full(a.shape) for a in aff_list],
        out_shape=[jax.ShapeDtypeStruct(a.shape, F32) for a in aff_list],
        compiler_params=_params("arbitrary"),
        name="route",
    )(*aff_list)


def _gather_part(cap, req_per_step, experts_per_step, first, slot_ref, aff_ref, h_ref, xs_ref, gate_ref):
    n = slot_ref.shape[1]
    want = lax.broadcasted_iota(jnp.int32, (cap, 1), 0).astype(F32)
    for rq in range(req_per_step):
        out_rows = slice(rq * cap, (rq + 1) * cap)
        rows = []
        for k in range(experts_per_step):
            row = rq * N_EXPERTS + first + k
            hit = slot_ref[pl.ds(row, 1), :] == want
            rows.append(jnp.where(hit, 1.0, 0.0).astype(BF16))
            gate_ref[k, out_rows, :] = jnp.sum(jnp.where(hit, aff_ref[pl.ds(row, 1), :], 0.0),
                                               axis=-1, keepdims=True)
        picked = _dot(jnp.concatenate(rows, axis=0), h_ref[rq * n:(rq + 1) * n, :]).astype(BF16)
        for k in range(experts_per_step):
            xs_ref[k, out_rows, :] = picked[k * cap:(k + 1) * cap]


def _gather_kernel(lat_steps, groups, caps, ctx_req_per_step,
                   sl_l, af_l, h_l, sl_c, af_c, h_c, xs_l, gt_l, xs_c, gt_c):
    s = pl.program_id(0)
    per_group = N_EXPERTS // groups

    @pl.when(s < lat_steps)
    def _():
        _gather_part(caps[0], 1, per_group, (s % groups) * per_group, sl_l, af_l, h_l, xs_l, gt_l)

    @pl.when(s >= lat_steps)
    def _():
        _gather_part(caps[1], ctx_req_per_step, N_EXPERTS, 0, sl_c, af_c, h_c, xs_c, gt_c)


def _gather(rows, caps, slots, affs, h2):
    groups = rows.lat_n // SUB_ROWS
    per_group = N_EXPERTS // groups
    rps = ROW_TILE // rows.ctx_n
    lat_steps = rows.lat_b * groups
    ctx_steps = rows.ctx_b // rps
    last_b = rows.lat_b - 1
    ctx_block0 = rows.lat_rows // (rps * rows.ctx_n)
    lat_req = lambda s: jnp.minimum(s // groups, last_b)
    lat_grp = lambda s: jnp.where(s < lat_steps, s % groups, groups - 1)
    ctx_step = lambda s: jnp.maximum(s - lat_steps, 0)
    cap_l, cap_c = caps
    lat_rows_spec = pl.BlockSpec((N_EXPERTS, rows.lat_n), lambda s: (lat_req(s), 0))
    ctx_rows_spec = pl.BlockSpec((rps * N_EXPERTS, rows.ctx_n), lambda s: (ctx_step(s), 0))
    return pl.pallas_call(
        functools.partial(_gather_kernel, lat_steps, groups, caps, rps),
        grid=(lat_steps + ctx_steps,),
        in_specs=[lat_rows_spec, lat_rows_spec,
                  pl.BlockSpec((rows.lat_n, D_MODEL), lambda s: (lat_req(s), 0)),
                  ctx_rows_spec, ctx_rows_spec,
                  pl.BlockSpec((rps * rows.ctx_n, D_MODEL), lambda s: (ctx_block0 + ctx_step(s), 0))],
        out_specs=[pl.BlockSpec((per_group, cap_l, D_MODEL), lambda s: (lat_grp(s), lat_req(s), 0)),
                   pl.BlockSpec((per_group, cap_l, 1), lambda s: (lat_grp(s), lat_req(s), 0)),
                   pl.BlockSpec((N_EXPERTS, rps * cap_c, D_MODEL), lambda s: (0, ctx_step(s), 0)),
                   pl.BlockSpec((N_EXPERTS, rps * cap_c, 1), lambda s: (0, ctx_step(s), 0))],
        out_shape=[jax.ShapeDtypeStruct((N_EXPERTS, rows.lat_b * cap_l, D_MODEL), BF16),
                   jax.ShapeDtypeStruct((N_EXPERTS, rows.lat_b * cap_l, 1), F32),
                   jax.ShapeDtypeStruct((N_EXPERTS, rows.ctx_b * cap_c, D_MODEL), BF16),
                   jax.ShapeDtypeStruct((N_EXPERTS, rows.ctx_b * cap_c, 1), F32)],
        compiler_params=_params("arbitrary"),
        name="gather",
    )(slots[0], affs[0], h2, slots[1], affs[1], h2)


def _ffn_kernel(n_groups, *refs):
    xs_refs = refs[:n_groups]
    gate_refs = refs[n_groups:2 * n_groups]
    w1_ref, w3_ref, w2_ref = refs[2 * n_groups:2 * n_groups + 3]
    ys_refs = refs[2 * n_groups + 3:3 * n_groups + 3]
    acc_refs = refs[3 * n_groups + 3:]
    f = pl.program_id(1)

    @pl.when(f == 0)
    def _():
        for acc_ref in acc_refs:
            acc_ref[...] = jnp.zeros(acc_ref.shape, F32)

    w1 = w1_ref[0, 0].astype(BF16)
    w3 = w3_ref[0, 0].astype(BF16)
    w2 = w2_ref[0, 0].astype(BF16)
    for xs_ref, acc_ref in zip(xs_refs, acc_refs):
        xs = xs_ref[0]
        hid = (_silu(_dot(xs, w1)) * _dot(xs, w3)).astype(BF16)
        acc_ref[...] += _dot(hid, w2)

    @pl.when(f == pl.num_programs(1) - 1)
    def _():
        for gate_ref, ys_ref, acc_ref in zip(gate_refs, ys_refs, acc_refs):
            ys_ref[0] = (acc_ref[...] * gate_ref[0]).astype(BF16)


def _ffn(layer, xs_list, gate_list, w1, w3, w2):
    n_groups = len(xs_list)
    n_chunks = EXPERT_FF // FF_CHUNK
    xs_specs = [pl.BlockSpec((1,) + xs.shape[1:], lambda e, f: (e, 0, 0)) for xs in xs_list]
    gate_specs = [pl.BlockSpec((1,) + g.shape[1:], lambda e, f: (e, 0, 0)) for g in gate_list]
    return pl.pallas_call(
        functools.partial(_ffn_kernel, n_groups),
        grid=(N_EXPERTS, n_chunks),
        in_specs=xs_specs + gate_specs + [
            pl.BlockSpec((1, 1, D_MODEL, FF_CHUNK), lambda e, f: (layer, e, 0, f)),
            pl.BlockSpec((1, 1, D_MODEL, FF_CHUNK), lambda e, f: (layer, e, 0, f)),
            pl.BlockSpec((1, 1, FF_CHUNK, D_MODEL), lambda e, f: (layer, e, f, 0))],
        out_specs=xs_specs,
        out_shape=[jax.ShapeDtypeStruct(xs.shape, BF16) for xs in xs_list],
        scratch_shapes=[pltpu.VMEM(xs.shape[1:], F32) for xs in xs_list],
        compiler_params=_params("parallel", "arbitrary", vmem=FFN_VMEM_LIMIT),
        name="ffn",
    )(*xs_list, *gate_list, w1, w3, w2)


def _scatter_part(cap, slots, ys_ref, x, gate2):
    lane = lax.broadcasted_iota(jnp.int32, (1, LANES), 1).astype(F32)
    per_block = LANES // cap
    pad = jnp.zeros((LANES - N_EXPERTS, SUB_ROWS), F32)
    slots_t = jnp.concatenate([slots, pad], axis=0).T
    blocks = []
    for k in range(N_EXPERTS // per_block):
        hit = None
        for e in range(k * per_block, (k + 1) * per_block):
            col = slots_t[:, e:e + 1]
            target = jnp.where(col >= 0.0, col + float((e - k * per_block) * cap), -1.0)
            he = jnp.where(target == lane, 1.0, 0.0)
            hit = he if hit is None else hit + he
        blocks.append(hit.astype(BF16))
    onehot = jnp.concatenate(blocks, axis=1)
    ys = jnp.concatenate([ys_ref[e] for e in range(N_EXPERTS)], axis=0)
    return x + gate2 * _dot(onehot, ys)


def _combine_kernel(lat_steps, chunks_per_req, caps, sl_l, ys_l, sl_c, ys_c, x_ref, mod_ref, ol_ref, oc_ref):
    s = pl.program_id(0)
    gate2 = mod_ref[0][5:6]

    @pl.when(s < lat_steps)
    def _():
        tokens = pl.multiple_of((s % chunks_per_req) * SUB_ROWS, SUB_ROWS)
        ol_ref[...] = _scatter_part(caps[0], sl_l[:, pl.ds(tokens, SUB_ROWS)], ys_l, x_ref[...], gate2)

    @pl.when(s >= lat_steps)
    def _():
        oc_ref[...] = _scatter_part(caps[1], sl_c[...], ys_c, x_ref[...], gate2)


def _combine(rows, caps, slots, ys, x1, mods):
    per_req = rows.lat_n // SUB_ROWS
    lat_steps = rows.lat_b * per_req
    last_b = rows.lat_b - 1
    lat_req = lambda s: jnp.minimum(s // per_req, last_b)
    ctx_req = lambda s: jnp.maximum(s - lat_steps, 0)
    cap_l, cap_c = caps
    return pl.pallas_call(
        functools.partial(_combine_kernel, lat_steps, per_req, caps),
        grid=(lat_steps + rows.ctx_b,),
        in_specs=[pl.BlockSpec((N_EXPERTS, rows.lat_n), lambda s: (lat_req(s), 0)),
                  pl.BlockSpec((N_EXPERTS, cap_l, D_MODEL), lambda s: (0, lat_req(s), 0)),
                  pl.BlockSpec((N_EXPERTS, rows.ctx_n), lambda s: (ctx_req(s), 0)),
                  pl.BlockSpec((N_EXPERTS, cap_c, D_MODEL), lambda s: (0, ctx_req(s), 0)),
                  pl.BlockSpec((SUB_ROWS, D_MODEL), lambda s: (s, 0)),
                  pl.BlockSpec((1, 6, D_MODEL), lambda s: (jnp.where(s < lat_steps, 1 + s // per_req, 0), 0, 0))],
        out_specs=[pl.BlockSpec((SUB_ROWS, D_MODEL), lambda s: (jnp.minimum(s, lat_steps - 1), 0)),
                   pl.BlockSpec((SUB_ROWS, D_MODEL), lambda s: (ctx_req(s), 0))],
        out_shape=[jax.ShapeDtypeStruct((rows.lat_rows, D_MODEL), F32),
                   jax.ShapeDtypeStruct((rows.ctx_rows, D_MODEL), F32)],
        compiler_params=_params("arbitrary"),
        name="combine",
    )(slots[0], ys[0], slots[1], ys[1], x1, mods)


def _rope_tables(n_pos, rot_dim):
    t = jnp.arange(n_pos)
    row = t // GRID_W
    col = t % GRID_W
    nf = rot_dim // 4
    inv = ROPE_THETA ** (-jnp.arange(nf, dtype=F32) / nf)
    ang_r = row[:, None] * inv
    ang_c = col[:, None] * inv
    cr, sr, cc, sc = jnp.cos(ang_r), jnp.sin(ang_r), jnp.cos(ang_c), jnp.sin(ang_c)
    cos = jnp.concatenate([cr, cr, cc, cc], axis=1)
    sin = jnp.concatenate([-sr, sr, -sc, sc], axis=1)
    reps = LANES // rot_dim
    cos = jnp.concatenate([jnp.tile(cos, (1, reps)), jnp.ones((ROW_TILE, LANES), F32)], axis=0)
    sin = jnp.concatenate([jnp.tile(sin, (1, reps)), jnp.zeros((ROW_TILE, LANES), F32)], axis=0)
    return cos, sin


def _group_mean_matrix(width):
    idx = jnp.arange(MXU_DIM) // width
    return jnp.where(idx[:, None] == idx[None, :], 1.0 / width, 0.0).astype(BF16)


def _tile_row(g, reps):
    return jnp.tile(g, reps).reshape(1, -1)


def _split_heads(w, n_heads, first):
    k = w.shape[0]
    w3 = w.reshape(k, n_heads, -1)
    return jnp.concatenate([w3[:, :, :first].reshape(k, -1), w3[:, :, first:].reshape(k, -1)], axis=1)


def _feature_major(cache):
    b, l, t, h, d = cache.shape
    return jnp.transpose(cache, (0, 1, 3, 4, 2)).reshape(b, l, h * d, t)


def kernel(x_prompt, x_sample, cache_mla_ckv, cache_mla_krope, cache_diff_k, cache_diff_v,
           cache_win_k, cache_win_v, cache_axial_k, cache_axial_v, c, c_ctx,
           g_norm, w_mod, b_mod, w_in_even, w_out_even, mla_g_qa, mla_g_kva, mla_w_uq, mla_w_ukv,
           mla_g_q, mla_g_k, diff_g_q, diff_g_k, diff_lambda, diff_g_sub,
           w_in_odd, w_out_odd, odd_g_qk, win_sink, moe_w_router, moe_w1, moe_w3, moe_w2):
    rows = _Rows(lat_b=x_sample.shape[0], lat_n=x_sample.shape[1], ctx_b=x_prompt.shape[0], ctx_n=x_prompt.shape[1])
    past = cache_mla_ckv.shape[2]
    n_even, n_odd = w_in_even.shape[0], w_in_odd.shape[0]
    assert rows.ctx_n == SUB_ROWS and rows.lat_n % ROW_TILE == 0 and rows.ctx_b % N_SUB == 0
    caps = (EC_CAPACITY_FACTOR * rows.lat_n // N_EXPERTS, EC_CAPACITY_FACTOR * rows.ctx_n // N_EXPERTS)

    cvec = jnp.zeros((16, D_MODEL), F32).at[0].set(c_ctx).at[1:1 + rows.lat_b].set(c)
    mods = _adaln(cvec, w_mod, b_mod).reshape(DEPTH, 16, 6, D_MODEL)

    g64 = _group_mean_matrix(64)
    g32 = _group_mean_matrix(32)
    tab64 = _rope_tables(rows.lat_n, HEAD_DIM)
    tab32 = _rope_tables(rows.lat_n, MLA_ROPE)
    odd_caches = [_feature_major(a) for a in (cache_win_k, cache_win_v, cache_axial_k, cache_axial_v)]
    diff_caches = [a.reshape(rows.lat_b, n_even, past * DIFF_HEADS, LANES) for a in (cache_diff_k, cache_diff_v)]

    even_states = [jnp.zeros((rows.ctx_b, n_even, rows.ctx_n, MLA_KV_LORA), F32),
                   jnp.zeros((rows.ctx_b, n_even, MLA_ROPE, rows.ctx_n), F32),
                   jnp.zeros((rows.ctx_b, n_even, rows.ctx_n * DIFF_HEADS, LANES), F32),
                   jnp.zeros((rows.ctx_b, n_even, rows.ctx_n * DIFF_HEADS, LANES), F32)]
    odd_states = [jnp.zeros((rows.ctx_b, n_odd, LANES, rows.ctx_n), F32) for _ in range(4)]

    x_lat = x_sample.reshape(-1, D_MODEL)
    x_ctx = x_prompt.reshape(-1, D_MODEL)
    lat_chunks = rows.lat_rows // SUB_ROWS

    for layer in range(DEPTH):
        j = layer // 2
        mods_l = mods[layer]
        g1 = g_norm[layer, 0].reshape(1, -1)
        g2 = g_norm[layer, 1].reshape(1, -1)
        if layer % 2 == 0:
            wt = jnp.swapaxes(w_in_even[j], 0, 1)
            wt = jnp.concatenate([wt[:384], wt[416:]] + [wt[384:416]] * 4, axis=0).astype(BF16)
            wuq = _split_heads(mla_w_uq[j], MLA_HEADS, MLA_NOPE).astype(BF16)
            wukv = _split_heads(mla_w_ukv[j], MLA_HEADS, MLA_NOPE).astype(BF16)
            w_out = w_out_even
            gkn = _tile_row(mla_g_k[j, :MLA_NOPE], MLA_HEADS)
            gains = [mla_g_qa[j].reshape(1, -1), mla_g_kva[j].reshape(1, -1),
                     _tile_row(mla_g_q[j, :MLA_NOPE], MLA_HEADS), _tile_row(mla_g_q[j, MLA_NOPE:], MLA_HEADS),
                     gkn, _tile_row(mla_g_k[j, MLA_NOPE:], 4),
                     _tile_row(diff_g_q[j], 2 * DIFF_HEADS), _tile_row(diff_g_k[j], 2 * DIFF_HEADS)]
            lam_init = 0.8 - 0.6 * math.exp(-0.3 * layer)
            gsub = diff_g_sub[j].reshape(1, -1)
            outs = _in_even(rows, j, x_lat, x_ctx, mods_l, g1, wt, wuq, wukv, gains, (g64, g32),
                            tab64 + tab32, even_states)
            qn, qr, kn, vm, krt, mq, mk, mv = outs[:8]
            even_states = list(outs[8:])
            kn_c, vm_c = _cache_kv(cache_mla_ckv[:, j].reshape(-1, MLA_KV_LORA), wukv, gkn, g64)
            cache_arrs = (kn_c, vm_c,
                          jnp.tile(cache_mla_krope[:, j].reshape(-1, MLA_ROPE), (1, 4)).astype(BF16),
                          diff_caches[0], diff_caches[1])
            o_lat = _attn_even(True, rows, j, lam_init, (qn, qr, mq), (kn, vm, krt, mk, mv), cache_arrs,
                               diff_lambda[j], gsub)
            o_ctx = _attn_even(False, rows, j, lam_init, (qn, qr, mq), (kn, vm, krt, mk, mv), None,
                               diff_lambda[j], gsub)
        else:
            w_out = w_out_odd
            gains = [_tile_row(odd_g_qk[j, 0], 8), _tile_row(odd_g_qk[j, 1], 2),
                     _tile_row(odd_g_qk[j, 2], 8), _tile_row(odd_g_qk[j, 3], 2)]
            outs = _in_odd(rows, j, x_lat, x_ctx, mods_l, g1, w_in_odd, gains, g64, tab64, odd_states)
            odd_states = list(outs[6:])
            o_lat = _attn_odd(True, rows, j, outs[:2], outs[2:6], odd_caches, win_sink[j])
            o_ctx = _attn_odd(False, rows, j, outs[:2], odd_states, None, win_sink[j])

        w_r = jnp.pad(moe_w_router[layer], ((0, 0), (0, LANES - N_EXPERTS)))
        wrh = w_r.astype(BF16)
        wrl = (w_r - wrh.astype(F32)).astype(BF16)
        x1, h2, afft = _post(rows, j, o_lat, o_ctx, w_out, x_lat, x_ctx, mods_l, g2,
                             jnp.concatenate([wrh, wrl], axis=1), wrh)
        aff_lat = (afft[:lat_chunks].reshape(rows.lat_b, rows.lat_n // SUB_ROWS, N_EXPERTS, SUB_ROWS)
                   .transpose(0, 2, 1, 3).reshape(rows.lat_b * N_EXPERTS, rows.lat_n))
        aff_ctx = afft[lat_chunks:].reshape(rows.ctx_b * N_EXPERTS, rows.ctx_n)
        affs = (aff_lat, aff_ctx)
        slots = _route(caps, affs)
        xs_l, gt_l, xs_c, gt_c = _gather(rows, caps, slots, affs, h2)
        ys = _ffn(layer, [xs_l, xs_c], [gt_l, gt_c], moe_w1, moe_w3, moe_w2)
        x_lat, x_ctx = _combine(rows, caps, slots, ys, x1, mods_l)

    def token_major(arr, heads):
        b, l, f, t = arr.shape
        return jnp.transpose(arr.reshape(b, l, heads, f // heads, t), (0, 1, 4, 2, 3))

    diff_shape = (rows.ctx_b, n_even, rows.ctx_n, DIFF_HEADS, LANES)
    return (x_ctx.reshape(x_prompt.shape), x_lat.reshape(x_sample.shape),
            even_states[0],
            jnp.swapaxes(even_states[1], 2, 3),
            even_states[2].reshape(diff_shape),
            even_states[3].reshape(diff_shape),
            token_major(odd_states[0], 2), token_major(odd_states[1], 2),
            token_major(odd_states[2], 2), token_major(odd_states[3], 2))
```

```python
import functools
import math
from typing import NamedTuple

import jax
import jax.numpy as jnp
from jax import lax
from jax.experimental import pallas as pl
from jax.experimental.pallas import tpu as pltpu

F32 = jnp.float32
BF16 = jnp.bfloat16

D_MODEL = 1024
DEPTH = 4
GRID_W = 64
ROPE_THETA = 10000.0
WINDOW = 128
RMS_EPS = 1e-6
MLA_HEADS = 8
MLA_Q_LORA = 256
MLA_KV_LORA = 128
MLA_NOPE = 64
MLA_ROPE = 32
MLA_VD = 64
DIFF_HEADS = 4
DIFF_HD = 64
HEAD_DIM = 64
N_EXPERTS = 16
EC_CAPACITY_FACTOR = 2
EXPERT_FF = 2048

LANES = 128
MXU_DIM = 256
ROW_TILE = 1024
SUB_ROWS = 256
N_SUB = ROW_TILE // SUB_ROWS
Q_TILE_DENSE = 512
Q_TILE_BANDED = 256
FF_CHUNK = 1024
VMEM_LIMIT = 50 * 1024 * 1024
FFN_VMEM_LIMIT = 60 * 1024 * 1024
NEG_BIG = -1e30
LOG2E = 1.4426950408889634


def _params(*sem, vmem=VMEM_LIMIT):
    return pltpu.CompilerParams(dimension_semantics=sem, vmem_limit_bytes=vmem)


def _full(shape):
    zeros = (0,) * len(shape)
    return pl.BlockSpec(shape, lambda *_: zeros)


def _dot(a, b):
    return jnp.dot(a, b, preferred_element_type=F32)


def _dot_nt(a, b):
    return lax.dot_general(a, b, (((1,), (1,)), ((), ())), preferred_element_type=F32)


def _rms(x, g):
    ms = jnp.mean(x * x, axis=-1, keepdims=True)
    return x * lax.rsqrt(ms + RMS_EPS) * g


def _group_rms(x, g, gmat):
    outs = []
    width = x.shape[1]
    for k in range(0, width, MXU_DIM):
        wd = min(MXU_DIM, width - k)
        xb = x[:, k:k + wd]
        ms = _dot((xb * xb).astype(BF16), gmat[:wd, :wd])
        outs.append(xb * lax.rsqrt(ms + RMS_EPS))
    y = outs[0] if len(outs) == 1 else jnp.concatenate(outs, axis=1)
    return y * g


def _rope(x, cos, sin, half):
    lane = lax.broadcasted_iota(jnp.int32, (1, LANES), 1)
    first = (lane % (2 * half)) < half
    outs = []
    for k in range(x.shape[1] // LANES):
        xb = x[:, k * LANES:(k + 1) * LANES]
        fwd = pltpu.roll(xb, LANES - half, 1)
        bwd = pltpu.roll(xb, half, 1)
        outs.append(xb * cos + jnp.where(first, fwd, bwd) * sin)
    return outs[0] if len(outs) == 1 else jnp.concatenate(outs, axis=1)


def _silu(a):
    return a / (1.0 + jnp.exp(-a))


def _sub(c):
    return slice(c * SUB_ROWS, (c + 1) * SUB_ROWS)


def _adaln_kernel(c_ref, w_ref, b_ref, o_ref):
    a = _silu(c_ref[...]).astype(BF16)
    bias = b_ref[pl.ds(pl.program_id(0), 1), :]
    o_ref[0] = _dot(a, w_ref[0].astype(BF16)) + bias


def _adaln(cvec, w_mod, b_mod):
    rows = cvec.shape[0]
    nc = 4
    wc = 6 * D_MODEL // nc
    return pl.pallas_call(
        _adaln_kernel,
        grid=(DEPTH, nc),
        in_specs=[_full((rows, D_MODEL)),
                  pl.BlockSpec((1, D_MODEL, wc), lambda l, n: (l, 0, n)),
                  pl.BlockSpec((DEPTH, wc), lambda l, n: (0, n))],
        out_specs=pl.BlockSpec((1, rows, wc), lambda l, n: (l, 0, n)),
        out_shape=jax.ShapeDtypeStruct((DEPTH, rows, 6 * D_MODEL), F32),
        compiler_params=_params("parallel", "parallel"),
        name="adaln",
    )(cvec, w_mod, b_mod)


class _Rows(NamedTuple):
    lat_b: int
    lat_n: int
    ctx_b: int
    ctx_n: int

    @property
    def lat_rows(self):
        return self.lat_b * self.lat_n

    @property
    def ctx_rows(self):
        return self.ctx_b * self.ctx_n

    @property
    def lat_tiles(self):
        return self.lat_rows // ROW_TILE

    @property
    def tiles(self):
        return (self.lat_rows + self.ctx_rows) // ROW_TILE

    @property
    def tiles_per_lat_req(self):
        return self.lat_n // ROW_TILE


class _TileSpecs:
    def __init__(self, rows):
        lat_tiles = rows.lat_tiles
        per_req = rows.tiles_per_lat_req
        self.lat_tiles = lat_tiles
        self.mod = pl.BlockSpec((1, 6, D_MODEL), lambda i: (jnp.where(i < lat_tiles, 1 + i // per_req, 0), 0, 0))
        self.table = pl.BlockSpec((ROW_TILE, LANES), lambda i: (jnp.where(i < lat_tiles, i % per_req, per_req), 0))

    def rows(self, width):
        return pl.BlockSpec((ROW_TILE, width), lambda i: (i, 0))

    def lat_rows(self, width):
        last = self.lat_tiles - 1
        return pl.BlockSpec((ROW_TILE, width), lambda i: (jnp.minimum(i, last), 0))

    def ctx_rows(self, width):
        first = self.lat_tiles
        return pl.BlockSpec((ROW_TILE, width), lambda i: (jnp.maximum(i - first, 0), 0))

    def chunks(self, feats):
        return pl.BlockSpec((N_SUB, feats, SUB_ROWS), lambda i: (i, 0, 0))

    def state(self, layer_j, feats, cols):
        first = self.lat_tiles
        return pl.BlockSpec((N_SUB, None, feats, cols), lambda i: (jnp.maximum(i - first, 0), layer_j, 0, 0))


def _pick_rows(lat_tiles, lat_ref, ctx_ref, r):
    return jnp.where(pl.program_id(0) < lat_tiles, lat_ref[r, :], ctx_ref[r, :])


def _modulated(x, mod, g, shift_row, scale_row):
    return _rms(x, g) * (1.0 + mod[scale_row:scale_row + 1]) + mod[shift_row:shift_row + 1]


def _in_even_kernel(lat_tiles, *refs):
    (xl_ref, xc_ref, mod_ref, g1_ref, wt_ref, gqa_ref, gkva_ref, wuq_ref, wukv_ref,
     gqn_ref, gqr_ref, gkn_ref, gkr_ref, gdq_ref, gdk_ref, g64_ref, g32_ref,
     c64_ref, s64_ref, c32_ref, s32_ref) = refs[:21]
    refs = refs[21 + 4:]
    qn_ref, qr_ref, kn_ref, vm_ref, krt_ref, mq_ref, mk_ref, mv_ref = refs[:8]
    ckv_s_ref, krt_s_ref, mk_s_ref, mv_s_ref = refs[8:]
    g64 = g64_ref[...]
    g32 = g32_ref[...]
    mod = mod_ref[0]
    mla_scale = (MLA_NOPE + MLA_ROPE) ** -0.5 * LOG2E
    diff_scale = DIFF_HD ** -0.5 * LOG2E

    for c in range(N_SUB):
        r = _sub(c)
        h = _modulated(_pick_rows(lat_tiles, xl_ref, xc_ref, r), mod, g1_ref[...], 0, 1)
        proj = _dot_nt(h.astype(BF16), wt_ref[...])
        cq = _rms(proj[:, 0:256], gqa_ref[...])
        q = _dot(cq.astype(BF16), wuq_ref[...])
        qn = _group_rms(q[:, 0:512], gqn_ref[...], g64)
        qr = _group_rms(q[:, 512:768], gqr_ref[...], g32)
        ckv = _rms(proj[:, 256:384], gkva_ref[...])
        kv = _dot(ckv.astype(BF16), wukv_ref[...])
        kn = _group_rms(kv[:, 0:512], gkn_ref[...], g64)
        mq = _group_rms(proj[:, 384:896], gdq_ref[...], g64)
        mk = _group_rms(proj[:, 896:1408], gdk_ref[...], g64)
        mv = proj[:, 1408:1920]
        kr = _group_rms(proj[:, 1920:2048], gkr_ref[...], g32)
        ckv_s_ref[c] = ckv
        krt_s_ref[c] = kr.T[0:MLA_ROPE]
        for hd in range(DIFF_HEADS):
            mk_s_ref[c, pl.ds(hd, SUB_ROWS, stride=DIFF_HEADS), :] = mk[:, hd * LANES:(hd + 1) * LANES]
            mv_s_ref[c, pl.ds(hd, SUB_ROWS, stride=DIFF_HEADS), :] = mv[:, hd * LANES:(hd + 1) * LANES]
        c64, s64, c32, s32 = c64_ref[r, :], s64_ref[r, :], c32_ref[r, :], s32_ref[r, :]
        qr = _rope(qr, c32, s32, MLA_ROPE // 4)
        kr = _rope(kr, c32, s32, MLA_ROPE // 4)
        mq = _rope(mq, c64, s64, DIFF_HD // 4)
        mk = _rope(mk, c64, s64, DIFF_HD // 4)
        qn_ref[r, :] = (qn * mla_scale).astype(BF16)
        qr_ref[r, :] = (qr * mla_scale).astype(BF16)
        kn_ref[r, :] = kn.astype(BF16)
        vm_ref[r, :] = kv[:, 512:1024].astype(BF16)
        krt_ref[r, :] = kr.astype(BF16)
        mq_ref[r, :] = (mq * diff_scale).astype(BF16)
        mk_ref[r, :] = mk.astype(BF16)
        mv_ref[r, :] = mv.astype(BF16)


def _in_odd_kernel(lat_tiles, *refs):
    (xl_ref, xc_ref, mod_ref, g1_ref, w_ref, gqc_ref, gkc_ref, gqd_ref, gkd_ref, g64_ref,
     c64_ref, s64_ref) = refs[:12]
    refs = refs[12 + 4:]
    qc_ref, qd_ref = refs[:2]
    chunk_refs = refs[2:6]
    state_refs = refs[6:10]
    g64 = g64_ref[...]
    mod = mod_ref[0]
    w = w_ref[...].astype(BF16)
    scale = HEAD_DIM ** -0.5 * LOG2E

    for c in range(N_SUB):
        r = _sub(c)
        h = _modulated(_pick_rows(lat_tiles, xl_ref, xc_ref, r), mod, g1_ref[...], 0, 1)
        proj = _dot(h.astype(BF16), w)
        qc = _group_rms(proj[:, 0:512], gqc_ref[...], g64)
        kc = _group_rms(proj[:, 512:640], gkc_ref[...], g64)
        qd = _group_rms(proj[:, 768:1280], gqd_ref[...], g64)
        kd = _group_rms(proj[:, 1280:1408], gkd_ref[...], g64)
        vct = proj[:, 640:768].T
        vdt = proj[:, 1408:1536].T
        state_refs[0][c] = kc.T
        state_refs[1][c] = vct
        state_refs[2][c] = kd.T
        state_refs[3][c] = vdt
        c64, s64 = c64_ref[r, :], s64_ref[r, :]
        qc = _rope(qc, c64, s64, HEAD_DIM // 4)
        kc = _rope(kc, c64, s64, HEAD_DIM // 4)
        qd = _rope(qd, c64, s64, HEAD_DIM // 4)
        kd = _rope(kd, c64, s64, HEAD_DIM // 4)
        qc_ref[r, :] = (qc * scale).astype(BF16)
        qd_ref[r, :] = (qd * scale).astype(BF16)
        chunk_refs[0][c] = kc.T
        chunk_refs[1][c] = vct
        chunk_refs[2][c] = kd.T
        chunk_refs[3][c] = vdt


def _with_states(specs, layer_j, in_specs, args, out_shape, out_specs, states):
    aliases = {}
    for st in states:
        aliases[len(args)] = len(out_shape)
        in_specs.append(pl.BlockSpec(memory_space=pl.ANY))
        args.append(st)
        out_shape.append(jax.ShapeDtypeStruct(st.shape, st.dtype))
        out_specs.append(specs.state(layer_j, st.shape[2], st.shape[3]))
    return aliases


def _in_even(rows, layer_j, x_lat, x_ctx, mods, g1, wt, wuq, wukv, gains, gmats, tables, states):
    specs = _TileSpecs(rows)
    n_rows = rows.lat_rows + rows.ctx_rows
    vec_specs = [_full(g.shape) for g in gains]
    in_specs = ([specs.lat_rows(D_MODEL), specs.ctx_rows(D_MODEL), specs.mod, _full(g1.shape), _full(wt.shape)]
                + vec_specs[:2] + [_full(wuq.shape), _full(wukv.shape)] + vec_specs[2:]
                + [_full((MXU_DIM, MXU_DIM))] * 2 + [specs.table] * 4)
    args = ([x_lat, x_ctx, mods, g1, wt, gains[0], gains[1], wuq, wukv] + list(gains[2:]) + list(gmats)
            + list(tables))
    widths = [512, 256, 512, 512, 128, 512, 512, 512]
    out_shape = [jax.ShapeDtypeStruct((n_rows, wd), BF16) for wd in widths]
    out_specs = [specs.rows(wd) for wd in widths]
    aliases = _with_states(specs, layer_j, in_specs, args, out_shape, out_specs, states)
    return pl.pallas_call(
        functools.partial(_in_even_kernel, rows.lat_tiles),
        grid=(rows.tiles,),
        in_specs=in_specs, out_specs=out_specs, out_shape=out_shape,
        input_output_aliases=aliases,
        compiler_params=_params("arbitrary"),
        name="in_even",
    )(*args)


def _in_odd(rows, layer_j, x_lat, x_ctx, mods, g1, w_all, gains, g64, tables, states):
    specs = _TileSpecs(rows)
    n_rows = rows.lat_rows + rows.ctx_rows
    w_spec = pl.BlockSpec((None,) + w_all.shape[1:], lambda i: (layer_j, 0, 0))
    in_specs = ([specs.lat_rows(D_MODEL), specs.ctx_rows(D_MODEL), specs.mod, _full(g1.shape), w_spec]
                + [_full(g.shape) for g in gains] + [_full((MXU_DIM, MXU_DIM))] + [specs.table] * 2)
    args = [x_lat, x_ctx, mods, g1, w_all] + list(gains) + [g64] + list(tables)
    out_shape = ([jax.ShapeDtypeStruct((n_rows, 512), BF16)] * 2
                 + [jax.ShapeDtypeStruct((n_rows // SUB_ROWS, LANES, SUB_ROWS), F32)] * 4)
    out_specs = [specs.rows(512), specs.rows(512)] + [specs.chunks(LANES)] * 4
    aliases = _with_states(specs, layer_j, in_specs, args, out_shape, out_specs, states)
    return pl.pallas_call(
        functools.partial(_in_odd_kernel, rows.lat_tiles),
        grid=(rows.tiles,),
        in_specs=in_specs, out_specs=out_specs, out_shape=out_shape,
        input_output_aliases=aliases,
        compiler_params=_params("arbitrary"),
        name="in_odd",
    )(*args)


def _cache_kv_kernel(ckv_ref, wukv_ref, gkn_ref, g64_ref, kn_ref, vm_ref):
    kv = _dot(ckv_ref[...].astype(BF16), wukv_ref[...])
    kn_ref[...] = _group_rms(kv[:, 0:512], gkn_ref[...], g64_ref[...]).astype(BF16)
    vm_ref[...] = kv[:, 512:1024].astype(BF16)


def _cache_kv(ckv, wukv, gkn, g64):
    rows = ckv.shape[0]
    tile = 512
    spec = lambda wd: pl.BlockSpec((tile, wd), lambda i: (i, 0))
    return pl.pallas_call(
        _cache_kv_kernel,
        grid=(rows // tile,),
        in_specs=[spec(MLA_KV_LORA), _full(wukv.shape), _full(gkn.shape), _full((MXU_DIM, MXU_DIM))],
        out_specs=[spec(512), spec(512)],
        out_shape=[jax.ShapeDtypeStruct((rows, 512), BF16)] * 2,
        compiler_params=_params("parallel"),
        name="cache_kv",
    )(ckv, wukv, gkn, g64)


def _softmax_pv(s_parts, v_parts, v_is_feature_major, sink=None):
    m = None
    for s in s_parts:
        mi = jnp.max(s, axis=-1, keepdims=True)
        m = mi if m is None else jnp.maximum(m, mi)
    if sink is not None:
        m = jnp.maximum(m, sink)
    acc = None
    for s, v in zip(s_parts, v_parts):
        p = jnp.exp2(s - m).astype(BF16)
        oi = _dot_nt(p, v) if v_is_feature_major else _dot(p, v)
        acc = oi if acc is None else acc + oi
    denom = acc[:, LANES:]
    if sink is not None:
        denom = denom + jnp.exp2(sink - m)
    return acc[:, :LANES] / denom


def _lane_masks(width):
    lane = lax.broadcasted_iota(jnp.int32, (1, LANES), 1)
    return [jnp.where(lane // width == k, 1.0, 0.0).astype(BF16) for k in range(LANES // width)]


def _attn_even_kernel(latent, lam_init, *refs):
    qn_ref, qr_ref, mq_ref = refs[:3]
    refs = refs[3:]
    n_parts = 2 if latent else 1
    parts = [refs[5 * k:5 * k + 5] for k in range(n_parts)]
    lamv_ref, gsub_ref, o_ref = refs[5 * n_parts:]
    lane = lax.broadcasted_iota(jnp.int32, (1, LANES), 1)
    low = lane < 64
    m64 = _lane_masks(64)
    m32 = _lane_masks(32)
    ones = [jnp.ones((p[0].shape[0], LANES), BF16) for p in parts]

    def diff_head(ref, idx, hd):
        if latent and idx == 0:
            n_keys = ref.shape[0] // DIFF_HEADS
            return ref[pl.ds(hd, n_keys, stride=DIFF_HEADS), :].astype(BF16)
        return ref[:, hd * LANES:(hd + 1) * LANES]

    for i in range(MLA_HEADS // 2):
        cols = slice(i * LANES, (i + 1) * LANES)
        qn_b = qn_ref[:, cols]
        kcat = [jnp.concatenate([p[0][:, cols], p[2][...]], axis=1) for p in parts]
        vms = [jnp.concatenate([p[1][:, cols], one], axis=1) for p, one in zip(parts, ones)]
        outs = []
        for half in range(2):
            head = 2 * i + half
            rb = head // 4
            qr_b = qr_ref[:, rb * LANES:(rb + 1) * LANES]
            lhs = jnp.concatenate([qn_b * m64[half], qr_b * m32[head % 4]], axis=1)
            outs.append(_softmax_pv([_dot_nt(lhs, kc) for kc in kcat], vms, False))
        o_ref[:, cols] = jnp.where(low, outs[0], outs[1]).astype(BF16)

    lv = lamv_ref[...]
    lam = (jnp.exp(jnp.sum(lv[0:1] * lv[1:2], axis=-1, keepdims=True))
           - jnp.exp(jnp.sum(lv[2:3] * lv[3:4], axis=-1, keepdims=True)) + lam_init)
    gsub = gsub_ref[...]
    for hd in range(DIFF_HEADS):
        q_b = mq_ref[:, hd * LANES:(hd + 1) * LANES]
        ks = [diff_head(p[3], idx, hd) for idx, p in enumerate(parts)]
        vs = [jnp.concatenate([diff_head(p[4], idx, hd), one], axis=1)
              for idx, (p, one) in enumerate(zip(parts, ones))]
        a1 = _softmax_pv([_dot_nt(q_b * m64[0], k) for k in ks], vs, False)
        a2 = _softmax_pv([_dot_nt(q_b * m64[1], k) for k in ks], vs, False)
        d = _rms(a1 - lam * a2, gsub) * (1.0 - lam_init)
        o_ref[:, 512 + hd * LANES:512 + (hd + 1) * LANES] = d.astype(BF16)


def _attn_even(latent, rows, layer_j, lam_init, q_arrs, kv_arrs, cache_arrs, lamv, gsub):
    batch, n_tokens, row0 = (rows.lat_b, rows.lat_n, 0) if latent else (rows.ctx_b, rows.ctx_n, rows.lat_rows)
    tq = min(Q_TILE_DENSE, n_tokens)
    nt = n_tokens // tq
    q0 = row0 // tq
    k0 = row0 // n_tokens
    qspec = lambda wd: pl.BlockSpec((tq, wd), lambda b, t: (q0 + b * nt + t, 0))
    kspec = lambda wd: pl.BlockSpec((n_tokens, wd), lambda b, t: (k0 + b, 0))
    kwidths = [512, 512, 128, 512, 512]
    in_specs = [qspec(512), qspec(256), qspec(512)]
    args = list(q_arrs)
    if latent:
        past = cache_arrs[0].shape[0] // batch
        in_specs += [pl.BlockSpec((past, wd), lambda b, t: (b, 0)) for wd in kwidths[:3]]
        in_specs += [pl.BlockSpec((None, None, past * DIFF_HEADS, LANES), lambda b, t: (b, layer_j, 0, 0))] * 2
        args += list(cache_arrs)
    in_specs += [kspec(wd) for wd in kwidths]
    args += list(kv_arrs)
    in_specs += [_full(lamv.shape), _full(gsub.shape)]
    args += [lamv, gsub]
    return pl.pallas_call(
        functools.partial(_attn_even_kernel, latent, lam_init),
        grid=(batch, nt),
        in_specs=in_specs,
        out_specs=pl.BlockSpec((tq, D_MODEL), lambda b, t: (b * nt + t, 0)),
        out_shape=jax.ShapeDtypeStruct((batch * n_tokens, D_MODEL), BF16),
        compiler_params=_params("parallel", "parallel"),
        name="attn_even_lat" if latent else "attn_even_ctx",
    )(*args)


def _attn_odd_kernel(latent, tq, *refs):
    qc_ref, qd_ref = refs[:2]
    refs = refs[2:]
    n_parts = 2 if latent else 1
    parts = [list(refs[4 * k:4 * k + 4]) for k in range(n_parts)]
    sink_ref, o_ref = refs[4 * n_parts:4 * n_parts + 2]
    scratch = refs[4 * n_parts + 2:]
    lane = lax.broadcasted_iota(jnp.int32, (1, LANES), 1)
    low = lane < 64
    m64 = _lane_masks(64)

    band = None
    start = 0
    span = 0
    if latent:
        for ref, scr in zip(parts[-1], scratch):
            for c in range(ref.shape[0]):
                scr[:, _sub(c)] = ref[c]
        parts[-1] = list(scratch)
        n_new = scratch[0].shape[1]
        span = min(n_new, tq + 2 * WINDOW)
        t = pl.program_id(1)
        start = pl.multiple_of(jnp.clip(t * tq - WINDOW, 0, n_new - span), LANES)
        qpos = t * tq + lax.broadcasted_iota(jnp.int32, (tq, 1), 0)
        kpos = start + lax.broadcasted_iota(jnp.int32, (1, span), 1)
        band = jnp.abs(qpos - kpos) <= WINDOW

    for kind in range(2):
        q_ref = qc_ref if kind == 0 else qd_ref
        windowed = latent and kind == 0
        for g in range(2):
            rows = slice(g * HEAD_DIM, (g + 1) * HEAD_DIM)
            ks, vs = [], []
            for idx, p in enumerate(parts):
                k_ref, v_ref = p[2 * kind], p[2 * kind + 1]
                if windowed and idx == n_parts - 1:
                    k = k_ref[rows, pl.ds(start, span)]
                    v = v_ref[rows, pl.ds(start, span)]
                else:
                    k = k_ref[rows, :]
                    v = v_ref[rows, :]
                ks.append(jnp.concatenate([k, k], axis=0).astype(BF16))
                vs.append(jnp.concatenate([v, v, jnp.ones((LANES, v.shape[1]), F32)], axis=0).astype(BF16))
            for i in (2 * g, 2 * g + 1):
                cols = slice(i * LANES, (i + 1) * LANES)
                q_b = q_ref[:, cols]
                outs = []
                for half in range(2):
                    s_parts = [_dot(q_b * m64[half], k) for k in ks]
                    sink = None
                    if kind == 0:
                        sink = sink_ref[2 * i + half] * LOG2E
                        if latent:
                            s_parts[-1] = jnp.where(band, s_parts[-1], NEG_BIG)
                    outs.append(_softmax_pv(s_parts, vs, True, sink))
                o_ref[:, kind * 512 + i * LANES:kind * 512 + (i + 1) * LANES] = (
                    jnp.where(low, outs[0], outs[1]).astype(BF16))


def _attn_odd(latent, rows, layer_j, q_arrs, new_arrs, cache_arrs, sink):
    batch, n_tokens, row0 = (rows.lat_b, rows.lat_n, 0) if latent else (rows.ctx_b, rows.ctx_n, rows.lat_rows)
    tq = min(Q_TILE_BANDED, n_tokens)
    nt = n_tokens // tq
    q0 = row0 // tq
    in_specs = [pl.BlockSpec((tq, 512), lambda b, t: (q0 + b * nt + t, 0))] * 2
    args = list(q_arrs)
    scratch = []
    if latent:
        past = cache_arrs[0].shape[-1]
        in_specs += [pl.BlockSpec((None, None, LANES, past), lambda b, t: (b, layer_j, 0, 0))] * 4
        args += list(cache_arrs)
        chunks = n_tokens // SUB_ROWS
        in_specs += [pl.BlockSpec((chunks, LANES, SUB_ROWS), lambda b, t: (b, 0, 0))] * 4
        scratch = [pltpu.VMEM((LANES, n_tokens), F32)] * 4
    else:
        in_specs += [pl.BlockSpec((None, None, LANES, n_tokens), lambda b, t: (b, layer_j, 0, 0))] * 4
    args += list(new_arrs)
    in_specs += [pl.BlockSpec(memory_space=pltpu.SMEM)]
    args += [sink]
    return pl.pallas_call(
        functools.partial(_attn_odd_kernel, latent, tq),
        grid=(batch, nt),
        in_specs=in_specs,
        out_specs=pl.BlockSpec((tq, D_MODEL), lambda b, t: (b * nt + t, 0)),
        out_shape=jax.ShapeDtypeStruct((batch * n_tokens, D_MODEL), BF16),
        scratch_shapes=scratch,
        compiler_params=_params("parallel", "parallel"),
        name="attn_odd_lat" if latent else "attn_odd_ctx",
    )(*args)


def _post_kernel(lat_tiles, ol_ref, oc_ref, wout_ref, xl_ref, xc_ref, mod_ref, g2_ref, wr2_ref, wrh_ref,
                 x1_ref, h2_ref, afft_ref):
    mod = mod_ref[0]
    wout = wout_ref[...].astype(BF16)
    lane = lax.broadcasted_iota(jnp.int32, (1, LANES), 1)
    for c in range(N_SUB):
        r = _sub(c)
        o = _pick_rows(lat_tiles, ol_ref, oc_ref, r)
        x1 = _pick_rows(lat_tiles, xl_ref, xc_ref, r) + mod[2:3] * _dot(o, wout)
        x1_ref[r, :] = x1
        h2 = _modulated(x1, mod, g2_ref[...], 3, 4)
        h_hi = h2.astype(BF16)
        h_lo = (h2 - h_hi.astype(F32)).astype(BF16)
        h2_ref[r, :] = h_hi
        both = _dot(h_hi, wr2_ref[...])
        logits = both[:, :LANES] + both[:, LANES:] + _dot(h_lo, wrh_ref[...])
        logits = jnp.where(lane < N_EXPERTS, logits, NEG_BIG)
        e = jnp.exp(logits - jnp.max(logits, axis=-1, keepdims=True))
        aff = e / jnp.sum(e, axis=-1, keepdims=True)
        afft_ref[c] = aff.T[0:N_EXPERTS]


def _post(rows, layer_j, o_lat, o_ctx, wout_all, x_lat, x_ctx, mods, g2, wr2, wrh):
    specs = _TileSpecs(rows)
    n_rows = rows.lat_rows + rows.ctx_rows
    w_spec = pl.BlockSpec((None,) + wout_all.shape[1:], lambda i: (layer_j, 0, 0))
    return pl.pallas_call(
        functools.partial(_post_kernel, rows.lat_tiles),
        grid=(rows.tiles,),
        in_specs=[specs.lat_rows(D_MODEL), specs.ctx_rows(D_MODEL), w_spec,
                  specs.lat_rows(D_MODEL), specs.ctx_rows(D_MODEL), specs.mod, _full(g2.shape),
                  _full(wr2.shape), _full(wrh.shape)],
        out_specs=[specs.rows(D_MODEL), specs.rows(D_MODEL), specs.chunks(N_EXPERTS)],
        out_shape=[jax.ShapeDtypeStruct((n_rows, D_MODEL), F32),
                   jax.ShapeDtypeStruct((n_rows, D_MODEL), BF16),
                   jax.ShapeDtypeStruct((n_rows // SUB_ROWS, N_EXPERTS, SUB_ROWS), F32)],
        compiler_params=_params("parallel"),
        name="post",
    )(o_lat, o_ctx, wout_all, x_lat, x_ctx, mods, g2, wr2, wrh)


def _route_kernel(caps, *refs):
    n_groups = len(caps)
    aff_refs, out_refs = refs[:n_groups], refs[n_groups:]
    bits = [lax.bitcast_convert_type(ref[...], jnp.int32) for ref in aff_refs]

    def body(_, carry):
        new = []
        for b, cap, (lo, hi) in zip(bits, caps, carry):
            mid = lo + ((hi - lo + 1) >> 1)
            cnt = jnp.sum(jnp.where(b >= mid, 1.0, 0.0), axis=-1, keepdims=True)
            ok = cnt >= cap
            new.append((jnp.where(ok, mid, lo), jnp.where(ok, hi, mid - 1)))
        return tuple(new)

    init = tuple((jnp.zeros((b.shape[0], 1), jnp.int32), jnp.full((b.shape[0], 1), 0x7F800000, jnp.int32))
                 for b in bits)
    found = lax.fori_loop(0, 31, body, init)
    for b, cap, (thr, _), out_ref in zip(bits, caps, found, out_refs):
        n = b.shape[1]
        above = jnp.where(b > thr, 1.0, 0.0)
        equal = jnp.where(b == thr, 1.0, 0.0)
        room = cap - jnp.sum(above, axis=-1, keepdims=True)
        before = jnp.where(lax.broadcasted_iota(jnp.int32, (n, n), 0)
                           < lax.broadcasted_iota(jnp.int32, (n, n), 1), 1.0, 0.0).astype(BF16)
        equal_before = _dot(equal.astype(BF16), before)
        chosen = above + equal * jnp.where(equal_before < room, 1.0, 0.0)
        slot = _dot(chosen.astype(BF16), before)
        out_ref[...] = jnp.where(chosen > 0.5, slot, -1.0)


def _route(caps, aff_list):
    return pl.pallas_call(
        functools.partial(_route_kernel, tuple(caps)),
        grid=(1,),
        in_specs=[_full(a.shape) for a in aff_list],
        out_specs=[_full(a.shape) for a in aff_list],
        out_shape=[jax.ShapeDtypeStruct(a.shape, F32) for a in aff_list],
        compiler_params=_params("arbitrary"),
        name="route",
    )(*aff_list)


def _gather_part(cap, req_per_step, experts_per_step, first, slot_ref, aff_ref, h_ref, xs_ref, gate_ref):
    n = slot_ref.shape[1]
    want = lax.broadcasted_iota(jnp.int32, (cap, 1), 0).astype(F32)
    for rq in range(req_per_step):
        out_rows = slice(rq * cap, (rq + 1) * cap)
        rows = []
        for k in range(experts_per_step):
            row = rq * N_EXPERTS + first + k
            hit = slot_ref[pl.ds(row, 1), :] == want
            rows.append(jnp.where(hit, 1.0, 0.0).astype(BF16))
            gate_ref[k, out_rows, :] = jnp.sum(jnp.where(hit, aff_ref[pl.ds(row, 1), :], 0.0),
                                               axis=-1, keepdims=True)
        picked = _dot(jnp.concatenate(rows, axis=0), h_ref[rq * n:(rq + 1) * n, :]).astype(BF16)
        for k in range(experts_per_step):
            xs_ref[k, out_rows, :] = picked[k * cap:(k + 1) * cap]


def _gather_kernel(lat_steps, groups, caps, ctx_req_per_step,
                   sl_l, af_l, h_l, sl_c, af_c, h_c, xs_l, gt_l, xs_c, gt_c):
    s = pl.program_id(0)
    per_group = N_EXPERTS // groups

    @pl.when(s < lat_steps)
    def _():
        _gather_part(caps[0], 1, per_group, (s % groups) * per_group, sl_l, af_l, h_l, xs_l, gt_l)

    @pl.when(s >= lat_steps)
    def _():
        _gather_part(caps[1], ctx_req_per_step, N_EXPERTS, 0, sl_c, af_c, h_c, xs_c, gt_c)


def _gather(rows, caps, slots, affs, h2):
    groups = rows.lat_n // SUB_ROWS
    per_group = N_EXPERTS // groups
    rps = ROW_TILE // rows.ctx_n
    lat_steps = rows.lat_b * groups
    ctx_steps = rows.ctx_b // rps
    last_b = rows.lat_b - 1
    ctx_block0 = rows.lat_rows // (rps * rows.ctx_n)
    lat_req = lambda s: jnp.minimum(s // groups, last_b)
    lat_grp = lambda s: jnp.where(s < lat_steps, s % groups, groups - 1)
    ctx_step = lambda s: jnp.maximum(s - lat_steps, 0)
    cap_l, cap_c = caps
    lat_rows_spec = pl.BlockSpec((N_EXPERTS, rows.lat_n), lambda s: (lat_req(s), 0))
    ctx_rows_spec = pl.BlockSpec((rps * N_EXPERTS, rows.ctx_n), lambda s: (ctx_step(s), 0))
    return pl.pallas_call(
        functools.partial(_gather_kernel, lat_steps, groups, caps, rps),
        grid=(lat_steps + ctx_steps,),
        in_specs=[lat_rows_spec, lat_rows_spec,
                  pl.BlockSpec((rows.lat_n, D_MODEL), lambda s: (lat_req(s), 0)),
                  ctx_rows_spec, ctx_rows_spec,
                  pl.BlockSpec((rps * rows.ctx_n, D_MODEL), lambda s: (ctx_block0 + ctx_step(s), 0))],
        out_specs=[pl.BlockSpec((per_group, cap_l, D_MODEL), lambda s: (lat_grp(s), lat_req(s), 0)),
                   pl.BlockSpec((per_group, cap_l, 1), lambda s: (lat_grp(s), lat_req(s), 0)),
                   pl.BlockSpec((N_EXPERTS, rps * cap_c, D_MODEL), lambda s: (0, ctx_step(s), 0)),
                   pl.BlockSpec((N_EXPERTS, rps * cap_c, 1), lambda s: (0, ctx_step(s), 0))],
        out_shape=[jax.ShapeDtypeStruct((N_EXPERTS, rows.lat_b * cap_l, D_MODEL), BF16),
                   jax.ShapeDtypeStruct((N_EXPERTS, rows.lat_b * cap_l, 1), F32),
                   jax.ShapeDtypeStruct((N_EXPERTS, rows.ctx_b * cap_c, D_MODEL), BF16),
                   jax.ShapeDtypeStruct((N_EXPERTS, rows.ctx_b * cap_c, 1), F32)],
        compiler_params=_params("arbitrary"),
        name="gather",
    )(slots[0], affs[0], h2, slots[1], affs[1], h2)


def _ffn_kernel(n_groups, *refs):
    xs_refs = refs[:n_groups]
    gate_refs = refs[n_groups:2 * n_groups]
    w1_ref, w3_ref, w2_ref = refs[2 * n_groups:2 * n_groups + 3]
    ys_refs = refs[2 * n_groups + 3:3 * n_groups + 3]
    acc_refs = refs[3 * n_groups + 3:]
    f = pl.program_id(1)

    @pl.when(f == 0)
    def _():
        for acc_ref in acc_refs:
            acc_ref[...] = jnp.zeros(acc_ref.shape, F32)

    hidden = [[] for _ in xs_refs]
    for n in range(0, w1_ref.shape[3], MXU_DIM):
        w1 = w1_ref[0, 0, :, n:n + MXU_DIM].astype(BF16)
        w3 = w3_ref[0, 0, :, n:n + MXU_DIM].astype(BF16)
        for parts, xs_ref in zip(hidden, xs_refs):
            xs = xs_ref[0]
            parts.append((_silu(_dot(xs, w1)) * _dot(xs, w3)).astype(BF16))
    hidden = [jnp.concatenate(parts, axis=1) for parts in hidden]
    for n in range(0, w2_ref.shape[3], MXU_DIM):
        w2 = w2_ref[0, 0, :, n:n + MXU_DIM].astype(BF16)
        for hid, acc_ref in zip(hidden, acc_refs):
            acc_ref[:, n:n + MXU_DIM] += _dot(hid, w2)

    @pl.when(f == pl.num_programs(1) - 1)
    def _():
        for gate_ref, ys_ref, acc_ref in zip(gate_refs, ys_refs, acc_refs):
            ys_ref[0] = (acc_ref[...] * gate_ref[0]).astype(BF16)


def _ffn(layer, xs_list, gate_list, w1, w3, w2):
    n_groups = len(xs_list)
    n_chunks = EXPERT_FF // FF_CHUNK
    xs_specs = [pl.BlockSpec((1,) + xs.shape[1:], lambda e, f: (e, 0, 0)) for xs in xs_list]
    gate_specs = [pl.BlockSpec((1,) + g.shape[1:], lambda e, f: (e, 0, 0)) for g in gate_list]
    return pl.pallas_call(
        functools.partial(_ffn_kernel, n_groups),
        grid=(N_EXPERTS, n_chunks),
        in_specs=xs_specs + gate_specs + [
            pl.BlockSpec((1, 1, D_MODEL, FF_CHUNK), lambda e, f: (layer, e, 0, f)),
            pl.BlockSpec((1, 1, D_MODEL, FF_CHUNK), lambda e, f: (layer, e, 0, f)),
            pl.BlockSpec((1, 1, FF_CHUNK, D_MODEL), lambda e, f: (layer, e, f, 0))],
        out_specs=xs_specs,
        out_shape=[jax.ShapeDtypeStruct(xs.shape, BF16) for xs in xs_list],
        scratch_shapes=[pltpu.VMEM(xs.shape[1:], F32) for xs in xs_list],
        compiler_params=_params("parallel", "arbitrary", vmem=FFN_VMEM_LIMIT),
        name="ffn",
    )(*xs_list, *gate_list, w1, w3, w2)


def _scatter_part(cap, slots, ys_ref, x, gate2):
    lane = lax.broadcasted_iota(jnp.int32, (1, LANES), 1).astype(F32)
    per_block = LANES // cap
    pad = jnp.zeros((LANES - N_EXPERTS, SUB_ROWS), F32)
    slots_t = jnp.concatenate([slots, pad], axis=0).T
    blocks = []
    for k in range(N_EXPERTS // per_block):
        hit = None
        for e in range(k * per_block, (k + 1) * per_block):
            col = slots_t[:, e:e + 1]
            target = jnp.where(col >= 0.0, col + float((e - k * per_block) * cap), -1.0)
            he = jnp.where(target == lane, 1.0, 0.0)
            hit = he if hit is None else hit + he
        blocks.append(hit.astype(BF16))
    onehot = jnp.concatenate(blocks, axis=1)
    ys = jnp.concatenate([ys_ref[e] for e in range(N_EXPERTS)], axis=0)
    return x + gate2 * _dot(onehot, ys)


def _combine_kernel(lat_steps, chunks_per_req, caps, sl_l, ys_l, sl_c, ys_c, x_ref, mod_ref, ol_ref, oc_ref):
    s = pl.program_id(0)
    gate2 = mod_ref[0][5:6]

    @pl.when(s < lat_steps)
    def _():
        tokens = pl.multiple_of((s % chunks_per_req) * SUB_ROWS, SUB_ROWS)
        ol_ref[...] = _scatter_part(caps[0], sl_l[:, pl.ds(tokens, SUB_ROWS)], ys_l, x_ref[...], gate2)

    @pl.when(s >= lat_steps)
    def _():
        oc_ref[...] = _scatter_part(caps[1], sl_c[...], ys_c, x_ref[...], gate2)


def _combine(rows, caps, slots, ys, x1, mods):
    per_req = rows.lat_n // SUB_ROWS
    lat_steps = rows.lat_b * per_req
    last_b = rows.lat_b - 1
    lat_req = lambda s: jnp.minimum(s // per_req, last_b)
    ctx_req = lambda s: jnp.maximum(s - lat_steps, 0)
    cap_l, cap_c = caps
    return pl.pallas_call(
        functools.partial(_combine_kernel, lat_steps, per_req, caps),
        grid=(lat_steps + rows.ctx_b,),
        in_specs=[pl.BlockSpec((N_EXPERTS, rows.lat_n), lambda s: (lat_req(s), 0)),
                  pl.BlockSpec((N_EXPERTS, cap_l, D_MODEL), lambda s: (0, lat_req(s), 0)),
                  pl.BlockSpec((N_EXPERTS, rows.ctx_n), lambda s: (ctx_req(s), 0)),
                  pl.BlockSpec((N_EXPERTS, cap_c, D_MODEL), lambda s: (0, ctx_req(s), 0)),
                  pl.BlockSpec((SUB_ROWS, D_MODEL), lambda s: (s, 0)),
                  pl.BlockSpec((1, 6, D_MODEL), lambda s: (jnp.where(s < lat_steps, 1 + s // per_req, 0), 0, 0))],
        out_specs=[pl.BlockSpec((SUB_ROWS, D_MODEL), lambda s: (jnp.minimum(s, lat_steps - 1), 0)),
                   pl.BlockSpec((SUB_ROWS, D_MODEL), lambda s: (ctx_req(s), 0))],
        out_shape=[jax.ShapeDtypeStruct((rows.lat_rows, D_MODEL), F32),
                   jax.ShapeDtypeStruct((rows.ctx_rows, D_MODEL), F32)],
        compiler_params=_params("arbitrary"),
        name="combine",
    )(slots[0], ys[0], slots[1], ys[1], x1, mods)


def _rope_tables(n_pos, rot_dim):
    t = jnp.arange(n_pos)
    row = t // GRID_W
    col = t % GRID_W
    nf = rot_dim // 4
    inv = ROPE_THETA ** (-jnp.arange(nf, dtype=F32) / nf)
    ang_r = row[:, None] * inv
    ang_c = col[:, None] * inv
    cr, sr, cc, sc = jnp.cos(ang_r), jnp.sin(ang_r), jnp.cos(ang_c), jnp.sin(ang_c)
    cos = jnp.concatenate([cr, cr, cc, cc], axis=1)
    sin = jnp.concatenate([-sr, sr, -sc, sc], axis=1)
    reps = LANES // rot_dim
    cos = jnp.concatenate([jnp.tile(cos, (1, reps)), jnp.ones((ROW_TILE, LANES), F32)], axis=0)
    sin = jnp.concatenate([jnp.tile(sin, (1, reps)), jnp.zeros((ROW_TILE, LANES), F32)], axis=0)
    return cos, sin


def _group_mean_matrix(width):
    idx = jnp.arange(MXU_DIM) // width
    return jnp.where(idx[:, None] == idx[None, :], 1.0 / width, 0.0).astype(BF16)


def _tile_row(g, reps):
    return jnp.tile(g, reps).reshape(1, -1)


def _split_heads(w, n_heads, first):
    k = w.shape[0]
    w3 = w.reshape(k, n_heads, -1)
    return jnp.concatenate([w3[:, :, :first].reshape(k, -1), w3[:, :, first:].reshape(k, -1)], axis=1)


def _feature_major(cache):
    b, l, t, h, d = cache.shape
    return jnp.transpose(cache, (0, 1, 3, 4, 2)).reshape(b, l, h * d, t)


def kernel(x_prompt, x_sample, cache_mla_ckv, cache_mla_krope, cache_diff_k, cache_diff_v,
           cache_win_k, cache_win_v, cache_axial_k, cache_axial_v, c, c_ctx,
           g_norm, w_mod, b_mod, w_in_even, w_out_even, mla_g_qa, mla_g_kva, mla_w_uq, mla_w_ukv,
           mla_g_q, mla_g_k, diff_g_q, diff_g_k, diff_lambda, diff_g_sub,
           w_in_odd, w_out_odd, odd_g_qk, win_sink, moe_w_router, moe_w1, moe_w3, moe_w2):
    rows = _Rows(lat_b=x_sample.shape[0], lat_n=x_sample.shape[1], ctx_b=x_prompt.shape[0], ctx_n=x_prompt.shape[1])
    past = cache_mla_ckv.shape[2]
    n_even, n_odd = w_in_even.shape[0], w_in_odd.shape[0]
    assert rows.ctx_n == SUB_ROWS and rows.lat_n % ROW_TILE == 0 and rows.ctx_b % N_SUB == 0
    caps = (EC_CAPACITY_FACTOR * rows.lat_n // N_EXPERTS, EC_CAPACITY_FACTOR * rows.ctx_n // N_EXPERTS)

    cvec = jnp.zeros((16, D_MODEL), F32).at[0].set(c_ctx).at[1:1 + rows.lat_b].set(c)
    mods = _adaln(cvec, w_mod, b_mod).reshape(DEPTH, 16, 6, D_MODEL)

    g64 = _group_mean_matrix(64)
    g32 = _group_mean_matrix(32)
    tab64 = _rope_tables(rows.lat_n, HEAD_DIM)
    tab32 = _rope_tables(rows.lat_n, MLA_ROPE)
    odd_caches = [_feature_major(a) for a in (cache_win_k, cache_win_v, cache_axial_k, cache_axial_v)]
    diff_caches = [a.reshape(rows.lat_b, n_even, past * DIFF_HEADS, LANES) for a in (cache_diff_k, cache_diff_v)]

    even_states = [jnp.zeros((rows.ctx_b, n_even, rows.ctx_n, MLA_KV_LORA), F32),
                   jnp.zeros((rows.ctx_b, n_even, MLA_ROPE, rows.ctx_n), F32),
                   jnp.zeros((rows.ctx_b, n_even, rows.ctx_n * DIFF_HEADS, LANES), F32),
                   jnp.zeros((rows.ctx_b, n_even, rows.ctx_n * DIFF_HEADS, LANES), F32)]
    odd_states = [jnp.zeros((rows.ctx_b, n_odd, LANES, rows.ctx_n), F32) for _ in range(4)]

    x_lat = x_sample.reshape(-1, D_MODEL)
    x_ctx = x_prompt.reshape(-1, D_MODEL)
    lat_chunks = rows.lat_rows // SUB_ROWS

    for layer in range(DEPTH):
        j = layer // 2
        mods_l = mods[layer]
        g1 = g_norm[layer, 0].reshape(1, -1)
        g2 = g_norm[layer, 1].reshape(1, -1)
        if layer % 2 == 0:
            wt = jnp.swapaxes(w_in_even[j], 0, 1)
            wt = jnp.concatenate([wt[:384], wt[416:]] + [wt[384:416]] * 4, axis=0).astype(BF16)
            wuq = _split_heads(mla_w_uq[j], MLA_HEADS, MLA_NOPE).astype(BF16)
            wukv = _split_heads(mla_w_ukv[j], MLA_HEADS, MLA_NOPE).astype(BF16)
            w_out = w_out_even
            gkn = _tile_row(mla_g_k[j, :MLA_NOPE], MLA_HEADS)
            gains = [mla_g_qa[j].reshape(1, -1), mla_g_kva[j].reshape(1, -1),
                     _tile_row(mla_g_q[j, :MLA_NOPE], MLA_HEADS), _tile_row(mla_g_q[j, MLA_NOPE:], MLA_HEADS),
                     gkn, _tile_row(mla_g_k[j, MLA_NOPE:], 4),
                     _tile_row(diff_g_q[j], 2 * DIFF_HEADS), _tile_row(diff_g_k[j], 2 * DIFF_HEADS)]
            lam_init = 0.8 - 0.6 * math.exp(-0.3 * layer)
            gsub = diff_g_sub[j].reshape(1, -1)
            outs = _in_even(rows, j, x_lat, x_ctx, mods_l, g1, wt, wuq, wukv, gains, (g64, g32),
                            tab64 + tab32, even_states)
            qn, qr, kn, vm, krt, mq, mk, mv = outs[:8]
            even_states = list(outs[8:])
            kn_c, vm_c = _cache_kv(cache_mla_ckv[:, j].reshape(-1, MLA_KV_LORA), wukv, gkn, g64)
            cache_arrs = (kn_c, vm_c,
                          jnp.tile(cache_mla_krope[:, j].reshape(-1, MLA_ROPE), (1, 4)).astype(BF16),
                          diff_caches[0], diff_caches[1])
            o_lat = _attn_even(True, rows, j, lam_init, (qn, qr, mq), (kn, vm, krt, mk, mv), cache_arrs,
                               diff_lambda[j], gsub)
            o_ctx = _attn_even(False, rows, j, lam_init, (qn, qr, mq), (kn, vm, krt, mk, mv), None,
                               diff_lambda[j], gsub)
        else:
            w_out = w_out_odd
            gains = [_tile_row(odd_g_qk[j, 0], 8), _tile_row(odd_g_qk[j, 1], 2),
                     _tile_row(odd_g_qk[j, 2], 8), _tile_row(odd_g_qk[j, 3], 2)]
            outs = _in_odd(rows, j, x_lat, x_ctx, mods_l, g1, w_in_odd, gains, g64, tab64, odd_states)
            odd_states = list(outs[6:])
            o_lat = _attn_odd(True, rows, j, outs[:2], outs[2:6], odd_caches, win_sink[j])
            o_ctx = _attn_odd(False, rows, j, outs[:2], odd_states, None, win_sink[j])

        w_r = jnp.pad(moe_w_router[layer], ((0, 0), (0, LANES - N_EXPERTS)))
        wrh = w_r.astype(BF16)
        wrl = (w_r - wrh.astype(F32)).astype(BF16)
        x1, h2, afft = _post(rows, j, o_lat, o_ctx, w_out, x_lat, x_ctx, mods_l, g2,
                             jnp.concatenate([wrh, wrl], axis=1), wrh)
        aff_lat = (afft[:lat_chunks].reshape(rows.lat_b, rows.lat_n // SUB_ROWS, N_EXPERTS, SUB_ROWS)
                   .transpose(0, 2, 1, 3).reshape(rows.lat_b * N_EXPERTS, rows.lat_n))
        aff_ctx = afft[lat_chunks:].reshape(rows.ctx_b * N_EXPERTS, rows.ctx_n)
        affs = (aff_lat, aff_ctx)
        slots = _route(caps, affs)
        xs_l, gt_l, xs_c, gt_c = _gather(rows, caps, slots, affs, h2)
        ys = _ffn(layer, [xs_l, xs_c], [gt_l, gt_c], moe_w1, moe_w3, moe_w2)
        x_lat, x_ctx = _combine(rows, caps, slots, ys, x1, mods_l)

    def token_major(arr, heads):
        b, l, f, t = arr.shape
        return jnp.transpose(arr.reshape(b, l, heads, f // heads, t), (0, 1, 4, 2, 3))

    diff_shape = (rows.ctx_b, n_even, rows.ctx_n, DIFF_HEADS, LANES)
    return (x_ctx.reshape(x_prompt.shape), x_lat.reshape(x_sample.shape),
            even_states[0],
            jnp.swapaxes(even_states[1], 2, 3),
            even_states[2].reshape(diff_shape),
            even_states[3].reshape(diff_shape),
            token_major(odd_states[0], 2), token_major(odd_states[1], 2),
            token_major(odd_states[2], 2), token_major(odd_states[3], 2))
```

```python
import functools
import math
from typing import NamedTuple

import jax
import jax.numpy as jnp
from jax import lax
from jax.experimental import pallas as pl
from jax.experimental.pallas import tpu as pltpu

F32 = jnp.float32
BF16 = jnp.bfloat16

D_MODEL = 1024
DEPTH = 4
GRID_W = 64
ROPE_THETA = 10000.0
WINDOW = 128
RMS_EPS = 1e-6
MLA_HEADS = 8
MLA_Q_LORA = 256
MLA_KV_LORA = 128
MLA_NOPE = 64
MLA_ROPE = 32
MLA_VD = 64
DIFF_HEADS = 4
DIFF_HD = 64
HEAD_DIM = 64
N_EXPERTS = 16
EC_CAPACITY_FACTOR = 2
EXPERT_FF = 2048

LANES = 128
MXU_DIM = 256
ROW_TILE = 1024
SUB_ROWS = 256
N_SUB = ROW_TILE // SUB_ROWS
SCATTER_ROWS = 512
Q_TILE_DENSE = 512
Q_TILE_BANDED = 256
FF_CHUNK = 1024
VMEM_LIMIT = 50 * 1024 * 1024
FFN_VMEM_LIMIT = 60 * 1024 * 1024
NEG_BIG = -1e30
LOG2E = 1.4426950408889634


def _params(*sem, vmem=VMEM_LIMIT):
    return pltpu.CompilerParams(dimension_semantics=sem, vmem_limit_bytes=vmem)


def _full(shape):
    zeros = (0,) * len(shape)
    return pl.BlockSpec(shape, lambda *_: zeros)


def _dot(a, b):
    return jnp.dot(a, b, preferred_element_type=F32)


def _dot_nt(a, b):
    return lax.dot_general(a, b, (((1,), (1,)), ((), ())), preferred_element_type=F32)


def _rms(x, g):
    ms = jnp.mean(x * x, axis=-1, keepdims=True)
    return x * lax.rsqrt(ms + RMS_EPS) * g


def _group_rms(x, g, gmat):
    outs = []
    width = x.shape[1]
    for k in range(0, width, MXU_DIM):
        wd = min(MXU_DIM, width - k)
        xb = x[:, k:k + wd]
        ms = _dot((xb * xb).astype(BF16), gmat[:wd, :wd])
        outs.append(xb * lax.rsqrt(ms + RMS_EPS))
    y = outs[0] if len(outs) == 1 else jnp.concatenate(outs, axis=1)
    return y * g


def _rope(x, cos, sin, half):
    lane = lax.broadcasted_iota(jnp.int32, (1, LANES), 1)
    first = (lane % (2 * half)) < half
    outs = []
    for k in range(x.shape[1] // LANES):
        xb = x[:, k * LANES:(k + 1) * LANES]
        fwd = pltpu.roll(xb, LANES - half, 1)
        bwd = pltpu.roll(xb, half, 1)
        outs.append(xb * cos + jnp.where(first, fwd, bwd) * sin)
    return outs[0] if len(outs) == 1 else jnp.concatenate(outs, axis=1)


def _silu(a):
    return a / (1.0 + jnp.exp(-a))


def _sub(c):
    return slice(c * SUB_ROWS, (c + 1) * SUB_ROWS)


def _adaln_kernel(c_ref, w_ref, b_ref, o_ref):
    a = _silu(c_ref[...]).astype(BF16)
    bias = b_ref[pl.ds(pl.program_id(0), 1), :]
    o_ref[0] = _dot(a, w_ref[0].astype(BF16)) + bias


def _adaln(cvec, w_mod, b_mod):
    rows = cvec.shape[0]
    nc = 4
    wc = 6 * D_MODEL // nc
    return pl.pallas_call(
        _adaln_kernel,
        grid=(DEPTH, nc),
        in_specs=[_full((rows, D_MODEL)),
                  pl.BlockSpec((1, D_MODEL, wc), lambda l, n: (l, 0, n)),
                  pl.BlockSpec((DEPTH, wc), lambda l, n: (0, n))],
        out_specs=pl.BlockSpec((1, rows, wc), lambda l, n: (l, 0, n)),
        out_shape=jax.ShapeDtypeStruct((DEPTH, rows, 6 * D_MODEL), F32),
        compiler_params=_params("parallel", "parallel"),
        name="adaln",
    )(cvec, w_mod, b_mod)


class _Rows(NamedTuple):
    lat_b: int
    lat_n: int
    ctx_b: int
    ctx_n: int

    @property
    def lat_rows(self):
        return self.lat_b * self.lat_n

    @property
    def ctx_rows(self):
        return self.ctx_b * self.ctx_n

    @property
    def lat_tiles(self):
        return self.lat_rows // ROW_TILE

    @property
    def tiles(self):
        return (self.lat_rows + self.ctx_rows) // ROW_TILE

    @property
    def tiles_per_lat_req(self):
        return self.lat_n // ROW_TILE


class _TileSpecs:
    def __init__(self, rows):
        lat_tiles = rows.lat_tiles
        per_req = rows.tiles_per_lat_req
        self.lat_tiles = lat_tiles
        self.mod = pl.BlockSpec((1, 6, D_MODEL), lambda i: (jnp.where(i < lat_tiles, 1 + i // per_req, 0), 0, 0))
        self.table = pl.BlockSpec((ROW_TILE, LANES), lambda i: (jnp.where(i < lat_tiles, i % per_req, per_req), 0))

    def rows(self, width):
        return pl.BlockSpec((ROW_TILE, width), lambda i: (i, 0))

    def lat_rows(self, width):
        last = self.lat_tiles - 1
        return pl.BlockSpec((ROW_TILE, width), lambda i: (jnp.minimum(i, last), 0))

    def ctx_rows(self, width):
        first = self.lat_tiles
        return pl.BlockSpec((ROW_TILE, width), lambda i: (jnp.maximum(i - first, 0), 0))

    def chunks(self, feats):
        return pl.BlockSpec((N_SUB, feats, SUB_ROWS), lambda i: (i, 0, 0))

    def state(self, layer_j, feats, cols):
        first = self.lat_tiles
        return pl.BlockSpec((N_SUB, None, feats, cols), lambda i: (jnp.maximum(i - first, 0), layer_j, 0, 0))


def _pick_rows(lat_tiles, lat_ref, ctx_ref, r):
    return jnp.where(pl.program_id(0) < lat_tiles, lat_ref[r, :], ctx_ref[r, :])


def _modulated(x, mod, g, shift_row, scale_row):
    return _rms(x, g) * (1.0 + mod[scale_row:scale_row + 1]) + mod[shift_row:shift_row + 1]


def _in_even_kernel(lat_tiles, *refs):
    (xl_ref, xc_ref, mod_ref, g1_ref, wt_ref, gqa_ref, gkva_ref, wuq_ref, wukv_ref,
     gqn_ref, gqr_ref, gkn_ref, gkr_ref, gdq_ref, gdk_ref, g64_ref, g32_ref,
     c64_ref, s64_ref, c32_ref, s32_ref) = refs[:21]
    refs = refs[21 + 4:]
    qn_ref, qr_ref, kn_ref, vm_ref, krt_ref, mq_ref, mk_ref, mv_ref = refs[:8]
    ckv_s_ref, krt_s_ref, mk_s_ref, mv_s_ref = refs[8:]
    g64 = g64_ref[...]
    g32 = g32_ref[...]
    mod = mod_ref[0]
    mla_scale = (MLA_NOPE + MLA_ROPE) ** -0.5 * LOG2E
    diff_scale = DIFF_HD ** -0.5 * LOG2E

    for c in range(N_SUB):
        r = _sub(c)
        h = _modulated(_pick_rows(lat_tiles, xl_ref, xc_ref, r), mod, g1_ref[...], 0, 1)
        proj = _dot_nt(h.astype(BF16), wt_ref[...])
        cq = _rms(proj[:, 0:256], gqa_ref[...])
        q = _dot(cq.astype(BF16), wuq_ref[...])
        qn = _group_rms(q[:, 0:512], gqn_ref[...], g64)
        qr = _group_rms(q[:, 512:768], gqr_ref[...], g32)
        ckv = _rms(proj[:, 256:384], gkva_ref[...])
        kv = _dot(ckv.astype(BF16), wukv_ref[...])
        kn = _group_rms(kv[:, 0:512], gkn_ref[...], g64)
        mq = _group_rms(proj[:, 384:896], gdq_ref[...], g64)
        mk = _group_rms(proj[:, 896:1408], gdk_ref[...], g64)
        mv = proj[:, 1408:1920]
        kr = _group_rms(proj[:, 1920:2048], gkr_ref[...], g32)
        ckv_s_ref[c] = ckv
        krt_s_ref[c] = kr.T[0:MLA_ROPE]
        for hd in range(DIFF_HEADS):
            mk_s_ref[c, pl.ds(hd, SUB_ROWS, stride=DIFF_HEADS), :] = mk[:, hd * LANES:(hd + 1) * LANES]
            mv_s_ref[c, pl.ds(hd, SUB_ROWS, stride=DIFF_HEADS), :] = mv[:, hd * LANES:(hd + 1) * LANES]
        c64, s64, c32, s32 = c64_ref[r, :], s64_ref[r, :], c32_ref[r, :], s32_ref[r, :]
        qr = _rope(qr, c32, s32, MLA_ROPE // 4)
        kr = _rope(kr, c32, s32, MLA_ROPE // 4)
        mq = _rope(mq, c64, s64, DIFF_HD // 4)
        mk = _rope(mk, c64, s64, DIFF_HD // 4)
        qn_ref[r, :] = (qn * mla_scale).astype(BF16)
        qr_ref[r, :] = (qr * mla_scale).astype(BF16)
        kn_ref[r, :] = kn.astype(BF16)
        vm_ref[r, :] = kv[:, 512:1024].astype(BF16)
        krt_ref[r, :] = kr.astype(BF16)
        mq_ref[r, :] = (mq * diff_scale).astype(BF16)
        mk_ref[r, :] = mk.astype(BF16)
        mv_ref[r, :] = mv.astype(BF16)


def _in_odd_kernel(lat_tiles, *refs):
    (xl_ref, xc_ref, mod_ref, g1_ref, w_ref, gqc_ref, gkc_ref, gqd_ref, gkd_ref, g64_ref,
     c64_ref, s64_ref) = refs[:12]
    refs = refs[12 + 4:]
    qc_ref, qd_ref = refs[:2]
    chunk_refs = refs[2:6]
    state_refs = refs[6:10]
    g64 = g64_ref[...]
    mod = mod_ref[0]
    w = w_ref[...].astype(BF16)
    scale = HEAD_DIM ** -0.5 * LOG2E

    for c in range(N_SUB):
        r = _sub(c)
        h = _modulated(_pick_rows(lat_tiles, xl_ref, xc_ref, r), mod, g1_ref[...], 0, 1)
        proj = _dot(h.astype(BF16), w)
        qc = _group_rms(proj[:, 0:512], gqc_ref[...], g64)
        kc = _group_rms(proj[:, 512:640], gkc_ref[...], g64)
        qd = _group_rms(proj[:, 768:1280], gqd_ref[...], g64)
        kd = _group_rms(proj[:, 1280:1408], gkd_ref[...], g64)
        vct = proj[:, 640:768].T
        vdt = proj[:, 1408:1536].T
        state_refs[0][c] = kc.T
        state_refs[1][c] = vct
        state_refs[2][c] = kd.T
        state_refs[3][c] = vdt
        c64, s64 = c64_ref[r, :], s64_ref[r, :]
        qc = _rope(qc, c64, s64, HEAD_DIM // 4)
        kc = _rope(kc, c64, s64, HEAD_DIM // 4)
        qd = _rope(qd, c64, s64, HEAD_DIM // 4)
        kd = _rope(kd, c64, s64, HEAD_DIM // 4)
        qc_ref[r, :] = (qc * scale).astype(BF16)
        qd_ref[r, :] = (qd * scale).astype(BF16)
        chunk_refs[0][c] = kc.T
        chunk_refs[1][c] = vct
        chunk_refs[2][c] = kd.T
        chunk_refs[3][c] = vdt


def _with_states(specs, layer_j, in_specs, args, out_shape, out_specs, states):
    aliases = {}
    for st in states:
        aliases[len(args)] = len(out_shape)
        in_specs.append(pl.BlockSpec(memory_space=pl.ANY))
        args.append(st)
        out_shape.append(jax.ShapeDtypeStruct(st.shape, st.dtype))
        out_specs.append(specs.state(layer_j, st.shape[2], st.shape[3]))
    return aliases


def _in_even(rows, layer_j, x_lat, x_ctx, mods, g1, wt, wuq, wukv, gains, gmats, tables, states):
    specs = _TileSpecs(rows)
    n_rows = rows.lat_rows + rows.ctx_rows
    vec_specs = [_full(g.shape) for g in gains]
    in_specs = ([specs.lat_rows(D_MODEL), specs.ctx_rows(D_MODEL), specs.mod, _full(g1.shape), _full(wt.shape)]
                + vec_specs[:2] + [_full(wuq.shape), _full(wukv.shape)] + vec_specs[2:]
                + [_full((MXU_DIM, MXU_DIM))] * 2 + [specs.table] * 4)
    args = ([x_lat, x_ctx, mods, g1, wt, gains[0], gains[1], wuq, wukv] + list(gains[2:]) + list(gmats)
            + list(tables))
    widths = [512, 256, 512, 512, 128, 512, 512, 512]
    out_shape = [jax.ShapeDtypeStruct((n_rows, wd), BF16) for wd in widths]
    out_specs = [specs.rows(wd) for wd in widths]
    aliases = _with_states(specs, layer_j, in_specs, args, out_shape, out_specs, states)
    return pl.pallas_call(
        functools.partial(_in_even_kernel, rows.lat_tiles),
        grid=(rows.tiles,),
        in_specs=in_specs, out_specs=out_specs, out_shape=out_shape,
        input_output_aliases=aliases,
        compiler_params=_params("arbitrary"),
        name="in_even",
    )(*args)


def _in_odd(rows, layer_j, x_lat, x_ctx, mods, g1, w_all, gains, g64, tables, states):
    specs = _TileSpecs(rows)
    n_rows = rows.lat_rows + rows.ctx_rows
    w_spec = pl.BlockSpec((None,) + w_all.shape[1:], lambda i: (layer_j, 0, 0))
    in_specs = ([specs.lat_rows(D_MODEL), specs.ctx_rows(D_MODEL), specs.mod, _full(g1.shape), w_spec]
                + [_full(g.shape) for g in gains] + [_full((MXU_DIM, MXU_DIM))] + [specs.table] * 2)
    args = [x_lat, x_ctx, mods, g1, w_all] + list(gains) + [g64] + list(tables)
    out_shape = ([jax.ShapeDtypeStruct((n_rows, 512), BF16)] * 2
                 + [jax.ShapeDtypeStruct((n_rows // SUB_ROWS, LANES, SUB_ROWS), F32)] * 4)
    out_specs = [specs.rows(512), specs.rows(512)] + [specs.chunks(LANES)] * 4
    aliases = _with_states(specs, layer_j, in_specs, args, out_shape, out_specs, states)
    return pl.pallas_call(
        functools.partial(_in_odd_kernel, rows.lat_tiles),
        grid=(rows.tiles,),
        in_specs=in_specs, out_specs=out_specs, out_shape=out_shape,
        input_output_aliases=aliases,
        compiler_params=_params("arbitrary"),
        name="in_odd",
    )(*args)


def _cache_kv_kernel(ckv_ref, wukv_ref, gkn_ref, g64_ref, kn_ref, vm_ref):
    kv = _dot(ckv_ref[...].astype(BF16), wukv_ref[...])
    kn_ref[...] = _group_rms(kv[:, 0:512], gkn_ref[...], g64_ref[...]).astype(BF16)
    vm_ref[...] = kv[:, 512:1024].astype(BF16)


def _cache_kv(ckv, wukv, gkn, g64):
    rows = ckv.shape[0]
    tile = 512
    spec = lambda wd: pl.BlockSpec((tile, wd), lambda i: (i, 0))
    return pl.pallas_call(
        _cache_kv_kernel,
        grid=(rows // tile,),
        in_specs=[spec(MLA_KV_LORA), _full(wukv.shape), _full(gkn.shape), _full((MXU_DIM, MXU_DIM))],
        out_specs=[spec(512), spec(512)],
        out_shape=[jax.ShapeDtypeStruct((rows, 512), BF16)] * 2,
        compiler_params=_params("parallel"),
        name="cache_kv",
    )(ckv, wukv, gkn, g64)


def _softmax_pv(s_parts, v_parts, v_is_feature_major, sink=None):
    m = None
    for s in s_parts:
        mi = jnp.max(s, axis=-1, keepdims=True)
        m = mi if m is None else jnp.maximum(m, mi)
    if sink is not None:
        m = jnp.maximum(m, sink)
    acc = None
    for s, v in zip(s_parts, v_parts):
        p = jnp.exp2(s - m).astype(BF16)
        oi = _dot_nt(p, v) if v_is_feature_major else _dot(p, v)
        acc = oi if acc is None else acc + oi
    denom = acc[:, LANES:]
    if sink is not None:
        denom = denom + jnp.exp2(sink - m)
    return acc[:, :LANES] / denom


def _lane_masks(width):
    lane = lax.broadcasted_iota(jnp.int32, (1, LANES), 1)
    return [jnp.where(lane // width == k, 1.0, 0.0).astype(BF16) for k in range(LANES // width)]


def _attn_even_kernel(latent, lam_init, *refs):
    qn_ref, qr_ref, mq_ref = refs[:3]
    refs = refs[3:]
    n_parts = 2 if latent else 1
    parts = [refs[5 * k:5 * k + 5] for k in range(n_parts)]
    lamv_ref, gsub_ref, o_ref = refs[5 * n_parts:]
    lane = lax.broadcasted_iota(jnp.int32, (1, LANES), 1)
    low = lane < 64
    m64 = _lane_masks(64)
    m32 = _lane_masks(32)
    ones = [jnp.ones((p[0].shape[0], LANES), BF16) for p in parts]

    def diff_head(ref, idx, hd):
        if latent and idx == 0:
            n_keys = ref.shape[0] // DIFF_HEADS
            return ref[pl.ds(hd, n_keys, stride=DIFF_HEADS), :].astype(BF16)
        return ref[:, hd * LANES:(hd + 1) * LANES]

    for i in range(MLA_HEADS // 2):
        cols = slice(i * LANES, (i + 1) * LANES)
        qn_b = qn_ref[:, cols]
        kcat = [jnp.concatenate([p[0][:, cols], p[2][...]], axis=1) for p in parts]
        vms = [jnp.concatenate([p[1][:, cols], one], axis=1) for p, one in zip(parts, ones)]
        outs = []
        for half in range(2):
            head = 2 * i + half
            rb = head // 4
            qr_b = qr_ref[:, rb * LANES:(rb + 1) * LANES]
            lhs = jnp.concatenate([qn_b * m64[half], qr_b * m32[head % 4]], axis=1)
            outs.append(_softmax_pv([_dot_nt(lhs, kc) for kc in kcat], vms, False))
        o_ref[:, cols] = jnp.where(low, outs[0], outs[1]).astype(BF16)

    lv = lamv_ref[...]
    lam = (jnp.exp(jnp.sum(lv[0:1] * lv[1:2], axis=-1, keepdims=True))
           - jnp.exp(jnp.sum(lv[2:3] * lv[3:4], axis=-1, keepdims=True)) + lam_init)
    gsub = gsub_ref[...]
    for hd in range(DIFF_HEADS):
        q_b = mq_ref[:, hd * LANES:(hd + 1) * LANES]
        ks = [diff_head(p[3], idx, hd) for idx, p in enumerate(parts)]
        vs = [jnp.concatenate([diff_head(p[4], idx, hd), one], axis=1)
              for idx, (p, one) in enumerate(zip(parts, ones))]
        a1 = _softmax_pv([_dot_nt(q_b * m64[0], k) for k in ks], vs, False)
        a2 = _softmax_pv([_dot_nt(q_b * m64[1], k) for k in ks], vs, False)
        d = _rms(a1 - lam * a2, gsub) * (1.0 - lam_init)
        o_ref[:, 512 + hd * LANES:512 + (hd + 1) * LANES] = d.astype(BF16)


def _attn_even(latent, rows, layer_j, lam_init, q_arrs, kv_arrs, cache_arrs, lamv, gsub):
    batch, n_tokens, row0 = (rows.lat_b, rows.lat_n, 0) if latent else (rows.ctx_b, rows.ctx_n, rows.lat_rows)
    tq = min(Q_TILE_DENSE, n_tokens)
    nt = n_tokens // tq
    q0 = row0 // tq
    k0 = row0 // n_tokens
    qspec = lambda wd: pl.BlockSpec((tq, wd), lambda b, t: (q0 + b * nt + t, 0))
    kspec = lambda wd: pl.BlockSpec((n_tokens, wd), lambda b, t: (k0 + b, 0))
    kwidths = [512, 512, 128, 512, 512]
    in_specs = [qspec(512), qspec(256), qspec(512)]
    args = list(q_arrs)
    if latent:
        past = cache_arrs[0].shape[0] // batch
        in_specs += [pl.BlockSpec((past, wd), lambda b, t: (b, 0)) for wd in kwidths[:3]]
        in_specs += [pl.BlockSpec((None, None, past * DIFF_HEADS, LANES), lambda b, t: (b, layer_j, 0, 0))] * 2
        args += list(cache_arrs)
    in_specs += [kspec(wd) for wd in kwidths]
    args += list(kv_arrs)
    in_specs += [_full(lamv.shape), _full(gsub.shape)]
    args += [lamv, gsub]
    return pl.pallas_call(
        functools.partial(_attn_even_kernel, latent, lam_init),
        grid=(batch, nt),
        in_specs=in_specs,
        out_specs=pl.BlockSpec((tq, D_MODEL), lambda b, t: (b * nt + t, 0)),
        out_shape=jax.ShapeDtypeStruct((batch * n_tokens, D_MODEL), BF16),
        compiler_params=_params("parallel", "parallel"),
        name="attn_even_lat" if latent else "attn_even_ctx",
    )(*args)


def _attn_odd_kernel(latent, tq, *refs):
    qc_ref, qd_ref = refs[:2]
    refs = refs[2:]
    n_parts = 2 if latent else 1
    parts = [list(refs[4 * k:4 * k + 4]) for k in range(n_parts)]
    sink_ref, o_ref = refs[4 * n_parts:4 * n_parts + 2]
    scratch = refs[4 * n_parts + 2:]
    lane = lax.broadcasted_iota(jnp.int32, (1, LANES), 1)
    low = lane < 64
    m64 = _lane_masks(64)

    band = None
    start = 0
    span = 0
    if latent:
        for ref, scr in zip(parts[-1], scratch):
            for c in range(ref.shape[0]):
                scr[:, _sub(c)] = ref[c]
        parts[-1] = list(scratch)
        n_new = scratch[0].shape[1]
        span = min(n_new, tq + 2 * WINDOW)
        t = pl.program_id(1)
        start = pl.multiple_of(jnp.clip(t * tq - WINDOW, 0, n_new - span), LANES)
        qpos = t * tq + lax.broadcasted_iota(jnp.int32, (tq, 1), 0)
        kpos = start + lax.broadcasted_iota(jnp.int32, (1, span), 1)
        band = jnp.abs(qpos - kpos) <= WINDOW

    for kind in range(2):
        q_ref = qc_ref if kind == 0 else qd_ref
        windowed = latent and kind == 0
        for g in range(2):
            rows = slice(g * HEAD_DIM, (g + 1) * HEAD_DIM)
            ks, vs = [], []
            for idx, p in enumerate(parts):
                k_ref, v_ref = p[2 * kind], p[2 * kind + 1]
                if windowed and idx == n_parts - 1:
                    k = k_ref[rows, pl.ds(start, span)]
                    v = v_ref[rows, pl.ds(start, span)]
                else:
                    k = k_ref[rows, :]
                    v = v_ref[rows, :]
                ks.append(jnp.concatenate([k, k], axis=0).astype(BF16))
                vs.append(jnp.concatenate([v, v, jnp.ones((LANES, v.shape[1]), F32)], axis=0).astype(BF16))
            for i in (2 * g, 2 * g + 1):
                cols = slice(i * LANES, (i + 1) * LANES)
                q_b = q_ref[:, cols]
                outs = []
                for half in range(2):
                    s_parts = [_dot(q_b * m64[half], k) for k in ks]
                    sink = None
                    if kind == 0:
                        sink = sink_ref[2 * i + half] * LOG2E
                        if latent:
                            s_parts[-1] = jnp.where(band, s_parts[-1], NEG_BIG)
                    outs.append(_softmax_pv(s_parts, vs, True, sink))
                o_ref[:, kind * 512 + i * LANES:kind * 512 + (i + 1) * LANES] = (
                    jnp.where(low, outs[0], outs[1]).astype(BF16))


def _attn_odd(latent, rows, layer_j, q_arrs, new_arrs, cache_arrs, sink):
    batch, n_tokens, row0 = (rows.lat_b, rows.lat_n, 0) if latent else (rows.ctx_b, rows.ctx_n, rows.lat_rows)
    tq = min(Q_TILE_BANDED, n_tokens)
    nt = n_tokens // tq
    q0 = row0 // tq
    in_specs = [pl.BlockSpec((tq, 512), lambda b, t: (q0 + b * nt + t, 0))] * 2
    args = list(q_arrs)
    scratch = []
    if latent:
        past = cache_arrs[0].shape[-1]
        in_specs += [pl.BlockSpec((None, None, LANES, past), lambda b, t: (b, layer_j, 0, 0))] * 4
        args += list(cache_arrs)
        chunks = n_tokens // SUB_ROWS
        in_specs += [pl.BlockSpec((chunks, LANES, SUB_ROWS), lambda b, t: (b, 0, 0))] * 4
        scratch = [pltpu.VMEM((LANES, n_tokens), F32)] * 4
    else:
        in_specs += [pl.BlockSpec((None, None, LANES, n_tokens), lambda b, t: (b, layer_j, 0, 0))] * 4
    args += list(new_arrs)
    in_specs += [pl.BlockSpec(memory_space=pltpu.SMEM)]
    args += [sink]
    return pl.pallas_call(
        functools.partial(_attn_odd_kernel, latent, tq),
        grid=(batch, nt),
        in_specs=in_specs,
        out_specs=pl.BlockSpec((tq, D_MODEL), lambda b, t: (b * nt + t, 0)),
        out_shape=jax.ShapeDtypeStruct((batch * n_tokens, D_MODEL), BF16),
        scratch_shapes=scratch,
        compiler_params=_params("parallel", "parallel"),
        name="attn_odd_lat" if latent else "attn_odd_ctx",
    )(*args)


def _post_kernel(lat_tiles, ol_ref, oc_ref, wout_ref, xl_ref, xc_ref, mod_ref, g2_ref, wr2_ref, wrh_ref,
                 x1_ref, h2_ref, afft_ref):
    mod = mod_ref[0]
    wout = wout_ref[...].astype(BF16)
    lane = lax.broadcasted_iota(jnp.int32, (1, LANES), 1)
    for c in range(N_SUB):
        r = _sub(c)
        o = _pick_rows(lat_tiles, ol_ref, oc_ref, r)
        x1 = _pick_rows(lat_tiles, xl_ref, xc_ref, r) + mod[2:3] * _dot(o, wout)
        x1_ref[r, :] = x1
        h2 = _modulated(x1, mod, g2_ref[...], 3, 4)
        h_hi = h2.astype(BF16)
        h_lo = (h2 - h_hi.astype(F32)).astype(BF16)
        h2_ref[r, :] = h_hi
        both = _dot(h_hi, wr2_ref[...])
        logits = both[:, :LANES] + both[:, LANES:] + _dot(h_lo, wrh_ref[...])
        logits = jnp.where(lane < N_EXPERTS, logits, NEG_BIG)
        e = jnp.exp(logits - jnp.max(logits, axis=-1, keepdims=True))
        aff = e / jnp.sum(e, axis=-1, keepdims=True)
        afft_ref[c] = aff.T[0:N_EXPERTS]


def _post(rows, layer_j, o_lat, o_ctx, wout_all, x_lat, x_ctx, mods, g2, wr2, wrh):
    specs = _TileSpecs(rows)
    n_rows = rows.lat_rows + rows.ctx_rows
    w_spec = pl.BlockSpec((None,) + wout_all.shape[1:], lambda i: (layer_j, 0, 0))
    return pl.pallas_call(
        functools.partial(_post_kernel, rows.lat_tiles),
        grid=(rows.tiles,),
        in_specs=[specs.lat_rows(D_MODEL), specs.ctx_rows(D_MODEL), w_spec,
                  specs.lat_rows(D_MODEL), specs.ctx_rows(D_MODEL), specs.mod, _full(g2.shape),
                  _full(wr2.shape), _full(wrh.shape)],
        out_specs=[specs.rows(D_MODEL), specs.rows(D_MODEL), specs.chunks(N_EXPERTS)],
        out_shape=[jax.ShapeDtypeStruct((n_rows, D_MODEL), F32),
                   jax.ShapeDtypeStruct((n_rows, D_MODEL), BF16),
                   jax.ShapeDtypeStruct((n_rows // SUB_ROWS, N_EXPERTS, SUB_ROWS), F32)],
        compiler_params=_params("parallel"),
        name="post",
    )(o_lat, o_ctx, wout_all, x_lat, x_ctx, mods, g2, wr2, wrh)


def _route_kernel(caps, *refs):
    n_groups = len(caps)
    aff_refs, out_refs = refs[:n_groups], refs[n_groups:]
    bits = [lax.bitcast_convert_type(ref[...], jnp.int32) for ref in aff_refs]

    def body(_, carry):
        new = []
        for b, cap, (lo, hi) in zip(bits, caps, carry):
            mid = lo + ((hi - lo + 1) >> 1)
            cnt = jnp.sum(jnp.where(b >= mid, 1.0, 0.0), axis=-1, keepdims=True)
            ok = cnt >= cap
            new.append((jnp.where(ok, mid, lo), jnp.where(ok, hi, mid - 1)))
        return tuple(new)

    init = tuple((jnp.zeros((b.shape[0], 1), jnp.int32), jnp.full((b.shape[0], 1), 0x7F800000, jnp.int32))
                 for b in bits)
    found = lax.fori_loop(0, 31, body, init)
    for b, cap, (thr, _), out_ref in zip(bits, caps, found, out_refs):
        n = b.shape[1]
        above = jnp.where(b > thr, 1.0, 0.0)
        equal = jnp.where(b == thr, 1.0, 0.0)
        room = cap - jnp.sum(above, axis=-1, keepdims=True)
        before = jnp.where(lax.broadcasted_iota(jnp.int32, (n, n), 0)
                           < lax.broadcasted_iota(jnp.int32, (n, n), 1), 1.0, 0.0).astype(BF16)
        equal_before = _dot(equal.astype(BF16), before)
        chosen = above + equal * jnp.where(equal_before < room, 1.0, 0.0)
        slot = _dot(chosen.astype(BF16), before)
        out_ref[...] = jnp.where(chosen > 0.5, slot, -1.0)


def _route(caps, aff_list):
    return pl.pallas_call(
        functools.partial(_route_kernel, tuple(caps)),
        grid=(1,),
        in_specs=[_full(a.shape) for a in aff_list],
        out_specs=[_full(a.shape) for a in aff_list],
        out_shape=[jax.ShapeDtypeStruct(a.shape, F32) for a in aff_list],
        compiler_params=_params("arbitrary"),
        name="route",
    )(*aff_list)


def _gather_part(cap, req_per_step, experts_per_step, first, slot_ref, aff_ref, h_ref, xs_ref, gate_ref):
    n = slot_ref.shape[1]
    want = lax.broadcasted_iota(jnp.int32, (cap, 1), 0).astype(F32)
    for rq in range(req_per_step):
        out_rows = slice(rq * cap, (rq + 1) * cap)
        rows = []
        for k in range(experts_per_step):
            row = rq * N_EXPERTS + first + k
            hit = slot_ref[pl.ds(row, 1), :] == want
            rows.append(jnp.where(hit, 1.0, 0.0).astype(BF16))
            gate_ref[k, out_rows, :] = jnp.sum(jnp.where(hit, aff_ref[pl.ds(row, 1), :], 0.0),
                                               axis=-1, keepdims=True)
        picked = _dot(jnp.concatenate(rows, axis=0), h_ref[rq * n:(rq + 1) * n, :]).astype(BF16)
        for k in range(experts_per_step):
            xs_ref[k, out_rows, :] = picked[k * cap:(k + 1) * cap]


def _gather_kernel(lat_steps, groups, caps, ctx_req_per_step,
                   sl_l, af_l, h_l, sl_c, af_c, h_c, xs_l, gt_l, xs_c, gt_c):
    s = pl.program_id(0)
    per_group = N_EXPERTS // groups

    @pl.when(s < lat_steps)
    def _():
        _gather_part(caps[0], 1, per_group, (s % groups) * per_group, sl_l, af_l, h_l, xs_l, gt_l)

    @pl.when(s >= lat_steps)
    def _():
        _gather_part(caps[1], ctx_req_per_step, N_EXPERTS, 0, sl_c, af_c, h_c, xs_c, gt_c)


def _gather(rows, caps, slots, affs, h2):
    groups = rows.lat_n // SUB_ROWS
    per_group = N_EXPERTS // groups
    rps = ROW_TILE // rows.ctx_n
    lat_steps = rows.lat_b * groups
    ctx_steps = rows.ctx_b // rps
    last_b = rows.lat_b - 1
    ctx_block0 = rows.lat_rows // (rps * rows.ctx_n)
    lat_req = lambda s: jnp.minimum(s // groups, last_b)
    lat_grp = lambda s: jnp.where(s < lat_steps, s % groups, groups - 1)
    ctx_step = lambda s: jnp.maximum(s - lat_steps, 0)
    cap_l, cap_c = caps
    lat_rows_spec = pl.BlockSpec((N_EXPERTS, rows.lat_n), lambda s: (lat_req(s), 0))
    ctx_rows_spec = pl.BlockSpec((rps * N_EXPERTS, rows.ctx_n), lambda s: (ctx_step(s), 0))
    return pl.pallas_call(
        functools.partial(_gather_kernel, lat_steps, groups, caps, rps),
        grid=(lat_steps + ctx_steps,),
        in_specs=[lat_rows_spec, lat_rows_spec,
                  pl.BlockSpec((rows.lat_n, D_MODEL), lambda s: (lat_req(s), 0)),
                  ctx_rows_spec, ctx_rows_spec,
                  pl.BlockSpec((rps * rows.ctx_n, D_MODEL), lambda s: (ctx_block0 + ctx_step(s), 0))],
        out_specs=[pl.BlockSpec((per_group, cap_l, D_MODEL), lambda s: (lat_grp(s), lat_req(s), 0)),
                   pl.BlockSpec((per_group, cap_l, 1), lambda s: (lat_grp(s), lat_req(s), 0)),
                   pl.BlockSpec((N_EXPERTS, rps * cap_c, D_MODEL), lambda s: (0, ctx_step(s), 0)),
                   pl.BlockSpec((N_EXPERTS, rps * cap_c, 1), lambda s: (0, ctx_step(s), 0))],
        out_shape=[jax.ShapeDtypeStruct((N_EXPERTS, rows.lat_b * cap_l, D_MODEL), BF16),
                   jax.ShapeDtypeStruct((N_EXPERTS, rows.lat_b * cap_l, 1), F32),
                   jax.ShapeDtypeStruct((N_EXPERTS, rows.ctx_b * cap_c, D_MODEL), BF16),
                   jax.ShapeDtypeStruct((N_EXPERTS, rows.ctx_b * cap_c, 1), F32)],
        compiler_params=_params("arbitrary"),
        name="gather",
    )(slots[0], affs[0], h2, slots[1], affs[1], h2)


def _ffn_kernel(n_groups, *refs):
    xs_refs = refs[:n_groups]
    gate_refs = refs[n_groups:2 * n_groups]
    w1_ref, w3_ref, w2_ref = refs[2 * n_groups:2 * n_groups + 3]
    ys_refs = refs[2 * n_groups + 3:3 * n_groups + 3]
    acc_refs = refs[3 * n_groups + 3:]
    f = pl.program_id(1)

    @pl.when(f == 0)
    def _():
        for acc_ref in acc_refs:
            acc_ref[...] = jnp.zeros(acc_ref.shape, F32)

    hidden = [[] for _ in xs_refs]
    for n in range(0, w1_ref.shape[3], MXU_DIM):
        w1 = w1_ref[0, 0, :, n:n + MXU_DIM].astype(BF16)
        w3 = w3_ref[0, 0, :, n:n + MXU_DIM].astype(BF16)
        for parts, xs_ref in zip(hidden, xs_refs):
            xs = xs_ref[0]
            parts.append((_silu(_dot(xs, w1)) * _dot(xs, w3)).astype(BF16))
    hidden = [jnp.concatenate(parts, axis=1) for parts in hidden]
    for n in range(0, w2_ref.shape[3], MXU_DIM):
        w2 = w2_ref[0, 0, :, n:n + MXU_DIM].astype(BF16)
        for hid, acc_ref in zip(hidden, acc_refs):
            acc_ref[:, n:n + MXU_DIM] += _dot(hid, w2)

    @pl.when(f == pl.num_programs(1) - 1)
    def _():
        for gate_ref, ys_ref, acc_ref in zip(gate_refs, ys_refs, acc_refs):
            ys_ref[0] = (acc_ref[...] * gate_ref[0]).astype(BF16)


def _ffn(layer, xs_list, gate_list, w1, w3, w2):
    n_groups = len(xs_list)
    n_chunks = EXPERT_FF // FF_CHUNK
    xs_specs = [pl.BlockSpec((1,) + xs.shape[1:], lambda e, f: (e, 0, 0)) for xs in xs_list]
    gate_specs = [pl.BlockSpec((1,) + g.shape[1:], lambda e, f: (e, 0, 0)) for g in gate_list]
    return pl.pallas_call(
        functools.partial(_ffn_kernel, n_groups),
        grid=(N_EXPERTS, n_chunks),
        in_specs=xs_specs + gate_specs + [
            pl.BlockSpec((1, 1, D_MODEL, FF_CHUNK), lambda e, f: (layer, e, 0, f)),
            pl.BlockSpec((1, 1, D_MODEL, FF_CHUNK), lambda e, f: (layer, e, 0, f)),
            pl.BlockSpec((1, 1, FF_CHUNK, D_MODEL), lambda e, f: (layer, e, f, 0))],
        out_specs=xs_specs,
        out_shape=[jax.ShapeDtypeStruct(xs.shape, BF16) for xs in xs_list],
        scratch_shapes=[pltpu.VMEM(xs.shape[1:], F32) for xs in xs_list],
        compiler_params=_params("parallel", "arbitrary", vmem=FFN_VMEM_LIMIT),
        name="ffn",
    )(*xs_list, *gate_list, w1, w3, w2)


def _scatter_part(cap, slots, ys_ref, ys_rows, x, gate2):
    lane = lax.broadcasted_iota(jnp.int32, (1, LANES), 1).astype(F32)
    per_block = LANES // cap
    pad = jnp.zeros((LANES - N_EXPERTS, SUB_ROWS), F32)
    slots_t = jnp.concatenate([slots, pad], axis=0).T
    blocks = []
    for k in range(N_EXPERTS // per_block):
        hit = None
        for e in range(k * per_block, (k + 1) * per_block):
            col = slots_t[:, e:e + 1]
            target = jnp.where(col >= 0.0, col + float((e - k * per_block) * cap), -1.0)
            he = jnp.where(target == lane, 1.0, 0.0)
            hit = he if hit is None else hit + he
        blocks.append(hit.astype(BF16))
    onehot = jnp.concatenate(blocks, axis=1)
    ys = jnp.concatenate([ys_ref[e, ys_rows, :] for e in range(N_EXPERTS)], axis=0)
    return x + gate2 * _dot(onehot, ys)


def _combine_kernel(lat_steps, steps_per_req, caps, sl_l, ys_l, sl_c, ys_c, x_ref, mod_ref, ol_ref, oc_ref):
    s = pl.program_id(0)
    gate2 = mod_ref[0][5:6]
    chunks = SCATTER_ROWS // SUB_ROWS

    @pl.when(s < lat_steps)
    def _():
        first = (s % steps_per_req) * SCATTER_ROWS
        for c in range(chunks):
            tokens = pl.multiple_of(first + c * SUB_ROWS, SUB_ROWS)
            ol_ref[_sub(c), :] = _scatter_part(caps[0], sl_l[:, pl.ds(tokens, SUB_ROWS)], ys_l,
                                               slice(0, caps[0]), x_ref[_sub(c), :], gate2)

    @pl.when(s >= lat_steps)
    def _():
        for rq in range(chunks):
            oc_ref[_sub(rq), :] = _scatter_part(caps[1], sl_c[rq * N_EXPERTS:(rq + 1) * N_EXPERTS, :], ys_c,
                                                slice(rq * caps[1], (rq + 1) * caps[1]), x_ref[_sub(rq), :], gate2)


def _combine(rows, caps, slots, ys, x1, mods):
    per_req = rows.lat_n // SCATTER_ROWS
    req_per_step = SCATTER_ROWS // rows.ctx_n
    lat_steps = rows.lat_b * per_req
    ctx_steps = rows.ctx_b // req_per_step
    last_b = rows.lat_b - 1
    lat_req = lambda s: jnp.minimum(s // per_req, last_b)
    ctx_step = lambda s: jnp.maximum(s - lat_steps, 0)
    cap_l, cap_c = caps
    return pl.pallas_call(
        functools.partial(_combine_kernel, lat_steps, per_req, caps),
        grid=(lat_steps + ctx_steps,),
        in_specs=[pl.BlockSpec((N_EXPERTS, rows.lat_n), lambda s: (lat_req(s), 0)),
                  pl.BlockSpec((N_EXPERTS, cap_l, D_MODEL), lambda s: (0, lat_req(s), 0)),
                  pl.BlockSpec((req_per_step * N_EXPERTS, rows.ctx_n), lambda s: (ctx_step(s), 0)),
                  pl.BlockSpec((N_EXPERTS, req_per_step * cap_c, D_MODEL), lambda s: (0, ctx_step(s), 0)),
                  pl.BlockSpec((SCATTER_ROWS, D_MODEL), lambda s: (s, 0)),
                  pl.BlockSpec((1, 6, D_MODEL), lambda s: (jnp.where(s < lat_steps, 1 + s // per_req, 0), 0, 0))],
        out_specs=[pl.BlockSpec((SCATTER_ROWS, D_MODEL), lambda s: (jnp.minimum(s, lat_steps - 1), 0)),
                   pl.BlockSpec((SCATTER_ROWS, D_MODEL), lambda s: (ctx_step(s), 0))],
        out_shape=[jax.ShapeDtypeStruct((rows.lat_rows, D_MODEL), F32),
                   jax.ShapeDtypeStruct((rows.ctx_rows, D_MODEL), F32)],
        compiler_params=_params("arbitrary"),
        name="combine",
    )(slots[0], ys[0], slots[1], ys[1], x1, mods)


def _rope_tables(n_pos, rot_dim):
    t = jnp.arange(n_pos)
    row = t // GRID_W
    col = t % GRID_W
    nf = rot_dim // 4
    inv = ROPE_THETA ** (-jnp.arange(nf, dtype=F32) / nf)
    ang_r = row[:, None] * inv
    ang_c = col[:, None] * inv
    cr, sr, cc, sc = jnp.cos(ang_r), jnp.sin(ang_r), jnp.cos(ang_c), jnp.sin(ang_c)
    cos = jnp.concatenate([cr, cr, cc, cc], axis=1)
    sin = jnp.concatenate([-sr, sr, -sc, sc], axis=1)
    reps = LANES // rot_dim
    cos = jnp.concatenate([jnp.tile(cos, (1, reps)), jnp.ones((ROW_TILE, LANES), F32)], axis=0)
    sin = jnp.concatenate([jnp.tile(sin, (1, reps)), jnp.zeros((ROW_TILE, LANES), F32)], axis=0)
    return cos, sin


def _group_mean_matrix(width):
    idx = jnp.arange(MXU_DIM) // width
    return jnp.where(idx[:, None] == idx[None, :], 1.0 / width, 0.0).astype(BF16)


def _tile_row(g, reps):
    return jnp.tile(g, reps).reshape(1, -1)


def _split_heads(w, n_heads, first):
    k = w.shape[0]
    w3 = w.reshape(k, n_heads, -1)
    return jnp.concatenate([w3[:, :, :first].reshape(k, -1), w3[:, :, first:].reshape(k, -1)], axis=1)


def _feature_major(cache):
    b, l, t, h, d = cache.shape
    return jnp.transpose(cache, (0, 1, 3, 4, 2)).reshape(b, l, h * d, t)


def kernel(x_prompt, x_sample, cache_mla_ckv, cache_mla_krope, cache_diff_k, cache_diff_v,
           cache_win_k, cache_win_v, cache_axial_k, cache_axial_v, c, c_ctx,
           g_norm, w_mod, b_mod, w_in_even, w_out_even, mla_g_qa, mla_g_kva, mla_w_uq, mla_w_ukv,
           mla_g_q, mla_g_k, diff_g_q, diff_g_k, diff_lambda, diff_g_sub,
           w_in_odd, w_out_odd, odd_g_qk, win_sink, moe_w_router, moe_w1, moe_w3, moe_w2):
    rows = _Rows(lat_b=x_sample.shape[0], lat_n=x_sample.shape[1], ctx_b=x_prompt.shape[0], ctx_n=x_prompt.shape[1])
    past = cache_mla_ckv.shape[2]
    n_even, n_odd = w_in_even.shape[0], w_in_odd.shape[0]
    assert rows.ctx_n == SUB_ROWS and rows.lat_n % ROW_TILE == 0 and rows.ctx_b % N_SUB == 0
    caps = (EC_CAPACITY_FACTOR * rows.lat_n // N_EXPERTS, EC_CAPACITY_FACTOR * rows.ctx_n // N_EXPERTS)

    cvec = jnp.zeros((16, D_MODEL), F32).at[0].set(c_ctx).at[1:1 + rows.lat_b].set(c)
    mods = _adaln(cvec, w_mod, b_mod).reshape(DEPTH, 16, 6, D_MODEL)

    g64 = _group_mean_matrix(64)
    g32 = _group_mean_matrix(32)
    tab64 = _rope_tables(rows.lat_n, HEAD_DIM)
    tab32 = _rope_tables(rows.lat_n, MLA_ROPE)
    odd_caches = [_feature_major(a) for a in (cache_win_k, cache_win_v, cache_axial_k, cache_axial_v)]
    diff_caches = [a.reshape(rows.lat_b, n_even, past * DIFF_HEADS, LANES) for a in (cache_diff_k, cache_diff_v)]

    even_states = [jnp.zeros((rows.ctx_b, n_even, rows.ctx_n, MLA_KV_LORA), F32),
                   jnp.zeros((rows.ctx_b, n_even, MLA_ROPE, rows.ctx_n), F32),
                   jnp.zeros((rows.ctx_b, n_even, rows.ctx_n * DIFF_HEADS, LANES), F32),
                   jnp.zeros((rows.ctx_b, n_even, rows.ctx_n * DIFF_HEADS, LANES), F32)]
    odd_states = [jnp.zeros((rows.ctx_b, n_odd, LANES, rows.ctx_n), F32) for _ in range(4)]

    x_lat = x_sample.reshape(-1, D_MODEL)
    x_ctx = x_prompt.reshape(-1, D_MODEL)
    lat_chunks = rows.lat_rows // SUB_ROWS

    for layer in range(DEPTH):
        j = layer // 2
        mods_l = mods[layer]
        g1 = g_norm[layer, 0].reshape(1, -1)
        g2 = g_norm[layer, 1].reshape(1, -1)
        if layer % 2 == 0:
            wt = jnp.swapaxes(w_in_even[j], 0, 1)
            wt = jnp.concatenate([wt[:384], wt[416:]] + [wt[384:416]] * 4, axis=0).astype(BF16)
            wuq = _split_heads(mla_w_uq[j], MLA_HEADS, MLA_NOPE).astype(BF16)
            wukv = _split_heads(mla_w_ukv[j], MLA_HEADS, MLA_NOPE).astype(BF16)
            w_out = w_out_even
            gkn = _tile_row(mla_g_k[j, :MLA_NOPE], MLA_HEADS)
            gains = [mla_g_qa[j].reshape(1, -1), mla_g_kva[j].reshape(1, -1),
                     _tile_row(mla_g_q[j, :MLA_NOPE], MLA_HEADS), _tile_row(mla_g_q[j, MLA_NOPE:], MLA_HEADS),
                     gkn, _tile_row(mla_g_k[j, MLA_NOPE:], 4),
                     _tile_row(diff_g_q[j], 2 * DIFF_HEADS), _tile_row(diff_g_k[j], 2 * DIFF_HEADS)]
            lam_init = 0.8 - 0.6 * math.exp(-0.3 * layer)
            gsub = diff_g_sub[j].reshape(1, -1)
            outs = _in_even(rows, j, x_lat, x_ctx, mods_l, g1, wt, wuq, wukv, gains, (g64, g32),
                            tab64 + tab32, even_states)
            qn, qr, kn, vm, krt, mq, mk, mv = outs[:8]
            even_states = list(outs[8:])
            kn_c, vm_c = _cache_kv(cache_mla_ckv[:, j].reshape(-1, MLA_KV_LORA), wukv, gkn, g64)
            cache_arrs = (kn_c, vm_c,
                          jnp.tile(cache_mla_krope[:, j].reshape(-1, MLA_ROPE), (1, 4)).astype(BF16),
                          diff_caches[0], diff_caches[1])
            o_lat = _attn_even(True, rows, j, lam_init, (qn, qr, mq), (kn, vm, krt, mk, mv), cache_arrs,
                               diff_lambda[j], gsub)
            o_ctx = _attn_even(False, rows, j, lam_init, (qn, qr, mq), (kn, vm, krt, mk, mv), None,
                               diff_lambda[j], gsub)
        else:
            w_out = w_out_odd
            gains = [_tile_row(odd_g_qk[j, 0], 8), _tile_row(odd_g_qk[j, 1], 2),
                     _tile_row(odd_g_qk[j, 2], 8), _tile_row(odd_g_qk[j, 3], 2)]
            outs = _in_odd(rows, j, x_lat, x_ctx, mods_l, g1, w_in_odd, gains, g64, tab64, odd_states)
            odd_states = list(outs[6:])
            o_lat = _attn_odd(True, rows, j, outs[:2], outs[2:6], odd_caches, win_sink[j])
            o_ctx = _attn_odd(False, rows, j, outs[:2], odd_states, None, win_sink[j])

        w_r = jnp.pad(moe_w_router[layer], ((0, 0), (0, LANES - N_EXPERTS)))
        wrh = w_r.astype(BF16)
        wrl = (w_r - wrh.astype(F32)).astype(BF16)
        x1, h2, afft = _post(rows, j, o_lat, o_ctx, w_out, x_lat, x_ctx, mods_l, g2,
                             jnp.concatenate([wrh, wrl], axis=1), wrh)
        aff_lat = (afft[:lat_chunks].reshape(rows.lat_b, rows.lat_n // SUB_ROWS, N_EXPERTS, SUB_ROWS)
                   .transpose(0, 2, 1, 3).reshape(rows.lat_b * N_EXPERTS, rows.lat_n))
        aff_ctx = afft[lat_chunks:].reshape(rows.ctx_b * N_EXPERTS, rows.ctx_n)
        affs = (aff_lat, aff_ctx)
        slots = _route(caps, affs)
        xs_l, gt_l, xs_c, gt_c = _gather(rows, caps, slots, affs, h2)
        ys = _ffn(layer, [xs_l, xs_c], [gt_l, gt_c], moe_w1, moe_w3, moe_w2)
        x_lat, x_ctx = _combine(rows, caps, slots, ys, x1, mods_l)

    def token_major(arr, heads):
        b, l, f, t = arr.shape
        return jnp.transpose(arr.reshape(b, l, heads, f // heads, t), (0, 1, 4, 2, 3))

    diff_shape = (rows.ctx_b, n_even, rows.ctx_n, DIFF_HEADS, LANES)
    return (x_ctx.reshape(x_prompt.shape), x_lat.reshape(x_sample.shape),
            even_states[0],
            jnp.swapaxes(even_states[1], 2, 3),
            even_states[2].reshape(diff_shape),
            even_states[3].reshape(diff_shape),
            token_major(odd_states[0], 2), token_major(odd_states[1], 2),
            token_major(odd_states[2], 2), token_major(odd_states[3], 2))
```

```python
import functools
import math
from typing import NamedTuple

import jax
import jax.numpy as jnp
from jax import lax
from jax.experimental import pallas as pl
from jax.experimental.pallas import tpu as pltpu

F32 = jnp.float32
BF16 = jnp.bfloat16

D_MODEL = 1024
DEPTH = 4
GRID_W = 64
ROPE_THETA = 10000.0
WINDOW = 128
RMS_EPS = 1e-6
MLA_HEADS = 8
MLA_Q_LORA = 256
MLA_KV_LORA = 128
MLA_NOPE = 64
MLA_ROPE = 32
MLA_VD = 64
DIFF_HEADS = 4
DIFF_HD = 64
HEAD_DIM = 64
N_EXPERTS = 16
EC_CAPACITY_FACTOR = 2
EXPERT_FF = 2048

LANES = 128
MXU_DIM = 256
ROW_TILE = 1024
SUB_ROWS = 256
N_SUB = ROW_TILE // SUB_ROWS
SCATTER_ROWS = 512
Q_TILE_DENSE = 512
Q_TILE_BANDED = 256
FF_CHUNK = 1024
VMEM_LIMIT = 50 * 1024 * 1024
FFN_VMEM_LIMIT = 60 * 1024 * 1024
NEG_BIG = -1e30
LOG2E = 1.4426950408889634


def _params(*sem, vmem=VMEM_LIMIT):
    return pltpu.CompilerParams(dimension_semantics=sem, vmem_limit_bytes=vmem)


def _full(shape):
    zeros = (0,) * len(shape)
    return pl.BlockSpec(shape, lambda *_: zeros)


def _dot(a, b):
    return jnp.dot(a, b, preferred_element_type=F32)


def _dot_nt(a, b):
    return lax.dot_general(a, b, (((1,), (1,)), ((), ())), preferred_element_type=F32)


def _rms(x, g):
    ms = jnp.mean(x * x, axis=-1, keepdims=True)
    return x * lax.rsqrt(ms + RMS_EPS) * g


def _group_rms(x, g, gmat):
    outs = []
    width = x.shape[1]
    for k in range(0, width, MXU_DIM):
        wd = min(MXU_DIM, width - k)
        xb = x[:, k:k + wd]
        ms = _dot((xb * xb).astype(BF16), gmat[:wd, :wd])
        outs.append(xb * lax.rsqrt(ms + RMS_EPS))
    y = outs[0] if len(outs) == 1 else jnp.concatenate(outs, axis=1)
    return y * g


def _rope(x, cos, sin, half):
    lane = lax.broadcasted_iota(jnp.int32, (1, LANES), 1)
    first = (lane % (2 * half)) < half
    outs = []
    for k in range(x.shape[1] // LANES):
        xb = x[:, k * LANES:(k + 1) * LANES]
        fwd = pltpu.roll(xb, LANES - half, 1)
        bwd = pltpu.roll(xb, half, 1)
        outs.append(xb * cos + jnp.where(first, fwd, bwd) * sin)
    return outs[0] if len(outs) == 1 else jnp.concatenate(outs, axis=1)


def _silu(a):
    return a / (1.0 + jnp.exp(-a))


def _sub(c):
    return slice(c * SUB_ROWS, (c + 1) * SUB_ROWS)


def _adaln_kernel(c_ref, w_ref, b_ref, o_ref):
    a = _silu(c_ref[...]).astype(BF16)
    bias = b_ref[pl.ds(pl.program_id(0), 1), :]
    o_ref[0] = _dot(a, w_ref[0].astype(BF16)) + bias


def _adaln(cvec, w_mod, b_mod):
    rows = cvec.shape[0]
    nc = 4
    wc = 6 * D_MODEL // nc
    return pl.pallas_call(
        _adaln_kernel,
        grid=(DEPTH, nc),
        in_specs=[_full((rows, D_MODEL)),
                  pl.BlockSpec((1, D_MODEL, wc), lambda l, n: (l, 0, n)),
                  pl.BlockSpec((DEPTH, wc), lambda l, n: (0, n))],
        out_specs=pl.BlockSpec((1, rows, wc), lambda l, n: (l, 0, n)),
        out_shape=jax.ShapeDtypeStruct((DEPTH, rows, 6 * D_MODEL), F32),
        compiler_params=_params("parallel", "parallel"),
        name="adaln",
    )(cvec, w_mod, b_mod)


class _Rows(NamedTuple):
    lat_b: int
    lat_n: int
    ctx_b: int
    ctx_n: int

    @property
    def lat_rows(self):
        return self.lat_b * self.lat_n

    @property
    def ctx_rows(self):
        return self.ctx_b * self.ctx_n

    @property
    def lat_tiles(self):
        return self.lat_rows // ROW_TILE

    @property
    def tiles(self):
        return (self.lat_rows + self.ctx_rows) // ROW_TILE

    @property
    def tiles_per_lat_req(self):
        return self.lat_n // ROW_TILE


class _TileSpecs:
    def __init__(self, rows):
        lat_tiles = rows.lat_tiles
        per_req = rows.tiles_per_lat_req
        self.lat_tiles = lat_tiles
        self.mod = pl.BlockSpec((1, 6, D_MODEL), lambda i: (jnp.where(i < lat_tiles, 1 + i // per_req, 0), 0, 0))
        self.table = pl.BlockSpec((ROW_TILE, LANES), lambda i: (jnp.where(i < lat_tiles, i % per_req, per_req), 0))

    def rows(self, width):
        return pl.BlockSpec((ROW_TILE, width), lambda i: (i, 0))

    def lat_rows(self, width):
        last = self.lat_tiles - 1
        return pl.BlockSpec((ROW_TILE, width), lambda i: (jnp.minimum(i, last), 0))

    def ctx_rows(self, width):
        first = self.lat_tiles
        return pl.BlockSpec((ROW_TILE, width), lambda i: (jnp.maximum(i - first, 0), 0))

    def chunks(self, feats):
        return pl.BlockSpec((N_SUB, feats, SUB_ROWS), lambda i: (i, 0, 0))

    def state(self, layer_j, feats, cols):
        first = self.lat_tiles
        return pl.BlockSpec((N_SUB, None, feats, cols), lambda i: (jnp.maximum(i - first, 0), layer_j, 0, 0))


def _pick_rows(lat_tiles, lat_ref, ctx_ref, r):
    return jnp.where(pl.program_id(0) < lat_tiles, lat_ref[r, :], ctx_ref[r, :])


def _modulated(x, mod, g, shift_row, scale_row):
    return _rms(x, g) * (1.0 + mod[scale_row:scale_row + 1]) + mod[shift_row:shift_row + 1]


def _in_even_kernel(lat_tiles, n_aliased, *refs):
    (xl_ref, xc_ref, mod_ref, g1_ref, wt_ref, gqa_ref, gkva_ref, wuq_ref, wukv_ref,
     gqn_ref, gqr_ref, gkn_ref, gkr_ref, gdq_ref, gdk_ref, g64_ref, g32_ref,
     c64_ref, s64_ref, c32_ref, s32_ref) = refs[:21]
    refs = refs[21 + n_aliased:]
    qn_ref, qr_ref, kn_ref, vm_ref, krt_ref, mq_ref, mk_ref, mv_ref = refs[:8]
    ckv_s_ref, krt_s_ref, mk_s_ref, mv_s_ref = refs[8:]
    g64 = g64_ref[...]
    g32 = g32_ref[...]
    mod = mod_ref[0]
    mla_scale = (MLA_NOPE + MLA_ROPE) ** -0.5 * LOG2E
    diff_scale = DIFF_HD ** -0.5 * LOG2E

    for c in range(N_SUB):
        r = _sub(c)
        h = _modulated(_pick_rows(lat_tiles, xl_ref, xc_ref, r), mod, g1_ref[...], 0, 1)
        proj = _dot_nt(h.astype(BF16), wt_ref[...])
        cq = _rms(proj[:, 0:256], gqa_ref[...])
        q = _dot(cq.astype(BF16), wuq_ref[...])
        qn = _group_rms(q[:, 0:512], gqn_ref[...], g64)
        qr = _group_rms(q[:, 512:768], gqr_ref[...], g32)
        ckv = _rms(proj[:, 256:384], gkva_ref[...])
        kv = _dot(ckv.astype(BF16), wukv_ref[...])
        kn = _group_rms(kv[:, 0:512], gkn_ref[...], g64)
        mq = _group_rms(proj[:, 384:896], gdq_ref[...], g64)
        mk = _group_rms(proj[:, 896:1408], gdk_ref[...], g64)
        mv = proj[:, 1408:1920]
        kr = _group_rms(proj[:, 1920:2048], gkr_ref[...], g32)
        ckv_s_ref[c] = ckv
        krt_s_ref[c] = kr.T[0:MLA_ROPE]
        for hd in range(DIFF_HEADS):
            mk_s_ref[c, pl.ds(hd, SUB_ROWS, stride=DIFF_HEADS), :] = mk[:, hd * LANES:(hd + 1) * LANES]
            mv_s_ref[c, pl.ds(hd, SUB_ROWS, stride=DIFF_HEADS), :] = mv[:, hd * LANES:(hd + 1) * LANES]
        c64, s64, c32, s32 = c64_ref[r, :], s64_ref[r, :], c32_ref[r, :], s32_ref[r, :]
        qr = _rope(qr, c32, s32, MLA_ROPE // 4)
        kr = _rope(kr, c32, s32, MLA_ROPE // 4)
        mq = _rope(mq, c64, s64, DIFF_HD // 4)
        mk = _rope(mk, c64, s64, DIFF_HD // 4)
        qn_ref[r, :] = (qn * mla_scale).astype(BF16)
        qr_ref[r, :] = (qr * mla_scale).astype(BF16)
        kn_ref[r, :] = kn.astype(BF16)
        vm_ref[r, :] = kv[:, 512:1024].astype(BF16)
        krt_ref[r, :] = kr.astype(BF16)
        mq_ref[r, :] = (mq * diff_scale).astype(BF16)
        mk_ref[r, :] = mk.astype(BF16)
        mv_ref[r, :] = mv.astype(BF16)


def _in_odd_kernel(lat_tiles, n_aliased, *refs):
    (xl_ref, xc_ref, mod_ref, g1_ref, w_ref, gqc_ref, gkc_ref, gqd_ref, gkd_ref, g64_ref,
     c64_ref, s64_ref) = refs[:12]
    refs = refs[12 + n_aliased:]
    qc_ref, qd_ref = refs[:2]
    chunk_refs = refs[2:6]
    state_refs = refs[6:10]
    g64 = g64_ref[...]
    mod = mod_ref[0]
    w = w_ref[...].astype(BF16)
    scale = HEAD_DIM ** -0.5 * LOG2E

    for c in range(N_SUB):
        r = _sub(c)
        h = _modulated(_pick_rows(lat_tiles, xl_ref, xc_ref, r), mod, g1_ref[...], 0, 1)
        proj = _dot(h.astype(BF16), w)
        qc = _group_rms(proj[:, 0:512], gqc_ref[...], g64)
        kc = _group_rms(proj[:, 512:640], gkc_ref[...], g64)
        qd = _group_rms(proj[:, 768:1280], gqd_ref[...], g64)
        kd = _group_rms(proj[:, 1280:1408], gkd_ref[...], g64)
        vct = proj[:, 640:768].T
        vdt = proj[:, 1408:1536].T
        state_refs[0][c] = kc.T
        state_refs[1][c] = vct
        state_refs[2][c] = kd.T
        state_refs[3][c] = vdt
        c64, s64 = c64_ref[r, :], s64_ref[r, :]
        qc = _rope(qc, c64, s64, HEAD_DIM // 4)
        kc = _rope(kc, c64, s64, HEAD_DIM // 4)
        qd = _rope(qd, c64, s64, HEAD_DIM // 4)
        kd = _rope(kd, c64, s64, HEAD_DIM // 4)
        qc_ref[r, :] = (qc * scale).astype(BF16)
        qd_ref[r, :] = (qd * scale).astype(BF16)
        chunk_refs[0][c] = kc.T
        chunk_refs[1][c] = vct
        chunk_refs[2][c] = kd.T
        chunk_refs[3][c] = vdt


def _with_states(specs, layer_j, in_specs, args, out_shape, out_specs, states):
    aliases = {}
    for st in states:
        if not isinstance(st, jax.ShapeDtypeStruct):
            aliases[len(args)] = len(out_shape)
            in_specs.append(pl.BlockSpec(memory_space=pl.ANY))
            args.append(st)
        out_shape.append(jax.ShapeDtypeStruct(st.shape, st.dtype))
        out_specs.append(specs.state(layer_j, st.shape[2], st.shape[3]))
    return aliases


def _in_even(rows, layer_j, x_lat, x_ctx, mods, g1, wt, wuq, wukv, gains, gmats, tables, states):
    specs = _TileSpecs(rows)
    n_rows = rows.lat_rows + rows.ctx_rows
    vec_specs = [_full(g.shape) for g in gains]
    in_specs = ([specs.lat_rows(D_MODEL), specs.ctx_rows(D_MODEL), specs.mod, _full(g1.shape), _full(wt.shape)]
                + vec_specs[:2] + [_full(wuq.shape), _full(wukv.shape)] + vec_specs[2:]
                + [_full((MXU_DIM, MXU_DIM))] * 2 + [specs.table] * 4)
    args = ([x_lat, x_ctx, mods, g1, wt, gains[0], gains[1], wuq, wukv] + list(gains[2:]) + list(gmats)
            + list(tables))
    widths = [512, 256, 512, 512, 128, 512, 512, 512]
    out_shape = [jax.ShapeDtypeStruct((n_rows, wd), BF16) for wd in widths]
    out_specs = [specs.rows(wd) for wd in widths]
    aliases = _with_states(specs, layer_j, in_specs, args, out_shape, out_specs, states)
    return pl.pallas_call(
        functools.partial(_in_even_kernel, rows.lat_tiles, len(aliases)),
        grid=(rows.tiles,),
        in_specs=in_specs, out_specs=out_specs, out_shape=out_shape,
        input_output_aliases=aliases,
        compiler_params=_params("arbitrary"),
        name="in_even",
    )(*args)


def _in_odd(rows, layer_j, x_lat, x_ctx, mods, g1, w_all, gains, g64, tables, states):
    specs = _TileSpecs(rows)
    n_rows = rows.lat_rows + rows.ctx_rows
    w_spec = pl.BlockSpec((None,) + w_all.shape[1:], lambda i: (layer_j, 0, 0))
    in_specs = ([specs.lat_rows(D_MODEL), specs.ctx_rows(D_MODEL), specs.mod, _full(g1.shape), w_spec]
                + [_full(g.shape) for g in gains] + [_full((MXU_DIM, MXU_DIM))] + [specs.table] * 2)
    args = [x_lat, x_ctx, mods, g1, w_all] + list(gains) + [g64] + list(tables)
    out_shape = ([jax.ShapeDtypeStruct((n_rows, 512), BF16)] * 2
                 + [jax.ShapeDtypeStruct((n_rows // SUB_ROWS, LANES, SUB_ROWS), F32)] * 4)
    out_specs = [specs.rows(512), specs.rows(512)] + [specs.chunks(LANES)] * 4
    aliases = _with_states(specs, layer_j, in_specs, args, out_shape, out_specs, states)
    return pl.pallas_call(
        functools.partial(_in_odd_kernel, rows.lat_tiles, len(aliases)),
        grid=(rows.tiles,),
        in_specs=in_specs, out_specs=out_specs, out_shape=out_shape,
        input_output_aliases=aliases,
        compiler_params=_params("arbitrary"),
        name="in_odd",
    )(*args)


def _cache_kv_kernel(ckv_ref, wukv_ref, gkn_ref, g64_ref, kn_ref, vm_ref):
    kv = _dot(ckv_ref[...].astype(BF16), wukv_ref[...])
    kn_ref[...] = _group_rms(kv[:, 0:512], gkn_ref[...], g64_ref[...]).astype(BF16)
    vm_ref[...] = kv[:, 512:1024].astype(BF16)


def _cache_kv(ckv, wukv, gkn, g64):
    rows = ckv.shape[0]
    tile = 512
    spec = lambda wd: pl.BlockSpec((tile, wd), lambda i: (i, 0))
    return pl.pallas_call(
        _cache_kv_kernel,
        grid=(rows // tile,),
        in_specs=[spec(MLA_KV_LORA), _full(wukv.shape), _full(gkn.shape), _full((MXU_DIM, MXU_DIM))],
        out_specs=[spec(512), spec(512)],
        out_shape=[jax.ShapeDtypeStruct((rows, 512), BF16)] * 2,
        compiler_params=_params("parallel"),
        name="cache_kv",
    )(ckv, wukv, gkn, g64)


def _softmax_pv(s_parts, v_parts, v_is_feature_major, sink=None):
    m = None
    for s in s_parts:
        mi = jnp.max(s, axis=-1, keepdims=True)
        m = mi if m is None else jnp.maximum(m, mi)
    if sink is not None:
        m = jnp.maximum(m, sink)
    acc = None
    for s, v in zip(s_parts, v_parts):
        p = jnp.exp2(s - m).astype(BF16)
        oi = _dot_nt(p, v) if v_is_feature_major else _dot(p, v)
        acc = oi if acc is None else acc + oi
    denom = acc[:, LANES:]
    if sink is not None:
        denom = denom + jnp.exp2(sink - m)
    return acc[:, :LANES] / denom


def _lane_masks(width):
    lane = lax.broadcasted_iota(jnp.int32, (1, LANES), 1)
    return [jnp.where(lane // width == k, 1.0, 0.0).astype(BF16) for k in range(LANES // width)]


def _attn_even_kernel(latent, lam_init, *refs):
    qn_ref, qr_ref, mq_ref = refs[:3]
    refs = refs[3:]
    n_parts = 2 if latent else 1
    parts = [refs[5 * k:5 * k + 5] for k in range(n_parts)]
    lamv_ref, gsub_ref, o_ref = refs[5 * n_parts:]
    lane = lax.broadcasted_iota(jnp.int32, (1, LANES), 1)
    low = lane < 64
    m64 = _lane_masks(64)
    m32 = _lane_masks(32)
    ones = [jnp.ones((p[0].shape[0], LANES), BF16) for p in parts]

    def diff_head(ref, idx, hd):
        if latent and idx == 0:
            n_keys = ref.shape[0] // DIFF_HEADS
            return ref[pl.ds(hd, n_keys, stride=DIFF_HEADS), :].astype(BF16)
        return ref[:, hd * LANES:(hd + 1) * LANES]

    for i in range(MLA_HEADS // 2):
        cols = slice(i * LANES, (i + 1) * LANES)
        qn_b = qn_ref[:, cols]
        kcat = [jnp.concatenate([p[0][:, cols], p[2][...]], axis=1) for p in parts]
        vms = [jnp.concatenate([p[1][:, cols], one], axis=1) for p, one in zip(parts, ones)]
        outs = []
        for half in range(2):
            head = 2 * i + half
            rb = head // 4
            qr_b = qr_ref[:, rb * LANES:(rb + 1) * LANES]
            lhs = jnp.concatenate([qn_b * m64[half], qr_b * m32[head % 4]], axis=1)
            outs.append(_softmax_pv([_dot_nt(lhs, kc) for kc in kcat], vms, False))
        o_ref[:, cols] = jnp.where(low, outs[0], outs[1]).astype(BF16)

    lv = lamv_ref[...]
    lam = (jnp.exp(jnp.sum(lv[0:1] * lv[1:2], axis=-1, keepdims=True))
           - jnp.exp(jnp.sum(lv[2:3] * lv[3:4], axis=-1, keepdims=True)) + lam_init)
    gsub = gsub_ref[...]
    for hd in range(DIFF_HEADS):
        q_b = mq_ref[:, hd * LANES:(hd + 1) * LANES]
        ks = [diff_head(p[3], idx, hd) for idx, p in enumerate(parts)]
        vs = [jnp.concatenate([diff_head(p[4], idx, hd), one], axis=1)
              for idx, (p, one) in enumerate(zip(parts, ones))]
        a1 = _softmax_pv([_dot_nt(q_b * m64[0], k) for k in ks], vs, False)
        a2 = _softmax_pv([_dot_nt(q_b * m64[1], k) for k in ks], vs, False)
        d = _rms(a1 - lam * a2, gsub) * (1.0 - lam_init)
        o_ref[:, 512 + hd * LANES:512 + (hd + 1) * LANES] = d.astype(BF16)


def _attn_even(latent, rows, layer_j, lam_init, q_arrs, kv_arrs, cache_arrs, lamv, gsub):
    batch, n_tokens, row0 = (rows.lat_b, rows.lat_n, 0) if latent else (rows.ctx_b, rows.ctx_n, rows.lat_rows)
    tq = min(Q_TILE_DENSE, n_tokens)
    nt = n_tokens // tq
    q0 = row0 // tq
    k0 = row0 // n_tokens
    qspec = lambda wd: pl.BlockSpec((tq, wd), lambda b, t: (q0 + b * nt + t, 0))
    kspec = lambda wd: pl.BlockSpec((n_tokens, wd), lambda b, t: (k0 + b, 0))
    kwidths = [512, 512, 128, 512, 512]
    in_specs = [qspec(512), qspec(256), qspec(512)]
    args = list(q_arrs)
    if latent:
        past = cache_arrs[0].shape[0] // batch
        in_specs += [pl.BlockSpec((past, wd), lambda b, t: (b, 0)) for wd in kwidths[:3]]
        in_specs += [pl.BlockSpec((None, None, past * DIFF_HEADS, LANES), lambda b, t: (b, layer_j, 0, 0))] * 2
        args += list(cache_arrs)
    in_specs += [kspec(wd) for wd in kwidths]
    args += list(kv_arrs)
    in_specs += [_full(lamv.shape), _full(gsub.shape)]
    args += [lamv, gsub]
    return pl.pallas_call(
        functools.partial(_attn_even_kernel, latent, lam_init),
        grid=(batch, nt),
        in_specs=in_specs,
        out_specs=pl.BlockSpec((tq, D_MODEL), lambda b, t: (b * nt + t, 0)),
        out_shape=jax.ShapeDtypeStruct((batch * n_tokens, D_MODEL), BF16),
        compiler_params=_params("parallel", "parallel"),
        name="attn_even_lat" if latent else "attn_even_ctx",
    )(*args)


def _attn_odd_kernel(latent, tq, *refs):
    qc_ref, qd_ref = refs[:2]
    refs = refs[2:]
    n_parts = 2 if latent else 1
    parts = [list(refs[4 * k:4 * k + 4]) for k in range(n_parts)]
    sink_ref, o_ref = refs[4 * n_parts:4 * n_parts + 2]
    scratch = refs[4 * n_parts + 2:]
    lane = lax.broadcasted_iota(jnp.int32, (1, LANES), 1)
    low = lane < 64
    m64 = _lane_masks(64)

    band = None
    start = 0
    span = 0
    if latent:
        for ref, scr in zip(parts[-1], scratch):
            for c in range(ref.shape[0]):
                scr[:, _sub(c)] = ref[c]
        parts[-1] = list(scratch)
        n_new = scratch[0].shape[1]
        span = min(n_new, tq + 2 * WINDOW)
        t = pl.program_id(1)
        start = pl.multiple_of(jnp.clip(t * tq - WINDOW, 0, n_new - span), LANES)
        qpos = t * tq + lax.broadcasted_iota(jnp.int32, (tq, 1), 0)
        kpos = start + lax.broadcasted_iota(jnp.int32, (1, span), 1)
        band = jnp.abs(qpos - kpos) <= WINDOW

    for kind in range(2):
        q_ref = qc_ref if kind == 0 else qd_ref
        windowed = latent and kind == 0
        for g in range(2):
            rows = slice(g * HEAD_DIM, (g + 1) * HEAD_DIM)
            ks, vs = [], []
            for idx, p in enumerate(parts):
                k_ref, v_ref = p[2 * kind], p[2 * kind + 1]
                if windowed and idx == n_parts - 1:
                    k = k_ref[rows, pl.ds(start, span)]
                    v = v_ref[rows, pl.ds(start, span)]
                else:
                    k = k_ref[rows, :]
                    v = v_ref[rows, :]
                ks.append(jnp.concatenate([k, k], axis=0).astype(BF16))
                vs.append(jnp.concatenate([v, v, jnp.ones((LANES, v.shape[1]), F32)], axis=0).astype(BF16))
            for i in (2 * g, 2 * g + 1):
                cols = slice(i * LANES, (i + 1) * LANES)
                q_b = q_ref[:, cols]
                outs = []
                for half in range(2):
                    s_parts = [_dot(q_b * m64[half], k) for k in ks]
                    sink = None
                    if kind == 0:
                        sink = sink_ref[2 * i + half] * LOG2E
                        if latent:
                            s_parts[-1] = jnp.where(band, s_parts[-1], NEG_BIG)
                    outs.append(_softmax_pv(s_parts, vs, True, sink))
                o_ref[:, kind * 512 + i * LANES:kind * 512 + (i + 1) * LANES] = (
                    jnp.where(low, outs[0], outs[1]).astype(BF16))


def _attn_odd(latent, rows, layer_j, q_arrs, new_arrs, cache_arrs, sink):
    batch, n_tokens, row0 = (rows.lat_b, rows.lat_n, 0) if latent else (rows.ctx_b, rows.ctx_n, rows.lat_rows)
    tq = min(Q_TILE_BANDED, n_tokens)
    nt = n_tokens // tq
    q0 = row0 // tq
    in_specs = [pl.BlockSpec((tq, 512), lambda b, t: (q0 + b * nt + t, 0))] * 2
    args = list(q_arrs)
    scratch = []
    if latent:
        past = cache_arrs[0].shape[-1]
        in_specs += [pl.BlockSpec((None, None, LANES, past), lambda b, t: (b, layer_j, 0, 0))] * 4
        args += list(cache_arrs)
        chunks = n_tokens // SUB_ROWS
        in_specs += [pl.BlockSpec((chunks, LANES, SUB_ROWS), lambda b, t: (b, 0, 0))] * 4
        scratch = [pltpu.VMEM((LANES, n_tokens), F32)] * 4
    else:
        in_specs += [pl.BlockSpec((None, None, LANES, n_tokens), lambda b, t: (b, layer_j, 0, 0))] * 4
    args += list(new_arrs)
    in_specs += [pl.BlockSpec(memory_space=pltpu.SMEM)]
    args += [sink]
    return pl.pallas_call(
        functools.partial(_attn_odd_kernel, latent, tq),
        grid=(batch, nt),
        in_specs=in_specs,
        out_specs=pl.BlockSpec((tq, D_MODEL), lambda b, t: (b * nt + t, 0)),
        out_shape=jax.ShapeDtypeStruct((batch * n_tokens, D_MODEL), BF16),
        scratch_shapes=scratch,
        compiler_params=_params("parallel", "parallel"),
        name="attn_odd_lat" if latent else "attn_odd_ctx",
    )(*args)


def _post_kernel(lat_tiles, ol_ref, oc_ref, wout_ref, xl_ref, xc_ref, mod_ref, g2_ref, wr2_ref, wrh_ref,
                 x1_ref, h2_ref, afft_ref):
    mod = mod_ref[0]
    wout = wout_ref[...].astype(BF16)
    lane = lax.broadcasted_iota(jnp.int32, (1, LANES), 1)
    for c in range(N_SUB):
        r = _sub(c)
        o = _pick_rows(lat_tiles, ol_ref, oc_ref, r)
        x1 = _pick_rows(lat_tiles, xl_ref, xc_ref, r) + mod[2:3] * _dot(o, wout)
        x1_ref[r, :] = x1
        h2 = _modulated(x1, mod, g2_ref[...], 3, 4)
        h_hi = h2.astype(BF16)
        h_lo = (h2 - h_hi.astype(F32)).astype(BF16)
        h2_ref[r, :] = h_hi
        both = _dot(h_hi, wr2_ref[...])
        logits = both[:, :LANES] + both[:, LANES:] + _dot(h_lo, wrh_ref[...])
        logits = jnp.where(lane < N_EXPERTS, logits, NEG_BIG)
        e = jnp.exp(logits - jnp.max(logits, axis=-1, keepdims=True))
        aff = e / jnp.sum(e, axis=-1, keepdims=True)
        afft_ref[c] = aff.T[0:N_EXPERTS]


def _post(rows, layer_j, o_lat, o_ctx, wout_all, x_lat, x_ctx, mods, g2, wr2, wrh):
    specs = _TileSpecs(rows)
    n_rows = rows.lat_rows + rows.ctx_rows
    w_spec = pl.BlockSpec((None,) + wout_all.shape[1:], lambda i: (layer_j, 0, 0))
    return pl.pallas_call(
        functools.partial(_post_kernel, rows.lat_tiles),
        grid=(rows.tiles,),
        in_specs=[specs.lat_rows(D_MODEL), specs.ctx_rows(D_MODEL), w_spec,
                  specs.lat_rows(D_MODEL), specs.ctx_rows(D_MODEL), specs.mod, _full(g2.shape),
                  _full(wr2.shape), _full(wrh.shape)],
        out_specs=[specs.rows(D_MODEL), specs.rows(D_MODEL), specs.chunks(N_EXPERTS)],
        out_shape=[jax.ShapeDtypeStruct((n_rows, D_MODEL), F32),
                   jax.ShapeDtypeStruct((n_rows, D_MODEL), BF16),
                   jax.ShapeDtypeStruct((n_rows // SUB_ROWS, N_EXPERTS, SUB_ROWS), F32)],
        compiler_params=_params("parallel"),
        name="post",
    )(o_lat, o_ctx, wout_all, x_lat, x_ctx, mods, g2, wr2, wrh)


def _route_kernel(caps, *refs):
    n_groups = len(caps)
    aff_refs, out_refs = refs[:n_groups], refs[n_groups:]
    bits = [lax.bitcast_convert_type(ref[...], jnp.int32) for ref in aff_refs]

    def body(_, carry):
        new = []
        for b, cap, (lo, hi) in zip(bits, caps, carry):
            mid = lo + ((hi - lo + 1) >> 1)
            cnt = jnp.sum(jnp.where(b >= mid, 1.0, 0.0), axis=-1, keepdims=True)
            ok = cnt >= cap
            new.append((jnp.where(ok, mid, lo), jnp.where(ok, hi, mid - 1)))
        return tuple(new)

    init = tuple((jnp.zeros((b.shape[0], 1), jnp.int32), jnp.full((b.shape[0], 1), 0x7F800000, jnp.int32))
                 for b in bits)
    found = lax.fori_loop(0, 31, body, init)
    for b, cap, (thr, _), out_ref in zip(bits, caps, found, out_refs):
        n = b.shape[1]
        above = jnp.where(b > thr, 1.0, 0.0)
        equal = jnp.where(b == thr, 1.0, 0.0)
        room = cap - jnp.sum(above, axis=-1, keepdims=True)
        before = jnp.where(lax.broadcasted_iota(jnp.int32, (n, n), 0)
                           < lax.broadcasted_iota(jnp.int32, (n, n), 1), 1.0, 0.0).astype(BF16)
        equal_before = _dot(equal.astype(BF16), before)
        chosen = above + equal * jnp.where(equal_before < room, 1.0, 0.0)
        slot = _dot(chosen.astype(BF16), before)
        out_ref[...] = jnp.where(chosen > 0.5, slot, -1.0)


def _route(caps, aff_list):
    return pl.pallas_call(
        functools.partial(_route_kernel, tuple(caps)),
        grid=(1,),
        in_specs=[_full(a.shape) for a in aff_list],
        out_specs=[_full(a.shape) for a in aff_list],
        out_shape=[jax.ShapeDtypeStruct(a.shape, F32) for a in aff_list],
        compiler_params=_params("arbitrary"),
        name="route",
    )(*aff_list)


def _gather_part(cap, req_per_step, experts_per_step, first, slot_ref, aff_ref, h_ref, xs_ref, gate_ref):
    n = slot_ref.shape[1]
    want = lax.broadcasted_iota(jnp.int32, (cap, 1), 0).astype(F32)
    for rq in range(req_per_step):
        out_rows = slice(rq * cap, (rq + 1) * cap)
        rows = []
        for k in range(experts_per_step):
            row = rq * N_EXPERTS + first + k
            hit = slot_ref[pl.ds(row, 1), :] == want
            rows.append(jnp.where(hit, 1.0, 0.0).astype(BF16))
            gate_ref[k, out_rows, :] = jnp.sum(jnp.where(hit, aff_ref[pl.ds(row, 1), :], 0.0),
                                               axis=-1, keepdims=True)
        picked = _dot(jnp.concatenate(rows, axis=0), h_ref[rq * n:(rq + 1) * n, :]).astype(BF16)
        for k in range(experts_per_step):
            xs_ref[k, out_rows, :] = picked[k * cap:(k + 1) * cap]


def _gather_kernel(lat_steps, groups, caps, ctx_req_per_step,
                   sl_l, af_l, h_l, sl_c, af_c, h_c, xs_l, gt_l, xs_c, gt_c):
    s = pl.program_id(0)
    per_group = N_EXPERTS // groups

    @pl.when(s < lat_steps)
    def _():
        _gather_part(caps[0], 1, per_group, (s % groups) * per_group, sl_l, af_l, h_l, xs_l, gt_l)

    @pl.when(s >= lat_steps)
    def _():
        _gather_part(caps[1], ctx_req_per_step, N_EXPERTS, 0, sl_c, af_c, h_c, xs_c, gt_c)


def _gather(rows, caps, slots, affs, h2):
    groups = rows.lat_n // SCATTER_ROWS
    per_group = N_EXPERTS // groups
    rps = ROW_TILE // rows.ctx_n
    lat_steps = rows.lat_b * groups
    ctx_steps = rows.ctx_b // rps
    last_b = rows.lat_b - 1
    ctx_block0 = rows.lat_rows // (rps * rows.ctx_n)
    lat_req = lambda s: jnp.minimum(s // groups, last_b)
    lat_grp = lambda s: jnp.where(s < lat_steps, s % groups, groups - 1)
    ctx_step = lambda s: jnp.maximum(s - lat_steps, 0)
    cap_l, cap_c = caps
    lat_rows_spec = pl.BlockSpec((N_EXPERTS, rows.lat_n), lambda s: (lat_req(s), 0))
    ctx_rows_spec = pl.BlockSpec((rps * N_EXPERTS, rows.ctx_n), lambda s: (ctx_step(s), 0))
    return pl.pallas_call(
        functools.partial(_gather_kernel, lat_steps, groups, caps, rps),
        grid=(lat_steps + ctx_steps,),
        in_specs=[lat_rows_spec, lat_rows_spec,
                  pl.BlockSpec((rows.lat_n, D_MODEL), lambda s: (lat_req(s), 0)),
                  ctx_rows_spec, ctx_rows_spec,
                  pl.BlockSpec((rps * rows.ctx_n, D_MODEL), lambda s: (ctx_block0 + ctx_step(s), 0))],
        out_specs=[pl.BlockSpec((per_group, cap_l, D_MODEL), lambda s: (lat_grp(s), lat_req(s), 0)),
                   pl.BlockSpec((per_group, cap_l, 1), lambda s: (lat_grp(s), lat_req(s), 0)),
                   pl.BlockSpec((N_EXPERTS, rps * cap_c, D_MODEL), lambda s: (0, ctx_step(s), 0)),
                   pl.BlockSpec((N_EXPERTS, rps * cap_c, 1), lambda s: (0, ctx_step(s), 0))],
        out_shape=[jax.ShapeDtypeStruct((N_EXPERTS, rows.lat_b * cap_l, D_MODEL), BF16),
                   jax.ShapeDtypeStruct((N_EXPERTS, rows.lat_b * cap_l, 1), F32),
                   jax.ShapeDtypeStruct((N_EXPERTS, rows.ctx_b * cap_c, D_MODEL), BF16),
                   jax.ShapeDtypeStruct((N_EXPERTS, rows.ctx_b * cap_c, 1), F32)],
        compiler_params=_params("arbitrary"),
        name="gather",
    )(slots[0], affs[0], h2, slots[1], affs[1], h2)


def _ffn_kernel(n_groups, *refs):
    xs_refs = refs[:n_groups]
    gate_refs = refs[n_groups:2 * n_groups]
    w1_ref, w3_ref, w2_ref = refs[2 * n_groups:2 * n_groups + 3]
    ys_refs = refs[2 * n_groups + 3:3 * n_groups + 3]
    acc_refs = refs[3 * n_groups + 3:]
    f = pl.program_id(1)

    @pl.when(f == 0)
    def _():
        for acc_ref in acc_refs:
            acc_ref[...] = jnp.zeros(acc_ref.shape, F32)

    hidden = [[] for _ in xs_refs]
    for n in range(0, w1_ref.shape[3], MXU_DIM):
        w1 = w1_ref[0, 0, :, n:n + MXU_DIM].astype(BF16)
        w3 = w3_ref[0, 0, :, n:n + MXU_DIM].astype(BF16)
        for parts, xs_ref in zip(hidden, xs_refs):
            xs = xs_ref[0]
            parts.append((_silu(_dot(xs, w1)) * _dot(xs, w3)).astype(BF16))
    hidden = [jnp.concatenate(parts, axis=1) for parts in hidden]
    for n in range(0, w2_ref.shape[3], MXU_DIM):
        w2 = w2_ref[0, 0, :, n:n + MXU_DIM].astype(BF16)
        for hid, acc_ref in zip(hidden, acc_refs):
            acc_ref[:, n:n + MXU_DIM] += _dot(hid, w2)

    @pl.when(f == pl.num_programs(1) - 1)
    def _():
        for gate_ref, ys_ref, acc_ref in zip(gate_refs, ys_refs, acc_refs):
            ys_ref[0] = (acc_ref[...] * gate_ref[0]).astype(BF16)


def _ffn(layer, xs_list, gate_list, w1, w3, w2):
    n_groups = len(xs_list)
    n_chunks = EXPERT_FF // FF_CHUNK
    xs_specs = [pl.BlockSpec((1,) + xs.shape[1:], lambda e, f: (e, 0, 0)) for xs in xs_list]
    gate_specs = [pl.BlockSpec((1,) + g.shape[1:], lambda e, f: (e, 0, 0)) for g in gate_list]
    return pl.pallas_call(
        functools.partial(_ffn_kernel, n_groups),
        grid=(N_EXPERTS, n_chunks),
        in_specs=xs_specs + gate_specs + [
            pl.BlockSpec((1, 1, D_MODEL, FF_CHUNK), lambda e, f: (layer, e, 0, f)),
            pl.BlockSpec((1, 1, D_MODEL, FF_CHUNK), lambda e, f: (layer, e, 0, f)),
            pl.BlockSpec((1, 1, FF_CHUNK, D_MODEL), lambda e, f: (layer, e, f, 0))],
        out_specs=xs_specs,
        out_shape=[jax.ShapeDtypeStruct(xs.shape, BF16) for xs in xs_list],
        scratch_shapes=[pltpu.VMEM(xs.shape[1:], F32) for xs in xs_list],
        compiler_params=_params("parallel", "arbitrary", vmem=FFN_VMEM_LIMIT),
        name="ffn",
    )(*xs_list, *gate_list, w1, w3, w2)


def _scatter_part(cap, slots, ys_ref, ys_rows, x, gate2):
    lane = lax.broadcasted_iota(jnp.int32, (1, LANES), 1).astype(F32)
    per_block = LANES // cap
    pad = jnp.zeros((LANES - N_EXPERTS, SUB_ROWS), F32)
    slots_t = jnp.concatenate([slots, pad], axis=0).T
    blocks = []
    for k in range(N_EXPERTS // per_block):
        hit = None
        for e in range(k * per_block, (k + 1) * per_block):
            col = slots_t[:, e:e + 1]
            target = jnp.where(col >= 0.0, col + float((e - k * per_block) * cap), -1.0)
            he = jnp.where(target == lane, 1.0, 0.0)
            hit = he if hit is None else hit + he
        blocks.append(hit.astype(BF16))
    onehot = jnp.concatenate(blocks, axis=1)
    ys = jnp.concatenate([ys_ref[e, ys_rows, :] for e in range(N_EXPERTS)], axis=0)
    return x + gate2 * _dot(onehot, ys)


def _combine_kernel(lat_steps, steps_per_req, caps, sl_l, ys_l, sl_c, ys_c, x_ref, mod_ref, ol_ref, oc_ref):
    s = pl.program_id(0)
    gate2 = mod_ref[0][5:6]
    chunks = SCATTER_ROWS // SUB_ROWS

    @pl.when(s < lat_steps)
    def _():
        first = (s % steps_per_req) * SCATTER_ROWS
        for c in range(chunks):
            tokens = pl.multiple_of(first + c * SUB_ROWS, SUB_ROWS)
            ol_ref[_sub(c), :] = _scatter_part(caps[0], sl_l[:, pl.ds(tokens, SUB_ROWS)], ys_l,
                                               slice(0, caps[0]), x_ref[_sub(c), :], gate2)

    @pl.when(s >= lat_steps)
    def _():
        for rq in range(chunks):
            oc_ref[_sub(rq), :] = _scatter_part(caps[1], sl_c[rq * N_EXPERTS:(rq + 1) * N_EXPERTS, :], ys_c,
                                                slice(rq * caps[1], (rq + 1) * caps[1]), x_ref[_sub(rq), :], gate2)


def _combine(rows, caps, slots, ys, x1, mods):
    per_req = rows.lat_n // SCATTER_ROWS
    req_per_step = SCATTER_ROWS // rows.ctx_n
    lat_steps = rows.lat_b * per_req
    ctx_steps = rows.ctx_b // req_per_step
    last_b = rows.lat_b - 1
    lat_req = lambda s: jnp.minimum(s // per_req, last_b)
    ctx_step = lambda s: jnp.maximum(s - lat_steps, 0)
    cap_l, cap_c = caps
    return pl.pallas_call(
        functools.partial(_combine_kernel, lat_steps, per_req, caps),
        grid=(lat_steps + ctx_steps,),
        in_specs=[pl.BlockSpec((N_EXPERTS, rows.lat_n), lambda s: (lat_req(s), 0)),
                  pl.BlockSpec((N_EXPERTS, cap_l, D_MODEL), lambda s: (0, lat_req(s), 0)),
                  pl.BlockSpec((req_per_step * N_EXPERTS, rows.ctx_n), lambda s: (ctx_step(s), 0)),
                  pl.BlockSpec((N_EXPERTS, req_per_step * cap_c, D_MODEL), lambda s: (0, ctx_step(s), 0)),
                  pl.BlockSpec((SCATTER_ROWS, D_MODEL), lambda s: (s, 0)),
                  pl.BlockSpec((1, 6, D_MODEL), lambda s: (jnp.where(s < lat_steps, 1 + s // per_req, 0), 0, 0))],
        out_specs=[pl.BlockSpec((SCATTER_ROWS, D_MODEL), lambda s: (jnp.minimum(s, lat_steps - 1), 0)),
                   pl.BlockSpec((SCATTER_ROWS, D_MODEL), lambda s: (ctx_step(s), 0))],
        out_shape=[jax.ShapeDtypeStruct((rows.lat_rows, D_MODEL), F32),
                   jax.ShapeDtypeStruct((rows.ctx_rows, D_MODEL), F32)],
        compiler_params=_params("arbitrary"),
        name="combine",
    )(slots[0], ys[0], slots[1], ys[1], x1, mods)


def _rope_tables(n_pos, rot_dim):
    t = jnp.arange(n_pos)
    row = t // GRID_W
    col = t % GRID_W
    nf = rot_dim // 4
    inv = ROPE_THETA ** (-jnp.arange(nf, dtype=F32) / nf)
    ang_r = row[:, None] * inv
    ang_c = col[:, None] * inv
    cr, sr, cc, sc = jnp.cos(ang_r), jnp.sin(ang_r), jnp.cos(ang_c), jnp.sin(ang_c)
    cos = jnp.concatenate([cr, cr, cc, cc], axis=1)
    sin = jnp.concatenate([-sr, sr, -sc, sc], axis=1)
    reps = LANES // rot_dim
    cos = jnp.concatenate([jnp.tile(cos, (1, reps)), jnp.ones((ROW_TILE, LANES), F32)], axis=0)
    sin = jnp.concatenate([jnp.tile(sin, (1, reps)), jnp.zeros((ROW_TILE, LANES), F32)], axis=0)
    return cos, sin


def _group_mean_matrix(width):
    idx = jnp.arange(MXU_DIM) // width
    return jnp.where(idx[:, None] == idx[None, :], 1.0 / width, 0.0).astype(BF16)


def _tile_row(g, reps):
    return jnp.tile(g, reps).reshape(1, -1)


def _split_heads(w, n_heads, first):
    k = w.shape[0]
    w3 = w.reshape(k, n_heads, -1)
    return jnp.concatenate([w3[:, :, :first].reshape(k, -1), w3[:, :, first:].reshape(k, -1)], axis=1)


def _feature_major(cache):
    b, l, t, h, d = cache.shape
    return jnp.transpose(cache, (0, 1, 3, 4, 2)).reshape(b, l, h * d, t)


def kernel(x_prompt, x_sample, cache_mla_ckv, cache_mla_krope, cache_diff_k, cache_diff_v,
           cache_win_k, cache_win_v, cache_axial_k, cache_axial_v, c, c_ctx,
           g_norm, w_mod, b_mod, w_in_even, w_out_even, mla_g_qa, mla_g_kva, mla_w_uq, mla_w_ukv,
           mla_g_q, mla_g_k, diff_g_q, diff_g_k, diff_lambda, diff_g_sub,
           w_in_odd, w_out_odd, odd_g_qk, win_sink, moe_w_router, moe_w1, moe_w3, moe_w2):
    rows = _Rows(lat_b=x_sample.shape[0], lat_n=x_sample.shape[1], ctx_b=x_prompt.shape[0], ctx_n=x_prompt.shape[1])
    past = cache_mla_ckv.shape[2]
    n_even, n_odd = w_in_even.shape[0], w_in_odd.shape[0]
    assert rows.ctx_n == SUB_ROWS and rows.lat_n % ROW_TILE == 0 and rows.ctx_b % N_SUB == 0
    caps = (EC_CAPACITY_FACTOR * rows.lat_n // N_EXPERTS, EC_CAPACITY_FACTOR * rows.ctx_n // N_EXPERTS)

    cvec = jnp.zeros((16, D_MODEL), F32).at[0].set(c_ctx).at[1:1 + rows.lat_b].set(c)
    mods = _adaln(cvec, w_mod, b_mod).reshape(DEPTH, 16, 6, D_MODEL)

    g64 = _group_mean_matrix(64)
    g32 = _group_mean_matrix(32)
    tab64 = _rope_tables(rows.lat_n, HEAD_DIM)
    tab32 = _rope_tables(rows.lat_n, MLA_ROPE)
    odd_caches = [_feature_major(a) for a in (cache_win_k, cache_win_v, cache_axial_k, cache_axial_v)]
    diff_caches = [a.reshape(rows.lat_b, n_even, past * DIFF_HEADS, LANES) for a in (cache_diff_k, cache_diff_v)]

    state = lambda *shape: jax.ShapeDtypeStruct(shape, F32)
    even_states = [state(rows.ctx_b, n_even, rows.ctx_n, MLA_KV_LORA),
                   state(rows.ctx_b, n_even, MLA_ROPE, rows.ctx_n),
                   state(rows.ctx_b, n_even, rows.ctx_n * DIFF_HEADS, LANES),
                   state(rows.ctx_b, n_even, rows.ctx_n * DIFF_HEADS, LANES)]
    odd_states = [state(rows.ctx_b, n_odd, LANES, rows.ctx_n) for _ in range(4)]

    x_lat = x_sample.reshape(-1, D_MODEL)
    x_ctx = x_prompt.reshape(-1, D_MODEL)
    lat_chunks = rows.lat_rows // SUB_ROWS

    for layer in range(DEPTH):
        j = layer // 2
        mods_l = mods[layer]
        g1 = g_norm[layer, 0].reshape(1, -1)
        g2 = g_norm[layer, 1].reshape(1, -1)
        if layer % 2 == 0:
            wt = jnp.swapaxes(w_in_even[j], 0, 1)
            wt = jnp.concatenate([wt[:384], wt[416:]] + [wt[384:416]] * 4, axis=0).astype(BF16)
            wuq = _split_heads(mla_w_uq[j], MLA_HEADS, MLA_NOPE).astype(BF16)
            wukv = _split_heads(mla_w_ukv[j], MLA_HEADS, MLA_NOPE).astype(BF16)
            w_out = w_out_even
            gkn = _tile_row(mla_g_k[j, :MLA_NOPE], MLA_HEADS)
            gains = [mla_g_qa[j].reshape(1, -1), mla_g_kva[j].reshape(1, -1),
                     _tile_row(mla_g_q[j, :MLA_NOPE], MLA_HEADS), _tile_row(mla_g_q[j, MLA_NOPE:], MLA_HEADS),
                     gkn, _tile_row(mla_g_k[j, MLA_NOPE:], 4),
                     _tile_row(diff_g_q[j], 2 * DIFF_HEADS), _tile_row(diff_g_k[j], 2 * DIFF_HEADS)]
            lam_init = 0.8 - 0.6 * math.exp(-0.3 * layer)
            gsub = diff_g_sub[j].reshape(1, -1)
            outs = _in_even(rows, j, x_lat, x_ctx, mods_l, g1, wt, wuq, wukv, gains, (g64, g32),
                            tab64 + tab32, even_states)
            qn, qr, kn, vm, krt, mq, mk, mv = outs[:8]
            even_states = list(outs[8:])
            kn_c, vm_c = _cache_kv(cache_mla_ckv[:, j].reshape(-1, MLA_KV_LORA), wukv, gkn, g64)
            cache_arrs = (kn_c, vm_c,
                          jnp.tile(cache_mla_krope[:, j].reshape(-1, MLA_ROPE), (1, 4)).astype(BF16),
                          diff_caches[0], diff_caches[1])
            o_lat = _attn_even(True, rows, j, lam_init, (qn, qr, mq), (kn, vm, krt, mk, mv), cache_arrs,
                               diff_lambda[j], gsub)
            o_ctx = _attn_even(False, rows, j, lam_init, (qn, qr, mq), (kn, vm, krt, mk, mv), None,
                               diff_lambda[j], gsub)
        else:
            w_out = w_out_odd
            gains = [_tile_row(odd_g_qk[j, 0], 8), _tile_row(odd_g_qk[j, 1], 2),
                     _tile_row(odd_g_qk[j, 2], 8), _tile_row(odd_g_qk[j, 3], 2)]
            outs = _in_odd(rows, j, x_lat, x_ctx, mods_l, g1, w_in_odd, gains, g64, tab64, odd_states)
            odd_states = list(outs[6:])
            o_lat = _attn_odd(True, rows, j, outs[:2], outs[2:6], odd_caches, win_sink[j])
            o_ctx = _attn_odd(False, rows, j, outs[:2], odd_states, None, win_sink[j])

        w_r = jnp.pad(moe_w_router[layer], ((0, 0), (0, LANES - N_EXPERTS)))
        wrh = w_r.astype(BF16)
        wrl = (w_r - wrh.astype(F32)).astype(BF16)
        x1, h2, afft = _post(rows, j, o_lat, o_ctx, w_out, x_lat, x_ctx, mods_l, g2,
                             jnp.concatenate([wrh, wrl], axis=1), wrh)
        aff_lat = (afft[:lat_chunks].reshape(rows.lat_b, rows.lat_n // SUB_ROWS, N_EXPERTS, SUB_ROWS)
                   .transpose(0, 2, 1, 3).reshape(rows.lat_b * N_EXPERTS, rows.lat_n))
        aff_ctx = afft[lat_chunks:].reshape(rows.ctx_b * N_EXPERTS, rows.ctx_n)
        affs = (aff_lat, aff_ctx)
        slots = _route(caps, affs)
        xs_l, gt_l, xs_c, gt_c = _gather(rows, caps, slots, affs, h2)
        ys = _ffn(layer, [xs_l, xs_c], [gt_l, gt_c], moe_w1, moe_w3, moe_w2)
        x_lat, x_ctx = _combine(rows, caps, slots, ys, x1, mods_l)

    def token_major(arr, heads):
        b, l, f, t = arr.shape
        return jnp.transpose(arr.reshape(b, l, heads, f // heads, t), (0, 1, 4, 2, 3))

    diff_shape = (rows.ctx_b, n_even, rows.ctx_n, DIFF_HEADS, LANES)
    return (x_ctx.reshape(x_prompt.shape), x_lat.reshape(x_sample.shape),
            even_states[0],
            jnp.swapaxes(even_states[1], 2, 3),
            even_states[2].reshape(diff_shape),
            even_states[3].reshape(diff_shape),
            token_major(odd_states[0], 2), token_major(odd_states[1], 2),
            token_major(odd_states[2], 2), token_major(odd_states[3], 2))
```

```python
import functools
import math
from typing import NamedTuple

import jax
import jax.numpy as jnp
from jax import lax
from jax.experimental import pallas as pl
from jax.experimental.pallas import tpu as pltpu

F32 = jnp.float32
BF16 = jnp.bfloat16

D_MODEL = 1024
DEPTH = 4
GRID_W = 64
ROPE_THETA = 10000.0
WINDOW = 128
RMS_EPS = 1e-6
MLA_HEADS = 8
MLA_Q_LORA = 256
MLA_KV_LORA = 128
MLA_NOPE = 64
MLA_ROPE = 32
MLA_VD = 64
DIFF_HEADS = 4
DIFF_HD = 64
HEAD_DIM = 64
N_EXPERTS = 16
EC_CAPACITY_FACTOR = 2
EXPERT_FF = 2048

LANES = 128
MXU_DIM = 256
ROW_TILE = 1024
SUB_ROWS = 256
N_SUB = ROW_TILE // SUB_ROWS
SCATTER_ROWS = 1024
Q_TILE_DENSE = 512
Q_TILE_BANDED = 256
FF_CHUNK = 1024
VMEM_LIMIT = 50 * 1024 * 1024
FFN_VMEM_LIMIT = 60 * 1024 * 1024
NEG_BIG = -1e30
LOG2E = 1.4426950408889634


def _params(*sem, vmem=VMEM_LIMIT):
    return pltpu.CompilerParams(dimension_semantics=sem, vmem_limit_bytes=vmem)


def _full(shape):
    zeros = (0,) * len(shape)
    return pl.BlockSpec(shape, lambda *_: zeros)


def _dot(a, b):
    return jnp.dot(a, b, preferred_element_type=F32)


def _dot_nt(a, b):
    return lax.dot_general(a, b, (((1,), (1,)), ((), ())), preferred_element_type=F32)


def _rms(x, g):
    ms = jnp.mean(x * x, axis=-1, keepdims=True)
    return x * lax.rsqrt(ms + RMS_EPS) * g


def _group_rms(x, g, gmat):
    outs = []
    width = x.shape[1]
    for k in range(0, width, MXU_DIM):
        wd = min(MXU_DIM, width - k)
        xb = x[:, k:k + wd]
        ms = _dot((xb * xb).astype(BF16), gmat[:wd, :wd])
        outs.append(xb * lax.rsqrt(ms + RMS_EPS))
    y = outs[0] if len(outs) == 1 else jnp.concatenate(outs, axis=1)
    return y * g


def _rope(x, cos, sin, half):
    lane = lax.broadcasted_iota(jnp.int32, (1, LANES), 1)
    first = (lane % (2 * half)) < half
    outs = []
    for k in range(x.shape[1] // LANES):
        xb = x[:, k * LANES:(k + 1) * LANES]
        fwd = pltpu.roll(xb, LANES - half, 1)
        bwd = pltpu.roll(xb, half, 1)
        outs.append(xb * cos + jnp.where(first, fwd, bwd) * sin)
    return outs[0] if len(outs) == 1 else jnp.concatenate(outs, axis=1)


def _silu(a):
    return a / (1.0 + jnp.exp(-a))


def _sub(c):
    return slice(c * SUB_ROWS, (c + 1) * SUB_ROWS)


def _adaln_block(layer, c_ref, w_ref, b_ref):
    a = _silu(c_ref[...]).astype(BF16)
    return _dot(a, w_ref[...].astype(BF16)) + b_ref[layer:layer + 1, :]


def _adaln_kernel(layer, c_ref, w_ref, b_ref, o_ref):
    o_ref[...] = _adaln_block(layer, c_ref, w_ref, b_ref)


def _adaln(layer, cvec, w_mod, b_mod):
    rows = cvec.shape[0]
    nc = 4
    wc = 6 * D_MODEL // nc
    return pl.pallas_call(
        functools.partial(_adaln_kernel, layer),
        grid=(nc,),
        in_specs=[_full((rows, D_MODEL)),
                  pl.BlockSpec((None, D_MODEL, wc), lambda n: (layer, 0, n)),
                  pl.BlockSpec((DEPTH, wc), lambda n: (0, n))],
        out_specs=pl.BlockSpec((rows, wc), lambda n: (0, n)),
        out_shape=jax.ShapeDtypeStruct((rows, 6 * D_MODEL), F32),
        compiler_params=_params("parallel"),
        name="adaln",
    )(cvec, w_mod, b_mod)


class _Rows(NamedTuple):
    lat_b: int
    lat_n: int
    ctx_b: int
    ctx_n: int

    @property
    def lat_rows(self):
        return self.lat_b * self.lat_n

    @property
    def ctx_rows(self):
        return self.ctx_b * self.ctx_n

    @property
    def lat_tiles(self):
        return self.lat_rows // ROW_TILE

    @property
    def tiles(self):
        return (self.lat_rows + self.ctx_rows) // ROW_TILE

    @property
    def tiles_per_lat_req(self):
        return self.lat_n // ROW_TILE


class _TileSpecs:
    def __init__(self, rows):
        lat_tiles = rows.lat_tiles
        per_req = rows.tiles_per_lat_req
        self.lat_tiles = lat_tiles
        self.mod = pl.BlockSpec((1, 6, D_MODEL), lambda i: (jnp.where(i < lat_tiles, 1 + i // per_req, 0), 0, 0))
        self.table = pl.BlockSpec((ROW_TILE, LANES), lambda i: (jnp.where(i < lat_tiles, i % per_req, per_req), 0))

    def rows(self, width):
        return pl.BlockSpec((ROW_TILE, width), lambda i: (i, 0))

    def lat_rows(self, width):
        last = self.lat_tiles - 1
        return pl.BlockSpec((ROW_TILE, width), lambda i: (jnp.minimum(i, last), 0))

    def ctx_rows(self, width):
        first = self.lat_tiles
        return pl.BlockSpec((ROW_TILE, width), lambda i: (jnp.maximum(i - first, 0), 0))

    def chunks(self, feats):
        return pl.BlockSpec((N_SUB, feats, SUB_ROWS), lambda i: (i, 0, 0))

    def state(self, layer_j, feats, cols):
        first = self.lat_tiles
        return pl.BlockSpec((N_SUB, None, feats, cols), lambda i: (jnp.maximum(i - first, 0), layer_j, 0, 0))


def _pick_rows(lat_tiles, lat_ref, ctx_ref, r):
    return jnp.where(pl.program_id(0) < lat_tiles, lat_ref[r, :], ctx_ref[r, :])


def _modulated(x, mod, g, shift_row, scale_row):
    return _rms(x, g) * (1.0 + mod[scale_row:scale_row + 1]) + mod[shift_row:shift_row + 1]


def _in_even_kernel(lat_tiles, n_aliased, *refs):
    (xl_ref, xc_ref, mod_ref, g1_ref, wt_ref, gqa_ref, gkva_ref, wuq_ref, wukv_ref,
     gqn_ref, gqr_ref, gkn_ref, gkr_ref, gdq_ref, gdk_ref, g64_ref, g32_ref,
     c64_ref, s64_ref, c32_ref, s32_ref) = refs[:21]
    refs = refs[21 + n_aliased:]
    qn_ref, qr_ref, kn_ref, vm_ref, krt_ref, mq_ref, mk_ref, mv_ref = refs[:8]
    ckv_s_ref, krt_s_ref, mk_s_ref, mv_s_ref = refs[8:]
    g64 = g64_ref[...]
    g32 = g32_ref[...]
    mod = mod_ref[0]
    mla_scale = (MLA_NOPE + MLA_ROPE) ** -0.5 * LOG2E
    diff_scale = DIFF_HD ** -0.5 * LOG2E

    for c in range(N_SUB):
        r = _sub(c)
        h = _modulated(_pick_rows(lat_tiles, xl_ref, xc_ref, r), mod, g1_ref[...], 0, 1)
        proj = _dot_nt(h.astype(BF16), wt_ref[...])
        cq = _rms(proj[:, 0:256], gqa_ref[...])
        q = _dot(cq.astype(BF16), wuq_ref[...])
        qn = _group_rms(q[:, 0:512], gqn_ref[...], g64)
        qr = _group_rms(q[:, 512:768], gqr_ref[...], g32)
        ckv = _rms(proj[:, 256:384], gkva_ref[...])
        kv = _dot(ckv.astype(BF16), wukv_ref[...])
        kn = _group_rms(kv[:, 0:512], gkn_ref[...], g64)
        mq = _group_rms(proj[:, 384:896], gdq_ref[...], g64)
        mk = _group_rms(proj[:, 896:1408], gdk_ref[...], g64)
        mv = proj[:, 1408:1920]
        kr = _group_rms(proj[:, 1920:2048], gkr_ref[...], g32)
        ckv_s_ref[c] = ckv
        krt_s_ref[c] = kr.T[0:MLA_ROPE]
        for hd in range(DIFF_HEADS):
            mk_s_ref[c, pl.ds(hd, SUB_ROWS, stride=DIFF_HEADS), :] = mk[:, hd * LANES:(hd + 1) * LANES]
            mv_s_ref[c, pl.ds(hd, SUB_ROWS, stride=DIFF_HEADS), :] = mv[:, hd * LANES:(hd + 1) * LANES]
        c64, s64, c32, s32 = c64_ref[r, :], s64_ref[r, :], c32_ref[r, :], s32_ref[r, :]
        qr = _rope(qr, c32, s32, MLA_ROPE // 4)
        kr = _rope(kr, c32, s32, MLA_ROPE // 4)
        mq = _rope(mq, c64, s64, DIFF_HD // 4)
        mk = _rope(mk, c64, s64, DIFF_HD // 4)
        qn_ref[r, :] = (qn * mla_scale).astype(BF16)
        qr_ref[r, :] = (qr * mla_scale).astype(BF16)
        kn_ref[r, :] = kn.astype(BF16)
        vm_ref[r, :] = kv[:, 512:1024].astype(BF16)
        krt_ref[r, :] = kr.astype(BF16)
        mq_ref[r, :] = (mq * diff_scale).astype(BF16)
        mk_ref[r, :] = mk.astype(BF16)
        mv_ref[r, :] = mv.astype(BF16)


def _in_odd_kernel(lat_tiles, n_aliased, *refs):
    (xl_ref, xc_ref, mod_ref, g1_ref, w_ref, gqc_ref, gkc_ref, gqd_ref, gkd_ref, g64_ref,
     c64_ref, s64_ref) = refs[:12]
    refs = refs[12 + n_aliased:]
    qc_ref, qd_ref = refs[:2]
    chunk_refs = refs[2:6]
    state_refs = refs[6:10]
    g64 = g64_ref[...]
    mod = mod_ref[0]
    w = w_ref[...].astype(BF16)
    scale = HEAD_DIM ** -0.5 * LOG2E

    for c in range(N_SUB):
        r = _sub(c)
        h = _modulated(_pick_rows(lat_tiles, xl_ref, xc_ref, r), mod, g1_ref[...], 0, 1)
        proj = _dot(h.astype(BF16), w)
        qc = _group_rms(proj[:, 0:512], gqc_ref[...], g64)
        kc = _group_rms(proj[:, 512:640], gkc_ref[...], g64)
        qd = _group_rms(proj[:, 768:1280], gqd_ref[...], g64)
        kd = _group_rms(proj[:, 1280:1408], gkd_ref[...], g64)
        c64, s64 = c64_ref[r, :], s64_ref[r, :]
        qc = _rope(qc, c64, s64, HEAD_DIM // 4)
        kc = _rope(kc, c64, s64, HEAD_DIM // 4)
        qd = _rope(qd, c64, s64, HEAD_DIM // 4)
        kd = _rope(kd, c64, s64, HEAD_DIM // 4)
        qc_ref[r, :] = (qc * scale).astype(BF16)
        qd_ref[r, :] = (qd * scale).astype(BF16)
        for k, val in enumerate((kc, proj[:, 640:768], kd, proj[:, 1408:1536])):
            val_t = val.T
            chunk_refs[k][c] = val_t
            state_refs[k][c] = val_t


def _with_states(specs, layer_j, in_specs, args, out_shape, out_specs, states):
    aliases = {}
    for st in states:
        if not isinstance(st, jax.ShapeDtypeStruct):
            aliases[len(args)] = len(out_shape)
            in_specs.append(pl.BlockSpec(memory_space=pl.ANY))
            args.append(st)
        out_shape.append(jax.ShapeDtypeStruct(st.shape, st.dtype))
        out_specs.append(specs.state(layer_j, st.shape[2], st.shape[3]))
    return aliases


def _in_even(rows, layer_j, x_lat, x_ctx, mods, g1, wt, wuq, wukv, gains, gmats, tables, states):
    specs = _TileSpecs(rows)
    n_rows = rows.lat_rows + rows.ctx_rows
    vec_specs = [_full(g.shape) for g in gains]
    in_specs = ([specs.lat_rows(D_MODEL), specs.ctx_rows(D_MODEL), specs.mod, _full(g1.shape), _full(wt.shape)]
                + vec_specs[:2] + [_full(wuq.shape), _full(wukv.shape)] + vec_specs[2:]
                + [_full((MXU_DIM, MXU_DIM))] * 2 + [specs.table] * 4)
    args = ([x_lat, x_ctx, mods, g1, wt, gains[0], gains[1], wuq, wukv] + list(gains[2:]) + list(gmats)
            + list(tables))
    widths = [512, 256, 512, 512, 128, 512, 512, 512]
    out_shape = [jax.ShapeDtypeStruct((n_rows, wd), BF16) for wd in widths]
    out_specs = [specs.rows(wd) for wd in widths]
    aliases = _with_states(specs, layer_j, in_specs, args, out_shape, out_specs, states)
    return pl.pallas_call(
        functools.partial(_in_even_kernel, rows.lat_tiles, len(aliases)),
        grid=(rows.tiles,),
        in_specs=in_specs, out_specs=out_specs, out_shape=out_shape,
        input_output_aliases=aliases,
        compiler_params=_params("arbitrary"),
        name="in_even",
    )(*args)


def _in_odd(rows, layer_j, x_lat, x_ctx, mods, g1, w_all, gains, g64, tables, states):
    specs = _TileSpecs(rows)
    n_rows = rows.lat_rows + rows.ctx_rows
    w_spec = pl.BlockSpec((None,) + w_all.shape[1:], lambda i: (layer_j, 0, 0))
    in_specs = ([specs.lat_rows(D_MODEL), specs.ctx_rows(D_MODEL), specs.mod, _full(g1.shape), w_spec]
                + [_full(g.shape) for g in gains] + [_full((MXU_DIM, MXU_DIM))] + [specs.table] * 2)
    args = [x_lat, x_ctx, mods, g1, w_all] + list(gains) + [g64] + list(tables)
    out_shape = ([jax.ShapeDtypeStruct((n_rows, 512), BF16)] * 2
                 + [jax.ShapeDtypeStruct((n_rows // SUB_ROWS, LANES, SUB_ROWS), F32)] * 4)
    out_specs = [specs.rows(512), specs.rows(512)] + [specs.chunks(LANES)] * 4
    aliases = _with_states(specs, layer_j, in_specs, args, out_shape, out_specs, states)
    return pl.pallas_call(
        functools.partial(_in_odd_kernel, rows.lat_tiles, len(aliases)),
        grid=(rows.tiles,),
        in_specs=in_specs, out_specs=out_specs, out_shape=out_shape,
        input_output_aliases=aliases,
        compiler_params=_params("arbitrary"),
        name="in_odd",
    )(*args)


def _cache_kv_kernel(ckv_ref, wukv_ref, gkn_ref, g64_ref, kn_ref, vm_ref):
    kv = _dot(ckv_ref[...].astype(BF16), wukv_ref[...])
    kn_ref[...] = _group_rms(kv[:, 0:512], gkn_ref[...], g64_ref[...]).astype(BF16)
    vm_ref[...] = kv[:, 512:1024].astype(BF16)


def _cache_kv(ckv, wukv, gkn, g64):
    rows = ckv.shape[0]
    tile = 512
    spec = lambda wd: pl.BlockSpec((tile, wd), lambda i: (i, 0))
    return pl.pallas_call(
        _cache_kv_kernel,
        grid=(rows // tile,),
        in_specs=[spec(MLA_KV_LORA), _full(wukv.shape), _full(gkn.shape), _full((MXU_DIM, MXU_DIM))],
        out_specs=[spec(512), spec(512)],
        out_shape=[jax.ShapeDtypeStruct((rows, 512), BF16)] * 2,
        compiler_params=_params("parallel"),
        name="cache_kv",
    )(ckv, wukv, gkn, g64)


def _softmax_pv(s_parts, v_parts, v_is_feature_major, sink=None):
    m = None
    for s in s_parts:
        mi = jnp.max(s, axis=-1, keepdims=True)
        m = mi if m is None else jnp.maximum(m, mi)
    if sink is not None:
        m = jnp.maximum(m, sink)
    acc = None
    for s, v in zip(s_parts, v_parts):
        p = jnp.exp2(s - m).astype(BF16)
        oi = _dot_nt(p, v) if v_is_feature_major else _dot(p, v)
        acc = oi if acc is None else acc + oi
    denom = acc[:, LANES:]
    if sink is not None:
        denom = denom + jnp.exp2(sink - m)
    return acc[:, :LANES] / denom


def _lane_masks(width):
    lane = lax.broadcasted_iota(jnp.int32, (1, LANES), 1)
    return [jnp.where(lane // width == k, 1.0, 0.0).astype(BF16) for k in range(LANES // width)]


def _attn_even_kernel(latent, lam_init, *refs):
    qn_ref, qr_ref, mq_ref = refs[:3]
    refs = refs[3:]
    n_parts = 2 if latent else 1
    parts = [refs[5 * k:5 * k + 5] for k in range(n_parts)]
    lamv_ref, gsub_ref, o_ref = refs[5 * n_parts:]
    lane = lax.broadcasted_iota(jnp.int32, (1, LANES), 1)
    low = lane < 64
    m64 = _lane_masks(64)
    m32 = _lane_masks(32)
    ones = [jnp.ones((p[0].shape[0], LANES), BF16) for p in parts]

    def diff_head(ref, idx, hd):
        if latent and idx == 0:
            n_keys = ref.shape[0] // DIFF_HEADS
            return ref[pl.ds(hd, n_keys, stride=DIFF_HEADS), :].astype(BF16)
        return ref[:, hd * LANES:(hd + 1) * LANES]

    for i in range(MLA_HEADS // 2):
        cols = slice(i * LANES, (i + 1) * LANES)
        qn_b = qn_ref[:, cols]
        kcat = [jnp.concatenate([p[0][:, cols], p[2][...]], axis=1) for p in parts]
        vms = [jnp.concatenate([p[1][:, cols], one], axis=1) for p, one in zip(parts, ones)]
        outs = []
        for half in range(2):
            head = 2 * i + half
            rb = head // 4
            qr_b = qr_ref[:, rb * LANES:(rb + 1) * LANES]
            lhs = jnp.concatenate([qn_b * m64[half], qr_b * m32[head % 4]], axis=1)
            outs.append(_softmax_pv([_dot_nt(lhs, kc) for kc in kcat], vms, False))
        o_ref[:, cols] = jnp.where(low, outs[0], outs[1]).astype(BF16)

    lv = lamv_ref[...]
    lam = (jnp.exp(jnp.sum(lv[0:1] * lv[1:2], axis=-1, keepdims=True))
           - jnp.exp(jnp.sum(lv[2:3] * lv[3:4], axis=-1, keepdims=True)) + lam_init)
    gsub = gsub_ref[...]
    for hd in range(DIFF_HEADS):
        q_b = mq_ref[:, hd * LANES:(hd + 1) * LANES]
        ks = [diff_head(p[3], idx, hd) for idx, p in enumerate(parts)]
        vs = [jnp.concatenate([diff_head(p[4], idx, hd), one], axis=1)
              for idx, (p, one) in enumerate(zip(parts, ones))]
        a1 = _softmax_pv([_dot_nt(q_b * m64[0], k) for k in ks], vs, False)
        a2 = _softmax_pv([_dot_nt(q_b * m64[1], k) for k in ks], vs, False)
        d = _rms(a1 - lam * a2, gsub) * (1.0 - lam_init)
        o_ref[:, 512 + hd * LANES:512 + (hd + 1) * LANES] = d.astype(BF16)


def _attn_even(latent, rows, layer_j, lam_init, q_arrs, kv_arrs, cache_arrs, lamv, gsub):
    batch, n_tokens, row0 = (rows.lat_b, rows.lat_n, 0) if latent else (rows.ctx_b, rows.ctx_n, rows.lat_rows)
    tq = min(Q_TILE_DENSE, n_tokens)
    nt = n_tokens // tq
    q0 = row0 // tq
    k0 = row0 // n_tokens
    qspec = lambda wd: pl.BlockSpec((tq, wd), lambda b, t: (q0 + b * nt + t, 0))
    kspec = lambda wd: pl.BlockSpec((n_tokens, wd), lambda b, t: (k0 + b, 0))
    kwidths = [512, 512, 128, 512, 512]
    in_specs = [qspec(512), qspec(256), qspec(512)]
    args = list(q_arrs)
    if latent:
        past = cache_arrs[0].shape[0] // batch
        in_specs += [pl.BlockSpec((past, wd), lambda b, t: (b, 0)) for wd in kwidths[:3]]
        in_specs += [pl.BlockSpec((None, None, past * DIFF_HEADS, LANES), lambda b, t: (b, layer_j, 0, 0))] * 2
        args += list(cache_arrs)
    in_specs += [kspec(wd) for wd in kwidths]
    args += list(kv_arrs)
    in_specs += [_full(lamv.shape), _full(gsub.shape)]
    args += [lamv, gsub]
    return pl.pallas_call(
        functools.partial(_attn_even_kernel, latent, lam_init),
        grid=(batch, nt),
        in_specs=in_specs,
        out_specs=pl.BlockSpec((tq, D_MODEL), lambda b, t: (b * nt + t, 0)),
        out_shape=jax.ShapeDtypeStruct((batch * n_tokens, D_MODEL), BF16),
        compiler_params=_params("parallel", "parallel"),
        name="attn_even_lat" if latent else "attn_even_ctx",
    )(*args)


def _attn_odd_kernel(latent, tq, *refs):
    qc_ref, qd_ref = refs[:2]
    refs = refs[2:]
    n_parts = 2 if latent else 1
    parts = [list(refs[4 * k:4 * k + 4]) for k in range(n_parts)]
    sink_ref, o_ref = refs[4 * n_parts:4 * n_parts + 2]
    scratch = refs[4 * n_parts + 2:]
    lane = lax.broadcasted_iota(jnp.int32, (1, LANES), 1)
    low = lane < 64
    m64 = _lane_masks(64)

    band = None
    start = 0
    span = 0
    if latent:
        for ref, scr in zip(parts[-1], scratch):
            for c in range(ref.shape[0]):
                scr[:, _sub(c)] = ref[c]
        parts[-1] = list(scratch)
        n_new = scratch[0].shape[1]
        span = min(n_new, tq + 2 * WINDOW)
        t = pl.program_id(1)
        start = pl.multiple_of(jnp.clip(t * tq - WINDOW, 0, n_new - span), LANES)
        qpos = t * tq + lax.broadcasted_iota(jnp.int32, (tq, 1), 0)
        kpos = start + lax.broadcasted_iota(jnp.int32, (1, span), 1)
        band = jnp.abs(qpos - kpos) <= WINDOW

    for kind in range(2):
        q_ref = qc_ref if kind == 0 else qd_ref
        windowed = latent and kind == 0
        for g in range(2):
            rows = slice(g * HEAD_DIM, (g + 1) * HEAD_DIM)
            ks, vs = [], []
            for idx, p in enumerate(parts):
                k_ref, v_ref = p[2 * kind], p[2 * kind + 1]
                if windowed and idx == n_parts - 1:
                    k = k_ref[rows, pl.ds(start, span)]
                    v = v_ref[rows, pl.ds(start, span)]
                else:
                    k = k_ref[rows, :]
                    v = v_ref[rows, :]
                ks.append(jnp.concatenate([k, k], axis=0).astype(BF16))
                vs.append(jnp.concatenate([v, v, jnp.ones((LANES, v.shape[1]), F32)], axis=0).astype(BF16))
            for i in (2 * g, 2 * g + 1):
                cols = slice(i * LANES, (i + 1) * LANES)
                q_b = q_ref[:, cols]
                outs = []
                for half in range(2):
                    s_parts = [_dot(q_b * m64[half], k) for k in ks]
                    sink = None
                    if kind == 0:
                        sink = sink_ref[2 * i + half] * LOG2E
                        if latent:
                            s_parts[-1] = jnp.where(band, s_parts[-1], NEG_BIG)
                    outs.append(_softmax_pv(s_parts, vs, True, sink))
                o_ref[:, kind * 512 + i * LANES:kind * 512 + (i + 1) * LANES] = (
                    jnp.where(low, outs[0], outs[1]).astype(BF16))


def _attn_odd(latent, rows, layer_j, q_arrs, new_arrs, cache_arrs, sink):
    batch, n_tokens, row0 = (rows.lat_b, rows.lat_n, 0) if latent else (rows.ctx_b, rows.ctx_n, rows.lat_rows)
    tq = min(Q_TILE_BANDED, n_tokens)
    nt = n_tokens // tq
    q0 = row0 // tq
    in_specs = [pl.BlockSpec((tq, 512), lambda b, t: (q0 + b * nt + t, 0))] * 2
    args = list(q_arrs)
    scratch = []
    if latent:
        past = cache_arrs[0].shape[-1]
        in_specs += [pl.BlockSpec((None, None, LANES, past), lambda b, t: (b, layer_j, 0, 0))] * 4
        args += list(cache_arrs)
        chunks = n_tokens // SUB_ROWS
        in_specs += [pl.BlockSpec((chunks, LANES, SUB_ROWS), lambda b, t: (b, 0, 0))] * 4
        scratch = [pltpu.VMEM((LANES, n_tokens), F32)] * 4
    else:
        in_specs += [pl.BlockSpec((None, None, LANES, n_tokens), lambda b, t: (b, layer_j, 0, 0))] * 4
    args += list(new_arrs)
    in_specs += [pl.BlockSpec(memory_space=pltpu.SMEM)]
    args += [sink]
    return pl.pallas_call(
        functools.partial(_attn_odd_kernel, latent, tq),
        grid=(batch, nt),
        in_specs=in_specs,
        out_specs=pl.BlockSpec((tq, D_MODEL), lambda b, t: (b * nt + t, 0)),
        out_shape=jax.ShapeDtypeStruct((batch * n_tokens, D_MODEL), BF16),
        scratch_shapes=scratch,
        compiler_params=_params("parallel", "parallel"),
        name="attn_odd_lat" if latent else "attn_odd_ctx",
    )(*args)


def _post_kernel(lat_tiles, ol_ref, oc_ref, wout_ref, xl_ref, xc_ref, mod_ref, g2_ref, wr2_ref, wrh_ref,
                 x1_ref, h2_ref, afft_ref):
    mod = mod_ref[0]
    wout = wout_ref[...].astype(BF16)
    lane = lax.broadcasted_iota(jnp.int32, (1, LANES), 1)
    for c in range(N_SUB):
        r = _sub(c)
        o = _pick_rows(lat_tiles, ol_ref, oc_ref, r)
        x1 = _pick_rows(lat_tiles, xl_ref, xc_ref, r) + mod[2:3] * _dot(o, wout)
        x1_ref[r, :] = x1
        h2 = _modulated(x1, mod, g2_ref[...], 3, 4)
        h_hi = h2.astype(BF16)
        h_lo = (h2 - h_hi.astype(F32)).astype(BF16)
        h2_ref[r, :] = h_hi
        both = _dot(h_hi, wr2_ref[...])
        logits = both[:, :LANES] + both[:, LANES:] + _dot(h_lo, wrh_ref[...])
        logits = jnp.where(lane < N_EXPERTS, logits, NEG_BIG)
        e = jnp.exp(logits - jnp.max(logits, axis=-1, keepdims=True))
        aff = e / jnp.sum(e, axis=-1, keepdims=True)
        afft_ref[c] = aff.T[0:N_EXPERTS]


def _post(rows, layer_j, o_lat, o_ctx, wout_all, x_lat, x_ctx, mods, g2, wr2, wrh):
    specs = _TileSpecs(rows)
    n_rows = rows.lat_rows + rows.ctx_rows
    w_spec = pl.BlockSpec((None,) + wout_all.shape[1:], lambda i: (layer_j, 0, 0))
    return pl.pallas_call(
        functools.partial(_post_kernel, rows.lat_tiles),
        grid=(rows.tiles,),
        in_specs=[specs.lat_rows(D_MODEL), specs.ctx_rows(D_MODEL), w_spec,
                  specs.lat_rows(D_MODEL), specs.ctx_rows(D_MODEL), specs.mod, _full(g2.shape),
                  _full(wr2.shape), _full(wrh.shape)],
        out_specs=[specs.rows(D_MODEL), specs.rows(D_MODEL), specs.chunks(N_EXPERTS)],
        out_shape=[jax.ShapeDtypeStruct((n_rows, D_MODEL), F32),
                   jax.ShapeDtypeStruct((n_rows, D_MODEL), BF16),
                   jax.ShapeDtypeStruct((n_rows // SUB_ROWS, N_EXPERTS, SUB_ROWS), F32)],
        compiler_params=_params("parallel"),
        name="post",
    )(o_lat, o_ctx, wout_all, x_lat, x_ctx, mods, g2, wr2, wrh)


def _route_kernel(caps, *refs):
    n_groups = len(caps)
    aff_refs, out_refs = refs[:n_groups], refs[n_groups:]
    bits = [lax.bitcast_convert_type(ref[...], jnp.int32) for ref in aff_refs]

    def body(_, carry):
        new = []
        for b, cap, (lo, hi) in zip(bits, caps, carry):
            mid = lo + ((hi - lo + 1) >> 1)
            cnt = jnp.sum(jnp.where(b >= mid, 1.0, 0.0), axis=-1, keepdims=True)
            ok = cnt >= cap
            new.append((jnp.where(ok, mid, lo), jnp.where(ok, hi, mid - 1)))
        return tuple(new)

    init = tuple((jnp.zeros((b.shape[0], 1), jnp.int32), jnp.full((b.shape[0], 1), 0x7F800000, jnp.int32))
                 for b in bits)
    found = lax.fori_loop(0, 31, body, init)
    for b, cap, (thr, _), out_ref in zip(bits, caps, found, out_refs):
        n = b.shape[1]
        above = jnp.where(b > thr, 1.0, 0.0)
        equal = jnp.where(b == thr, 1.0, 0.0)
        room = cap - jnp.sum(above, axis=-1, keepdims=True)
        before = jnp.where(lax.broadcasted_iota(jnp.int32, (n, n), 0)
                           < lax.broadcasted_iota(jnp.int32, (n, n), 1), 1.0, 0.0).astype(BF16)
        equal_before = _dot(equal.astype(BF16), before)
        chosen = above + equal * jnp.where(equal_before < room, 1.0, 0.0)
        slot = _dot(chosen.astype(BF16), before)
        out_ref[...] = jnp.where(chosen > 0.5, slot, -1.0)


def _route(caps, aff_list):
    return pl.pallas_call(
        functools.partial(_route_kernel, tuple(caps)),
        grid=(1,),
        in_specs=[_full(a.shape) for a in aff_list],
        out_specs=[_full(a.shape) for a in aff_list],
        out_shape=[jax.ShapeDtypeStruct(a.shape, F32) for a in aff_list],
        compiler_params=_params("arbitrary"),
        name="route",
    )(*aff_list)


def _gather_part(cap, req_per_step, experts_per_step, first, slot_ref, aff_ref, h_ref, xs_ref, gate_ref):
    n = slot_ref.shape[1]
    want = lax.broadcasted_iota(jnp.int32, (cap, 1), 0).astype(F32)
    for rq in range(req_per_step):
        out_rows = slice(rq * cap, (rq + 1) * cap)
        rows = []
        for k in range(experts_per_step):
            row = rq * N_EXPERTS + first + k
            hit = slot_ref[pl.ds(row, 1), :] == want
            rows.append(jnp.where(hit, 1.0, 0.0).astype(BF16))
            gate_ref[k, out_rows, :] = jnp.sum(jnp.where(hit, aff_ref[pl.ds(row, 1), :], 0.0),
                                               axis=-1, keepdims=True)
        picked = _dot(jnp.concatenate(rows, axis=0), h_ref[rq * n:(rq + 1) * n, :]).astype(BF16)
        for k in range(experts_per_step):
            xs_ref[k, out_rows, :] = picked[k * cap:(k + 1) * cap]


def _gather_kernel(lat_steps, groups, caps, ctx_req_per_step,
                   sl_l, af_l, h_l, sl_c, af_c, h_c, xs_l, gt_l, xs_c, gt_c):
    s = pl.program_id(0)
    per_group = N_EXPERTS // groups

    @pl.when(s < lat_steps)
    def _():
        _gather_part(caps[0], 1, per_group, (s % groups) * per_group, sl_l, af_l, h_l, xs_l, gt_l)

    @pl.when(s >= lat_steps)
    def _():
        _gather_part(caps[1], ctx_req_per_step, N_EXPERTS, 0, sl_c, af_c, h_c, xs_c, gt_c)


def _gather(rows, caps, slots, affs, h2):
    groups = rows.lat_n // SCATTER_ROWS
    per_group = N_EXPERTS // groups
    rps = ROW_TILE // rows.ctx_n
    lat_steps = rows.lat_b * groups
    ctx_steps = rows.ctx_b // rps
    last_b = rows.lat_b - 1
    ctx_block0 = rows.lat_rows // (rps * rows.ctx_n)
    lat_req = lambda s: jnp.minimum(s // groups, last_b)
    lat_grp = lambda s: jnp.where(s < lat_steps, s % groups, groups - 1)
    ctx_step = lambda s: jnp.maximum(s - lat_steps, 0)
    cap_l, cap_c = caps
    lat_rows_spec = pl.BlockSpec((N_EXPERTS, rows.lat_n), lambda s: (lat_req(s), 0))
    ctx_rows_spec = pl.BlockSpec((rps * N_EXPERTS, rows.ctx_n), lambda s: (ctx_step(s), 0))
    return pl.pallas_call(
        functools.partial(_gather_kernel, lat_steps, groups, caps, rps),
        grid=(lat_steps + ctx_steps,),
        in_specs=[lat_rows_spec, lat_rows_spec,
                  pl.BlockSpec((rows.lat_n, D_MODEL), lambda s: (lat_req(s), 0)),
                  ctx_rows_spec, ctx_rows_spec,
                  pl.BlockSpec((rps * rows.ctx_n, D_MODEL), lambda s: (ctx_block0 + ctx_step(s), 0))],
        out_specs=[pl.BlockSpec((per_group, cap_l, D_MODEL), lambda s: (lat_grp(s), lat_req(s), 0)),
                   pl.BlockSpec((per_group, cap_l, 1), lambda s: (lat_grp(s), lat_req(s), 0)),
                   pl.BlockSpec((N_EXPERTS, rps * cap_c, D_MODEL), lambda s: (0, ctx_step(s), 0)),
                   pl.BlockSpec((N_EXPERTS, rps * cap_c, 1), lambda s: (0, ctx_step(s), 0))],
        out_shape=[jax.ShapeDtypeStruct((N_EXPERTS, rows.lat_b * cap_l, D_MODEL), BF16),
                   jax.ShapeDtypeStruct((N_EXPERTS, rows.lat_b * cap_l, 1), F32),
                   jax.ShapeDtypeStruct((N_EXPERTS, rows.ctx_b * cap_c, D_MODEL), BF16),
                   jax.ShapeDtypeStruct((N_EXPERTS, rows.ctx_b * cap_c, 1), F32)],
        compiler_params=_params("arbitrary"),
        name="gather",
    )(slots[0], affs[0], h2, slots[1], affs[1], h2)


def _ffn_kernel(n_groups, next_layer, *refs):
    xs_refs = refs[:n_groups]
    gate_refs = refs[n_groups:2 * n_groups]
    w1_ref, w3_ref, w2_ref = refs[2 * n_groups:2 * n_groups + 3]
    refs = refs[2 * n_groups + 3:]
    if next_layer is not None:
        c_ref, wm_ref, bm_ref = refs[:3]
        refs = refs[3:]
        mods_ref = refs[n_groups]
        acc_refs = refs[n_groups + 1:]
    else:
        acc_refs = refs[n_groups:]
    ys_refs = refs[:n_groups]
    f = pl.program_id(1)

    @pl.when(f == 0)
    def _():
        for acc_ref in acc_refs:
            acc_ref[...] = jnp.zeros(acc_ref.shape, F32)

    if next_layer is not None:
        mods_ref[...] = _adaln_block(next_layer, c_ref, wm_ref, bm_ref)

    hidden = [[] for _ in xs_refs]
    for n in range(0, w1_ref.shape[3], MXU_DIM):
        w1 = w1_ref[0, 0, :, n:n + MXU_DIM].astype(BF16)
        w3 = w3_ref[0, 0, :, n:n + MXU_DIM].astype(BF16)
        for parts, xs_ref in zip(hidden, xs_refs):
            xs = xs_ref[0]
            parts.append((_silu(_dot(xs, w1)) * _dot(xs, w3)).astype(BF16))
    hidden = [jnp.concatenate(parts, axis=1) for parts in hidden]
    for n in range(0, w2_ref.shape[3], MXU_DIM):
        w2 = w2_ref[0, 0, :, n:n + MXU_DIM].astype(BF16)
        for hid, acc_ref in zip(hidden, acc_refs):
            acc_ref[:, n:n + MXU_DIM] += _dot(hid, w2)

    @pl.when(f == pl.num_programs(1) - 1)
    def _():
        for gate_ref, ys_ref, acc_ref in zip(gate_refs, ys_refs, acc_refs):
            ys_ref[0] = (acc_ref[...] * gate_ref[0]).astype(BF16)


def _ffn(layer, xs_list, gate_list, w1, w3, w2, mod_inputs=None):
    n_groups = len(xs_list)
    n_chunks = EXPERT_FF // FF_CHUNK
    xs_specs = [pl.BlockSpec((1,) + xs.shape[1:], lambda e, f: (e, 0, 0)) for xs in xs_list]
    gate_specs = [pl.BlockSpec((1,) + g.shape[1:], lambda e, f: (e, 0, 0)) for g in gate_list]
    in_specs = xs_specs + gate_specs + [
        pl.BlockSpec((1, 1, D_MODEL, FF_CHUNK), lambda e, f: (layer, e, 0, f)),
        pl.BlockSpec((1, 1, D_MODEL, FF_CHUNK), lambda e, f: (layer, e, 0, f)),
        pl.BlockSpec((1, 1, FF_CHUNK, D_MODEL), lambda e, f: (layer, e, f, 0))]
    args = [*xs_list, *gate_list, w1, w3, w2]
    out_specs = list(xs_specs)
    out_shape = [jax.ShapeDtypeStruct(xs.shape, BF16) for xs in xs_list]
    next_layer = None
    if mod_inputs is not None:
        next_layer = layer + 1
        cvec, w_mod, b_mod = mod_inputs
        last = 6 * D_MODEL // MXU_DIM - 1
        assert N_EXPERTS * n_chunks > last
        block = lambda e, f: jnp.minimum(e * n_chunks + f, last)
        in_specs += [_full(cvec.shape),
                     pl.BlockSpec((None, D_MODEL, MXU_DIM), lambda e, f: (next_layer, 0, block(e, f))),
                     pl.BlockSpec((DEPTH, MXU_DIM), lambda e, f: (0, block(e, f)))]
        args += [cvec, w_mod, b_mod]
        out_specs.append(pl.BlockSpec((cvec.shape[0], MXU_DIM), lambda e, f: (0, block(e, f))))
        out_shape.append(jax.ShapeDtypeStruct((cvec.shape[0], 6 * D_MODEL), F32))
    return pl.pallas_call(
        functools.partial(_ffn_kernel, n_groups, next_layer),
        grid=(N_EXPERTS, n_chunks),
        in_specs=in_specs, out_specs=out_specs, out_shape=out_shape,
        scratch_shapes=[pltpu.VMEM(xs.shape[1:], F32) for xs in xs_list],
        compiler_params=_params("arbitrary", "arbitrary", vmem=FFN_VMEM_LIMIT),
        name="ffn",
    )(*args)


def _scatter_part(cap, slots, ys_ref, ys_rows, x, gate2):
    lane = lax.broadcasted_iota(jnp.int32, (1, LANES), 1).astype(F32)
    per_block = LANES // cap
    pad = jnp.zeros((LANES - N_EXPERTS, SUB_ROWS), F32)
    slots_t = jnp.concatenate([slots, pad], axis=0).T
    blocks = []
    for k in range(N_EXPERTS // per_block):
        hit = None
        for e in range(k * per_block, (k + 1) * per_block):
            col = slots_t[:, e:e + 1]
            target = jnp.where(col >= 0.0, col + float((e - k * per_block) * cap), -1.0)
            he = jnp.where(target == lane, 1.0, 0.0)
            hit = he if hit is None else hit + he
        blocks.append(hit.astype(BF16))
    onehot = jnp.concatenate(blocks, axis=1)
    ys = jnp.concatenate([ys_ref[e, ys_rows, :] for e in range(N_EXPERTS)], axis=0)
    return x + gate2 * _dot(onehot, ys)


def _combine_kernel(lat_steps, steps_per_req, caps, sl_l, ys_l, sl_c, ys_c, x_ref, mod_ref, ol_ref, oc_ref):
    s = pl.program_id(0)
    gate2 = mod_ref[0][5:6]
    chunks = SCATTER_ROWS // SUB_ROWS

    @pl.when(s < lat_steps)
    def _():
        first = (s % steps_per_req) * SCATTER_ROWS
        for c in range(chunks):
            tokens = pl.multiple_of(first + c * SUB_ROWS, SUB_ROWS)
            ol_ref[_sub(c), :] = _scatter_part(caps[0], sl_l[:, pl.ds(tokens, SUB_ROWS)], ys_l,
                                               slice(0, caps[0]), x_ref[_sub(c), :], gate2)

    @pl.when(s >= lat_steps)
    def _():
        for rq in range(chunks):
            oc_ref[_sub(rq), :] = _scatter_part(caps[1], sl_c[rq * N_EXPERTS:(rq + 1) * N_EXPERTS, :], ys_c,
                                                slice(rq * caps[1], (rq + 1) * caps[1]), x_ref[_sub(rq), :], gate2)


def _combine(rows, caps, slots, ys, x1, mods):
    per_req = rows.lat_n // SCATTER_ROWS
    req_per_step = SCATTER_ROWS // rows.ctx_n
    lat_steps = rows.lat_b * per_req
    ctx_steps = rows.ctx_b // req_per_step
    last_b = rows.lat_b - 1
    lat_req = lambda s: jnp.minimum(s // per_req, last_b)
    ctx_step = lambda s: jnp.maximum(s - lat_steps, 0)
    cap_l, cap_c = caps
    return pl.pallas_call(
        functools.partial(_combine_kernel, lat_steps, per_req, caps),
        grid=(lat_steps + ctx_steps,),
        in_specs=[pl.BlockSpec((N_EXPERTS, rows.lat_n), lambda s: (lat_req(s), 0)),
                  pl.BlockSpec((N_EXPERTS, cap_l, D_MODEL), lambda s: (0, lat_req(s), 0)),
                  pl.BlockSpec((req_per_step * N_EXPERTS, rows.ctx_n), lambda s: (ctx_step(s), 0)),
                  pl.BlockSpec((N_EXPERTS, req_per_step * cap_c, D_MODEL), lambda s: (0, ctx_step(s), 0)),
                  pl.BlockSpec((SCATTER_ROWS, D_MODEL), lambda s: (s, 0)),
                  pl.BlockSpec((1, 6, D_MODEL), lambda s: (jnp.where(s < lat_steps, 1 + s // per_req, 0), 0, 0))],
        out_specs=[pl.BlockSpec((SCATTER_ROWS, D_MODEL), lambda s: (jnp.minimum(s, lat_steps - 1), 0)),
                   pl.BlockSpec((SCATTER_ROWS, D_MODEL), lambda s: (ctx_step(s), 0))],
        out_shape=[jax.ShapeDtypeStruct((rows.lat_rows, D_MODEL), F32),
                   jax.ShapeDtypeStruct((rows.ctx_rows, D_MODEL), F32)],
        compiler_params=_params("arbitrary", vmem=FFN_VMEM_LIMIT),
        name="combine",
    )(slots[0], ys[0], slots[1], ys[1], x1, mods)


def _rope_tables(n_pos, rot_dim):
    t = jnp.arange(n_pos)
    row = t // GRID_W
    col = t % GRID_W
    nf = rot_dim // 4
    inv = ROPE_THETA ** (-jnp.arange(nf, dtype=F32) / nf)
    ang_r = row[:, None] * inv
    ang_c = col[:, None] * inv
    cr, sr, cc, sc = jnp.cos(ang_r), jnp.sin(ang_r), jnp.cos(ang_c), jnp.sin(ang_c)
    cos = jnp.concatenate([cr, cr, cc, cc], axis=1)
    sin = jnp.concatenate([-sr, sr, -sc, sc], axis=1)
    reps = LANES // rot_dim
    cos = jnp.concatenate([jnp.tile(cos, (1, reps)), jnp.ones((ROW_TILE, LANES), F32)], axis=0)
    sin = jnp.concatenate([jnp.tile(sin, (1, reps)), jnp.zeros((ROW_TILE, LANES), F32)], axis=0)
    return cos, sin


def _group_mean_matrix(width):
    idx = jnp.arange(MXU_DIM) // width
    return jnp.where(idx[:, None] == idx[None, :], 1.0 / width, 0.0).astype(BF16)


def _tile_row(g, reps):
    return jnp.tile(g, reps).reshape(1, -1)


def _split_heads(w, n_heads, first):
    k = w.shape[0]
    w3 = w.reshape(k, n_heads, -1)
    return jnp.concatenate([w3[:, :, :first].reshape(k, -1), w3[:, :, first:].reshape(k, -1)], axis=1)


def _feature_major(cache):
    b, l, t, h, d = cache.shape
    return jnp.transpose(cache, (0, 1, 3, 4, 2)).reshape(b, l, h * d, t)


def kernel(x_prompt, x_sample, cache_mla_ckv, cache_mla_krope, cache_diff_k, cache_diff_v,
           cache_win_k, cache_win_v, cache_axial_k, cache_axial_v, c, c_ctx,
           g_norm, w_mod, b_mod, w_in_even, w_out_even, mla_g_qa, mla_g_kva, mla_w_uq, mla_w_ukv,
           mla_g_q, mla_g_k, diff_g_q, diff_g_k, diff_lambda, diff_g_sub,
           w_in_odd, w_out_odd, odd_g_qk, win_sink, moe_w_router, moe_w1, moe_w3, moe_w2):
    rows = _Rows(lat_b=x_sample.shape[0], lat_n=x_sample.shape[1], ctx_b=x_prompt.shape[0], ctx_n=x_prompt.shape[1])
    past = cache_mla_ckv.shape[2]
    n_even, n_odd = w_in_even.shape[0], w_in_odd.shape[0]
    assert rows.ctx_n == SUB_ROWS and rows.lat_n % ROW_TILE == 0 and rows.ctx_b % N_SUB == 0
    caps = (EC_CAPACITY_FACTOR * rows.lat_n // N_EXPERTS, EC_CAPACITY_FACTOR * rows.ctx_n // N_EXPERTS)

    cvec = jnp.zeros((16, D_MODEL), F32).at[0].set(c_ctx).at[1:1 + rows.lat_b].set(c)
    mods_l = _adaln(0, cvec, w_mod, b_mod).reshape(16, 6, D_MODEL)

    g64 = _group_mean_matrix(64)
    g32 = _group_mean_matrix(32)
    tab64 = _rope_tables(rows.lat_n, HEAD_DIM)
    tab32 = _rope_tables(rows.lat_n, MLA_ROPE)
    odd_caches = [_feature_major(a) for a in (cache_win_k, cache_win_v, cache_axial_k, cache_axial_v)]
    diff_caches = [a.reshape(rows.lat_b, n_even, past * DIFF_HEADS, LANES) for a in (cache_diff_k, cache_diff_v)]

    state = lambda *shape: jax.ShapeDtypeStruct(shape, F32)
    even_states = [state(rows.ctx_b, n_even, rows.ctx_n, MLA_KV_LORA),
                   state(rows.ctx_b, n_even, MLA_ROPE, rows.ctx_n),
                   state(rows.ctx_b, n_even, rows.ctx_n * DIFF_HEADS, LANES),
                   state(rows.ctx_b, n_even, rows.ctx_n * DIFF_HEADS, LANES)]
    odd_states = [state(rows.ctx_b, n_odd, LANES, rows.ctx_n) for _ in range(4)]

    x_lat = x_sample.reshape(-1, D_MODEL)
    x_ctx = x_prompt.reshape(-1, D_MODEL)
    lat_chunks = rows.lat_rows // SUB_ROWS

    for layer in range(DEPTH):
        j = layer // 2
        g1 = g_norm[layer, 0].reshape(1, -1)
        g2 = g_norm[layer, 1].reshape(1, -1)
        if layer % 2 == 0:
            wt = jnp.swapaxes(w_in_even[j], 0, 1)
            wt = jnp.concatenate([wt[:384], wt[416:]] + [wt[384:416]] * 4, axis=0).astype(BF16)
            wuq = _split_heads(mla_w_uq[j], MLA_HEADS, MLA_NOPE).astype(BF16)
            wukv = _split_heads(mla_w_ukv[j], MLA_HEADS, MLA_NOPE).astype(BF16)
            w_out = w_out_even
            gkn = _tile_row(mla_g_k[j, :MLA_NOPE], MLA_HEADS)
            gains = [mla_g_qa[j].reshape(1, -1), mla_g_kva[j].reshape(1, -1),
                     _tile_row(mla_g_q[j, :MLA_NOPE], MLA_HEADS), _tile_row(mla_g_q[j, MLA_NOPE:], MLA_HEADS),
                     gkn, _tile_row(mla_g_k[j, MLA_NOPE:], 4),
                     _tile_row(diff_g_q[j], 2 * DIFF_HEADS), _tile_row(diff_g_k[j], 2 * DIFF_HEADS)]
            lam_init = 0.8 - 0.6 * math.exp(-0.3 * layer)
            gsub = diff_g_sub[j].reshape(1, -1)
            outs = _in_even(rows, j, x_lat, x_ctx, mods_l, g1, wt, wuq, wukv, gains, (g64, g32),
                            tab64 + tab32, even_states)
            qn, qr, kn, vm, krt, mq, mk, mv = outs[:8]
            even_states = list(outs[8:])
            kn_c, vm_c = _cache_kv(cache_mla_ckv[:, j].reshape(-1, MLA_KV_LORA), wukv, gkn, g64)
            cache_arrs = (kn_c, vm_c,
                          jnp.tile(cache_mla_krope[:, j].reshape(-1, MLA_ROPE), (1, 4)).astype(BF16),
                          diff_caches[0], diff_caches[1])
            o_lat = _attn_even(True, rows, j, lam_init, (qn, qr, mq), (kn, vm, krt, mk, mv), cache_arrs,
                               diff_lambda[j], gsub)
            o_ctx = _attn_even(False, rows, j, lam_init, (qn, qr, mq), (kn, vm, krt, mk, mv), None,
                               diff_lambda[j], gsub)
        else:
            w_out = w_out_odd
            gains = [_tile_row(odd_g_qk[j, 0], 8), _tile_row(odd_g_qk[j, 1], 2),
                     _tile_row(odd_g_qk[j, 2], 8), _tile_row(odd_g_qk[j, 3], 2)]
            outs = _in_odd(rows, j, x_lat, x_ctx, mods_l, g1, w_in_odd, gains, g64, tab64, odd_states)
            odd_states = list(outs[6:])
            o_lat = _attn_odd(True, rows, j, outs[:2], outs[2:6], odd_caches, win_sink[j])
            o_ctx = _attn_odd(False, rows, j, outs[:2], odd_states, None, win_sink[j])

        w_r = jnp.pad(moe_w_router[layer], ((0, 0), (0, LANES - N_EXPERTS)))
        wrh = w_r.astype(BF16)
        wrl = (w_r - wrh.astype(F32)).astype(BF16)
        x1, h2, afft = _post(rows, j, o_lat, o_ctx, w_out, x_lat, x_ctx, mods_l, g2,
                             jnp.concatenate([wrh, wrl], axis=1), wrh)
        aff_lat = (afft[:lat_chunks].reshape(rows.lat_b, rows.lat_n // SUB_ROWS, N_EXPERTS, SUB_ROWS)
                   .transpose(0, 2, 1, 3).reshape(rows.lat_b * N_EXPERTS, rows.lat_n))
        aff_ctx = afft[lat_chunks:].reshape(rows.ctx_b * N_EXPERTS, rows.ctx_n)
        affs = (aff_lat, aff_ctx)
        slots = _route(caps, affs)
        xs_l, gt_l, xs_c, gt_c = _gather(rows, caps, slots, affs, h2)
        more = layer + 1 < DEPTH
        outs = _ffn(layer, [xs_l, xs_c], [gt_l, gt_c], moe_w1, moe_w3, moe_w2,
                    (cvec, w_mod, b_mod) if more else None)
        x_lat, x_ctx = _combine(rows, caps, slots, outs[:2], x1, mods_l)
        if more:
            mods_l = outs[2].reshape(16, 6, D_MODEL)

    def token_major(arr, heads):
        b, l, f, t = arr.shape
        return jnp.transpose(arr.reshape(b, l, heads, f // heads, t), (0, 1, 4, 2, 3))

    diff_shape = (rows.ctx_b, n_even, rows.ctx_n, DIFF_HEADS, LANES)
    return (x_ctx.reshape(x_prompt.shape), x_lat.reshape(x_sample.shape),
            even_states[0],
            jnp.swapaxes(even_states[1], 2, 3),
            even_states[2].reshape(diff_shape),
            even_states[3].reshape(diff_shape),
            token_major(odd_states[0], 2), token_major(odd_states[1], 2),
            token_major(odd_states[2], 2), token_major(odd_states[3], 2))
```

```python
import functools
import math
from typing import NamedTuple

import jax
import jax.numpy as jnp
from jax import lax
from jax.experimental import pallas as pl
from jax.experimental.pallas import tpu as pltpu

F32 = jnp.float32
BF16 = jnp.bfloat16

D_MODEL = 1024
DEPTH = 4
GRID_W = 64
ROPE_THETA = 10000.0
WINDOW = 128
RMS_EPS = 1e-6
MLA_HEADS = 8
MLA_Q_LORA = 256
MLA_KV_LORA = 128
MLA_NOPE = 64
MLA_ROPE = 32
MLA_VD = 64
DIFF_HEADS = 4
DIFF_HD = 64
HEAD_DIM = 64
N_EXPERTS = 16
EC_CAPACITY_FACTOR = 2
EXPERT_FF = 2048

LANES = 128
MXU_DIM = 256
ROW_TILE = 1024
SUB_ROWS = 256
N_SUB = ROW_TILE // SUB_ROWS
SCATTER_ROWS = 1024
Q_TILE_DENSE = 512
Q_TILE_BANDED = 256
FF_CHUNK = 1024
VMEM_LIMIT = 50 * 1024 * 1024
FFN_VMEM_LIMIT = 60 * 1024 * 1024
NEG_BIG = -1e30
LOG2E = 1.4426950408889634


def _params(*sem, vmem=VMEM_LIMIT):
    return pltpu.CompilerParams(dimension_semantics=sem, vmem_limit_bytes=vmem)


def _full(shape):
    zeros = (0,) * len(shape)
    return pl.BlockSpec(shape, lambda *_: zeros)


def _dot(a, b):
    return jnp.dot(a, b, preferred_element_type=F32)


def _dot_nt(a, b):
    return lax.dot_general(a, b, (((1,), (1,)), ((), ())), preferred_element_type=F32)


def _rms(x, g):
    ms = jnp.mean(x * x, axis=-1, keepdims=True)
    return x * lax.rsqrt(ms + RMS_EPS) * g


def _group_rms(x, g, gmat):
    outs = []
    width = x.shape[1]
    for k in range(0, width, MXU_DIM):
        wd = min(MXU_DIM, width - k)
        xb = x[:, k:k + wd]
        ms = _dot((xb * xb).astype(BF16), gmat[:wd, :wd])
        outs.append(xb * lax.rsqrt(ms + RMS_EPS))
    y = outs[0] if len(outs) == 1 else jnp.concatenate(outs, axis=1)
    return y * g


def _rope(x, cos, sin, half):
    lane = lax.broadcasted_iota(jnp.int32, (1, LANES), 1)
    first = (lane % (2 * half)) < half
    outs = []
    for k in range(x.shape[1] // LANES):
        xb = x[:, k * LANES:(k + 1) * LANES]
        fwd = pltpu.roll(xb, LANES - half, 1)
        bwd = pltpu.roll(xb, half, 1)
        outs.append(xb * cos + jnp.where(first, fwd, bwd) * sin)
    return outs[0] if len(outs) == 1 else jnp.concatenate(outs, axis=1)


def _silu(a):
    return a / (1.0 + jnp.exp(-a))


def _sub(c):
    return slice(c * SUB_ROWS, (c + 1) * SUB_ROWS)


def _adaln_block(layer, c_ref, w_ref, b_ref):
    a = _silu(c_ref[...]).astype(BF16)
    return _dot(a, w_ref[...].astype(BF16)) + b_ref[layer:layer + 1, :]


def _adaln_kernel(layer, c_ref, w_ref, b_ref, o_ref):
    o_ref[...] = _adaln_block(layer, c_ref, w_ref, b_ref)


def _adaln(layer, cvec, w_mod, b_mod):
    rows = cvec.shape[0]
    nc = 4
    wc = 6 * D_MODEL // nc
    return pl.pallas_call(
        functools.partial(_adaln_kernel, layer),
        grid=(nc,),
        in_specs=[_full((rows, D_MODEL)),
                  pl.BlockSpec((None, D_MODEL, wc), lambda n: (layer, 0, n)),
                  pl.BlockSpec((DEPTH, wc), lambda n: (0, n))],
        out_specs=pl.BlockSpec((rows, wc), lambda n: (0, n)),
        out_shape=jax.ShapeDtypeStruct((rows, 6 * D_MODEL), F32),
        compiler_params=_params("parallel"),
        name="adaln",
    )(cvec, w_mod, b_mod)


class _Rows(NamedTuple):
    lat_b: int
    lat_n: int
    ctx_b: int
    ctx_n: int

    @property
    def lat_rows(self):
        return self.lat_b * self.lat_n

    @property
    def ctx_rows(self):
        return self.ctx_b * self.ctx_n

    @property
    def lat_tiles(self):
        return self.lat_rows // ROW_TILE

    @property
    def tiles(self):
        return (self.lat_rows + self.ctx_rows) // ROW_TILE

    @property
    def tiles_per_lat_req(self):
        return self.lat_n // ROW_TILE


class _TileSpecs:
    def __init__(self, rows):
        lat_tiles = rows.lat_tiles
        per_req = rows.tiles_per_lat_req
        self.lat_tiles = lat_tiles
        self.mod = pl.BlockSpec((1, 6, D_MODEL), lambda i: (jnp.where(i < lat_tiles, 1 + i // per_req, 0), 0, 0))
        self.table = pl.BlockSpec((ROW_TILE, LANES), lambda i: (jnp.where(i < lat_tiles, i % per_req, per_req), 0))

    def rows(self, width):
        return pl.BlockSpec((ROW_TILE, width), lambda i: (i, 0))

    def lat_rows(self, width):
        last = self.lat_tiles - 1
        return pl.BlockSpec((ROW_TILE, width), lambda i: (jnp.minimum(i, last), 0))

    def ctx_rows(self, width):
        first = self.lat_tiles
        return pl.BlockSpec((ROW_TILE, width), lambda i: (jnp.maximum(i - first, 0), 0))

    def chunks(self, feats):
        return pl.BlockSpec((N_SUB, feats, SUB_ROWS), lambda i: (i, 0, 0))

    def state(self, layer_j, feats, cols):
        first = self.lat_tiles
        return pl.BlockSpec((N_SUB, None, feats, cols), lambda i: (jnp.maximum(i - first, 0), layer_j, 0, 0))


def _pick_rows(lat_tiles, lat_ref, ctx_ref, r):
    return jnp.where(pl.program_id(0) < lat_tiles, lat_ref[r, :], ctx_ref[r, :])


def _modulated(x, mod, g, shift_row, scale_row):
    return _rms(x, g) * (1.0 + mod[scale_row:scale_row + 1]) + mod[shift_row:shift_row + 1]


def _in_even_kernel(lat_tiles, n_aliased, *refs):
    (xl_ref, xc_ref, mod_ref, g1_ref, wt_ref, gqa_ref, gkva_ref, wuq_ref, wukv_ref,
     gqn_ref, gqr_ref, gkn_ref, gkr_ref, gdq_ref, gdk_ref, g64_ref, g32_ref,
     c64_ref, s64_ref, c32_ref, s32_ref) = refs[:21]
    refs = refs[21 + n_aliased:]
    qn_ref, qr_ref, kn_ref, vm_ref, krt_ref, mq_ref, mk_ref, mv_ref = refs[:8]
    ckv_s_ref, krt_s_ref, mk_s_ref, mv_s_ref = refs[8:]
    g64 = g64_ref[...]
    g32 = g32_ref[...]
    mod = mod_ref[0]
    mla_scale = (MLA_NOPE + MLA_ROPE) ** -0.5 * LOG2E
    diff_scale = DIFF_HD ** -0.5 * LOG2E

    for c in range(N_SUB):
        r = _sub(c)
        h = _modulated(_pick_rows(lat_tiles, xl_ref, xc_ref, r), mod, g1_ref[...], 0, 1)
        proj = _dot_nt(h.astype(BF16), wt_ref[...])
        cq = _rms(proj[:, 0:256], gqa_ref[...])
        q = _dot(cq.astype(BF16), wuq_ref[...])
        qn = _group_rms(q[:, 0:512], gqn_ref[...], g64)
        qr = _group_rms(q[:, 512:768], gqr_ref[...], g32)
        ckv = _rms(proj[:, 256:384], gkva_ref[...])
        kv = _dot(ckv.astype(BF16), wukv_ref[...])
        kn = _group_rms(kv[:, 0:512], gkn_ref[...], g64)
        mq = _group_rms(proj[:, 384:896], gdq_ref[...], g64)
        mk = _group_rms(proj[:, 896:1408], gdk_ref[...], g64)
        mv = proj[:, 1408:1920]
        kr = _group_rms(proj[:, 1920:2048], gkr_ref[...], g32)
        ckv_s_ref[c] = ckv
        krt_s_ref[c] = kr.T[0:MLA_ROPE]
        for hd in range(DIFF_HEADS):
            mk_s_ref[c, pl.ds(hd, SUB_ROWS, stride=DIFF_HEADS), :] = mk[:, hd * LANES:(hd + 1) * LANES]
            mv_s_ref[c, pl.ds(hd, SUB_ROWS, stride=DIFF_HEADS), :] = mv[:, hd * LANES:(hd + 1) * LANES]
        c64, s64, c32, s32 = c64_ref[r, :], s64_ref[r, :], c32_ref[r, :], s32_ref[r, :]
        qr = _rope(qr, c32, s32, MLA_ROPE // 4)
        kr = _rope(kr, c32, s32, MLA_ROPE // 4)
        mq = _rope(mq, c64, s64, DIFF_HD // 4)
        mk = _rope(mk, c64, s64, DIFF_HD // 4)
        qn_ref[r, :] = (qn * mla_scale).astype(BF16)
        qr_ref[r, :] = (qr * mla_scale).astype(BF16)
        kn_ref[r, :] = kn.astype(BF16)
        vm_ref[r, :] = kv[:, 512:1024].astype(BF16)
        krt_ref[r, :] = kr.astype(BF16)
        mq_ref[r, :] = (mq * diff_scale).astype(BF16)
        mk_ref[r, :] = mk.astype(BF16)
        mv_ref[r, :] = mv.astype(BF16)


def _in_odd_kernel(lat_tiles, n_aliased, *refs):
    (xl_ref, xc_ref, mod_ref, g1_ref, w_ref, gqc_ref, gkc_ref, gqd_ref, gkd_ref, g64_ref,
     c64_ref, s64_ref) = refs[:12]
    refs = refs[12 + n_aliased:]
    qc_ref, qd_ref = refs[:2]
    chunk_refs = refs[2:6]
    state_refs = refs[6:10]
    g64 = g64_ref[...]
    mod = mod_ref[0]
    w = w_ref[...].astype(BF16)
    scale = HEAD_DIM ** -0.5 * LOG2E

    for c in range(N_SUB):
        r = _sub(c)
        h = _modulated(_pick_rows(lat_tiles, xl_ref, xc_ref, r), mod, g1_ref[...], 0, 1)
        proj = _dot(h.astype(BF16), w)
        qc = _group_rms(proj[:, 0:512], gqc_ref[...], g64)
        kc = _group_rms(proj[:, 512:640], gkc_ref[...], g64)
        qd = _group_rms(proj[:, 768:1280], gqd_ref[...], g64)
        kd = _group_rms(proj[:, 1280:1408], gkd_ref[...], g64)
        c64, s64 = c64_ref[r, :], s64_ref[r, :]
        qc = _rope(qc, c64, s64, HEAD_DIM // 4)
        kc = _rope(kc, c64, s64, HEAD_DIM // 4)
        qd = _rope(qd, c64, s64, HEAD_DIM // 4)
        kd = _rope(kd, c64, s64, HEAD_DIM // 4)
        qc_ref[r, :] = (qc * scale).astype(BF16)
        qd_ref[r, :] = (qd * scale).astype(BF16)
        for k, val in enumerate((kc, proj[:, 640:768], kd, proj[:, 1408:1536])):
            val_t = val.T
            chunk_refs[k][c] = val_t
            state_refs[k][c] = val_t


def _with_states(specs, layer_j, in_specs, args, out_shape, out_specs, states):
    aliases = {}
    for st in states:
        if not isinstance(st, jax.ShapeDtypeStruct):
            aliases[len(args)] = len(out_shape)
            in_specs.append(pl.BlockSpec(memory_space=pl.ANY))
            args.append(st)
        out_shape.append(jax.ShapeDtypeStruct(st.shape, st.dtype))
        out_specs.append(specs.state(layer_j, st.shape[2], st.shape[3]))
    return aliases


def _in_even(rows, layer_j, x_lat, x_ctx, mods, g1, wt, wuq, wukv, gains, gmats, tables, states):
    specs = _TileSpecs(rows)
    n_rows = rows.lat_rows + rows.ctx_rows
    vec_specs = [_full(g.shape) for g in gains]
    in_specs = ([specs.lat_rows(D_MODEL), specs.ctx_rows(D_MODEL), specs.mod, _full(g1.shape), _full(wt.shape)]
                + vec_specs[:2] + [_full(wuq.shape), _full(wukv.shape)] + vec_specs[2:]
                + [_full((MXU_DIM, MXU_DIM))] * 2 + [specs.table] * 4)
    args = ([x_lat, x_ctx, mods, g1, wt, gains[0], gains[1], wuq, wukv] + list(gains[2:]) + list(gmats)
            + list(tables))
    widths = [512, 256, 512, 512, 128, 512, 512, 512]
    out_shape = [jax.ShapeDtypeStruct((n_rows, wd), BF16) for wd in widths]
    out_specs = [specs.rows(wd) for wd in widths]
    aliases = _with_states(specs, layer_j, in_specs, args, out_shape, out_specs, states)
    return pl.pallas_call(
        functools.partial(_in_even_kernel, rows.lat_tiles, len(aliases)),
        grid=(rows.tiles,),
        in_specs=in_specs, out_specs=out_specs, out_shape=out_shape,
        input_output_aliases=aliases,
        compiler_params=_params("arbitrary"),
        name="in_even",
    )(*args)


def _in_odd(rows, layer_j, x_lat, x_ctx, mods, g1, w_all, gains, g64, tables, states):
    specs = _TileSpecs(rows)
    n_rows = rows.lat_rows + rows.ctx_rows
    w_spec = pl.BlockSpec((None,) + w_all.shape[1:], lambda i: (layer_j, 0, 0))
    in_specs = ([specs.lat_rows(D_MODEL), specs.ctx_rows(D_MODEL), specs.mod, _full(g1.shape), w_spec]
                + [_full(g.shape) for g in gains] + [_full((MXU_DIM, MXU_DIM))] + [specs.table] * 2)
    args = [x_lat, x_ctx, mods, g1, w_all] + list(gains) + [g64] + list(tables)
    out_shape = ([jax.ShapeDtypeStruct((n_rows, 512), BF16)] * 2
                 + [jax.ShapeDtypeStruct((n_rows // SUB_ROWS, LANES, SUB_ROWS), F32)] * 4)
    out_specs = [specs.rows(512), specs.rows(512)] + [specs.chunks(LANES)] * 4
    aliases = _with_states(specs, layer_j, in_specs, args, out_shape, out_specs, states)
    return pl.pallas_call(
        functools.partial(_in_odd_kernel, rows.lat_tiles, len(aliases)),
        grid=(rows.tiles,),
        in_specs=in_specs, out_specs=out_specs, out_shape=out_shape,
        input_output_aliases=aliases,
        compiler_params=_params("arbitrary"),
        name="in_odd",
    )(*args)


def _cache_kv_kernel(ckv_ref, wukv_ref, gkn_ref, g64_ref, kn_ref, vm_ref):
    kv = _dot(ckv_ref[...].astype(BF16), wukv_ref[...])
    kn_ref[...] = _group_rms(kv[:, 0:512], gkn_ref[...], g64_ref[...]).astype(BF16)
    vm_ref[...] = kv[:, 512:1024].astype(BF16)


def _cache_kv(ckv, wukv, gkn, g64):
    rows = ckv.shape[0]
    tile = 512
    spec = lambda wd: pl.BlockSpec((tile, wd), lambda i: (i, 0))
    return pl.pallas_call(
        _cache_kv_kernel,
        grid=(rows // tile,),
        in_specs=[spec(MLA_KV_LORA), _full(wukv.shape), _full(gkn.shape), _full((MXU_DIM, MXU_DIM))],
        out_specs=[spec(512), spec(512)],
        out_shape=[jax.ShapeDtypeStruct((rows, 512), BF16)] * 2,
        compiler_params=_params("parallel"),
        name="cache_kv",
    )(ckv, wukv, gkn, g64)


def _softmax_pv(s_parts, v_parts, v_is_feature_major, sink=None):
    m = None
    for s in s_parts:
        mi = jnp.max(s, axis=-1, keepdims=True)
        m = mi if m is None else jnp.maximum(m, mi)
    if sink is not None:
        m = jnp.maximum(m, sink)
    acc = None
    for s, v in zip(s_parts, v_parts):
        p = jnp.exp2(s - m).astype(BF16)
        oi = _dot_nt(p, v) if v_is_feature_major else _dot(p, v)
        acc = oi if acc is None else acc + oi
    denom = acc[:, LANES:]
    if sink is not None:
        denom = denom + jnp.exp2(sink - m)
    return acc[:, :LANES] / denom


def _lane_masks(width):
    lane = lax.broadcasted_iota(jnp.int32, (1, LANES), 1)
    return [jnp.where(lane // width == k, 1.0, 0.0).astype(BF16) for k in range(LANES // width)]


def _attn_even_kernel(latent, lam_init, *refs):
    qn_ref, qr_ref, mq_ref = refs[:3]
    refs = refs[3:]
    n_parts = 2 if latent else 1
    parts = [refs[5 * k:5 * k + 5] for k in range(n_parts)]
    lamv_ref, gsub_ref, o_ref = refs[5 * n_parts:]
    lane = lax.broadcasted_iota(jnp.int32, (1, LANES), 1)
    low = lane < 64
    m64 = _lane_masks(64)
    m32 = _lane_masks(32)
    ones = [jnp.ones((p[0].shape[0], LANES), BF16) for p in parts]

    def diff_head(ref, idx, hd):
        if latent and idx == 0:
            n_keys = ref.shape[0] // DIFF_HEADS
            return ref[pl.ds(hd, n_keys, stride=DIFF_HEADS), :].astype(BF16)
        return ref[:, hd * LANES:(hd + 1) * LANES]

    for i in range(MLA_HEADS // 2):
        cols = slice(i * LANES, (i + 1) * LANES)
        qn_b = qn_ref[:, cols]
        kcat = [jnp.concatenate([p[0][:, cols], p[2][...]], axis=1) for p in parts]
        vms = [jnp.concatenate([p[1][:, cols], one], axis=1) for p, one in zip(parts, ones)]
        outs = []
        for half in range(2):
            head = 2 * i + half
            rb = head // 4
            qr_b = qr_ref[:, rb * LANES:(rb + 1) * LANES]
            lhs = jnp.concatenate([qn_b * m64[half], qr_b * m32[head % 4]], axis=1)
            outs.append(_softmax_pv([_dot_nt(lhs, kc) for kc in kcat], vms, False))
        o_ref[:, cols] = jnp.where(low, outs[0], outs[1]).astype(BF16)

    lv = lamv_ref[...]
    lam = (jnp.exp(jnp.sum(lv[0:1] * lv[1:2], axis=-1, keepdims=True))
           - jnp.exp(jnp.sum(lv[2:3] * lv[3:4], axis=-1, keepdims=True)) + lam_init)
    gsub = gsub_ref[...]
    for hd in range(DIFF_HEADS):
        q_b = mq_ref[:, hd * LANES:(hd + 1) * LANES]
        ks = [diff_head(p[3], idx, hd) for idx, p in enumerate(parts)]
        vs = [jnp.concatenate([diff_head(p[4], idx, hd), one], axis=1)
              for idx, (p, one) in enumerate(zip(parts, ones))]
        a1 = _softmax_pv([_dot_nt(q_b * m64[0], k) for k in ks], vs, False)
        a2 = _softmax_pv([_dot_nt(q_b * m64[1], k) for k in ks], vs, False)
        d = _rms(a1 - lam * a2, gsub) * (1.0 - lam_init)
        o_ref[:, 512 + hd * LANES:512 + (hd + 1) * LANES] = d.astype(BF16)


def _attn_even(latent, rows, layer_j, lam_init, q_arrs, kv_arrs, cache_arrs, lamv, gsub):
    batch, n_tokens, row0 = (rows.lat_b, rows.lat_n, 0) if latent else (rows.ctx_b, rows.ctx_n, rows.lat_rows)
    tq = min(Q_TILE_DENSE, n_tokens)
    nt = n_tokens // tq
    q0 = row0 // tq
    k0 = row0 // n_tokens
    qspec = lambda wd: pl.BlockSpec((tq, wd), lambda b, t: (q0 + b * nt + t, 0))
    kspec = lambda wd: pl.BlockSpec((n_tokens, wd), lambda b, t: (k0 + b, 0))
    kwidths = [512, 512, 128, 512, 512]
    in_specs = [qspec(512), qspec(256), qspec(512)]
    args = list(q_arrs)
    if latent:
        past = cache_arrs[0].shape[0] // batch
        in_specs += [pl.BlockSpec((past, wd), lambda b, t: (b, 0)) for wd in kwidths[:3]]
        in_specs += [pl.BlockSpec((None, None, past * DIFF_HEADS, LANES), lambda b, t: (b, layer_j, 0, 0))] * 2
        args += list(cache_arrs)
    in_specs += [kspec(wd) for wd in kwidths]
    args += list(kv_arrs)
    in_specs += [_full(lamv.shape), _full(gsub.shape)]
    args += [lamv, gsub]
    return pl.pallas_call(
        functools.partial(_attn_even_kernel, latent, lam_init),
        grid=(batch, nt),
        in_specs=in_specs,
        out_specs=pl.BlockSpec((tq, D_MODEL), lambda b, t: (b * nt + t, 0)),
        out_shape=jax.ShapeDtypeStruct((batch * n_tokens, D_MODEL), BF16),
        compiler_params=_params("parallel", "parallel"),
        name="attn_even_lat" if latent else "attn_even_ctx",
    )(*args)


def _attn_odd_kernel(latent, tq, *refs):
    qc_ref, qd_ref = refs[:2]
    refs = refs[2:]
    n_parts = 2 if latent else 1
    parts = [list(refs[4 * k:4 * k + 4]) for k in range(n_parts)]
    sink_ref, o_ref = refs[4 * n_parts:4 * n_parts + 2]
    scratch = refs[4 * n_parts + 2:]
    lane = lax.broadcasted_iota(jnp.int32, (1, LANES), 1)
    low = lane < 64
    m64 = _lane_masks(64)

    band = None
    start = 0
    span = 0
    if latent:
        for ref, scr in zip(parts[-1], scratch):
            for c in range(ref.shape[0]):
                scr[:, _sub(c)] = ref[c]
        parts[-1] = list(scratch)
        n_new = scratch[0].shape[1]
        span = min(n_new, tq + 2 * WINDOW)
        t = pl.program_id(1)
        start = pl.multiple_of(jnp.clip(t * tq - WINDOW, 0, n_new - span), LANES)
        qpos = t * tq + lax.broadcasted_iota(jnp.int32, (tq, 1), 0)
        kpos = start + lax.broadcasted_iota(jnp.int32, (1, span), 1)
        band = jnp.abs(qpos - kpos) <= WINDOW

    for kind in range(2):
        q_ref = qc_ref if kind == 0 else qd_ref
        windowed = latent and kind == 0
        for g in range(2):
            rows = slice(g * HEAD_DIM, (g + 1) * HEAD_DIM)
            ks, vs = [], []
            for idx, p in enumerate(parts):
                k_ref, v_ref = p[2 * kind], p[2 * kind + 1]
                if windowed and idx == n_parts - 1:
                    k = k_ref[rows, pl.ds(start, span)]
                    v = v_ref[rows, pl.ds(start, span)]
                else:
                    k = k_ref[rows, :]
                    v = v_ref[rows, :]
                ks.append(jnp.concatenate([k, k], axis=0).astype(BF16))
                vs.append(jnp.concatenate([v, v, jnp.ones((LANES, v.shape[1]), F32)], axis=0).astype(BF16))
            for i in (2 * g, 2 * g + 1):
                cols = slice(i * LANES, (i + 1) * LANES)
                q_b = q_ref[:, cols]
                outs = []
                for half in range(2):
                    s_parts = [_dot(q_b * m64[half], k) for k in ks]
                    sink = None
                    if kind == 0:
                        sink = sink_ref[2 * i + half] * LOG2E
                        if latent:
                            s_parts[-1] = jnp.where(band, s_parts[-1], NEG_BIG)
                    outs.append(_softmax_pv(s_parts, vs, True, sink))
                o_ref[:, kind * 512 + i * LANES:kind * 512 + (i + 1) * LANES] = (
                    jnp.where(low, outs[0], outs[1]).astype(BF16))


def _attn_odd(latent, rows, layer_j, q_arrs, new_arrs, cache_arrs, sink):
    batch, n_tokens, row0 = (rows.lat_b, rows.lat_n, 0) if latent else (rows.ctx_b, rows.ctx_n, rows.lat_rows)
    tq = min(Q_TILE_BANDED, n_tokens)
    nt = n_tokens // tq
    q0 = row0 // tq
    in_specs = [pl.BlockSpec((tq, 512), lambda b, t: (q0 + b * nt + t, 0))] * 2
    args = list(q_arrs)
    scratch = []
    if latent:
        past = cache_arrs[0].shape[-1]
        in_specs += [pl.BlockSpec((None, None, LANES, past), lambda b, t: (b, layer_j, 0, 0))] * 4
        args += list(cache_arrs)
        chunks = n_tokens // SUB_ROWS
        in_specs += [pl.BlockSpec((chunks, LANES, SUB_ROWS), lambda b, t: (b, 0, 0))] * 4
        scratch = [pltpu.VMEM((LANES, n_tokens), F32)] * 4
    else:
        in_specs += [pl.BlockSpec((None, None, LANES, n_tokens), lambda b, t: (b, layer_j, 0, 0))] * 4
    args += list(new_arrs)
    in_specs += [pl.BlockSpec(memory_space=pltpu.SMEM)]
    args += [sink]
    return pl.pallas_call(
        functools.partial(_attn_odd_kernel, latent, tq),
        grid=(batch, nt),
        in_specs=in_specs,
        out_specs=pl.BlockSpec((tq, D_MODEL), lambda b, t: (b * nt + t, 0)),
        out_shape=jax.ShapeDtypeStruct((batch * n_tokens, D_MODEL), BF16),
        scratch_shapes=scratch,
        compiler_params=_params("parallel", "parallel"),
        name="attn_odd_lat" if latent else "attn_odd_ctx",
    )(*args)


def _post_kernel(lat_tiles, ol_ref, oc_ref, wout_ref, xl_ref, xc_ref, mod_ref, g2_ref, wr2_ref, wrh_ref,
                 x1_ref, h2_ref, afft_ref):
    mod = mod_ref[0]
    wout = wout_ref[...].astype(BF16)
    lane = lax.broadcasted_iota(jnp.int32, (1, LANES), 1)
    for c in range(N_SUB):
        r = _sub(c)
        o = _pick_rows(lat_tiles, ol_ref, oc_ref, r)
        x1 = _pick_rows(lat_tiles, xl_ref, xc_ref, r) + mod[2:3] * _dot(o, wout)
        x1_ref[r, :] = x1
        h2 = _modulated(x1, mod, g2_ref[...], 3, 4)
        h_hi = h2.astype(BF16)
        h_lo = (h2 - h_hi.astype(F32)).astype(BF16)
        h2_ref[r, :] = h_hi
        both = _dot(h_hi, wr2_ref[...])
        logits = both[:, :LANES] + both[:, LANES:] + _dot(h_lo, wrh_ref[...])
        logits = jnp.where(lane < N_EXPERTS, logits, NEG_BIG)
        e = jnp.exp(logits - jnp.max(logits, axis=-1, keepdims=True))
        aff = e / jnp.sum(e, axis=-1, keepdims=True)
        afft_ref[c] = aff.T[0:N_EXPERTS]


def _post(rows, layer_j, o_lat, o_ctx, wout_all, x_lat, x_ctx, mods, g2, wr2, wrh):
    specs = _TileSpecs(rows)
    n_rows = rows.lat_rows + rows.ctx_rows
    w_spec = pl.BlockSpec((None,) + wout_all.shape[1:], lambda i: (layer_j, 0, 0))
    return pl.pallas_call(
        functools.partial(_post_kernel, rows.lat_tiles),
        grid=(rows.tiles,),
        in_specs=[specs.lat_rows(D_MODEL), specs.ctx_rows(D_MODEL), w_spec,
                  specs.lat_rows(D_MODEL), specs.ctx_rows(D_MODEL), specs.mod, _full(g2.shape),
                  _full(wr2.shape), _full(wrh.shape)],
        out_specs=[specs.rows(D_MODEL), specs.rows(D_MODEL), specs.chunks(N_EXPERTS)],
        out_shape=[jax.ShapeDtypeStruct((n_rows, D_MODEL), F32),
                   jax.ShapeDtypeStruct((n_rows, D_MODEL), BF16),
                   jax.ShapeDtypeStruct((n_rows // SUB_ROWS, N_EXPERTS, SUB_ROWS), F32)],
        compiler_params=_params("parallel"),
        name="post",
    )(o_lat, o_ctx, wout_all, x_lat, x_ctx, mods, g2, wr2, wrh)


def _route_kernel(caps, *refs):
    n_groups = len(caps)
    aff_refs, out_refs = refs[:n_groups], refs[n_groups:]
    bits = [lax.bitcast_convert_type(ref[...], jnp.int32) for ref in aff_refs]

    def body(_, carry):
        new = []
        for b, cap, (lo, hi) in zip(bits, caps, carry):
            mid = lo + ((hi - lo + 1) >> 1)
            cnt = jnp.sum(jnp.where(b >= mid, 1.0, 0.0), axis=-1, keepdims=True)
            ok = cnt >= cap
            new.append((jnp.where(ok, mid, lo), jnp.where(ok, hi, mid - 1)))
        return tuple(new)

    init = tuple((jnp.zeros((b.shape[0], 1), jnp.int32), jnp.full((b.shape[0], 1), 0x7F800000, jnp.int32))
                 for b in bits)
    found = lax.fori_loop(0, 31, body, init)
    for b, cap, (thr, _), out_ref in zip(bits, caps, found, out_refs):
        n = b.shape[1]
        above = jnp.where(b > thr, 1.0, 0.0)
        equal = jnp.where(b == thr, 1.0, 0.0)
        room = cap - jnp.sum(above, axis=-1, keepdims=True)
        before = jnp.where(lax.broadcasted_iota(jnp.int32, (n, n), 0)
                           < lax.broadcasted_iota(jnp.int32, (n, n), 1), 1.0, 0.0).astype(BF16)
        equal_before = _dot(equal.astype(BF16), before)
        chosen = above + equal * jnp.where(equal_before < room, 1.0, 0.0)
        slot = _dot(chosen.astype(BF16), before)
        out_ref[...] = jnp.where(chosen > 0.5, slot, -1.0)


def _route(caps, aff_list):
    return pl.pallas_call(
        functools.partial(_route_kernel, tuple(caps)),
        grid=(1,),
        in_specs=[_full(a.shape) for a in aff_list],
        out_specs=[_full(a.shape) for a in aff_list],
        out_shape=[jax.ShapeDtypeStruct(a.shape, F32) for a in aff_list],
        compiler_params=_params("arbitrary"),
        name="route",
    )(*aff_list)


def _gather_part(cap, req_per_step, experts_per_step, first, slot_ref, aff_ref, h_ref, xs_ref, gate_ref):
    n = slot_ref.shape[1]
    want = lax.broadcasted_iota(jnp.int32, (cap, 1), 0).astype(F32)
    for rq in range(req_per_step):
        out_rows = slice(rq * cap, (rq + 1) * cap)
        rows = []
        for k in range(experts_per_step):
            row = rq * N_EXPERTS + first + k
            hit = slot_ref[pl.ds(row, 1), :] == want
            rows.append(jnp.where(hit, 1.0, 0.0).astype(BF16))
            gate_ref[k, out_rows, :] = jnp.sum(jnp.where(hit, aff_ref[pl.ds(row, 1), :], 0.0),
                                               axis=-1, keepdims=True)
        picked = _dot(jnp.concatenate(rows, axis=0), h_ref[rq * n:(rq + 1) * n, :]).astype(BF16)
        for k in range(experts_per_step):
            xs_ref[k, out_rows, :] = picked[k * cap:(k + 1) * cap]


def _gather_kernel(lat_steps, groups, caps, ctx_req_per_step,
                   sl_l, af_l, h_l, sl_c, af_c, h_c, xs_l, gt_l, xs_c, gt_c):
    s = pl.program_id(0)
    per_group = N_EXPERTS // groups

    @pl.when(s < lat_steps)
    def _():
        _gather_part(caps[0], 1, per_group, (s % groups) * per_group, sl_l, af_l, h_l, xs_l, gt_l)

    @pl.when(s >= lat_steps)
    def _():
        _gather_part(caps[1], ctx_req_per_step, N_EXPERTS, 0, sl_c, af_c, h_c, xs_c, gt_c)


def _gather(rows, caps, slots, affs, h2):
    groups = rows.lat_n // SCATTER_ROWS
    per_group = N_EXPERTS // groups
    rps = ROW_TILE // rows.ctx_n
    lat_steps = rows.lat_b * groups
    ctx_steps = rows.ctx_b // rps
    last_b = rows.lat_b - 1
    ctx_block0 = rows.lat_rows // (rps * rows.ctx_n)
    lat_req = lambda s: jnp.minimum(s // groups, last_b)
    lat_grp = lambda s: jnp.where(s < lat_steps, s % groups, groups - 1)
    ctx_step = lambda s: jnp.maximum(s - lat_steps, 0)
    cap_l, cap_c = caps
    lat_rows_spec = pl.BlockSpec((N_EXPERTS, rows.lat_n), lambda s: (lat_req(s), 0))
    ctx_rows_spec = pl.BlockSpec((rps * N_EXPERTS, rows.ctx_n), lambda s: (ctx_step(s), 0))
    return pl.pallas_call(
        functools.partial(_gather_kernel, lat_steps, groups, caps, rps),
        grid=(lat_steps + ctx_steps,),
        in_specs=[lat_rows_spec, lat_rows_spec,
                  pl.BlockSpec((rows.lat_n, D_MODEL), lambda s: (lat_req(s), 0)),
                  ctx_rows_spec, ctx_rows_spec,
                  pl.BlockSpec((rps * rows.ctx_n, D_MODEL), lambda s: (ctx_block0 + ctx_step(s), 0))],
        out_specs=[pl.BlockSpec((per_group, cap_l, D_MODEL), lambda s: (lat_grp(s), lat_req(s), 0)),
                   pl.BlockSpec((per_group, cap_l, 1), lambda s: (lat_grp(s), lat_req(s), 0)),
                   pl.BlockSpec((N_EXPERTS, rps * cap_c, D_MODEL), lambda s: (0, ctx_step(s), 0)),
                   pl.BlockSpec((N_EXPERTS, rps * cap_c, 1), lambda s: (0, ctx_step(s), 0))],
        out_shape=[jax.ShapeDtypeStruct((N_EXPERTS, rows.lat_b * cap_l, D_MODEL), BF16),
                   jax.ShapeDtypeStruct((N_EXPERTS, rows.lat_b * cap_l, 1), F32),
                   jax.ShapeDtypeStruct((N_EXPERTS, rows.ctx_b * cap_c, D_MODEL), BF16),
                   jax.ShapeDtypeStruct((N_EXPERTS, rows.ctx_b * cap_c, 1), F32)],
        compiler_params=_params("arbitrary"),
        name="gather",
    )(slots[0], affs[0], h2, slots[1], affs[1], h2)


def _ffn_kernel(n_groups, next_layer, *refs):
    xs_refs = refs[:n_groups]
    gate_refs = refs[n_groups:2 * n_groups]
    w1_ref, w3_ref, w2_ref = refs[2 * n_groups:2 * n_groups + 3]
    refs = refs[2 * n_groups + 3:]
    if next_layer is not None:
        c_ref, wm_ref, bm_ref = refs[:3]
        refs = refs[3:]
        mods_ref = refs[n_groups]
        acc_refs = refs[n_groups + 1:]
    else:
        acc_refs = refs[n_groups:]
    ys_refs = refs[:n_groups]
    f = pl.program_id(1)

    @pl.when(f == 0)
    def _():
        for acc_ref in acc_refs:
            acc_ref[...] = jnp.zeros(acc_ref.shape, F32)

    if next_layer is not None:
        mods_ref[...] = _adaln_block(next_layer, c_ref, wm_ref, bm_ref)

    hidden = [[] for _ in xs_refs]
    for n in range(0, w1_ref.shape[3], MXU_DIM):
        w1 = w1_ref[0, 0, :, n:n + MXU_DIM].astype(BF16)
        w3 = w3_ref[0, 0, :, n:n + MXU_DIM].astype(BF16)
        for parts, xs_ref in zip(hidden, xs_refs):
            xs = xs_ref[0]
            parts.append((_silu(_dot(xs, w1)) * _dot(xs, w3)).astype(BF16))
    hidden = [jnp.concatenate(parts, axis=1) for parts in hidden]
    for n in range(0, w2_ref.shape[3], MXU_DIM):
        w2 = w2_ref[0, 0, :, n:n + MXU_DIM].astype(BF16)
        for hid, acc_ref in zip(hidden, acc_refs):
            acc_ref[:, n:n + MXU_DIM] += _dot(hid, w2)

    @pl.when(f == pl.num_programs(1) - 1)
    def _():
        for gate_ref, ys_ref, acc_ref in zip(gate_refs, ys_refs, acc_refs):
            ys_ref[0] = (acc_ref[...] * gate_ref[0]).astype(BF16)


def _ffn(layer, xs_list, gate_list, w1, w3, w2, mod_inputs=None):
    n_groups = len(xs_list)
    n_chunks = EXPERT_FF // FF_CHUNK
    xs_specs = [pl.BlockSpec((1,) + xs.shape[1:], lambda e, f: (e, 0, 0)) for xs in xs_list]
    gate_specs = [pl.BlockSpec((1,) + g.shape[1:], lambda e, f: (e, 0, 0)) for g in gate_list]
    in_specs = xs_specs + gate_specs + [
        pl.BlockSpec((1, 1, D_MODEL, FF_CHUNK), lambda e, f: (layer, e, 0, f)),
        pl.BlockSpec((1, 1, D_MODEL, FF_CHUNK), lambda e, f: (layer, e, 0, f)),
        pl.BlockSpec((1, 1, FF_CHUNK, D_MODEL), lambda e, f: (layer, e, f, 0))]
    args = [*xs_list, *gate_list, w1, w3, w2]
    out_specs = list(xs_specs)
    out_shape = [jax.ShapeDtypeStruct(xs.shape, BF16) for xs in xs_list]
    next_layer = None
    if mod_inputs is not None:
        next_layer = layer + 1
        cvec, w_mod, b_mod = mod_inputs
        last = 6 * D_MODEL // MXU_DIM - 1
        assert N_EXPERTS * n_chunks > last
        block = lambda e, f: jnp.minimum(e * n_chunks + f, last)
        in_specs += [_full(cvec.shape),
                     pl.BlockSpec((None, D_MODEL, MXU_DIM), lambda e, f: (next_layer, 0, block(e, f))),
                     pl.BlockSpec((DEPTH, MXU_DIM), lambda e, f: (0, block(e, f)))]
        args += [cvec, w_mod, b_mod]
        out_specs.append(pl.BlockSpec((cvec.shape[0], MXU_DIM), lambda e, f: (0, block(e, f))))
        out_shape.append(jax.ShapeDtypeStruct((cvec.shape[0], 6 * D_MODEL), F32))
    return pl.pallas_call(
        functools.partial(_ffn_kernel, n_groups, next_layer),
        grid=(N_EXPERTS, n_chunks),
        in_specs=in_specs, out_specs=out_specs, out_shape=out_shape,
        scratch_shapes=[pltpu.VMEM(xs.shape[1:], F32) for xs in xs_list],
        compiler_params=_params("arbitrary", "arbitrary", vmem=FFN_VMEM_LIMIT),
        name="ffn",
    )(*args)


def _scatter_part(cap, slots, ys_ref, ys_rows, x, gate2):
    lane = lax.broadcasted_iota(jnp.int32, (1, LANES), 1).astype(F32)
    per_block = LANES // cap
    pad = jnp.zeros((LANES - N_EXPERTS, SUB_ROWS), F32)
    slots_t = jnp.concatenate([slots, pad], axis=0).T
    blocks = []
    for k in range(N_EXPERTS // per_block):
        hit = None
        for e in range(k * per_block, (k + 1) * per_block):
            col = slots_t[:, e:e + 1]
            target = jnp.where(col >= 0.0, col + float((e - k * per_block) * cap), -1.0)
            he = jnp.where(target == lane, 1.0, 0.0)
            hit = he if hit is None else hit + he
        blocks.append(hit.astype(BF16))
    onehot = jnp.concatenate(blocks, axis=1)
    ys = jnp.concatenate([ys_ref[e, ys_rows, :] for e in range(N_EXPERTS)], axis=0)
    return x + gate2 * _dot(onehot, ys)


def _combine_kernel(lat_steps, steps_per_req, caps, sl_l, ys_l, sl_c, ys_c, x_ref, mod_ref, ol_ref, oc_ref):
    s = pl.program_id(0)
    gate2 = mod_ref[0][5:6]
    chunks = SCATTER_ROWS // SUB_ROWS

    @pl.when(s < lat_steps)
    def _():
        first = (s % steps_per_req) * SCATTER_ROWS
        for c in range(chunks):
            tokens = pl.multiple_of(first + c * SUB_ROWS, SUB_ROWS)
            ol_ref[_sub(c), :] = _scatter_part(caps[0], sl_l[:, pl.ds(tokens, SUB_ROWS)], ys_l,
                                               slice(0, caps[0]), x_ref[_sub(c), :], gate2)

    @pl.when(s >= lat_steps)
    def _():
        for rq in range(chunks):
            oc_ref[_sub(rq), :] = _scatter_part(caps[1], sl_c[rq * N_EXPERTS:(rq + 1) * N_EXPERTS, :], ys_c,
                                                slice(rq * caps[1], (rq + 1) * caps[1]), x_ref[_sub(rq), :], gate2)


def _combine(rows, caps, slots, ys, x1, mods):
    per_req = rows.lat_n // SCATTER_ROWS
    req_per_step = SCATTER_ROWS // rows.ctx_n
    lat_steps = rows.lat_b * per_req
    ctx_steps = rows.ctx_b // req_per_step
    last_b = rows.lat_b - 1
    lat_req = lambda s: jnp.minimum(s // per_req, last_b)
    ctx_step = lambda s: jnp.maximum(s - lat_steps, 0)
    cap_l, cap_c = caps
    return pl.pallas_call(
        functools.partial(_combine_kernel, lat_steps, per_req, caps),
        grid=(lat_steps + ctx_steps,),
        in_specs=[pl.BlockSpec((N_EXPERTS, rows.lat_n), lambda s: (lat_req(s), 0)),
                  pl.BlockSpec((N_EXPERTS, cap_l, D_MODEL), lambda s: (0, lat_req(s), 0)),
                  pl.BlockSpec((req_per_step * N_EXPERTS, rows.ctx_n), lambda s: (ctx_step(s), 0)),
                  pl.BlockSpec((N_EXPERTS, req_per_step * cap_c, D_MODEL), lambda s: (0, ctx_step(s), 0)),
                  pl.BlockSpec((SCATTER_ROWS, D_MODEL), lambda s: (s, 0)),
                  pl.BlockSpec((1, 6, D_MODEL), lambda s: (jnp.where(s < lat_steps, 1 + s // per_req, 0), 0, 0))],
        out_specs=[pl.BlockSpec((SCATTER_ROWS, D_MODEL), lambda s: (jnp.minimum(s, lat_steps - 1), 0)),
                   pl.BlockSpec((SCATTER_ROWS, D_MODEL), lambda s: (ctx_step(s), 0))],
        out_shape=[jax.ShapeDtypeStruct((rows.lat_rows, D_MODEL), F32),
                   jax.ShapeDtypeStruct((rows.ctx_rows, D_MODEL), F32)],
        compiler_params=_params("arbitrary", vmem=FFN_VMEM_LIMIT),
        name="combine",
    )(slots[0], ys[0], slots[1], ys[1], x1, mods)


def _rope_tables(n_pos, rot_dim):
    t = jnp.arange(n_pos)
    row = t // GRID_W
    col = t % GRID_W
    nf = rot_dim // 4
    inv = ROPE_THETA ** (-jnp.arange(nf, dtype=F32) / nf)
    ang_r = row[:, None] * inv
    ang_c = col[:, None] * inv
    cr, sr, cc, sc = jnp.cos(ang_r), jnp.sin(ang_r), jnp.cos(ang_c), jnp.sin(ang_c)
    cos = jnp.concatenate([cr, cr, cc, cc], axis=1)
    sin = jnp.concatenate([-sr, sr, -sc, sc], axis=1)
    reps = LANES // rot_dim
    cos = jnp.concatenate([jnp.tile(cos, (1, reps)), jnp.ones((ROW_TILE, LANES), F32)], axis=0)
    sin = jnp.concatenate([jnp.tile(sin, (1, reps)), jnp.zeros((ROW_TILE, LANES), F32)], axis=0)
    return cos, sin


def _group_mean_matrix(width):
    idx = jnp.arange(MXU_DIM) // width
    return jnp.where(idx[:, None] == idx[None, :], 1.0 / width, 0.0).astype(BF16)


def _tile_row(g, reps):
    return jnp.tile(g, reps).reshape(1, -1)


def _split_heads(w, n_heads, first):
    k = w.shape[0]
    w3 = w.reshape(k, n_heads, -1)
    return jnp.concatenate([w3[:, :, :first].reshape(k, -1), w3[:, :, first:].reshape(k, -1)], axis=1)


def _feature_major(cache):
    b, l, t, h, d = cache.shape
    return jnp.transpose(cache, (0, 1, 3, 4, 2)).reshape(b, l, h * d, t)


def kernel(x_prompt, x_sample, cache_mla_ckv, cache_mla_krope, cache_diff_k, cache_diff_v,
           cache_win_k, cache_win_v, cache_axial_k, cache_axial_v, c, c_ctx,
           g_norm, w_mod, b_mod, w_in_even, w_out_even, mla_g_qa, mla_g_kva, mla_w_uq, mla_w_ukv,
           mla_g_q, mla_g_k, diff_g_q, diff_g_k, diff_lambda, diff_g_sub,
           w_in_odd, w_out_odd, odd_g_qk, win_sink, moe_w_router, moe_w1, moe_w3, moe_w2):
    rows = _Rows(lat_b=x_sample.shape[0], lat_n=x_sample.shape[1], ctx_b=x_prompt.shape[0], ctx_n=x_prompt.shape[1])
    past = cache_mla_ckv.shape[2]
    n_even, n_odd = w_in_even.shape[0], w_in_odd.shape[0]
    assert rows.ctx_n == SUB_ROWS and rows.lat_n % ROW_TILE == 0 and rows.ctx_b % N_SUB == 0
    caps = (EC_CAPACITY_FACTOR * rows.lat_n // N_EXPERTS, EC_CAPACITY_FACTOR * rows.ctx_n // N_EXPERTS)

    cvec = jnp.zeros((16, D_MODEL), F32).at[0].set(c_ctx).at[1:1 + rows.lat_b].set(c)
    mods_l = _adaln(0, cvec, w_mod, b_mod).reshape(16, 6, D_MODEL)

    g64 = _group_mean_matrix(64)
    g32 = _group_mean_matrix(32)
    tab64 = _rope_tables(rows.lat_n, HEAD_DIM)
    tab32 = _rope_tables(rows.lat_n, MLA_ROPE)
    odd_caches = [_feature_major(a) for a in (cache_win_k, cache_win_v, cache_axial_k, cache_axial_v)]
    diff_caches = [a.reshape(rows.lat_b, n_even, past * DIFF_HEADS, LANES) for a in (cache_diff_k, cache_diff_v)]

    state = lambda *shape: jnp.zeros(shape, F32)
    even_states = [state(rows.ctx_b, n_even, rows.ctx_n, MLA_KV_LORA),
                   state(rows.ctx_b, n_even, MLA_ROPE, rows.ctx_n),
                   state(rows.ctx_b, n_even, rows.ctx_n * DIFF_HEADS, LANES),
                   state(rows.ctx_b, n_even, rows.ctx_n * DIFF_HEADS, LANES)]
    odd_states = [state(rows.ctx_b, n_odd, LANES, rows.ctx_n) for _ in range(4)]

    x_lat = x_sample.reshape(-1, D_MODEL)
    x_ctx = x_prompt.reshape(-1, D_MODEL)
    lat_chunks = rows.lat_rows // SUB_ROWS

    for layer in range(DEPTH):
        j = layer // 2
        g1 = g_norm[layer, 0].reshape(1, -1)
        g2 = g_norm[layer, 1].reshape(1, -1)
        if layer % 2 == 0:
            wt = jnp.swapaxes(w_in_even[j], 0, 1)
            wt = jnp.concatenate([wt[:384], wt[416:]] + [wt[384:416]] * 4, axis=0).astype(BF16)
            wuq = _split_heads(mla_w_uq[j], MLA_HEADS, MLA_NOPE).astype(BF16)
            wukv = _split_heads(mla_w_ukv[j], MLA_HEADS, MLA_NOPE).astype(BF16)
            w_out = w_out_even
            gkn = _tile_row(mla_g_k[j, :MLA_NOPE], MLA_HEADS)
            gains = [mla_g_qa[j].reshape(1, -1), mla_g_kva[j].reshape(1, -1),
                     _tile_row(mla_g_q[j, :MLA_NOPE], MLA_HEADS), _tile_row(mla_g_q[j, MLA_NOPE:], MLA_HEADS),
                     gkn, _tile_row(mla_g_k[j, MLA_NOPE:], 4),
                     _tile_row(diff_g_q[j], 2 * DIFF_HEADS), _tile_row(diff_g_k[j], 2 * DIFF_HEADS)]
            lam_init = 0.8 - 0.6 * math.exp(-0.3 * layer)
            gsub = diff_g_sub[j].reshape(1, -1)
            outs = _in_even(rows, j, x_lat, x_ctx, mods_l, g1, wt, wuq, wukv, gains, (g64, g32),
                            tab64 + tab32, even_states)
            qn, qr, kn, vm, krt, mq, mk, mv = outs[:8]
            even_states = list(outs[8:])
            kn_c, vm_c = _cache_kv(cache_mla_ckv[:, j].reshape(-1, MLA_KV_LORA), wukv, gkn, g64)
            cache_arrs = (kn_c, vm_c,
                          jnp.tile(cache_mla_krope[:, j].reshape(-1, MLA_ROPE), (1, 4)).astype(BF16),
                          diff_caches[0], diff_caches[1])
            o_lat = _attn_even(True, rows, j, lam_init, (qn, qr, mq), (kn, vm, krt, mk, mv), cache_arrs,
                               diff_lambda[j], gsub)
            o_ctx = _attn_even(False, rows, j, lam_init, (qn, qr, mq), (kn, vm, krt, mk, mv), None,
                               diff_lambda[j], gsub)
        else:
            w_out = w_out_odd
            gains = [_tile_row(odd_g_qk[j, 0], 8), _tile_row(odd_g_qk[j, 1], 2),
                     _tile_row(odd_g_qk[j, 2], 8), _tile_row(odd_g_qk[j, 3], 2)]
            outs = _in_odd(rows, j, x_lat, x_ctx, mods_l, g1, w_in_odd, gains, g64, tab64, odd_states)
            odd_states = list(outs[6:])
            o_lat = _attn_odd(True, rows, j, outs[:2], outs[2:6], odd_caches, win_sink[j])
            o_ctx = _attn_odd(False, rows, j, outs[:2], odd_states, None, win_sink[j])

        w_r = jnp.pad(moe_w_router[layer], ((0, 0), (0, LANES - N_EXPERTS)))
        wrh = w_r.astype(BF16)
        wrl = (w_r - wrh.astype(F32)).astype(BF16)
        x1, h2, afft = _post(rows, j, o_lat, o_ctx, w_out, x_lat, x_ctx, mods_l, g2,
                             jnp.concatenate([wrh, wrl], axis=1), wrh)
        aff_lat = (afft[:lat_chunks].reshape(rows.lat_b, rows.lat_n // SUB_ROWS, N_EXPERTS, SUB_ROWS)
                   .transpose(0, 2, 1, 3).reshape(rows.lat_b * N_EXPERTS, rows.lat_n))
        aff_ctx = afft[lat_chunks:].reshape(rows.ctx_b * N_EXPERTS, rows.ctx_n)
        affs = (aff_lat, aff_ctx)
        slots = _route(caps, affs)
        xs_l, gt_l, xs_c, gt_c = _gather(rows, caps, slots, affs, h2)
        more = layer + 1 < DEPTH
        outs = _ffn(layer, [xs_l, xs_c], [gt_l, gt_c], moe_w1, moe_w3, moe_w2,
                    (cvec, w_mod, b_mod) if more else None)
        x_lat, x_ctx = _combine(rows, caps, slots, outs[:2], x1, mods_l)
        if more:
            mods_l = outs[2].reshape(16, 6, D_MODEL)

    def token_major(arr, heads):
        b, l, f, t = arr.shape
        return jnp.transpose(arr.reshape(b, l, heads, f // heads, t), (0, 1, 4, 2, 3))

    diff_shape = (rows.ctx_b, n_even, rows.ctx_n, DIFF_HEADS, LANES)
    return (x_ctx.reshape(x_prompt.shape), x_lat.reshape(x_sample.shape),
            even_states[0],
            jnp.swapaxes(even_states[1], 2, 3),
            even_states[2].reshape(diff_shape),
            even_states[3].reshape(diff_shape),
            token_major(odd_states[0], 2), token_major(odd_states[1], 2),
            token_major(odd_states[2], 2), token_major(odd_states[3], 2))
```

```python
import functools
import math
from typing import NamedTuple

import jax
import jax.numpy as jnp
from jax import lax
from jax.experimental import pallas as pl
from jax.experimental.pallas import tpu as pltpu

F32 = jnp.float32
BF16 = jnp.bfloat16

D_MODEL = 1024
DEPTH = 4
GRID_W = 64
ROPE_THETA = 10000.0
WINDOW = 128
RMS_EPS = 1e-6
MLA_HEADS = 8
MLA_Q_LORA = 256
MLA_KV_LORA = 128
MLA_NOPE = 64
MLA_ROPE = 32
MLA_VD = 64
DIFF_HEADS = 4
DIFF_HD = 64
HEAD_DIM = 64
N_EXPERTS = 16
EC_CAPACITY_FACTOR = 2
EXPERT_FF = 2048

LANES = 128
MXU_DIM = 256
ROW_TILE = 1024
SUB_ROWS = 256
N_SUB = ROW_TILE // SUB_ROWS
SCATTER_ROWS = 1024
Q_TILE_DENSE = 512
Q_TILE_BANDED = 256
FF_CHUNK = 1024
VMEM_LIMIT = 50 * 1024 * 1024
FFN_VMEM_LIMIT = 60 * 1024 * 1024
NEG_BIG = -1e30
LOG2E = 1.4426950408889634


def _params(*sem, vmem=VMEM_LIMIT):
    return pltpu.CompilerParams(dimension_semantics=sem, vmem_limit_bytes=vmem)


def _full(shape):
    zeros = (0,) * len(shape)
    return pl.BlockSpec(shape, lambda *_: zeros)


def _dot(a, b):
    return jnp.dot(a, b, preferred_element_type=F32)


def _dot_nt(a, b):
    return lax.dot_general(a, b, (((1,), (1,)), ((), ())), preferred_element_type=F32)


def _rms(x, g):
    ms = jnp.mean(x * x, axis=-1, keepdims=True)
    return x * lax.rsqrt(ms + RMS_EPS) * g


def _group_rms(x, g, gmat):
    outs = []
    width = x.shape[1]
    for k in range(0, width, MXU_DIM):
        wd = min(MXU_DIM, width - k)
        xb = x[:, k:k + wd]
        ms = _dot((xb * xb).astype(BF16), gmat[:wd, :wd])
        outs.append(xb * lax.rsqrt(ms + RMS_EPS))
    y = outs[0] if len(outs) == 1 else jnp.concatenate(outs, axis=1)
    return y * g


def _rope(x, cos, sin, half):
    lane = lax.broadcasted_iota(jnp.int32, (1, LANES), 1)
    first = (lane % (2 * half)) < half
    outs = []
    for k in range(x.shape[1] // LANES):
        xb = x[:, k * LANES:(k + 1) * LANES]
        fwd = pltpu.roll(xb, LANES - half, 1)
        bwd = pltpu.roll(xb, half, 1)
        outs.append(xb * cos + jnp.where(first, fwd, bwd) * sin)
    return outs[0] if len(outs) == 1 else jnp.concatenate(outs, axis=1)


def _silu(a):
    return a / (1.0 + jnp.exp(-a))


def _sub(c):
    return slice(c * SUB_ROWS, (c + 1) * SUB_ROWS)


def _adaln_block(layer, c_ref, w_ref, b_ref):
    a = _silu(c_ref[...]).astype(BF16)
    return _dot(a, w_ref[...].astype(BF16)) + b_ref[layer:layer + 1, :]


def _adaln_kernel(layer, c_ref, w_ref, b_ref, o_ref):
    o_ref[...] = _adaln_block(layer, c_ref, w_ref, b_ref)


def _adaln(layer, cvec, w_mod, b_mod):
    rows = cvec.shape[0]
    nc = 4
    wc = 6 * D_MODEL // nc
    return pl.pallas_call(
        functools.partial(_adaln_kernel, layer),
        grid=(nc,),
        in_specs=[_full((rows, D_MODEL)),
                  pl.BlockSpec((None, D_MODEL, wc), lambda n: (layer, 0, n)),
                  pl.BlockSpec((DEPTH, wc), lambda n: (0, n))],
        out_specs=pl.BlockSpec((rows, wc), lambda n: (0, n)),
        out_shape=jax.ShapeDtypeStruct((rows, 6 * D_MODEL), F32),
        compiler_params=_params("parallel"),
        name="adaln",
    )(cvec, w_mod, b_mod)


class _Rows(NamedTuple):
    lat_b: int
    lat_n: int
    ctx_b: int
    ctx_n: int

    @property
    def lat_rows(self):
        return self.lat_b * self.lat_n

    @property
    def ctx_rows(self):
        return self.ctx_b * self.ctx_n

    @property
    def lat_tiles(self):
        return self.lat_rows // ROW_TILE

    @property
    def tiles(self):
        return (self.lat_rows + self.ctx_rows) // ROW_TILE

    @property
    def tiles_per_lat_req(self):
        return self.lat_n // ROW_TILE


class _TileSpecs:
    def __init__(self, rows):
        lat_tiles = rows.lat_tiles
        per_req = rows.tiles_per_lat_req
        self.lat_tiles = lat_tiles
        self.mod = pl.BlockSpec((1, 6, D_MODEL), lambda i: (jnp.where(i < lat_tiles, 1 + i // per_req, 0), 0, 0))
        self.table = pl.BlockSpec((ROW_TILE, LANES), lambda i: (jnp.where(i < lat_tiles, i % per_req, per_req), 0))

    def rows(self, width):
        return pl.BlockSpec((ROW_TILE, width), lambda i: (i, 0))

    def lat_rows(self, width):
        last = self.lat_tiles - 1
        return pl.BlockSpec((ROW_TILE, width), lambda i: (jnp.minimum(i, last), 0))

    def ctx_rows(self, width):
        first = self.lat_tiles
        return pl.BlockSpec((ROW_TILE, width), lambda i: (jnp.maximum(i - first, 0), 0))

    def chunks(self, feats):
        return pl.BlockSpec((N_SUB, feats, SUB_ROWS), lambda i: (i, 0, 0))

    def state(self, layer_j, feats, cols):
        first = self.lat_tiles
        return pl.BlockSpec((N_SUB, None, feats, cols), lambda i: (jnp.maximum(i - first, 0), layer_j, 0, 0))


def _pick_rows(lat_tiles, lat_ref, ctx_ref, r):
    return jnp.where(pl.program_id(0) < lat_tiles, lat_ref[r, :], ctx_ref[r, :])


def _modulated(x, mod, g, shift_row, scale_row):
    return _rms(x, g) * (1.0 + mod[scale_row:scale_row + 1]) + mod[shift_row:shift_row + 1]


def _in_even_kernel(lat_tiles, n_aliased, *refs):
    (xl_ref, xc_ref, mod_ref, g1_ref, wt_ref, gqa_ref, gkva_ref, wuq_ref, wukv_ref,
     gqn_ref, gqr_ref, gkn_ref, gkr_ref, gdq_ref, gdk_ref, g64_ref, g32_ref,
     c64_ref, s64_ref, c32_ref, s32_ref) = refs[:21]
    refs = refs[21 + n_aliased:]
    qn_ref, qr_ref, kn_ref, vm_ref, krt_ref, mq_ref, mk_ref, mv_ref = refs[:8]
    ckv_s_ref, krt_s_ref, mk_s_ref, mv_s_ref = refs[8:]
    g64 = g64_ref[...]
    g32 = g32_ref[...]
    mod = mod_ref[0]
    mla_scale = (MLA_NOPE + MLA_ROPE) ** -0.5 * LOG2E
    diff_scale = DIFF_HD ** -0.5 * LOG2E

    for c in range(N_SUB):
        r = _sub(c)
        h = _modulated(_pick_rows(lat_tiles, xl_ref, xc_ref, r), mod, g1_ref[...], 0, 1)
        proj = _dot_nt(h.astype(BF16), wt_ref[...])
        cq = _rms(proj[:, 0:256], gqa_ref[...])
        q = _dot(cq.astype(BF16), wuq_ref[...])
        qn = _group_rms(q[:, 0:512], gqn_ref[...], g64)
        qr = _group_rms(q[:, 512:768], gqr_ref[...], g32)
        ckv = _rms(proj[:, 256:384], gkva_ref[...])
        kv = _dot(ckv.astype(BF16), wukv_ref[...])
        kn = _group_rms(kv[:, 0:512], gkn_ref[...], g64)
        mq = _group_rms(proj[:, 384:896], gdq_ref[...], g64)
        mk = _group_rms(proj[:, 896:1408], gdk_ref[...], g64)
        mv = proj[:, 1408:1920]
        kr = _group_rms(proj[:, 1920:2048], gkr_ref[...], g32)
        ckv_s_ref[c] = ckv
        krt_s_ref[c] = kr.T[0:MLA_ROPE]
        for hd in range(DIFF_HEADS):
            mk_s_ref[c, pl.ds(hd, SUB_ROWS, stride=DIFF_HEADS), :] = mk[:, hd * LANES:(hd + 1) * LANES]
            mv_s_ref[c, pl.ds(hd, SUB_ROWS, stride=DIFF_HEADS), :] = mv[:, hd * LANES:(hd + 1) * LANES]
        c64, s64, c32, s32 = c64_ref[r, :], s64_ref[r, :], c32_ref[r, :], s32_ref[r, :]
        qr = _rope(qr, c32, s32, MLA_ROPE // 4)
        kr = _rope(kr, c32, s32, MLA_ROPE // 4)
        mq = _rope(mq, c64, s64, DIFF_HD // 4)
        mk = _rope(mk, c64, s64, DIFF_HD // 4)
        qn_ref[r, :] = (qn * mla_scale).astype(BF16)
        qr_ref[r, :] = (qr * mla_scale).astype(BF16)
        kn_ref[r, :] = kn.astype(BF16)
        vm_ref[r, :] = kv[:, 512:1024].astype(BF16)
        krt_ref[r, :] = kr.astype(BF16)
        mq_ref[r, :] = (mq * diff_scale).astype(BF16)
        mk_ref[r, :] = mk.astype(BF16)
        mv_ref[r, :] = mv.astype(BF16)


def _in_odd_kernel(lat_tiles, n_aliased, *refs):
    (xl_ref, xc_ref, mod_ref, g1_ref, w_ref, gqc_ref, gkc_ref, gqd_ref, gkd_ref, g64_ref,
     c64_ref, s64_ref) = refs[:12]
    refs = refs[12 + n_aliased:]
    qc_ref, qd_ref = refs[:2]
    chunk_refs = refs[2:6]
    state_refs = refs[6:10]
    g64 = g64_ref[...]
    mod = mod_ref[0]
    w = w_ref[...].astype(BF16)
    scale = HEAD_DIM ** -0.5 * LOG2E

    for c in range(N_SUB):
        r = _sub(c)
        h = _modulated(_pick_rows(lat_tiles, xl_ref, xc_ref, r), mod, g1_ref[...], 0, 1)
        proj = _dot(h.astype(BF16), w)
        qc = _group_rms(proj[:, 0:512], gqc_ref[...], g64)
        kc = _group_rms(proj[:, 512:640], gkc_ref[...], g64)
        qd = _group_rms(proj[:, 768:1280], gqd_ref[...], g64)
        kd = _group_rms(proj[:, 1280:1408], gkd_ref[...], g64)
        c64, s64 = c64_ref[r, :], s64_ref[r, :]
        qc = _rope(qc, c64, s64, HEAD_DIM // 4)
        kc = _rope(kc, c64, s64, HEAD_DIM // 4)
        qd = _rope(qd, c64, s64, HEAD_DIM // 4)
        kd = _rope(kd, c64, s64, HEAD_DIM // 4)
        qc_ref[r, :] = (qc * scale).astype(BF16)
        qd_ref[r, :] = (qd * scale).astype(BF16)
        for k, val in enumerate((kc, proj[:, 640:768], kd, proj[:, 1408:1536])):
            val_t = val.T
            chunk_refs[k][c] = val_t
            state_refs[k][c] = val_t


def _with_states(specs, layer_j, in_specs, args, out_shape, out_specs, states):
    aliases = {}
    for st in states:
        if not isinstance(st, jax.ShapeDtypeStruct):
            aliases[len(args)] = len(out_shape)
            in_specs.append(pl.BlockSpec(memory_space=pl.ANY))
            args.append(st)
        out_shape.append(jax.ShapeDtypeStruct(st.shape, st.dtype))
        out_specs.append(specs.state(layer_j, st.shape[2], st.shape[3]))
    return aliases


def _zero_states_kernel(*out_refs):
    for ref in out_refs:
        ref[...] = jnp.zeros(ref.shape, F32)


def _zero_states(shapes):
    return pl.pallas_call(
        _zero_states_kernel,
        grid=(shapes[0][0],),
        out_specs=[pl.BlockSpec((1,) + shape[1:], lambda b: (b, 0, 0, 0)) for shape in shapes],
        out_shape=[jax.ShapeDtypeStruct(shape, F32) for shape in shapes],
        compiler_params=_params("parallel"),
        name="zero_states",
    )()


def _in_even(rows, layer_j, x_lat, x_ctx, mods, g1, wt, wuq, wukv, gains, gmats, tables, states):
    specs = _TileSpecs(rows)
    n_rows = rows.lat_rows + rows.ctx_rows
    vec_specs = [_full(g.shape) for g in gains]
    in_specs = ([specs.lat_rows(D_MODEL), specs.ctx_rows(D_MODEL), specs.mod, _full(g1.shape), _full(wt.shape)]
                + vec_specs[:2] + [_full(wuq.shape), _full(wukv.shape)] + vec_specs[2:]
                + [_full((MXU_DIM, MXU_DIM))] * 2 + [specs.table] * 4)
    args = ([x_lat, x_ctx, mods, g1, wt, gains[0], gains[1], wuq, wukv] + list(gains[2:]) + list(gmats)
            + list(tables))
    widths = [512, 256, 512, 512, 128, 512, 512, 512]
    out_shape = [jax.ShapeDtypeStruct((n_rows, wd), BF16) for wd in widths]
    out_specs = [specs.rows(wd) for wd in widths]
    aliases = _with_states(specs, layer_j, in_specs, args, out_shape, out_specs, states)
    return pl.pallas_call(
        functools.partial(_in_even_kernel, rows.lat_tiles, len(aliases)),
        grid=(rows.tiles,),
        in_specs=in_specs, out_specs=out_specs, out_shape=out_shape,
        input_output_aliases=aliases,
        compiler_params=_params("arbitrary"),
        name="in_even",
    )(*args)


def _in_odd(rows, layer_j, x_lat, x_ctx, mods, g1, w_all, gains, g64, tables, states):
    specs = _TileSpecs(rows)
    n_rows = rows.lat_rows + rows.ctx_rows
    w_spec = pl.BlockSpec((None,) + w_all.shape[1:], lambda i: (layer_j, 0, 0))
    in_specs = ([specs.lat_rows(D_MODEL), specs.ctx_rows(D_MODEL), specs.mod, _full(g1.shape), w_spec]
                + [_full(g.shape) for g in gains] + [_full((MXU_DIM, MXU_DIM))] + [specs.table] * 2)
    args = [x_lat, x_ctx, mods, g1, w_all] + list(gains) + [g64] + list(tables)
    out_shape = ([jax.ShapeDtypeStruct((n_rows, 512), BF16)] * 2
                 + [jax.ShapeDtypeStruct((n_rows // SUB_ROWS, LANES, SUB_ROWS), F32)] * 4)
    out_specs = [specs.rows(512), specs.rows(512)] + [specs.chunks(LANES)] * 4
    aliases = _with_states(specs, layer_j, in_specs, args, out_shape, out_specs, states)
    return pl.pallas_call(
        functools.partial(_in_odd_kernel, rows.lat_tiles, len(aliases)),
        grid=(rows.tiles,),
        in_specs=in_specs, out_specs=out_specs, out_shape=out_shape,
        input_output_aliases=aliases,
        compiler_params=_params("arbitrary"),
        name="in_odd",
    )(*args)


def _cache_kv_kernel(ckv_ref, wukv_ref, gkn_ref, g64_ref, kn_ref, vm_ref):
    kv = _dot(ckv_ref[...].astype(BF16), wukv_ref[...])
    kn_ref[...] = _group_rms(kv[:, 0:512], gkn_ref[...], g64_ref[...]).astype(BF16)
    vm_ref[...] = kv[:, 512:1024].astype(BF16)


def _cache_kv(ckv, wukv, gkn, g64):
    rows = ckv.shape[0]
    tile = 512
    spec = lambda wd: pl.BlockSpec((tile, wd), lambda i: (i, 0))
    return pl.pallas_call(
        _cache_kv_kernel,
        grid=(rows // tile,),
        in_specs=[spec(MLA_KV_LORA), _full(wukv.shape), _full(gkn.shape), _full((MXU_DIM, MXU_DIM))],
        out_specs=[spec(512), spec(512)],
        out_shape=[jax.ShapeDtypeStruct((rows, 512), BF16)] * 2,
        compiler_params=_params("parallel"),
        name="cache_kv",
    )(ckv, wukv, gkn, g64)


def _softmax_pv(s_parts, v_parts, v_is_feature_major, sink=None):
    m = None
    for s in s_parts:
        mi = jnp.max(s, axis=-1, keepdims=True)
        m = mi if m is None else jnp.maximum(m, mi)
    if sink is not None:
        m = jnp.maximum(m, sink)
    acc = None
    for s, v in zip(s_parts, v_parts):
        p = jnp.exp2(s - m).astype(BF16)
        oi = _dot_nt(p, v) if v_is_feature_major else _dot(p, v)
        acc = oi if acc is None else acc + oi
    denom = acc[:, LANES:]
    if sink is not None:
        denom = denom + jnp.exp2(sink - m)
    return acc[:, :LANES] / denom


def _lane_masks(width):
    lane = lax.broadcasted_iota(jnp.int32, (1, LANES), 1)
    return [jnp.where(lane // width == k, 1.0, 0.0).astype(BF16) for k in range(LANES // width)]


def _attn_even_kernel(latent, lam_init, *refs):
    qn_ref, qr_ref, mq_ref = refs[:3]
    refs = refs[3:]
    n_parts = 2 if latent else 1
    parts = [refs[5 * k:5 * k + 5] for k in range(n_parts)]
    lamv_ref, gsub_ref, o_ref = refs[5 * n_parts:]
    lane = lax.broadcasted_iota(jnp.int32, (1, LANES), 1)
    low = lane < 64
    m64 = _lane_masks(64)
    m32 = _lane_masks(32)
    ones = [jnp.ones((p[0].shape[0], LANES), BF16) for p in parts]

    def diff_head(ref, idx, hd):
        if latent and idx == 0:
            n_keys = ref.shape[0] // DIFF_HEADS
            return ref[pl.ds(hd, n_keys, stride=DIFF_HEADS), :].astype(BF16)
        return ref[:, hd * LANES:(hd + 1) * LANES]

    for i in range(MLA_HEADS // 2):
        cols = slice(i * LANES, (i + 1) * LANES)
        qn_b = qn_ref[:, cols]
        kcat = [jnp.concatenate([p[0][:, cols], p[2][...]], axis=1) for p in parts]
        vms = [jnp.concatenate([p[1][:, cols], one], axis=1) for p, one in zip(parts, ones)]
        outs = []
        for half in range(2):
            head = 2 * i + half
            rb = head // 4
            qr_b = qr_ref[:, rb * LANES:(rb + 1) * LANES]
            lhs = jnp.concatenate([qn_b * m64[half], qr_b * m32[head % 4]], axis=1)
            outs.append(_softmax_pv([_dot_nt(lhs, kc) for kc in kcat], vms, False))
        o_ref[:, cols] = jnp.where(low, outs[0], outs[1]).astype(BF16)

    lv = lamv_ref[...]
    lam = (jnp.exp(jnp.sum(lv[0:1] * lv[1:2], axis=-1, keepdims=True))
           - jnp.exp(jnp.sum(lv[2:3] * lv[3:4], axis=-1, keepdims=True)) + lam_init)
    gsub = gsub_ref[...]
    for hd in range(DIFF_HEADS):
        q_b = mq_ref[:, hd * LANES:(hd + 1) * LANES]
        ks = [diff_head(p[3], idx, hd) for idx, p in enumerate(parts)]
        vs = [jnp.concatenate([diff_head(p[4], idx, hd), one], axis=1)
              for idx, (p, one) in enumerate(zip(parts, ones))]
        a1 = _softmax_pv([_dot_nt(q_b * m64[0], k) for k in ks], vs, False)
        a2 = _softmax_pv([_dot_nt(q_b * m64[1], k) for k in ks], vs, False)
        d = _rms(a1 - lam * a2, gsub) * (1.0 - lam_init)
        o_ref[:, 512 + hd * LANES:512 + (hd + 1) * LANES] = d.astype(BF16)


def _attn_even(latent, rows, layer_j, lam_init, q_arrs, kv_arrs, cache_arrs, lamv, gsub):
    batch, n_tokens, row0 = (rows.lat_b, rows.lat_n, 0) if latent else (rows.ctx_b, rows.ctx_n, rows.lat_rows)
    tq = min(Q_TILE_DENSE, n_tokens)
    nt = n_tokens // tq
    q0 = row0 // tq
    k0 = row0 // n_tokens
    qspec = lambda wd: pl.BlockSpec((tq, wd), lambda b, t: (q0 + b * nt + t, 0))
    kspec = lambda wd: pl.BlockSpec((n_tokens, wd), lambda b, t: (k0 + b, 0))
    kwidths = [512, 512, 128, 512, 512]
    in_specs = [qspec(512), qspec(256), qspec(512)]
    args = list(q_arrs)
    if latent:
        past = cache_arrs[0].shape[0] // batch
        in_specs += [pl.BlockSpec((past, wd), lambda b, t: (b, 0)) for wd in kwidths[:3]]
        in_specs += [pl.BlockSpec((None, None, past * DIFF_HEADS, LANES), lambda b, t: (b, layer_j, 0, 0))] * 2
        args += list(cache_arrs)
    in_specs += [kspec(wd) for wd in kwidths]
    args += list(kv_arrs)
    in_specs += [_full(lamv.shape), _full(gsub.shape)]
    args += [lamv, gsub]
    return pl.pallas_call(
        functools.partial(_attn_even_kernel, latent, lam_init),
        grid=(batch, nt),
        in_specs=in_specs,
        out_specs=pl.BlockSpec((tq, D_MODEL), lambda b, t: (b * nt + t, 0)),
        out_shape=jax.ShapeDtypeStruct((batch * n_tokens, D_MODEL), BF16),
        compiler_params=_params("parallel", "parallel"),
        name="attn_even_lat" if latent else "attn_even_ctx",
    )(*args)


def _attn_odd_kernel(latent, tq, *refs):
    qc_ref, qd_ref = refs[:2]
    refs = refs[2:]
    n_parts = 2 if latent else 1
    parts = [list(refs[4 * k:4 * k + 4]) for k in range(n_parts)]
    sink_ref, o_ref = refs[4 * n_parts:4 * n_parts + 2]
    scratch = refs[4 * n_parts + 2:]
    lane = lax.broadcasted_iota(jnp.int32, (1, LANES), 1)
    low = lane < 64
    m64 = _lane_masks(64)

    band = None
    start = 0
    span = 0
    if latent:
        for ref, scr in zip(parts[-1], scratch):
            for c in range(ref.shape[0]):
                scr[:, _sub(c)] = ref[c]
        parts[-1] = list(scratch)
        n_new = scratch[0].shape[1]
        span = min(n_new, tq + 2 * WINDOW)
        t = pl.program_id(1)
        start = pl.multiple_of(jnp.clip(t * tq - WINDOW, 0, n_new - span), LANES)
        qpos = t * tq + lax.broadcasted_iota(jnp.int32, (tq, 1), 0)
        kpos = start + lax.broadcasted_iota(jnp.int32, (1, span), 1)
        band = jnp.abs(qpos - kpos) <= WINDOW

    for kind in range(2):
        q_ref = qc_ref if kind == 0 else qd_ref
        windowed = latent and kind == 0
        for g in range(2):
            rows = slice(g * HEAD_DIM, (g + 1) * HEAD_DIM)
            ks, vs = [], []
            for idx, p in enumerate(parts):
                k_ref, v_ref = p[2 * kind], p[2 * kind + 1]
                if windowed and idx == n_parts - 1:
                    k = k_ref[rows, pl.ds(start, span)]
                    v = v_ref[rows, pl.ds(start, span)]
                else:
                    k = k_ref[rows, :]
                    v = v_ref[rows, :]
                ks.append(jnp.concatenate([k, k], axis=0).astype(BF16))
                vs.append(jnp.concatenate([v, v, jnp.ones((LANES, v.shape[1]), F32)], axis=0).astype(BF16))
            for i in (2 * g, 2 * g + 1):
                cols = slice(i * LANES, (i + 1) * LANES)
                q_b = q_ref[:, cols]
                outs = []
                for half in range(2):
                    s_parts = [_dot(q_b * m64[half], k) for k in ks]
                    sink = None
                    if kind == 0:
                        sink = sink_ref[2 * i + half] * LOG2E
                        if latent:
                            s_parts[-1] = jnp.where(band, s_parts[-1], NEG_BIG)
                    outs.append(_softmax_pv(s_parts, vs, True, sink))
                o_ref[:, kind * 512 + i * LANES:kind * 512 + (i + 1) * LANES] = (
                    jnp.where(low, outs[0], outs[1]).astype(BF16))


def _attn_odd(latent, rows, layer_j, q_arrs, new_arrs, cache_arrs, sink):
    batch, n_tokens, row0 = (rows.lat_b, rows.lat_n, 0) if latent else (rows.ctx_b, rows.ctx_n, rows.lat_rows)
    tq = min(Q_TILE_BANDED, n_tokens)
    nt = n_tokens // tq
    q0 = row0 // tq
    in_specs = [pl.BlockSpec((tq, 512), lambda b, t: (q0 + b * nt + t, 0))] * 2
    args = list(q_arrs)
    scratch = []
    if latent:
        past = cache_arrs[0].shape[-1]
        in_specs += [pl.BlockSpec((None, None, LANES, past), lambda b, t: (b, layer_j, 0, 0))] * 4
        args += list(cache_arrs)
        chunks = n_tokens // SUB_ROWS
        in_specs += [pl.BlockSpec((chunks, LANES, SUB_ROWS), lambda b, t: (b, 0, 0))] * 4
        scratch = [pltpu.VMEM((LANES, n_tokens), F32)] * 4
    else:
        in_specs += [pl.BlockSpec((None, None, LANES, n_tokens), lambda b, t: (b, layer_j, 0, 0))] * 4
    args += list(new_arrs)
    in_specs += [pl.BlockSpec(memory_space=pltpu.SMEM)]
    args += [sink]
    return pl.pallas_call(
        functools.partial(_attn_odd_kernel, latent, tq),
        grid=(batch, nt),
        in_specs=in_specs,
        out_specs=pl.BlockSpec((tq, D_MODEL), lambda b, t: (b * nt + t, 0)),
        out_shape=jax.ShapeDtypeStruct((batch * n_tokens, D_MODEL), BF16),
        scratch_shapes=scratch,
        compiler_params=_params("parallel", "parallel"),
        name="attn_odd_lat" if latent else "attn_odd_ctx",
    )(*args)


def _post_kernel(lat_tiles, ol_ref, oc_ref, wout_ref, xl_ref, xc_ref, mod_ref, g2_ref, wr2_ref, wrh_ref,
                 x1_ref, h2_ref, afft_ref):
    mod = mod_ref[0]
    wout = wout_ref[...].astype(BF16)
    lane = lax.broadcasted_iota(jnp.int32, (1, LANES), 1)
    for c in range(N_SUB):
        r = _sub(c)
        o = _pick_rows(lat_tiles, ol_ref, oc_ref, r)
        x1 = _pick_rows(lat_tiles, xl_ref, xc_ref, r) + mod[2:3] * _dot(o, wout)
        x1_ref[r, :] = x1
        h2 = _modulated(x1, mod, g2_ref[...], 3, 4)
        h_hi = h2.astype(BF16)
        h_lo = (h2 - h_hi.astype(F32)).astype(BF16)
        h2_ref[r, :] = h_hi
        both = _dot(h_hi, wr2_ref[...])
        logits = both[:, :LANES] + both[:, LANES:] + _dot(h_lo, wrh_ref[...])
        logits = jnp.where(lane < N_EXPERTS, logits, NEG_BIG)
        e = jnp.exp(logits - jnp.max(logits, axis=-1, keepdims=True))
        aff = e / jnp.sum(e, axis=-1, keepdims=True)
        afft_ref[c] = aff.T[0:N_EXPERTS]


def _post(rows, layer_j, o_lat, o_ctx, wout_all, x_lat, x_ctx, mods, g2, wr2, wrh):
    specs = _TileSpecs(rows)
    n_rows = rows.lat_rows + rows.ctx_rows
    w_spec = pl.BlockSpec((None,) + wout_all.shape[1:], lambda i: (layer_j, 0, 0))
    return pl.pallas_call(
        functools.partial(_post_kernel, rows.lat_tiles),
        grid=(rows.tiles,),
        in_specs=[specs.lat_rows(D_MODEL), specs.ctx_rows(D_MODEL), w_spec,
                  specs.lat_rows(D_MODEL), specs.ctx_rows(D_MODEL), specs.mod, _full(g2.shape),
                  _full(wr2.shape), _full(wrh.shape)],
        out_specs=[specs.rows(D_MODEL), specs.rows(D_MODEL), specs.chunks(N_EXPERTS)],
        out_shape=[jax.ShapeDtypeStruct((n_rows, D_MODEL), F32),
                   jax.ShapeDtypeStruct((n_rows, D_MODEL), BF16),
                   jax.ShapeDtypeStruct((n_rows // SUB_ROWS, N_EXPERTS, SUB_ROWS), F32)],
        compiler_params=_params("parallel"),
        name="post",
    )(o_lat, o_ctx, wout_all, x_lat, x_ctx, mods, g2, wr2, wrh)


def _route_kernel(caps, *refs):
    n_groups = len(caps)
    aff_refs, out_refs = refs[:n_groups], refs[n_groups:]
    bits = [lax.bitcast_convert_type(ref[...], jnp.int32) for ref in aff_refs]

    def body(_, carry):
        new = []
        for b, cap, (lo, hi) in zip(bits, caps, carry):
            mid = lo + ((hi - lo + 1) >> 1)
            cnt = jnp.sum(jnp.where(b >= mid, 1.0, 0.0), axis=-1, keepdims=True)
            ok = cnt >= cap
            new.append((jnp.where(ok, mid, lo), jnp.where(ok, hi, mid - 1)))
        return tuple(new)

    init = tuple((jnp.zeros((b.shape[0], 1), jnp.int32), jnp.full((b.shape[0], 1), 0x7F800000, jnp.int32))
                 for b in bits)
    found = lax.fori_loop(0, 31, body, init)
    for b, cap, (thr, _), out_ref in zip(bits, caps, found, out_refs):
        n = b.shape[1]
        above = jnp.where(b > thr, 1.0, 0.0)
        equal = jnp.where(b == thr, 1.0, 0.0)
        room = cap - jnp.sum(above, axis=-1, keepdims=True)
        before = jnp.where(lax.broadcasted_iota(jnp.int32, (n, n), 0)
                           < lax.broadcasted_iota(jnp.int32, (n, n), 1), 1.0, 0.0).astype(BF16)
        equal_before = _dot(equal.astype(BF16), before)
        chosen = above + equal * jnp.where(equal_before < room, 1.0, 0.0)
        slot = _dot(chosen.astype(BF16), before)
        out_ref[...] = jnp.where(chosen > 0.5, slot, -1.0)


def _route(caps, aff_list):
    return pl.pallas_call(
        functools.partial(_route_kernel, tuple(caps)),
        grid=(1,),
        in_specs=[_full(a.shape) for a in aff_list],
        out_specs=[_full(a.shape) for a in aff_list],
        out_shape=[jax.ShapeDtypeStruct(a.shape, F32) for a in aff_list],
        compiler_params=_params("arbitrary"),
        name="route",
    )(*aff_list)


def _gather_part(cap, req_per_step, experts_per_step, first, slot_ref, aff_ref, h_ref, xs_ref, gate_ref):
    n = slot_ref.shape[1]
    want = lax.broadcasted_iota(jnp.int32, (cap, 1), 0).astype(F32)
    for rq in range(req_per_step):
        out_rows = slice(rq * cap, (rq + 1) * cap)
        rows = []
        for k in range(experts_per_step):
            row = rq * N_EXPERTS + first + k
            hit = slot_ref[pl.ds(row, 1), :] == want
            rows.append(jnp.where(hit, 1.0, 0.0).astype(BF16))
            gate_ref[k, out_rows, :] = jnp.sum(jnp.where(hit, aff_ref[pl.ds(row, 1), :], 0.0),
                                               axis=-1, keepdims=True)
        picked = _dot(jnp.concatenate(rows, axis=0), h_ref[rq * n:(rq + 1) * n, :]).astype(BF16)
        for k in range(experts_per_step):
            xs_ref[k, out_rows, :] = picked[k * cap:(k + 1) * cap]


def _gather_kernel(lat_steps, groups, caps, ctx_req_per_step,
                   sl_l, af_l, h_l, sl_c, af_c, h_c, xs_l, gt_l, xs_c, gt_c):
    s = pl.program_id(0)
    per_group = N_EXPERTS // groups

    @pl.when(s < lat_steps)
    def _():
        _gather_part(caps[0], 1, per_group, (s % groups) * per_group, sl_l, af_l, h_l, xs_l, gt_l)

    @pl.when(s >= lat_steps)
    def _():
        _gather_part(caps[1], ctx_req_per_step, N_EXPERTS, 0, sl_c, af_c, h_c, xs_c, gt_c)


def _gather(rows, caps, slots, affs, h2):
    groups = rows.lat_n // SCATTER_ROWS
    per_group = N_EXPERTS // groups
    rps = ROW_TILE // rows.ctx_n
    lat_steps = rows.lat_b * groups
    ctx_steps = rows.ctx_b // rps
    last_b = rows.lat_b - 1
    ctx_block0 = rows.lat_rows // (rps * rows.ctx_n)
    lat_req = lambda s: jnp.minimum(s // groups, last_b)
    lat_grp = lambda s: jnp.where(s < lat_steps, s % groups, groups - 1)
    ctx_step = lambda s: jnp.maximum(s - lat_steps, 0)
    cap_l, cap_c = caps
    lat_rows_spec = pl.BlockSpec((N_EXPERTS, rows.lat_n), lambda s: (lat_req(s), 0))
    ctx_rows_spec = pl.BlockSpec((rps * N_EXPERTS, rows.ctx_n), lambda s: (ctx_step(s), 0))
    return pl.pallas_call(
        functools.partial(_gather_kernel, lat_steps, groups, caps, rps),
        grid=(lat_steps + ctx_steps,),
        in_specs=[lat_rows_spec, lat_rows_spec,
                  pl.BlockSpec((rows.lat_n, D_MODEL), lambda s: (lat_req(s), 0)),
                  ctx_rows_spec, ctx_rows_spec,
                  pl.BlockSpec((rps * rows.ctx_n, D_MODEL), lambda s: (ctx_block0 + ctx_step(s), 0))],
        out_specs=[pl.BlockSpec((per_group, cap_l, D_MODEL), lambda s: (lat_grp(s), lat_req(s), 0)),
                   pl.BlockSpec((per_group, cap_l, 1), lambda s: (lat_grp(s), lat_req(s), 0)),
                   pl.BlockSpec((N_EXPERTS, rps * cap_c, D_MODEL), lambda s: (0, ctx_step(s), 0)),
                   pl.BlockSpec((N_EXPERTS, rps * cap_c, 1), lambda s: (0, ctx_step(s), 0))],
        out_shape=[jax.ShapeDtypeStruct((N_EXPERTS, rows.lat_b * cap_l, D_MODEL), BF16),
                   jax.ShapeDtypeStruct((N_EXPERTS, rows.lat_b * cap_l, 1), F32),
                   jax.ShapeDtypeStruct((N_EXPERTS, rows.ctx_b * cap_c, D_MODEL), BF16),
                   jax.ShapeDtypeStruct((N_EXPERTS, rows.ctx_b * cap_c, 1), F32)],
        compiler_params=_params("arbitrary"),
        name="gather",
    )(slots[0], affs[0], h2, slots[1], affs[1], h2)


def _ffn_kernel(n_groups, next_layer, *refs):
    xs_refs = refs[:n_groups]
    gate_refs = refs[n_groups:2 * n_groups]
    w1_ref, w3_ref, w2_ref = refs[2 * n_groups:2 * n_groups + 3]
    refs = refs[2 * n_groups + 3:]
    if next_layer is not None:
        c_ref, wm_ref, bm_ref = refs[:3]
        refs = refs[3:]
        mods_ref = refs[n_groups]
        acc_refs = refs[n_groups + 1:]
    else:
        acc_refs = refs[n_groups:]
    ys_refs = refs[:n_groups]
    f = pl.program_id(1)

    @pl.when(f == 0)
    def _():
        for acc_ref in acc_refs:
            acc_ref[...] = jnp.zeros(acc_ref.shape, F32)

    if next_layer is not None:
        mods_ref[...] = _adaln_block(next_layer, c_ref, wm_ref, bm_ref)

    hidden = [[] for _ in xs_refs]
    for n in range(0, w1_ref.shape[3], MXU_DIM):
        w1 = w1_ref[0, 0, :, n:n + MXU_DIM].astype(BF16)
        w3 = w3_ref[0, 0, :, n:n + MXU_DIM].astype(BF16)
        for parts, xs_ref in zip(hidden, xs_refs):
            xs = xs_ref[0]
            parts.append((_silu(_dot(xs, w1)) * _dot(xs, w3)).astype(BF16))
    hidden = [jnp.concatenate(parts, axis=1) for parts in hidden]
    for n in range(0, w2_ref.shape[3], MXU_DIM):
        w2 = w2_ref[0, 0, :, n:n + MXU_DIM].astype(BF16)
        for hid, acc_ref in zip(hidden, acc_refs):
            acc_ref[:, n:n + MXU_DIM] += _dot(hid, w2)

    @pl.when(f == pl.num_programs(1) - 1)
    def _():
        for gate_ref, ys_ref, acc_ref in zip(gate_refs, ys_refs, acc_refs):
            ys_ref[0] = (acc_ref[...] * gate_ref[0]).astype(BF16)


def _ffn(layer, xs_list, gate_list, w1, w3, w2, mod_inputs=None):
    n_groups = len(xs_list)
    n_chunks = EXPERT_FF // FF_CHUNK
    xs_specs = [pl.BlockSpec((1,) + xs.shape[1:], lambda e, f: (e, 0, 0)) for xs in xs_list]
    gate_specs = [pl.BlockSpec((1,) + g.shape[1:], lambda e, f: (e, 0, 0)) for g in gate_list]
    in_specs = xs_specs + gate_specs + [
        pl.BlockSpec((1, 1, D_MODEL, FF_CHUNK), lambda e, f: (layer, e, 0, f)),
        pl.BlockSpec((1, 1, D_MODEL, FF_CHUNK), lambda e, f: (layer, e, 0, f)),
        pl.BlockSpec((1, 1, FF_CHUNK, D_MODEL), lambda e, f: (layer, e, f, 0))]
    args = [*xs_list, *gate_list, w1, w3, w2]
    out_specs = list(xs_specs)
    out_shape = [jax.ShapeDtypeStruct(xs.shape, BF16) for xs in xs_list]
    next_layer = None
    if mod_inputs is not None:
        next_layer = layer + 1
        cvec, w_mod, b_mod = mod_inputs
        last = 6 * D_MODEL // MXU_DIM - 1
        assert N_EXPERTS * n_chunks > last
        block = lambda e, f: jnp.minimum(e * n_chunks + f, last)
        in_specs += [_full(cvec.shape),
                     pl.BlockSpec((None, D_MODEL, MXU_DIM), lambda e, f: (next_layer, 0, block(e, f))),
                     pl.BlockSpec((DEPTH, MXU_DIM), lambda e, f: (0, block(e, f)))]
        args += [cvec, w_mod, b_mod]
        out_specs.append(pl.BlockSpec((cvec.shape[0], MXU_DIM), lambda e, f: (0, block(e, f))))
        out_shape.append(jax.ShapeDtypeStruct((cvec.shape[0], 6 * D_MODEL), F32))
    return pl.pallas_call(
        functools.partial(_ffn_kernel, n_groups, next_layer),
        grid=(N_EXPERTS, n_chunks),
        in_specs=in_specs, out_specs=out_specs, out_shape=out_shape,
        scratch_shapes=[pltpu.VMEM(xs.shape[1:], F32) for xs in xs_list],
        compiler_params=_params("arbitrary", "arbitrary", vmem=FFN_VMEM_LIMIT),
        name="ffn",
    )(*args)


def _scatter_part(cap, slots, ys_ref, ys_rows, x, gate2):
    lane = lax.broadcasted_iota(jnp.int32, (1, LANES), 1).astype(F32)
    per_block = LANES // cap
    pad = jnp.zeros((LANES - N_EXPERTS, SUB_ROWS), F32)
    slots_t = jnp.concatenate([slots, pad], axis=0).T
    blocks = []
    for k in range(N_EXPERTS // per_block):
        hit = None
        for e in range(k * per_block, (k + 1) * per_block):
            col = slots_t[:, e:e + 1]
            target = jnp.where(col >= 0.0, col + float((e - k * per_block) * cap), -1.0)
            he = jnp.where(target == lane, 1.0, 0.0)
            hit = he if hit is None else hit + he
        blocks.append(hit.astype(BF16))
    onehot = jnp.concatenate(blocks, axis=1)
    ys = jnp.concatenate([ys_ref[e, ys_rows, :] for e in range(N_EXPERTS)], axis=0)
    return x + gate2 * _dot(onehot, ys)


def _combine_kernel(lat_steps, steps_per_req, caps, sl_l, ys_l, sl_c, ys_c, x_ref, mod_ref, ol_ref, oc_ref):
    s = pl.program_id(0)
    gate2 = mod_ref[0][5:6]
    chunks = SCATTER_ROWS // SUB_ROWS

    @pl.when(s < lat_steps)
    def _():
        first = (s % steps_per_req) * SCATTER_ROWS
        for c in range(chunks):
            tokens = pl.multiple_of(first + c * SUB_ROWS, SUB_ROWS)
            ol_ref[_sub(c), :] = _scatter_part(caps[0], sl_l[:, pl.ds(tokens, SUB_ROWS)], ys_l,
                                               slice(0, caps[0]), x_ref[_sub(c), :], gate2)

    @pl.when(s >= lat_steps)
    def _():
        for rq in range(chunks):
            oc_ref[_sub(rq), :] = _scatter_part(caps[1], sl_c[rq * N_EXPERTS:(rq + 1) * N_EXPERTS, :], ys_c,
                                                slice(rq * caps[1], (rq + 1) * caps[1]), x_ref[_sub(rq), :], gate2)


def _combine(rows, caps, slots, ys, x1, mods):
    per_req = rows.lat_n // SCATTER_ROWS
    req_per_step = SCATTER_ROWS // rows.ctx_n
    lat_steps = rows.lat_b * per_req
    ctx_steps = rows.ctx_b // req_per_step
    last_b = rows.lat_b - 1
    lat_req = lambda s: jnp.minimum(s // per_req, last_b)
    ctx_step = lambda s: jnp.maximum(s - lat_steps, 0)
    cap_l, cap_c = caps
    return pl.pallas_call(
        functools.partial(_combine_kernel, lat_steps, per_req, caps),
        grid=(lat_steps + ctx_steps,),
        in_specs=[pl.BlockSpec((N_EXPERTS, rows.lat_n), lambda s: (lat_req(s), 0)),
                  pl.BlockSpec((N_EXPERTS, cap_l, D_MODEL), lambda s: (0, lat_req(s), 0)),
                  pl.BlockSpec((req_per_step * N_EXPERTS, rows.ctx_n), lambda s: (ctx_step(s), 0)),
                  pl.BlockSpec((N_EXPERTS, req_per_step * cap_c, D_MODEL), lambda s: (0, ctx_step(s), 0)),
                  pl.BlockSpec((SCATTER_ROWS, D_MODEL), lambda s: (s, 0)),
                  pl.BlockSpec((1, 6, D_MODEL), lambda s: (jnp.where(s < lat_steps, 1 + s // per_req, 0), 0, 0))],
        out_specs=[pl.BlockSpec((SCATTER_ROWS, D_MODEL), lambda s: (jnp.minimum(s, lat_steps - 1), 0)),
                   pl.BlockSpec((SCATTER_ROWS, D_MODEL), lambda s: (ctx_step(s), 0))],
        out_shape=[jax.ShapeDtypeStruct((rows.lat_rows, D_MODEL), F32),
                   jax.ShapeDtypeStruct((rows.ctx_rows, D_MODEL), F32)],
        compiler_params=_params("arbitrary", vmem=FFN_VMEM_LIMIT),
        name="combine",
    )(slots[0], ys[0], slots[1], ys[1], x1, mods)


def _rope_tables(n_pos, rot_dim):
    t = jnp.arange(n_pos)
    row = t // GRID_W
    col = t % GRID_W
    nf = rot_dim // 4
    inv = ROPE_THETA ** (-jnp.arange(nf, dtype=F32) / nf)
    ang_r = row[:, None] * inv
    ang_c = col[:, None] * inv
    cr, sr, cc, sc = jnp.cos(ang_r), jnp.sin(ang_r), jnp.cos(ang_c), jnp.sin(ang_c)
    cos = jnp.concatenate([cr, cr, cc, cc], axis=1)
    sin = jnp.concatenate([-sr, sr, -sc, sc], axis=1)
    reps = LANES // rot_dim
    cos = jnp.concatenate([jnp.tile(cos, (1, reps)), jnp.ones((ROW_TILE, LANES), F32)], axis=0)
    sin = jnp.concatenate([jnp.tile(sin, (1, reps)), jnp.zeros((ROW_TILE, LANES), F32)], axis=0)
    return cos, sin


def _group_mean_matrix(width):
    idx = jnp.arange(MXU_DIM) // width
    return jnp.where(idx[:, None] == idx[None, :], 1.0 / width, 0.0).astype(BF16)


def _tile_row(g, reps):
    return jnp.tile(g, reps).reshape(1, -1)


def _split_heads(w, n_heads, first):
    k = w.shape[0]
    w3 = w.reshape(k, n_heads, -1)
    return jnp.concatenate([w3[:, :, :first].reshape(k, -1), w3[:, :, first:].reshape(k, -1)], axis=1)


def _feature_major(cache):
    b, l, t, h, d = cache.shape
    return jnp.transpose(cache, (0, 1, 3, 4, 2)).reshape(b, l, h * d, t)


def kernel(x_prompt, x_sample, cache_mla_ckv, cache_mla_krope, cache_diff_k, cache_diff_v,
           cache_win_k, cache_win_v, cache_axial_k, cache_axial_v, c, c_ctx,
           g_norm, w_mod, b_mod, w_in_even, w_out_even, mla_g_qa, mla_g_kva, mla_w_uq, mla_w_ukv,
           mla_g_q, mla_g_k, diff_g_q, diff_g_k, diff_lambda, diff_g_sub,
           w_in_odd, w_out_odd, odd_g_qk, win_sink, moe_w_router, moe_w1, moe_w3, moe_w2):
    rows = _Rows(lat_b=x_sample.shape[0], lat_n=x_sample.shape[1], ctx_b=x_prompt.shape[0], ctx_n=x_prompt.shape[1])
    past = cache_mla_ckv.shape[2]
    n_even, n_odd = w_in_even.shape[0], w_in_odd.shape[0]
    assert rows.ctx_n == SUB_ROWS and rows.lat_n % ROW_TILE == 0 and rows.ctx_b % N_SUB == 0
    caps = (EC_CAPACITY_FACTOR * rows.lat_n // N_EXPERTS, EC_CAPACITY_FACTOR * rows.ctx_n // N_EXPERTS)

    cvec = jnp.zeros((16, D_MODEL), F32).at[0].set(c_ctx).at[1:1 + rows.lat_b].set(c)
    mods_l = _adaln(0, cvec, w_mod, b_mod).reshape(16, 6, D_MODEL)

    g64 = _group_mean_matrix(64)
    g32 = _group_mean_matrix(32)
    tab64 = _rope_tables(rows.lat_n, HEAD_DIM)
    tab32 = _rope_tables(rows.lat_n, MLA_ROPE)
    odd_caches = [_feature_major(a) for a in (cache_win_k, cache_win_v, cache_axial_k, cache_axial_v)]
    diff_caches = [a.reshape(rows.lat_b, n_even, past * DIFF_HEADS, LANES) for a in (cache_diff_k, cache_diff_v)]

    states = _zero_states([(rows.ctx_b, n_even, rows.ctx_n, MLA_KV_LORA),
                           (rows.ctx_b, n_even, MLA_ROPE, rows.ctx_n),
                           (rows.ctx_b, n_even, rows.ctx_n * DIFF_HEADS, LANES),
                           (rows.ctx_b, n_even, rows.ctx_n * DIFF_HEADS, LANES)]
                          + [(rows.ctx_b, n_odd, LANES, rows.ctx_n)] * 4)
    even_states, odd_states = list(states[:4]), list(states[4:])

    x_lat = x_sample.reshape(-1, D_MODEL)
    x_ctx = x_prompt.reshape(-1, D_MODEL)
    lat_chunks = rows.lat_rows // SUB_ROWS

    for layer in range(DEPTH):
        j = layer // 2
        g1 = g_norm[layer, 0].reshape(1, -1)
        g2 = g_norm[layer, 1].reshape(1, -1)
        if layer % 2 == 0:
            wt = jnp.swapaxes(w_in_even[j], 0, 1)
            wt = jnp.concatenate([wt[:384], wt[416:]] + [wt[384:416]] * 4, axis=0).astype(BF16)
            wuq = _split_heads(mla_w_uq[j], MLA_HEADS, MLA_NOPE).astype(BF16)
            wukv = _split_heads(mla_w_ukv[j], MLA_HEADS, MLA_NOPE).astype(BF16)
            w_out = w_out_even
            gkn = _tile_row(mla_g_k[j, :MLA_NOPE], MLA_HEADS)
            gains = [mla_g_qa[j].reshape(1, -1), mla_g_kva[j].reshape(1, -1),
                     _tile_row(mla_g_q[j, :MLA_NOPE], MLA_HEADS), _tile_row(mla_g_q[j, MLA_NOPE:], MLA_HEADS),
                     gkn, _tile_row(mla_g_k[j, MLA_NOPE:], 4),
                     _tile_row(diff_g_q[j], 2 * DIFF_HEADS), _tile_row(diff_g_k[j], 2 * DIFF_HEADS)]
            lam_init = 0.8 - 0.6 * math.exp(-0.3 * layer)
            gsub = diff_g_sub[j].reshape(1, -1)
            outs = _in_even(rows, j, x_lat, x_ctx, mods_l, g1, wt, wuq, wukv, gains, (g64, g32),
                            tab64 + tab32, even_states)
            qn, qr, kn, vm, krt, mq, mk, mv = outs[:8]
            even_states = list(outs[8:])
            kn_c, vm_c = _cache_kv(cache_mla_ckv[:, j].reshape(-1, MLA_KV_LORA), wukv, gkn, g64)
            cache_arrs = (kn_c, vm_c,
                          jnp.tile(cache_mla_krope[:, j].reshape(-1, MLA_ROPE), (1, 4)).astype(BF16),
                          diff_caches[0], diff_caches[1])
            o_lat = _attn_even(True, rows, j, lam_init, (qn, qr, mq), (kn, vm, krt, mk, mv), cache_arrs,
                               diff_lambda[j], gsub)
            o_ctx = _attn_even(False, rows, j, lam_init, (qn, qr, mq), (kn, vm, krt, mk, mv), None,
                               diff_lambda[j], gsub)
        else:
            w_out = w_out_odd
            gains = [_tile_row(odd_g_qk[j, 0], 8), _tile_row(odd_g_qk[j, 1], 2),
                     _tile_row(odd_g_qk[j, 2], 8), _tile_row(odd_g_qk[j, 3], 2)]
            outs = _in_odd(rows, j, x_lat, x_ctx, mods_l, g1, w_in_odd, gains, g64, tab64, odd_states)
            odd_states = list(outs[6:])
            o_lat = _attn_odd(True, rows, j, outs[:2], outs[2:6], odd_caches, win_sink[j])
            o_ctx = _attn_odd(False, rows, j, outs[:2], odd_states, None, win_sink[j])

        w_r = jnp.pad(moe_w_router[layer], ((0, 0), (0, LANES - N_EXPERTS)))
        wrh = w_r.astype(BF16)
        wrl = (w_r - wrh.astype(F32)).astype(BF16)
        x1, h2, afft = _post(rows, j, o_lat, o_ctx, w_out, x_lat, x_ctx, mods_l, g2,
                             jnp.concatenate([wrh, wrl], axis=1), wrh)
        aff_lat = (afft[:lat_chunks].reshape(rows.lat_b, rows.lat_n // SUB_ROWS, N_EXPERTS, SUB_ROWS)
                   .transpose(0, 2, 1, 3).reshape(rows.lat_b * N_EXPERTS, rows.lat_n))
        aff_ctx = afft[lat_chunks:].reshape(rows.ctx_b * N_EXPERTS, rows.ctx_n)
        affs = (aff_lat, aff_ctx)
        slots = _route(caps, affs)
        xs_l, gt_l, xs_c, gt_c = _gather(rows, caps, slots, affs, h2)
        more = layer + 1 < DEPTH
        outs = _ffn(layer, [xs_l, xs_c], [gt_l, gt_c], moe_w1, moe_w3, moe_w2,
                    (cvec, w_mod, b_mod) if more else None)
        x_lat, x_ctx = _combine(rows, caps, slots, outs[:2], x1, mods_l)
        if more:
            mods_l = outs[2].reshape(16, 6, D_MODEL)

    def token_major(arr, heads):
        b, l, f, t = arr.shape
        return jnp.transpose(arr.reshape(b, l, heads, f // heads, t), (0, 1, 4, 2, 3))

    diff_shape = (rows.ctx_b, n_even, rows.ctx_n, DIFF_HEADS, LANES)
    return (x_ctx.reshape(x_prompt.shape), x_lat.reshape(x_sample.shape),
            even_states[0],
            jnp.swapaxes(even_states[1], 2, 3),
            even_states[2].reshape(diff_shape),
            even_states[3].reshape(diff_shape),
            token_major(odd_states[0], 2), token_major(odd_states[1], 2),
            token_major(odd_states[2], 2), token_major(odd_states[3], 2))
```

```python
import functools
import math
from typing import NamedTuple

import jax
import jax.numpy as jnp
from jax import lax
from jax.experimental import pallas as pl
from jax.experimental.pallas import tpu as pltpu

F32 = jnp.float32
BF16 = jnp.bfloat16

D_MODEL = 1024
DEPTH = 4
GRID_W = 64
ROPE_THETA = 10000.0
WINDOW = 128
RMS_EPS = 1e-6
MLA_HEADS = 8
MLA_Q_LORA = 256
MLA_KV_LORA = 128
MLA_NOPE = 64
MLA_ROPE = 32
MLA_VD = 64
DIFF_HEADS = 4
DIFF_HD = 64
HEAD_DIM = 64
N_EXPERTS = 16
EC_CAPACITY_FACTOR = 2
EXPERT_FF = 2048

LANES = 128
MXU_DIM = 256
ROW_TILE = 1024
SUB_ROWS = 256
N_SUB = ROW_TILE // SUB_ROWS
SCATTER_ROWS = 1024
Q_TILE_DENSE = 512
Q_TILE_BANDED = 256
FF_CHUNK = 1024
VMEM_LIMIT = 50 * 1024 * 1024
FFN_VMEM_LIMIT = 60 * 1024 * 1024
NEG_BIG = -1e30
LOG2E = 1.4426950408889634


def _params(*sem, vmem=VMEM_LIMIT):
    return pltpu.CompilerParams(dimension_semantics=sem, vmem_limit_bytes=vmem)


def _full(shape):
    zeros = (0,) * len(shape)
    return pl.BlockSpec(shape, lambda *_: zeros)


def _dot(a, b):
    return jnp.dot(a, b, preferred_element_type=F32)


def _dot_nt(a, b):
    return lax.dot_general(a, b, (((1,), (1,)), ((), ())), preferred_element_type=F32)


def _rms(x, g):
    ms = jnp.mean(x * x, axis=-1, keepdims=True)
    return x * lax.rsqrt(ms + RMS_EPS) * g


def _group_rms(x, g, gmat):
    outs = []
    width = x.shape[1]
    for k in range(0, width, MXU_DIM):
        wd = min(MXU_DIM, width - k)
        xb = x[:, k:k + wd]
        ms = _dot((xb * xb).astype(BF16), gmat[:wd, :wd])
        outs.append(xb * lax.rsqrt(ms + RMS_EPS))
    y = outs[0] if len(outs) == 1 else jnp.concatenate(outs, axis=1)
    return y * g


def _rope(x, cos, sin, half):
    lane = lax.broadcasted_iota(jnp.int32, (1, LANES), 1)
    first = (lane % (2 * half)) < half
    outs = []
    for k in range(x.shape[1] // LANES):
        xb = x[:, k * LANES:(k + 1) * LANES]
        fwd = pltpu.roll(xb, LANES - half, 1)
        bwd = pltpu.roll(xb, half, 1)
        outs.append(xb * cos + jnp.where(first, fwd, bwd) * sin)
    return outs[0] if len(outs) == 1 else jnp.concatenate(outs, axis=1)


def _silu(a):
    return a / (1.0 + jnp.exp(-a))


def _sub(c):
    return slice(c * SUB_ROWS, (c + 1) * SUB_ROWS)


def _adaln_block(layer, c_ref, w_ref, b_ref):
    a = _silu(c_ref[...]).astype(BF16)
    return _dot(a, w_ref[...].astype(BF16)) + b_ref[layer:layer + 1, :]


def _adaln_kernel(layer, c_ref, w_ref, b_ref, o_ref):
    o_ref[...] = _adaln_block(layer, c_ref, w_ref, b_ref)


def _adaln(layer, cvec, w_mod, b_mod):
    rows = cvec.shape[0]
    nc = 4
    wc = 6 * D_MODEL // nc
    return pl.pallas_call(
        functools.partial(_adaln_kernel, layer),
        grid=(nc,),
        in_specs=[_full((rows, D_MODEL)),
                  pl.BlockSpec((None, D_MODEL, wc), lambda n: (layer, 0, n)),
                  pl.BlockSpec((DEPTH, wc), lambda n: (0, n))],
        out_specs=pl.BlockSpec((rows, wc), lambda n: (0, n)),
        out_shape=jax.ShapeDtypeStruct((rows, 6 * D_MODEL), F32),
        compiler_params=_params("parallel"),
        name="adaln",
    )(cvec, w_mod, b_mod)


class _Rows(NamedTuple):
    lat_b: int
    lat_n: int
    ctx_b: int
    ctx_n: int

    @property
    def lat_rows(self):
        return self.lat_b * self.lat_n

    @property
    def ctx_rows(self):
        return self.ctx_b * self.ctx_n

    @property
    def lat_tiles(self):
        return self.lat_rows // ROW_TILE

    @property
    def tiles(self):
        return (self.lat_rows + self.ctx_rows) // ROW_TILE

    @property
    def tiles_per_lat_req(self):
        return self.lat_n // ROW_TILE


class _TileSpecs:
    def __init__(self, rows):
        lat_tiles = rows.lat_tiles
        per_req = rows.tiles_per_lat_req
        self.lat_tiles = lat_tiles
        self.mod = pl.BlockSpec((1, 6, D_MODEL), lambda i: (jnp.where(i < lat_tiles, 1 + i // per_req, 0), 0, 0))
        self.table = pl.BlockSpec((ROW_TILE, LANES), lambda i: (jnp.where(i < lat_tiles, i % per_req, per_req), 0))

    def rows(self, width):
        return pl.BlockSpec((ROW_TILE, width), lambda i: (i, 0))

    def lat_rows(self, width):
        last = self.lat_tiles - 1
        return pl.BlockSpec((ROW_TILE, width), lambda i: (jnp.minimum(i, last), 0))

    def ctx_rows(self, width):
        first = self.lat_tiles
        return pl.BlockSpec((ROW_TILE, width), lambda i: (jnp.maximum(i - first, 0), 0))

    def chunks(self, feats):
        return pl.BlockSpec((N_SUB, feats, SUB_ROWS), lambda i: (i, 0, 0))

    def state(self, layer_j, feats, cols):
        first = self.lat_tiles
        return pl.BlockSpec((N_SUB, None, feats, cols), lambda i: (jnp.maximum(i - first, 0), layer_j, 0, 0))


def _pick_rows(lat_tiles, lat_ref, ctx_ref, r):
    return jnp.where(pl.program_id(0) < lat_tiles, lat_ref[r, :], ctx_ref[r, :])


def _modulated(x, mod, g, shift_row, scale_row):
    return _rms(x, g) * (1.0 + mod[scale_row:scale_row + 1]) + mod[shift_row:shift_row + 1]


def _in_even_kernel(lat_tiles, n_aliased, *refs):
    (xl_ref, xc_ref, mod_ref, g1_ref, wt_ref, gqa_ref, gkva_ref, wuq_ref, wukv_ref,
     gqn_ref, gqr_ref, gkn_ref, gkr_ref, gdq_ref, gdk_ref, g64_ref, g32_ref,
     c64_ref, s64_ref, c32_ref, s32_ref) = refs[:21]
    refs = refs[21 + n_aliased:]
    qn_ref, qr_ref, kn_ref, vm_ref, krt_ref, mq_ref, mk_ref, mv_ref = refs[:8]
    ckv_s_ref, krt_s_ref, mk_s_ref, mv_s_ref = refs[8:]
    g64 = g64_ref[...]
    g32 = g32_ref[...]
    mod = mod_ref[0]
    mla_scale = (MLA_NOPE + MLA_ROPE) ** -0.5 * LOG2E
    diff_scale = DIFF_HD ** -0.5 * LOG2E

    for c in range(N_SUB):
        r = _sub(c)
        h = _modulated(_pick_rows(lat_tiles, xl_ref, xc_ref, r), mod, g1_ref[...], 0, 1)
        proj = _dot_nt(h.astype(BF16), wt_ref[...])
        cq = _rms(proj[:, 0:256], gqa_ref[...])
        q = _dot(cq.astype(BF16), wuq_ref[...])
        qn = _group_rms(q[:, 0:512], gqn_ref[...], g64)
        qr = _group_rms(q[:, 512:768], gqr_ref[...], g32)
        ckv = _rms(proj[:, 256:384], gkva_ref[...])
        kv = _dot(ckv.astype(BF16), wukv_ref[...])
        kn = _group_rms(kv[:, 0:512], gkn_ref[...], g64)
        mq = _group_rms(proj[:, 384:896], gdq_ref[...], g64)
        mk = _group_rms(proj[:, 896:1408], gdk_ref[...], g64)
        mv = proj[:, 1408:1920]
        kr = _group_rms(proj[:, 1920:2048], gkr_ref[...], g32)
        ckv_s_ref[c] = ckv
        krt_s_ref[c] = kr.T[0:MLA_ROPE]
        for hd in range(DIFF_HEADS):
            mk_s_ref[c, pl.ds(hd, SUB_ROWS, stride=DIFF_HEADS), :] = mk[:, hd * LANES:(hd + 1) * LANES]
            mv_s_ref[c, pl.ds(hd, SUB_ROWS, stride=DIFF_HEADS), :] = mv[:, hd * LANES:(hd + 1) * LANES]
        c64, s64, c32, s32 = c64_ref[r, :], s64_ref[r, :], c32_ref[r, :], s32_ref[r, :]
        qr = _rope(qr, c32, s32, MLA_ROPE // 4)
        kr = _rope(kr, c32, s32, MLA_ROPE // 4)
        mq = _rope(mq, c64, s64, DIFF_HD // 4)
        mk = _rope(mk, c64, s64, DIFF_HD // 4)
        qn_ref[r, :] = (qn * mla_scale).astype(BF16)
        qr_ref[r, :] = (qr * mla_scale).astype(BF16)
        kn_ref[r, :] = kn.astype(BF16)
        vm_ref[r, :] = kv[:, 512:1024].astype(BF16)
        krt_ref[r, :] = kr.astype(BF16)
        mq_ref[r, :] = (mq * diff_scale).astype(BF16)
        mk_ref[r, :] = mk.astype(BF16)
        mv_ref[r, :] = mv.astype(BF16)


def _in_odd_kernel(lat_tiles, n_aliased, *refs):
    (xl_ref, xc_ref, mod_ref, g1_ref, w_ref, gqc_ref, gkc_ref, gqd_ref, gkd_ref, g64_ref,
     c64_ref, s64_ref) = refs[:12]
    refs = refs[12 + n_aliased:]
    qc_ref, qd_ref = refs[:2]
    chunk_refs = refs[2:6]
    state_refs = refs[6:10]
    g64 = g64_ref[...]
    mod = mod_ref[0]
    w = w_ref[...].astype(BF16)
    scale = HEAD_DIM ** -0.5 * LOG2E

    for c in range(N_SUB):
        r = _sub(c)
        h = _modulated(_pick_rows(lat_tiles, xl_ref, xc_ref, r), mod, g1_ref[...], 0, 1)
        proj = _dot(h.astype(BF16), w)
        qc = _group_rms(proj[:, 0:512], gqc_ref[...], g64)
        kc = _group_rms(proj[:, 512:640], gkc_ref[...], g64)
        qd = _group_rms(proj[:, 768:1280], gqd_ref[...], g64)
        kd = _group_rms(proj[:, 1280:1408], gkd_ref[...], g64)
        c64, s64 = c64_ref[r, :], s64_ref[r, :]
        qc = _rope(qc, c64, s64, HEAD_DIM // 4)
        kc = _rope(kc, c64, s64, HEAD_DIM // 4)
        qd = _rope(qd, c64, s64, HEAD_DIM // 4)
        kd = _rope(kd, c64, s64, HEAD_DIM // 4)
        qc_ref[r, :] = (qc * scale).astype(BF16)
        qd_ref[r, :] = (qd * scale).astype(BF16)
        for k, val in enumerate((kc, proj[:, 640:768], kd, proj[:, 1408:1536])):
            val_t = val.T
            chunk_refs[k][c] = val_t
            state_refs[k][c] = val_t


def _with_states(specs, layer_j, in_specs, args, out_shape, out_specs, states):
    aliases = {}
    for st in states:
        if not isinstance(st, jax.ShapeDtypeStruct):
            aliases[len(args)] = len(out_shape)
            in_specs.append(pl.BlockSpec(memory_space=pl.ANY))
            args.append(st)
        out_shape.append(jax.ShapeDtypeStruct(st.shape, st.dtype))
        out_specs.append(specs.state(layer_j, st.shape[2], st.shape[3]))
    return aliases


def _zero_states_kernel(*out_refs):
    for ref in out_refs:
        ref[...] = jnp.zeros(ref.shape, F32)


def _zero_states(shapes):
    return pl.pallas_call(
        _zero_states_kernel,
        grid=(shapes[0][0] // N_SUB,),
        out_specs=[pl.BlockSpec((N_SUB,) + shape[1:], lambda b: (b, 0, 0, 0)) for shape in shapes],
        out_shape=[jax.ShapeDtypeStruct(shape, F32) for shape in shapes],
        compiler_params=_params("parallel"),
        name="zero_states",
    )()


def _in_even(rows, layer_j, x_lat, x_ctx, mods, g1, wt, wuq, wukv, gains, gmats, tables, states):
    specs = _TileSpecs(rows)
    n_rows = rows.lat_rows + rows.ctx_rows
    vec_specs = [_full(g.shape) for g in gains]
    in_specs = ([specs.lat_rows(D_MODEL), specs.ctx_rows(D_MODEL), specs.mod, _full(g1.shape), _full(wt.shape)]
                + vec_specs[:2] + [_full(wuq.shape), _full(wukv.shape)] + vec_specs[2:]
                + [_full((MXU_DIM, MXU_DIM))] * 2 + [specs.table] * 4)
    args = ([x_lat, x_ctx, mods, g1, wt, gains[0], gains[1], wuq, wukv] + list(gains[2:]) + list(gmats)
            + list(tables))
    widths = [512, 256, 512, 512, 128, 512, 512, 512]
    out_shape = [jax.ShapeDtypeStruct((n_rows, wd), BF16) for wd in widths]
    out_specs = [specs.rows(wd) for wd in widths]
    aliases = _with_states(specs, layer_j, in_specs, args, out_shape, out_specs, states)
    return pl.pallas_call(
        functools.partial(_in_even_kernel, rows.lat_tiles, len(aliases)),
        grid=(rows.tiles,),
        in_specs=in_specs, out_specs=out_specs, out_shape=out_shape,
        input_output_aliases=aliases,
        compiler_params=_params("arbitrary"),
        name="in_even",
    )(*args)


def _in_odd(rows, layer_j, x_lat, x_ctx, mods, g1, w_all, gains, g64, tables, states):
    specs = _TileSpecs(rows)
    n_rows = rows.lat_rows + rows.ctx_rows
    w_spec = pl.BlockSpec((None,) + w_all.shape[1:], lambda i: (layer_j, 0, 0))
    in_specs = ([specs.lat_rows(D_MODEL), specs.ctx_rows(D_MODEL), specs.mod, _full(g1.shape), w_spec]
                + [_full(g.shape) for g in gains] + [_full((MXU_DIM, MXU_DIM))] + [specs.table] * 2)
    args = [x_lat, x_ctx, mods, g1, w_all] + list(gains) + [g64] + list(tables)
    out_shape = ([jax.ShapeDtypeStruct((n_rows, 512), BF16)] * 2
                 + [jax.ShapeDtypeStruct((n_rows // SUB_ROWS, LANES, SUB_ROWS), F32)] * 4)
    out_specs = [specs.rows(512), specs.rows(512)] + [specs.chunks(LANES)] * 4
    aliases = _with_states(specs, layer_j, in_specs, args, out_shape, out_specs, states)
    return pl.pallas_call(
        functools.partial(_in_odd_kernel, rows.lat_tiles, len(aliases)),
        grid=(rows.tiles,),
        in_specs=in_specs, out_specs=out_specs, out_shape=out_shape,
        input_output_aliases=aliases,
        compiler_params=_params("arbitrary"),
        name="in_odd",
    )(*args)


def _cache_kv_kernel(ckv_ref, wukv_ref, gkn_ref, g64_ref, kn_ref, vm_ref):
    kv = _dot(ckv_ref[...].astype(BF16), wukv_ref[...])
    kn_ref[...] = _group_rms(kv[:, 0:512], gkn_ref[...], g64_ref[...]).astype(BF16)
    vm_ref[...] = kv[:, 512:1024].astype(BF16)


def _cache_kv(ckv, wukv, gkn, g64):
    rows = ckv.shape[0]
    tile = 512
    spec = lambda wd: pl.BlockSpec((tile, wd), lambda i: (i, 0))
    return pl.pallas_call(
        _cache_kv_kernel,
        grid=(rows // tile,),
        in_specs=[spec(MLA_KV_LORA), _full(wukv.shape), _full(gkn.shape), _full((MXU_DIM, MXU_DIM))],
        out_specs=[spec(512), spec(512)],
        out_shape=[jax.ShapeDtypeStruct((rows, 512), BF16)] * 2,
        compiler_params=_params("parallel"),
        name="cache_kv",
    )(ckv, wukv, gkn, g64)


def _softmax_pv(s_parts, v_parts, v_is_feature_major, sink=None):
    m = None
    for s in s_parts:
        mi = jnp.max(s, axis=-1, keepdims=True)
        m = mi if m is None else jnp.maximum(m, mi)
    if sink is not None:
        m = jnp.maximum(m, sink)
    acc = None
    for s, v in zip(s_parts, v_parts):
        p = jnp.exp2(s - m).astype(BF16)
        oi = _dot_nt(p, v) if v_is_feature_major else _dot(p, v)
        acc = oi if acc is None else acc + oi
    denom = acc[:, LANES:]
    if sink is not None:
        denom = denom + jnp.exp2(sink - m)
    return acc[:, :LANES] / denom


def _lane_masks(width):
    lane = lax.broadcasted_iota(jnp.int32, (1, LANES), 1)
    return [jnp.where(lane // width == k, 1.0, 0.0).astype(BF16) for k in range(LANES // width)]


def _attn_even_kernel(latent, lam_init, *refs):
    qn_ref, qr_ref, mq_ref = refs[:3]
    refs = refs[3:]
    n_parts = 2 if latent else 1
    parts = [refs[5 * k:5 * k + 5] for k in range(n_parts)]
    lamv_ref, gsub_ref, o_ref = refs[5 * n_parts:]
    lane = lax.broadcasted_iota(jnp.int32, (1, LANES), 1)
    low = lane < 64
    m64 = _lane_masks(64)
    m32 = _lane_masks(32)
    ones = [jnp.ones((p[0].shape[0], LANES), BF16) for p in parts]

    def diff_head(ref, idx, hd):
        if latent and idx == 0:
            n_keys = ref.shape[0] // DIFF_HEADS
            return ref[pl.ds(hd, n_keys, stride=DIFF_HEADS), :].astype(BF16)
        return ref[:, hd * LANES:(hd + 1) * LANES]

    for i in range(MLA_HEADS // 2):
        cols = slice(i * LANES, (i + 1) * LANES)
        qn_b = qn_ref[:, cols]
        kcat = [jnp.concatenate([p[0][:, cols], p[2][...]], axis=1) for p in parts]
        vms = [jnp.concatenate([p[1][:, cols], one], axis=1) for p, one in zip(parts, ones)]
        outs = []
        for half in range(2):
            head = 2 * i + half
            rb = head // 4
            qr_b = qr_ref[:, rb * LANES:(rb + 1) * LANES]
            lhs = jnp.concatenate([qn_b * m64[half], qr_b * m32[head % 4]], axis=1)
            outs.append(_softmax_pv([_dot_nt(lhs, kc) for kc in kcat], vms, False))
        o_ref[:, cols] = jnp.where(low, outs[0], outs[1]).astype(BF16)

    lv = lamv_ref[...]
    lam = (jnp.exp(jnp.sum(lv[0:1] * lv[1:2], axis=-1, keepdims=True))
           - jnp.exp(jnp.sum(lv[2:3] * lv[3:4], axis=-1, keepdims=True)) + lam_init)
    gsub = gsub_ref[...]
    for hd in range(DIFF_HEADS):
        q_b = mq_ref[:, hd * LANES:(hd + 1) * LANES]
        ks = [diff_head(p[3], idx, hd) for idx, p in enumerate(parts)]
        vs = [jnp.concatenate([diff_head(p[4], idx, hd), one], axis=1)
              for idx, (p, one) in enumerate(zip(parts, ones))]
        a1 = _softmax_pv([_dot_nt(q_b * m64[0], k) for k in ks], vs, False)
        a2 = _softmax_pv([_dot_nt(q_b * m64[1], k) for k in ks], vs, False)
        d = _rms(a1 - lam * a2, gsub) * (1.0 - lam_init)
        o_ref[:, 512 + hd * LANES:512 + (hd + 1) * LANES] = d.astype(BF16)


def _attn_even(latent, rows, layer_j, lam_init, q_arrs, kv_arrs, cache_arrs, lamv, gsub):
    batch, n_tokens, row0 = (rows.lat_b, rows.lat_n, 0) if latent else (rows.ctx_b, rows.ctx_n, rows.lat_rows)
    tq = min(Q_TILE_DENSE, n_tokens)
    nt = n_tokens // tq
    q0 = row0 // tq
    k0 = row0 // n_tokens
    qspec = lambda wd: pl.BlockSpec((tq, wd), lambda b, t: (q0 + b * nt + t, 0))
    kspec = lambda wd: pl.BlockSpec((n_tokens, wd), lambda b, t: (k0 + b, 0))
    kwidths = [512, 512, 128, 512, 512]
    in_specs = [qspec(512), qspec(256), qspec(512)]
    args = list(q_arrs)
    if latent:
        past = cache_arrs[0].shape[0] // batch
        in_specs += [pl.BlockSpec((past, wd), lambda b, t: (b, 0)) for wd in kwidths[:3]]
        in_specs += [pl.BlockSpec((None, None, past * DIFF_HEADS, LANES), lambda b, t: (b, layer_j, 0, 0))] * 2
        args += list(cache_arrs)
    in_specs += [kspec(wd) for wd in kwidths]
    args += list(kv_arrs)
    in_specs += [_full(lamv.shape), _full(gsub.shape)]
    args += [lamv, gsub]
    return pl.pallas_call(
        functools.partial(_attn_even_kernel, latent, lam_init),
        grid=(batch, nt),
        in_specs=in_specs,
        out_specs=pl.BlockSpec((tq, D_MODEL), lambda b, t: (b * nt + t, 0)),
        out_shape=jax.ShapeDtypeStruct((batch * n_tokens, D_MODEL), BF16),
        compiler_params=_params("parallel", "parallel"),
        name="attn_even_lat" if latent else "attn_even_ctx",
    )(*args)


def _attn_odd_kernel(latent, tq, *refs):
    qc_ref, qd_ref = refs[:2]
    refs = refs[2:]
    n_parts = 2 if latent else 1
    parts = [list(refs[4 * k:4 * k + 4]) for k in range(n_parts)]
    sink_ref, o_ref = refs[4 * n_parts:4 * n_parts + 2]
    scratch = refs[4 * n_parts + 2:]
    lane = lax.broadcasted_iota(jnp.int32, (1, LANES), 1)
    low = lane < 64
    m64 = _lane_masks(64)

    band = None
    start = 0
    span = 0
    if latent:
        for ref, scr in zip(parts[-1], scratch):
            for c in range(ref.shape[0]):
                scr[:, _sub(c)] = ref[c]
        parts[-1] = list(scratch)
        n_new = scratch[0].shape[1]
        span = min(n_new, tq + 2 * WINDOW)
        t = pl.program_id(1)
        start = pl.multiple_of(jnp.clip(t * tq - WINDOW, 0, n_new - span), LANES)
        qpos = t * tq + lax.broadcasted_iota(jnp.int32, (tq, 1), 0)
        kpos = start + lax.broadcasted_iota(jnp.int32, (1, span), 1)
        band = jnp.abs(qpos - kpos) <= WINDOW

    for kind in range(2):
        q_ref = qc_ref if kind == 0 else qd_ref
        windowed = latent and kind == 0
        for g in range(2):
            rows = slice(g * HEAD_DIM, (g + 1) * HEAD_DIM)
            ks, vs = [], []
            for idx, p in enumerate(parts):
                k_ref, v_ref = p[2 * kind], p[2 * kind + 1]
                if windowed and idx == n_parts - 1:
                    k = k_ref[rows, pl.ds(start, span)]
                    v = v_ref[rows, pl.ds(start, span)]
                else:
                    k = k_ref[rows, :]
                    v = v_ref[rows, :]
                ks.append(jnp.concatenate([k, k], axis=0).astype(BF16))
                vs.append(jnp.concatenate([v, v, jnp.ones((LANES, v.shape[1]), F32)], axis=0).astype(BF16))
            for i in (2 * g, 2 * g + 1):
                cols = slice(i * LANES, (i + 1) * LANES)
                q_b = q_ref[:, cols]
                outs = []
                for half in range(2):
                    s_parts = [_dot(q_b * m64[half], k) for k in ks]
                    sink = None
                    if kind == 0:
                        sink = sink_ref[2 * i + half] * LOG2E
                        if latent:
                            s_parts[-1] = jnp.where(band, s_parts[-1], NEG_BIG)
                    outs.append(_softmax_pv(s_parts, vs, True, sink))
                o_ref[:, kind * 512 + i * LANES:kind * 512 + (i + 1) * LANES] = (
                    jnp.where(low, outs[0], outs[1]).astype(BF16))


def _attn_odd(latent, rows, layer_j, q_arrs, new_arrs, cache_arrs, sink):
    batch, n_tokens, row0 = (rows.lat_b, rows.lat_n, 0) if latent else (rows.ctx_b, rows.ctx_n, rows.lat_rows)
    tq = min(Q_TILE_BANDED, n_tokens)
    nt = n_tokens // tq
    q0 = row0 // tq
    in_specs = [pl.BlockSpec((tq, 512), lambda b, t: (q0 + b * nt + t, 0))] * 2
    args = list(q_arrs)
    scratch = []
    if latent:
        past = cache_arrs[0].shape[-1]
        in_specs += [pl.BlockSpec((None, None, LANES, past), lambda b, t: (b, layer_j, 0, 0))] * 4
        args += list(cache_arrs)
        chunks = n_tokens // SUB_ROWS
        in_specs += [pl.BlockSpec((chunks, LANES, SUB_ROWS), lambda b, t: (b, 0, 0))] * 4
        scratch = [pltpu.VMEM((LANES, n_tokens), F32)] * 4
    else:
        in_specs += [pl.BlockSpec((None, None, LANES, n_tokens), lambda b, t: (b, layer_j, 0, 0))] * 4
    args += list(new_arrs)
    in_specs += [pl.BlockSpec(memory_space=pltpu.SMEM)]
    args += [sink]
    return pl.pallas_call(
        functools.partial(_attn_odd_kernel, latent, tq),
        grid=(batch, nt),
        in_specs=in_specs,
        out_specs=pl.BlockSpec((tq, D_MODEL), lambda b, t: (b * nt + t, 0)),
        out_shape=jax.ShapeDtypeStruct((batch * n_tokens, D_MODEL), BF16),
        scratch_shapes=scratch,
        compiler_params=_params("parallel", "parallel"),
        name="attn_odd_lat" if latent else "attn_odd_ctx",
    )(*args)


def _post_kernel(lat_tiles, ol_ref, oc_ref, wout_ref, xl_ref, xc_ref, mod_ref, g2_ref, wr2_ref, wrh_ref,
                 x1_ref, h2_ref, afft_ref):
    mod = mod_ref[0]
    wout = wout_ref[...].astype(BF16)
    lane = lax.broadcasted_iota(jnp.int32, (1, LANES), 1)
    for c in range(N_SUB):
        r = _sub(c)
        o = _pick_rows(lat_tiles, ol_ref, oc_ref, r)
        x1 = _pick_rows(lat_tiles, xl_ref, xc_ref, r) + mod[2:3] * _dot(o, wout)
        x1_ref[r, :] = x1
        h2 = _modulated(x1, mod, g2_ref[...], 3, 4)
        h_hi = h2.astype(BF16)
        h_lo = (h2 - h_hi.astype(F32)).astype(BF16)
        h2_ref[r, :] = h_hi
        both = _dot(h_hi, wr2_ref[...])
        logits = both[:, :LANES] + both[:, LANES:] + _dot(h_lo, wrh_ref[...])
        logits = jnp.where(lane < N_EXPERTS, logits, NEG_BIG)
        e = jnp.exp(logits - jnp.max(logits, axis=-1, keepdims=True))
        aff = e / jnp.sum(e, axis=-1, keepdims=True)
        afft_ref[c] = aff.T[0:N_EXPERTS]


def _post(rows, layer_j, o_lat, o_ctx, wout_all, x_lat, x_ctx, mods, g2, wr2, wrh):
    specs = _TileSpecs(rows)
    n_rows = rows.lat_rows + rows.ctx_rows
    w_spec = pl.BlockSpec((None,) + wout_all.shape[1:], lambda i: (layer_j, 0, 0))
    return pl.pallas_call(
        functools.partial(_post_kernel, rows.lat_tiles),
        grid=(rows.tiles,),
        in_specs=[specs.lat_rows(D_MODEL), specs.ctx_rows(D_MODEL), w_spec,
                  specs.lat_rows(D_MODEL), specs.ctx_rows(D_MODEL), specs.mod, _full(g2.shape),
                  _full(wr2.shape), _full(wrh.shape)],
        out_specs=[specs.rows(D_MODEL), specs.rows(D_MODEL), specs.chunks(N_EXPERTS)],
        out_shape=[jax.ShapeDtypeStruct((n_rows, D_MODEL), F32),
                   jax.ShapeDtypeStruct((n_rows, D_MODEL), BF16),
                   jax.ShapeDtypeStruct((n_rows // SUB_ROWS, N_EXPERTS, SUB_ROWS), F32)],
        compiler_params=_params("parallel"),
        name="post",
    )(o_lat, o_ctx, wout_all, x_lat, x_ctx, mods, g2, wr2, wrh)


def _route_kernel(caps, *refs):
    n_groups = len(caps)
    aff_refs, out_refs = refs[:n_groups], refs[n_groups:]
    bits = [lax.bitcast_convert_type(ref[...], jnp.int32) for ref in aff_refs]

    def body(_, carry):
        new = []
        for b, cap, (lo, hi) in zip(bits, caps, carry):
            mid = lo + ((hi - lo + 1) >> 1)
            cnt = jnp.sum(jnp.where(b >= mid, 1.0, 0.0), axis=-1, keepdims=True)
            ok = cnt >= cap
            new.append((jnp.where(ok, mid, lo), jnp.where(ok, hi, mid - 1)))
        return tuple(new)

    init = tuple((jnp.zeros((b.shape[0], 1), jnp.int32), jnp.full((b.shape[0], 1), 0x7F800000, jnp.int32))
                 for b in bits)
    found = lax.fori_loop(0, 31, body, init)
    for b, cap, (thr, _), out_ref in zip(bits, caps, found, out_refs):
        n = b.shape[1]
        above = jnp.where(b > thr, 1.0, 0.0)
        equal = jnp.where(b == thr, 1.0, 0.0)
        room = cap - jnp.sum(above, axis=-1, keepdims=True)
        before = jnp.where(lax.broadcasted_iota(jnp.int32, (n, n), 0)
                           < lax.broadcasted_iota(jnp.int32, (n, n), 1), 1.0, 0.0).astype(BF16)
        equal_before = _dot(equal.astype(BF16), before)
        chosen = above + equal * jnp.where(equal_before < room, 1.0, 0.0)
        slot = _dot(chosen.astype(BF16), before)
        out_ref[...] = jnp.where(chosen > 0.5, slot, -1.0)


def _route(caps, aff_list):
    return pl.pallas_call(
        functools.partial(_route_kernel, tuple(caps)),
        grid=(1,),
        in_specs=[_full(a.shape) for a in aff_list],
        out_specs=[_full(a.shape) for a in aff_list],
        out_shape=[jax.ShapeDtypeStruct(a.shape, F32) for a in aff_list],
        compiler_params=_params("arbitrary"),
        name="route",
    )(*aff_list)


def _gather_part(cap, req_per_step, experts_per_step, first, slot_ref, aff_ref, h_ref, xs_ref, gate_ref):
    n = slot_ref.shape[1]
    want = lax.broadcasted_iota(jnp.int32, (cap, 1), 0).astype(F32)
    for rq in range(req_per_step):
        out_rows = slice(rq * cap, (rq + 1) * cap)
        rows = []
        for k in range(experts_per_step):
            row = rq * N_EXPERTS + first + k
            hit = slot_ref[pl.ds(row, 1), :] == want
            rows.append(jnp.where(hit, 1.0, 0.0).astype(BF16))
            gate_ref[k, out_rows, :] = jnp.sum(jnp.where(hit, aff_ref[pl.ds(row, 1), :], 0.0),
                                               axis=-1, keepdims=True)
        picked = _dot(jnp.concatenate(rows, axis=0), h_ref[rq * n:(rq + 1) * n, :]).astype(BF16)
        for k in range(experts_per_step):
            xs_ref[k, out_rows, :] = picked[k * cap:(k + 1) * cap]


def _gather_kernel(lat_steps, groups, caps, ctx_req_per_step,
                   sl_l, af_l, h_l, sl_c, af_c, h_c, xs_l, gt_l, xs_c, gt_c):
    s = pl.program_id(0)
    per_group = N_EXPERTS // groups

    @pl.when(s < lat_steps)
    def _():
        _gather_part(caps[0], 1, per_group, (s % groups) * per_group, sl_l, af_l, h_l, xs_l, gt_l)

    @pl.when(s >= lat_steps)
    def _():
        _gather_part(caps[1], ctx_req_per_step, N_EXPERTS, 0, sl_c, af_c, h_c, xs_c, gt_c)


def _gather(rows, caps, slots, affs, h2):
    groups = rows.lat_n // SCATTER_ROWS
    per_group = N_EXPERTS // groups
    rps = ROW_TILE // rows.ctx_n
    lat_steps = rows.lat_b * groups
    ctx_steps = rows.ctx_b // rps
    last_b = rows.lat_b - 1
    ctx_block0 = rows.lat_rows // (rps * rows.ctx_n)
    lat_req = lambda s: jnp.minimum(s // groups, last_b)
    lat_grp = lambda s: jnp.where(s < lat_steps, s % groups, groups - 1)
    ctx_step = lambda s: jnp.maximum(s - lat_steps, 0)
    cap_l, cap_c = caps
    lat_rows_spec = pl.BlockSpec((N_EXPERTS, rows.lat_n), lambda s: (lat_req(s), 0))
    ctx_rows_spec = pl.BlockSpec((rps * N_EXPERTS, rows.ctx_n), lambda s: (ctx_step(s), 0))
    return pl.pallas_call(
        functools.partial(_gather_kernel, lat_steps, groups, caps, rps),
        grid=(lat_steps + ctx_steps,),
        in_specs=[lat_rows_spec, lat_rows_spec,
                  pl.BlockSpec((rows.lat_n, D_MODEL), lambda s: (lat_req(s), 0)),
                  ctx_rows_spec, ctx_rows_spec,
                  pl.BlockSpec((rps * rows.ctx_n, D_MODEL), lambda s: (ctx_block0 + ctx_step(s), 0))],
        out_specs=[pl.BlockSpec((per_group, cap_l, D_MODEL), lambda s: (lat_grp(s), lat_req(s), 0)),
                   pl.BlockSpec((per_group, cap_l, 1), lambda s: (lat_grp(s), lat_req(s), 0)),
                   pl.BlockSpec((N_EXPERTS, rps * cap_c, D_MODEL), lambda s: (0, ctx_step(s), 0)),
                   pl.BlockSpec((N_EXPERTS, rps * cap_c, 1), lambda s: (0, ctx_step(s), 0))],
        out_shape=[jax.ShapeDtypeStruct((N_EXPERTS, rows.lat_b * cap_l, D_MODEL), BF16),
                   jax.ShapeDtypeStruct((N_EXPERTS, rows.lat_b * cap_l, 1), F32),
                   jax.ShapeDtypeStruct((N_EXPERTS, rows.ctx_b * cap_c, D_MODEL), BF16),
                   jax.ShapeDtypeStruct((N_EXPERTS, rows.ctx_b * cap_c, 1), F32)],
        compiler_params=_params("arbitrary"),
        name="gather",
    )(slots[0], affs[0], h2, slots[1], affs[1], h2)


def _ffn_kernel(n_groups, next_layer, *refs):
    xs_refs = refs[:n_groups]
    gate_refs = refs[n_groups:2 * n_groups]
    w1_ref, w3_ref, w2_ref = refs[2 * n_groups:2 * n_groups + 3]
    refs = refs[2 * n_groups + 3:]
    if next_layer is not None:
        c_ref, wm_ref, bm_ref = refs[:3]
        refs = refs[3:]
        mods_ref = refs[n_groups]
        acc_refs = refs[n_groups + 1:]
    else:
        acc_refs = refs[n_groups:]
    ys_refs = refs[:n_groups]
    f = pl.program_id(1)

    @pl.when(f == 0)
    def _():
        for acc_ref in acc_refs:
            acc_ref[...] = jnp.zeros(acc_ref.shape, F32)

    if next_layer is not None:
        mods_ref[...] = _adaln_block(next_layer, c_ref, wm_ref, bm_ref)

    hidden = [[] for _ in xs_refs]
    for n in range(0, w1_ref.shape[3], MXU_DIM):
        w1 = w1_ref[0, 0, :, n:n + MXU_DIM].astype(BF16)
        w3 = w3_ref[0, 0, :, n:n + MXU_DIM].astype(BF16)
        for parts, xs_ref in zip(hidden, xs_refs):
            xs = xs_ref[0]
            parts.append((_silu(_dot(xs, w1)) * _dot(xs, w3)).astype(BF16))
    hidden = [jnp.concatenate(parts, axis=1) for parts in hidden]
    for n in range(0, w2_ref.shape[3], MXU_DIM):
        w2 = w2_ref[0, 0, :, n:n + MXU_DIM].astype(BF16)
        for hid, acc_ref in zip(hidden, acc_refs):
            acc_ref[:, n:n + MXU_DIM] += _dot(hid, w2)

    @pl.when(f == pl.num_programs(1) - 1)
    def _():
        for gate_ref, ys_ref, acc_ref in zip(gate_refs, ys_refs, acc_refs):
            ys_ref[0] = (acc_ref[...] * gate_ref[0]).astype(BF16)


def _ffn(layer, xs_list, gate_list, w1, w3, w2, mod_inputs=None):
    n_groups = len(xs_list)
    n_chunks = EXPERT_FF // FF_CHUNK
    xs_specs = [pl.BlockSpec((1,) + xs.shape[1:], lambda e, f: (e, 0, 0)) for xs in xs_list]
    gate_specs = [pl.BlockSpec((1,) + g.shape[1:], lambda e, f: (e, 0, 0)) for g in gate_list]
    in_specs = xs_specs + gate_specs + [
        pl.BlockSpec((1, 1, D_MODEL, FF_CHUNK), lambda e, f: (layer, e, 0, f)),
        pl.BlockSpec((1, 1, D_MODEL, FF_CHUNK), lambda e, f: (layer, e, 0, f)),
        pl.BlockSpec((1, 1, FF_CHUNK, D_MODEL), lambda e, f: (layer, e, f, 0))]
    args = [*xs_list, *gate_list, w1, w3, w2]
    out_specs = list(xs_specs)
    out_shape = [jax.ShapeDtypeStruct(xs.shape, BF16) for xs in xs_list]
    next_layer = None
    if mod_inputs is not None:
        next_layer = layer + 1
        cvec, w_mod, b_mod = mod_inputs
        last = 6 * D_MODEL // MXU_DIM - 1
        assert N_EXPERTS * n_chunks > last
        block = lambda e, f: jnp.minimum(e * n_chunks + f, last)
        in_specs += [_full(cvec.shape),
                     pl.BlockSpec((None, D_MODEL, MXU_DIM), lambda e, f: (next_layer, 0, block(e, f))),
                     pl.BlockSpec((DEPTH, MXU_DIM), lambda e, f: (0, block(e, f)))]
        args += [cvec, w_mod, b_mod]
        out_specs.append(pl.BlockSpec((cvec.shape[0], MXU_DIM), lambda e, f: (0, block(e, f))))
        out_shape.append(jax.ShapeDtypeStruct((cvec.shape[0], 6 * D_MODEL), F32))
    return pl.pallas_call(
        functools.partial(_ffn_kernel, n_groups, next_layer),
        grid=(N_EXPERTS, n_chunks),
        in_specs=in_specs, out_specs=out_specs, out_shape=out_shape,
        scratch_shapes=[pltpu.VMEM(xs.shape[1:], F32) for xs in xs_list],
        compiler_params=_params("arbitrary", "arbitrary", vmem=FFN_VMEM_LIMIT),
        name="ffn",
    )(*args)


def _scatter_part(cap, slots, ys_ref, ys_rows, x, gate2):
    lane = lax.broadcasted_iota(jnp.int32, (1, LANES), 1).astype(F32)
    per_block = LANES // cap
    pad = jnp.zeros((LANES - N_EXPERTS, SUB_ROWS), F32)
    slots_t = jnp.concatenate([slots, pad], axis=0).T
    blocks = []
    for k in range(N_EXPERTS // per_block):
        hit = None
        for e in range(k * per_block, (k + 1) * per_block):
            col = slots_t[:, e:e + 1]
            target = jnp.where(col >= 0.0, col + float((e - k * per_block) * cap), -1.0)
            he = jnp.where(target == lane, 1.0, 0.0)
            hit = he if hit is None else hit + he
        blocks.append(hit.astype(BF16))
    onehot = jnp.concatenate(blocks, axis=1)
    ys = jnp.concatenate([ys_ref[e, ys_rows, :] for e in range(N_EXPERTS)], axis=0)
    return x + gate2 * _dot(onehot, ys)


def _combine_kernel(lat_steps, steps_per_req, caps, sl_l, ys_l, sl_c, ys_c, x_ref, mod_ref, ol_ref, oc_ref):
    s = pl.program_id(0)
    gate2 = mod_ref[0][5:6]
    chunks = SCATTER_ROWS // SUB_ROWS

    @pl.when(s < lat_steps)
    def _():
        first = (s % steps_per_req) * SCATTER_ROWS
        for c in range(chunks):
            tokens = pl.multiple_of(first + c * SUB_ROWS, SUB_ROWS)
            ol_ref[_sub(c), :] = _scatter_part(caps[0], sl_l[:, pl.ds(tokens, SUB_ROWS)], ys_l,
                                               slice(0, caps[0]), x_ref[_sub(c), :], gate2)

    @pl.when(s >= lat_steps)
    def _():
        for rq in range(chunks):
            oc_ref[_sub(rq), :] = _scatter_part(caps[1], sl_c[rq * N_EXPERTS:(rq + 1) * N_EXPERTS, :], ys_c,
                                                slice(rq * caps[1], (rq + 1) * caps[1]), x_ref[_sub(rq), :], gate2)


def _combine(rows, caps, slots, ys, x1, mods):
    per_req = rows.lat_n // SCATTER_ROWS
    req_per_step = SCATTER_ROWS // rows.ctx_n
    lat_steps = rows.lat_b * per_req
    ctx_steps = rows.ctx_b // req_per_step
    last_b = rows.lat_b - 1
    lat_req = lambda s: jnp.minimum(s // per_req, last_b)
    ctx_step = lambda s: jnp.maximum(s - lat_steps, 0)
    cap_l, cap_c = caps
    return pl.pallas_call(
        functools.partial(_combine_kernel, lat_steps, per_req, caps),
        grid=(lat_steps + ctx_steps,),
        in_specs=[pl.BlockSpec((N_EXPERTS, rows.lat_n), lambda s: (lat_req(s), 0)),
                  pl.BlockSpec((N_EXPERTS, cap_l, D_MODEL), lambda s: (0, lat_req(s), 0)),
                  pl.BlockSpec((req_per_step * N_EXPERTS, rows.ctx_n), lambda s: (ctx_step(s), 0)),
                  pl.BlockSpec((N_EXPERTS, req_per_step * cap_c, D_MODEL), lambda s: (0, ctx_step(s), 0)),
                  pl.BlockSpec((SCATTER_ROWS, D_MODEL), lambda s: (s, 0)),
                  pl.BlockSpec((1, 6, D_MODEL), lambda s: (jnp.where(s < lat_steps, 1 + s // per_req, 0), 0, 0))],
        out_specs=[pl.BlockSpec((SCATTER_ROWS, D_MODEL), lambda s: (jnp.minimum(s, lat_steps - 1), 0)),
                   pl.BlockSpec((SCATTER_ROWS, D_MODEL), lambda s: (ctx_step(s), 0))],
        out_shape=[jax.ShapeDtypeStruct((rows.lat_rows, D_MODEL), F32),
                   jax.ShapeDtypeStruct((rows.ctx_rows, D_MODEL), F32)],
        compiler_params=_params("arbitrary", vmem=FFN_VMEM_LIMIT),
        name="combine",
    )(slots[0], ys[0], slots[1], ys[1], x1, mods)


def _rope_tables(n_pos, rot_dim):
    t = jnp.arange(n_pos)
    row = t // GRID_W
    col = t % GRID_W
    nf = rot_dim // 4
    inv = ROPE_THETA ** (-jnp.arange(nf, dtype=F32) / nf)
    ang_r = row[:, None] * inv
    ang_c = col[:, None] * inv
    cr, sr, cc, sc = jnp.cos(ang_r), jnp.sin(ang_r), jnp.cos(ang_c), jnp.sin(ang_c)
    cos = jnp.concatenate([cr, cr, cc, cc], axis=1)
    sin = jnp.concatenate([-sr, sr, -sc, sc], axis=1)
    reps = LANES // rot_dim
    cos = jnp.concatenate([jnp.tile(cos, (1, reps)), jnp.ones((ROW_TILE, LANES), F32)], axis=0)
    sin = jnp.concatenate([jnp.tile(sin, (1, reps)), jnp.zeros((ROW_TILE, LANES), F32)], axis=0)
    return cos, sin


def _group_mean_matrix(width):
    idx = jnp.arange(MXU_DIM) // width
    return jnp.where(idx[:, None] == idx[None, :], 1.0 / width, 0.0).astype(BF16)


def _tile_row(g, reps):
    return jnp.tile(g, reps).reshape(1, -1)


def _split_heads(w, n_heads, first):
    k = w.shape[0]
    w3 = w.reshape(k, n_heads, -1)
    return jnp.concatenate([w3[:, :, :first].reshape(k, -1), w3[:, :, first:].reshape(k, -1)], axis=1)


def _feature_major(cache):
    b, l, t, h, d = cache.shape
    return jnp.transpose(cache, (0, 1, 3, 4, 2)).reshape(b, l, h * d, t)


def kernel(x_prompt, x_sample, cache_mla_ckv, cache_mla_krope, cache_diff_k, cache_diff_v,
           cache_win_k, cache_win_v, cache_axial_k, cache_axial_v, c, c_ctx,
           g_norm, w_mod, b_mod, w_in_even, w_out_even, mla_g_qa, mla_g_kva, mla_w_uq, mla_w_ukv,
           mla_g_q, mla_g_k, diff_g_q, diff_g_k, diff_lambda, diff_g_sub,
           w_in_odd, w_out_odd, odd_g_qk, win_sink, moe_w_router, moe_w1, moe_w3, moe_w2):
    rows = _Rows(lat_b=x_sample.shape[0], lat_n=x_sample.shape[1], ctx_b=x_prompt.shape[0], ctx_n=x_prompt.shape[1])
    past = cache_mla_ckv.shape[2]
    n_even, n_odd = w_in_even.shape[0], w_in_odd.shape[0]
    assert rows.ctx_n == SUB_ROWS and rows.lat_n % ROW_TILE == 0 and rows.ctx_b % N_SUB == 0
    caps = (EC_CAPACITY_FACTOR * rows.lat_n // N_EXPERTS, EC_CAPACITY_FACTOR * rows.ctx_n // N_EXPERTS)

    cvec = jnp.zeros((16, D_MODEL), F32).at[0].set(c_ctx).at[1:1 + rows.lat_b].set(c)
    mods_l = _adaln(0, cvec, w_mod, b_mod).reshape(16, 6, D_MODEL)

    g64 = _group_mean_matrix(64)
    g32 = _group_mean_matrix(32)
    tab64 = _rope_tables(rows.lat_n, HEAD_DIM)
    tab32 = _rope_tables(rows.lat_n, MLA_ROPE)
    odd_caches = [_feature_major(a) for a in (cache_win_k, cache_win_v, cache_axial_k, cache_axial_v)]
    diff_caches = [a.reshape(rows.lat_b, n_even, past * DIFF_HEADS, LANES) for a in (cache_diff_k, cache_diff_v)]

    states = _zero_states([(rows.ctx_b, n_even, rows.ctx_n, MLA_KV_LORA),
                           (rows.ctx_b, n_even, MLA_ROPE, rows.ctx_n),
                           (rows.ctx_b, n_even, rows.ctx_n * DIFF_HEADS, LANES),
                           (rows.ctx_b, n_even, rows.ctx_n * DIFF_HEADS, LANES)]
                          + [(rows.ctx_b, n_odd, LANES, rows.ctx_n)] * 4)
    even_states, odd_states = list(states[:4]), list(states[4:])

    x_lat = x_sample.reshape(-1, D_MODEL)
    x_ctx = x_prompt.reshape(-1, D_MODEL)
    lat_chunks = rows.lat_rows // SUB_ROWS

    for layer in range(DEPTH):
        j = layer // 2
        g1 = g_norm[layer, 0].reshape(1, -1)
        g2 = g_norm[layer, 1].reshape(1, -1)
        if layer % 2 == 0:
            wt = jnp.swapaxes(w_in_even[j], 0, 1)
            wt = jnp.concatenate([wt[:384], wt[416:]] + [wt[384:416]] * 4, axis=0).astype(BF16)
            wuq = _split_heads(mla_w_uq[j], MLA_HEADS, MLA_NOPE).astype(BF16)
            wukv = _split_heads(mla_w_ukv[j], MLA_HEADS, MLA_NOPE).astype(BF16)
            w_out = w_out_even
            gkn = _tile_row(mla_g_k[j, :MLA_NOPE], MLA_HEADS)
            gains = [mla_g_qa[j].reshape(1, -1), mla_g_kva[j].reshape(1, -1),
                     _tile_row(mla_g_q[j, :MLA_NOPE], MLA_HEADS), _tile_row(mla_g_q[j, MLA_NOPE:], MLA_HEADS),
                     gkn, _tile_row(mla_g_k[j, MLA_NOPE:], 4),
                     _tile_row(diff_g_q[j], 2 * DIFF_HEADS), _tile_row(diff_g_k[j], 2 * DIFF_HEADS)]
            lam_init = 0.8 - 0.6 * math.exp(-0.3 * layer)
            gsub = diff_g_sub[j].reshape(1, -1)
            outs = _in_even(rows, j, x_lat, x_ctx, mods_l, g1, wt, wuq, wukv, gains, (g64, g32),
                            tab64 + tab32, even_states)
            qn, qr, kn, vm, krt, mq, mk, mv = outs[:8]
            even_states = list(outs[8:])
            kn_c, vm_c = _cache_kv(cache_mla_ckv[:, j].reshape(-1, MLA_KV_LORA), wukv, gkn, g64)
            cache_arrs = (kn_c, vm_c,
                          jnp.tile(cache_mla_krope[:, j].reshape(-1, MLA_ROPE), (1, 4)).astype(BF16),
                          diff_caches[0], diff_caches[1])
            o_lat = _attn_even(True, rows, j, lam_init, (qn, qr, mq), (kn, vm, krt, mk, mv), cache_arrs,
                               diff_lambda[j], gsub)
            o_ctx = _attn_even(False, rows, j, lam_init, (qn, qr, mq), (kn, vm, krt, mk, mv), None,
                               diff_lambda[j], gsub)
        else:
            w_out = w_out_odd
            gains = [_tile_row(odd_g_qk[j, 0], 8), _tile_row(odd_g_qk[j, 1], 2),
                     _tile_row(odd_g_qk[j, 2], 8), _tile_row(odd_g_qk[j, 3], 2)]
            outs = _in_odd(rows, j, x_lat, x_ctx, mods_l, g1, w_in_odd, gains, g64, tab64, odd_states)
            odd_states = list(outs[6:])
            o_lat = _attn_odd(True, rows, j, outs[:2], outs[2:6], odd_caches, win_sink[j])
            o_ctx = _attn_odd(False, rows, j, outs[:2], odd_states, None, win_sink[j])

        w_r = jnp.pad(moe_w_router[layer], ((0, 0), (0, LANES - N_EXPERTS)))
        wrh = w_r.astype(BF16)
        wrl = (w_r - wrh.astype(F32)).astype(BF16)
        x1, h2, afft = _post(rows, j, o_lat, o_ctx, w_out, x_lat, x_ctx, mods_l, g2,
                             jnp.concatenate([wrh, wrl], axis=1), wrh)
        aff_lat = (afft[:lat_chunks].reshape(rows.lat_b, rows.lat_n // SUB_ROWS, N_EXPERTS, SUB_ROWS)
                   .transpose(0, 2, 1, 3).reshape(rows.lat_b * N_EXPERTS, rows.lat_n))
        aff_ctx = afft[lat_chunks:].reshape(rows.ctx_b * N_EXPERTS, rows.ctx_n)
        affs = (aff_lat, aff_ctx)
        slots = _route(caps, affs)
        xs_l, gt_l, xs_c, gt_c = _gather(rows, caps, slots, affs, h2)
        more = layer + 1 < DEPTH
        outs = _ffn(layer, [xs_l, xs_c], [gt_l, gt_c], moe_w1, moe_w3, moe_w2,
                    (cvec, w_mod, b_mod) if more else None)
        x_lat, x_ctx = _combine(rows, caps, slots, outs[:2], x1, mods_l)
        if more:
            mods_l = outs[2].reshape(16, 6, D_MODEL)

    def token_major(arr, heads):
        b, l, f, t = arr.shape
        return jnp.transpose(arr.reshape(b, l, heads, f // heads, t), (0, 1, 4, 2, 3))

    diff_shape = (rows.ctx_b, n_even, rows.ctx_n, DIFF_HEADS, LANES)
    return (x_ctx.reshape(x_prompt.shape), x_lat.reshape(x_sample.shape),
            even_states[0],
            jnp.swapaxes(even_states[1], 2, 3),
            even_states[2].reshape(diff_shape),
            even_states[3].reshape(diff_shape),
            token_major(odd_states[0], 2), token_major(odd_states[1], 2),
            token_major(odd_states[2], 2), token_major(odd_states[3], 2))
```
